```python
import math
import jax, jax.numpy as jnp
from jax import lax
import numpy as np

D_MODEL = 1024
BATCH = 16
SEQ = 256
DEPTH = 4
DEC_BATCH = 4
DEC_SEQ = 4096
PAST_LEN = 256

GRID_W = 64
N_EVEN = (DEPTH + 1) // 2
N_ODD = DEPTH // 2

FNET_W = D_MODEL // 2
FNET_GROUPS = 4
FNET_GW = FNET_W // FNET_GROUPS
CONV_W = D_MODEL // 2
CONV_K = 3
EVEN_IN = FNET_W + 3 * CONV_W
EVEN_MIX = FNET_W + CONV_W

LRU_W = D_MODEL // 2
LRU_HEADS = 8
LRU_HD = LRU_W // LRU_HEADS
LRU_CONV_K = 4
LRU_C = 8.0
WKV_W = D_MODEL // 2
WKV_N = 64
WKV_H = WKV_W // WKV_N
DECAY_RANK = 64
ICLR_RANK = 64
GATE_RANK = 128
WKV_IN = 3 * WKV_W + 2 * DECAY_RANK + 2 * ICLR_RANK + GATE_RANK
ODD_IN = 2 * LRU_W + WKV_IN
ODD_MIX = LRU_W + WKV_W
DECAY_SCALE = math.exp(-0.5)
WKV_GN_EPS = 64e-5

PEER_HEADS = 8
N_KEYS = 128
N_EXPERTS = N_KEYS * N_KEYS
PEER_TOPK = 16
PEER_DK = 256
PEER_HALF = PEER_DK // 2
PEER_BLOCK = 128

ALPHA = (2 * DEPTH) ** 0.25
BETA = (8 * DEPTH) ** -0.25
LN_EPS = 1e-6

kernel_name = 'hybrid_fnet_conv_rglru_rwkv7_peer_diffusion_step'


def _layer_norm(x, g=None, b=None, eps=LN_EPS):
    xf = x.astype(jnp.float32)
    mu = xf.mean(-1, keepdims=True)
    var = jnp.square(xf - mu).mean(-1, keepdims=True)
    y = (xf - mu) * lax.rsqrt(var + eps)
    if g is not None:
        y = y * g.astype(jnp.float32) + b.astype(jnp.float32)
    return y.astype(x.dtype)


def _sincos(pos, dim):
    omega = 1.0 / (10000.0 ** (jnp.arange(dim // 2, dtype=jnp.float32) / (dim // 2)))
    ang = pos.astype(jnp.float32)[:, None] * omega[None, :]
    return jnp.concatenate([jnp.sin(ang), jnp.cos(ang)], -1)


def _grid_pos_embed(n_tok):
    rows = n_tok // GRID_W
    half = D_MODEL // 2
    er = _sincos(jnp.arange(rows), half)
    ec = _sincos(jnp.arange(GRID_W), half)
    emb = jnp.concatenate([jnp.broadcast_to(er[:, None, :], (rows, GRID_W, half)),
                           jnp.broadcast_to(ec[None, :, :], (rows, GRID_W, half))], -1)
    return emb.reshape(rows * GRID_W, D_MODEL)


def _modulation(cvec, w_mod, b_mod):
    m = jax.nn.silu(cvec) @ w_mod + b_mod
    return jnp.split(m[:, None, :], 6, axis=-1)


def _fourier_mix(a):
    B, S, _ = a.shape
    ag = a.astype(jnp.float32).reshape(B, S, FNET_GROUPS, FNET_GW)
    y = jnp.fft.fft2(ag, axes=(1, 3), norm='ortho').real
    return y.reshape(B, S, FNET_W).astype(a.dtype)


def _short_conv_mix(bg, cg, xin, w, bias):
    S = xin.shape[1]
    zp = jnp.pad(cg * xin, ((0, 0), (1, 1), (0, 0)))
    y = zp[:, 0:S] * w[0] + zp[:, 1:S + 1] * w[1] + zp[:, 2:S + 2] * w[2] + bias
    return bg * y


def _linear_scan(a, b, h0):
    def comb(l, r):
        return (l[0] * r[0], r[0] * l[1] + r[1])
    a_cum, b_cum = lax.associative_scan(comb, (a, b), axis=1)
    h = a_cum * h0[:, None, :] + b_cum
    return h, h[:, -1]


def _rglru_dir(xs, conv_w, conv_b, wa, ba, wx, bx, lam, h0):
    f32 = jnp.float32
    B, S, _ = xs.shape
    xs = xs.astype(f32)
    xp = jnp.pad(xs, ((0, 0), (LRU_CONV_K - 1, 0), (0, 0)))
    xc = conv_b.astype(f32) + xp[:, 0:S] * conv_w[0].astype(f32)
    for j in range(1, LRU_CONV_K):
        xc = xc + xp[:, j:j + S] * conv_w[j].astype(f32)
    xh = xc.reshape(B, S, LRU_HEADS, LRU_HD)
    gate_r = jax.nn.sigmoid(jnp.einsum('bshi,hij->bshj', xh, wa.astype(f32)).reshape(B, S, LRU_W) + ba.astype(f32))
    gate_i = jax.nn.sigmoid(jnp.einsum('bshi,hij->bshj', xh, wx.astype(f32)).reshape(B, S, LRU_W) + bx.astype(f32))
    log_a = -LRU_C * gate_r * jax.nn.softplus(-lam.astype(f32))
    a = jnp.exp(log_a)
    b = jnp.sqrt(-jnp.expm1(2.0 * log_a)) * (gate_i * xc)
    return _linear_scan(a, b, h0)


def _lru_mix(xb, gb, conv_w, conv_b, wa, ba, wx, bx, lam, h0):
    y = 0.0
    finals = []
    for d in range(2):
        xs = xb if d == 0 else xb[:, ::-1]
        h, hf = _rglru_dir(xs, conv_w[d], conv_b[d], wa[d], ba[d], wx[d], bx[d], lam[d],
                           h0[:, d].astype(jnp.float32))
        y = y + (h if d == 0 else h[:, ::-1])
        finals.append(hf)
    y = y * jax.nn.gelu(gb.astype(jnp.float32))
    return y, jnp.stack(finals, 1)


def _wkv_scan(r, w, k, v, kk, a, s0):
    def step(state, inp):
        r_t, w_t, k_t, v_t, kk_t, a_t = inp
        sk = jnp.einsum('bhij,bhj->bhi', state, kk_t)
        state = (state * w_t[:, :, None, :]
                 - sk[..., None] * (kk_t * a_t)[:, :, None, :]
                 + v_t[..., None] * k_t[:, :, None, :])
        return state, jnp.einsum('bhij,bhj->bhi', state, r_t)
    xs = tuple(jnp.moveaxis(t, 1, 0) for t in (r, w, k, v, kk, a))
    s_fin, ys = lax.scan(step, s0, xs)
    return jnp.moveaxis(ys, 0, 1), s_fin


def _rwkv_mix(z, mu, w0, w2, a0, a2, k_k, k_a, r_k, g2, gn_g, gn_b, s0):
    f32 = jnp.float32
    B, S, _ = z.shape
    z = z.astype(f32)
    zp = jnp.pad(z, ((0, 0), (1, 1), (0, 0)))
    z = z + mu.astype(f32) * (0.5 * (zp[:, 0:S] + zp[:, 2:S + 2]) - z)
    o1 = WKV_W
    o2 = o1 + WKV_W
    o3 = o2 + WKV_W
    o4 = o3 + 2 * DECAY_RANK
    o5 = o4 + 2 * ICLR_RANK
    r, k, v = z[..., 0:o1], z[..., o1:o2], z[..., o2:o3]
    wd, ad, gd = z[..., o3:o4], z[..., o4:o5], z[..., o5:]
    g = jax.nn.sigmoid(gd) @ g2.astype(f32)
    heads = lambda t: t.reshape(B, S, WKV_H, WKV_N)
    r_h, v_h = heads(r), heads(v)
    y_sum = 0.0
    bonus = 0.0
    finals = []
    for d in range(2):
        wz = w0[d] + jnp.tanh(wd[..., d * DECAY_RANK:(d + 1) * DECAY_RANK]) @ w2[d].astype(f32)
        decay = heads(jnp.exp(-DECAY_SCALE * jax.nn.sigmoid(wz)))
        iclr = jax.nn.sigmoid(a0[d] + ad[..., d * ICLR_RANK:(d + 1) * ICLR_RANK] @ a2[d].astype(f32))
        kk = heads(k * k_k[d])
        kk = kk * lax.rsqrt(jnp.maximum(jnp.sum(kk * kk, -1, keepdims=True), 1e-24))
        km = heads(k * (1.0 + (iclr - 1.0) * k_a[d]))
        seqs = (r_h, decay, km, v_h, kk, heads(iclr))
        if d == 1:
            seqs = tuple(t[:, ::-1] for t in seqs)
        y, s_fin = _wkv_scan(*seqs, s0[:, d].astype(f32))
        y_sum = y_sum + (y if d == 0 else y[:, ::-1])
        bonus = bonus + jnp.sum(r_h * km * r_k.astype(f32), -1, keepdims=True) * v_h
        finals.append(s_fin)
    m = y_sum.mean(-1, keepdims=True)
    var = jnp.square(y_sum - m).mean(-1, keepdims=True)
    yn = ((y_sum - m) * lax.rsqrt(var + WKV_GN_EPS)).reshape(B, S, WKV_W) * gn_g + gn_b
    out = (yn + bonus.reshape(B, S, WKV_W)) * g
    return out, jnp.stack(finals, 1)


def _peer(h, wq, keys, u_tab, v_tab):
    B, S, D = h.shape
    T = B * S
    ht = h.reshape(T, D)
    q = (ht @ wq).reshape(T, PEER_HEADS, 2, PEER_HALF)
    s = jnp.einsum('thpc,hpkc->thpk', q, keys).astype(jnp.float32)
    v1, i1 = lax.top_k(s[:, :, 0], PEER_TOPK)
    v2, i2 = lax.top_k(s[:, :, 1], PEER_TOPK)
    cand = (v1[..., :, None] + v2[..., None, :]).reshape(T, PEER_HEADS, PEER_TOPK * PEER_TOPK)
    sc, ci = lax.top_k(cand, PEER_TOPK)
    e1 = jnp.take_along_axis(i1, ci // PEER_TOPK, -1)
    e2 = jnp.take_along_axis(i2, ci % PEER_TOPK, -1)
    idx = e1 * N_KEYS + e2
    gate = jax.nn.softmax(sc, axis=-1).astype(h.dtype)
    nb = T // PEER_BLOCK

    def block(args):
        hb, ib, gb = args
        act = jax.nn.gelu(jnp.einsum('td,thkd->thk', hb, u_tab[ib]))
        return jnp.einsum('thk,thkd->td', gb * act, v_tab[ib])

    out = lax.map(block, (ht.reshape(nb, PEER_BLOCK, D),
                          idx.reshape(nb, PEER_BLOCK, PEER_HEADS, PEER_TOPK),
                          gate.reshape(nb, PEER_BLOCK, PEER_HEADS, PEER_TOPK)))
    return out.reshape(B, S, D)


def _trunk(x, cvec, lru_state, wkv_state, P):
    lru_out = []
    wkv_out = []
    for l in range(DEPTH):
        sh1, sc1, g1, sh2, sc2, g2 = _modulation(cvec, P['w_mod'][l], P['b_mod'][l])
        h = _layer_norm(x) * (1.0 + sc1) + sh1
        j = l // 2
        if l % 2 == 0:
            z = h @ P['w_in_e'][j]
            a_in = z[..., 0:FNET_W]
            b_g = z[..., FNET_W:FNET_W + CONV_W]
            c_g = z[..., FNET_W + CONV_W:FNET_W + 2 * CONV_W]
            x_in = z[..., FNET_W + 2 * CONV_W:]
            y = jnp.concatenate([_fourier_mix(a_in),
                                 _short_conv_mix(b_g, c_g, x_in, P['sconv_w'][j], P['sconv_b'][j])], -1)
            mix = y.astype(x.dtype) @ P['w_out_e'][j]
        else:
            z = h @ P['w_in_o'][j]
            y_lru, h_fin = _lru_mix(z[..., 0:LRU_W], z[..., LRU_W:2 * LRU_W],
                                    P['lru_conv_w'][j], P['lru_conv_b'][j], P['lru_wa'][j], P['lru_ba'][j],
                                    P['lru_wx'][j], P['lru_bx'][j], P['lru_lambda'][j], lru_state[:, j])
            y_wkv, s_fin = _rwkv_mix(z[..., 2 * LRU_W:], P['wkv_mu'][j], P['wkv_w0'][j], P['wkv_w2'][j],
                                     P['wkv_a0'][j], P['wkv_a2'][j], P['wkv_kk'][j], P['wkv_ka'][j],
                                     P['wkv_rk'][j], P['wkv_g2'][j], P['wkv_gn_g'][j], P['wkv_gn_b'][j],
                                     wkv_state[:, j])
            mix = jnp.concatenate([y_lru, y_wkv], -1).astype(x.dtype) @ P['w_out_o'][j]
            lru_out.append(h_fin)
            wkv_out.append(s_fin)
        x = _layer_norm(ALPHA * x + g1 * mix, P['ln1_g'][l], P['ln1_b'][l])
        h2 = _layer_norm(x) * (1.0 + sc2) + sh2
        ffn = _peer(h2, P['peer_wq'][l], P['peer_keys'][l], P['peer_u'][l], P['peer_v'][l])
        x = _layer_norm(ALPHA * x + g2 * ffn, P['ln2_g'][l], P['ln2_b'][l])
    return x, jnp.stack(lru_out, 1), jnp.stack(wkv_out, 1)


def setup_inputs(seed: int = 0) -> dict:
    key = jax.random.key(seed)
    ks = list(jax.random.split(key, 48))
    f32 = jnp.float32

    def nrm(i, shape, scale):
        return jax.random.normal(ks[i], shape, f32) * scale

    a_init = 0.9 + 0.099 * jax.random.uniform(ks[40], (N_ODD, 2, LRU_W), f32)
    return {
        'x_prompt': nrm(0, (BATCH, SEQ, D_MODEL), 1.0),
        'x_sample': nrm(1, (DEC_BATCH, DEC_SEQ, D_MODEL), 1.0),
        'state_lru': nrm(2, (DEC_BATCH, N_ODD, 2, LRU_W), 0.5),
        'state_wkv': nrm(3, (DEC_BATCH, N_ODD, 2, WKV_H, WKV_N, WKV_N), 0.5),
        'c': nrm(4, (DEC_BATCH, D_MODEL), 1.0),
        'c_ctx': nrm(5, (D_MODEL,), 1.0),
        'w_mod': nrm(6, (DEPTH, D_MODEL, 6 * D_MODEL), 0.2 * D_MODEL ** -0.5),
        'b_mod': nrm(7, (DEPTH, 6 * D_MODEL), 0.01),
        'ln1_g': 1.0 + nrm(8, (DEPTH, D_MODEL), 0.01),
        'ln1_b': nrm(9, (DEPTH, D_MODEL), 0.01),
        'ln2_g': 1.0 + nrm(10, (DEPTH, D_MODEL), 0.01),
        'ln2_b': nrm(11, (DEPTH, D_MODEL), 0.01),
        'w_in_e': nrm(12, (N_EVEN, D_MODEL, EVEN_IN), D_MODEL ** -0.5),
        'w_out_e': nrm(13, (N_EVEN, EVEN_MIX, D_MODEL), BETA * EVEN_MIX ** -0.5),
        'sconv_w': nrm(14, (N_EVEN, CONV_K, CONV_W), CONV_K ** -0.5),
        'sconv_b': nrm(15, (N_EVEN, CONV_W), 0.01),
        'w_in_o': nrm(16, (N_ODD, D_MODEL, ODD_IN), D_MODEL ** -0.5),
        'w_out_o': nrm(17, (N_ODD, ODD_MIX, D_MODEL), BETA * ODD_MIX ** -0.5),
        'lru_conv_w': nrm(18, (N_ODD, 2, LRU_CONV_K, LRU_W), 0.5),
        'lru_conv_b': nrm(19, (N_ODD, 2, LRU_W), 0.01),
        'lru_wa': nrm(20, (N_ODD, 2, LRU_HEADS, LRU_HD, LRU_HD), LRU_HD ** -0.5),
        'lru_ba': nrm(21, (N_ODD, 2, LRU_W), 0.01),
        'lru_wx': nrm(22, (N_ODD, 2, LRU_HEADS, LRU_HD, LRU_HD), LRU_HD ** -0.5),
        'lru_bx': nrm(23, (N_ODD, 2, LRU_W), 0.01),
        'lru_lambda': jnp.log(a_init) - jnp.log1p(-a_init),
        'wkv_mu': jax.random.uniform(ks[24], (N_ODD, WKV_IN), f32),
        'wkv_w0': nrm(25, (N_ODD, 2, WKV_W), 0.5),
        'wkv_w2': nrm(26, (N_ODD, 2, DECAY_RANK, WKV_W), 0.1),
        'wkv_a0': nrm(27, (N_ODD, 2, WKV_W), 0.5),
        'wkv_a2': nrm(28, (N_ODD, 2, ICLR_RANK, WKV_W), 0.1),
        'wkv_kk': 1.0 + nrm(29, (N_ODD, 2, WKV_W), 0.1),
        'wkv_ka': 1.0 + nrm(30, (N_ODD, 2, WKV_W), 0.1),
        'wkv_rk': nrm(31, (N_ODD, WKV_H, WKV_N), 0.1),
        'wkv_g2': nrm(32, (N_ODD, GATE_RANK, WKV_W), GATE_RANK ** -0.5),
        'wkv_gn_g': 1.0 + nrm(33, (N_ODD, WKV_W), 0.01),
        'wkv_gn_b': nrm(34, (N_ODD, WKV_W), 0.01),
        'peer_wq': nrm(35, (DEPTH, D_MODEL, PEER_HEADS * PEER_DK), D_MODEL ** -0.5),
        'peer_keys': nrm(36, (DEPTH, PEER_HEADS, 2, N_KEYS, PEER_HALF), PEER_HALF ** -0.5),
        'peer_u': nrm(37, (DEPTH, N_EXPERTS, D_MODEL), D_MODEL ** -0.5),
        'peer_v': nrm(38, (DEPTH, N_EXPERTS, D_MODEL), BETA),
    }


def reference(x_prompt, x_sample, state_lru, state_wkv, c, c_ctx, w_mod, b_mod, ln1_g, ln1_b, ln2_g, ln2_b,
              w_in_e, w_out_e, sconv_w, sconv_b, w_in_o, w_out_o, lru_conv_w, lru_conv_b, lru_wa, lru_ba,
              lru_wx, lru_bx, lru_lambda, wkv_mu, wkv_w0, wkv_w2, wkv_a0, wkv_a2, wkv_kk, wkv_ka, wkv_rk,
              wkv_g2, wkv_gn_g, wkv_gn_b, peer_wq, peer_keys, peer_u, peer_v):
    P = dict(w_mod=w_mod, b_mod=b_mod, ln1_g=ln1_g, ln1_b=ln1_b, ln2_g=ln2_g, ln2_b=ln2_b,
             w_in_e=w_in_e, w_out_e=w_out_e, sconv_w=sconv_w, sconv_b=sconv_b,
             w_in_o=w_in_o, w_out_o=w_out_o, lru_conv_w=lru_conv_w, lru_conv_b=lru_conv_b,
             lru_wa=lru_wa, lru_ba=lru_ba, lru_wx=lru_wx, lru_bx=lru_bx, lru_lambda=lru_lambda,
             wkv_mu=wkv_mu, wkv_w0=wkv_w0, wkv_w2=wkv_w2, wkv_a0=wkv_a0, wkv_a2=wkv_a2,
             wkv_kk=wkv_kk, wkv_ka=wkv_ka, wkv_rk=wkv_rk, wkv_g2=wkv_g2, wkv_gn_g=wkv_gn_g,
             wkv_gn_b=wkv_gn_b, peer_wq=peer_wq, peer_keys=peer_keys, peer_u=peer_u, peer_v=peer_v)

    bp = x_prompt.shape[0]
    lru0 = jnp.zeros((bp, N_ODD, 2, LRU_W), jnp.float32)
    wkv0 = jnp.zeros((bp, N_ODD, 2, WKV_H, WKV_N, WKV_N), jnp.float32)
    y_prompt, lru_new, wkv_new = _trunk(x_prompt, c_ctx[None, :], lru0, wkv0, P)

    xs = x_sample + _grid_pos_embed(x_sample.shape[1]).astype(x_sample.dtype)[None]
    y_sample, _, _ = _trunk(xs, c, state_lru, state_wkv, P)

    return (y_prompt, y_sample, lru_new.astype(x_prompt.dtype), wkv_new.astype(x_prompt.dtype))
```

```python
import functools
import math

import numpy as np
import jax
import jax.numpy as jnp
from jax import lax
from jax.experimental import pallas as pl
from jax.experimental.pallas import tpu as pltpu

F32 = jnp.float32
BF16 = jnp.bfloat16

D_MODEL = 1024
GRID_W = 64
FNET_W = 512
FNET_GROUPS = 4
FNET_GW = FNET_W // FNET_GROUPS
CONV_W = 512
LRU_W = 512
LRU_HEADS = 8
LRU_CONV_K = 4
LRU_C = 8.0
WKV_W = 512
WKV_N = 64
WKV_H = 8
WKV_IN = 1920
DECAY_SCALE = math.exp(-0.5)
WKV_GN_EPS = 64e-5
PEER_HEADS = 8
N_KEYS = 128
PEER_TOPK = 16
PEER_HALF = 128
DEPTH = 4
ALPHA = (2 * DEPTH) ** 0.25
LN_EPS = 1e-6

ROW_TILE = 256
LANES = 128
HALO = 8
CHUNK = 64
PAIR = 2 * WKV_N
N_PAIRS = WKV_H // 2
PEER_TM = 512
PEER_TE = 1024
TOPK_TL = 256
VMEM_LIMIT = 56 * 1024 * 1024


def _cparams(sem):
    return pltpu.CompilerParams(dimension_semantics=sem, vmem_limit_bytes=VMEM_LIMIT)


def _dot(a, b):
    return jnp.dot(a, b, preferred_element_type=F32)


def _dot_nt(a, b):
    return lax.dot_general(a, b, (((1,), (1,)), ((), ())), preferred_element_type=F32)


def _mm3(a, b):
    ah = a.astype(BF16)
    al = (a - ah.astype(F32)).astype(BF16)
    bh = b.astype(BF16)
    bl = (b - bh.astype(F32)).astype(BF16)
    return _dot(ah, bh) + (_dot(ah, bl) + _dot(al, bh))


def _mm01(a, b01):
    h = a.astype(BF16)
    r = a - h.astype(F32)
    m = r.astype(BF16)
    lo = (r - m.astype(F32)).astype(BF16)
    return _dot(h, b01) + (_dot(m, b01) + _dot(lo, b01))


def _ln(x):
    mu = jnp.mean(x, axis=-1, keepdims=True)
    xc = x - mu
    var = jnp.mean(xc * xc, axis=-1, keepdims=True)
    return xc * lax.rsqrt(var + LN_EPS)


def _gelu(x):
    return 0.5 * x * (1.0 + jnp.tanh(0.7978845608028654 * (x + 0.044715 * (x * x * x))))


def _sigmoid(x):
    return 1.0 / (1.0 + jnp.exp(-x))


def _mod_kernel(c_ref, w_ref, b_ref, o_ref):
    c = c_ref[...]
    sc = c * _sigmoid(c)
    o_ref[0] = _dot(sc.astype(BF16), w_ref[0].astype(BF16)) + b_ref[0]


def _modulation(cv8, w_mod, b_mod):
    depth, d, n = w_mod.shape
    tn = 1536
    return pl.pallas_call(
        _mod_kernel,
        grid=(depth, n // tn),
        in_specs=[pl.BlockSpec((8, d), lambda l, j: (0, 0)),
                  pl.BlockSpec((1, d, tn), lambda l, j: (l, 0, j)),
                  pl.BlockSpec((1, 1, tn), lambda l, j: (l, 0, j))],
        out_specs=pl.BlockSpec((1, 8, tn), lambda l, j: (l, 0, j)),
        out_shape=jax.ShapeDtypeStruct((depth, 8, n), F32),
        compiler_params=_cparams(("arbitrary", "arbitrary")),
        name="modulation",
    )(cv8, w_mod, b_mod.reshape(depth, 1, n))


def _addpos_kernel(x_ref, p_ref, o_ref):
    o_ref[0] = x_ref[0] + p_ref[...]


def _add_pos(x, pos):
    b, s, d = x.shape
    ts = 512
    return pl.pallas_call(
        _addpos_kernel,
        grid=(s // ts, b),
        in_specs=[pl.BlockSpec((1, ts, d), lambda i, j: (j, i, 0)),
                  pl.BlockSpec((ts, d), lambda i, j: (i, 0))],
        out_specs=pl.BlockSpec((1, ts, d), lambda i, j: (j, i, 0)),
        out_shape=jax.ShapeDtypeStruct(x.shape, x.dtype),
        compiler_params=_cparams(("arbitrary", "arbitrary")),
        name="add_pos",
    )(x, pos)


def _inproj_even_kernel(tab_ref, x_ref, m_ref, w_ref, wc_ref, zr_ref, pq_ref):
    m = m_ref[0]
    h = _ln(x_ref[...]) * (1.0 + m[1:2]) + m[0:1]
    z = _dot(h.astype(BF16), w_ref[...])
    zr_ref[...] = z[:, FNET_W:]
    pq_ref[...] = _dot(z[:, :FNET_W].astype(BF16), wc_ref[...]).astype(BF16)


def _inproj_odd_kernel(tab_ref, x_ref, m_ref, w_ref, zl_ref, zw_ref):
    m = m_ref[0]
    h = _ln(x_ref[...]) * (1.0 + m[1:2]) + m[0:1]
    z = _dot(h.astype(BF16), w_ref[...])
    zl_ref[...] = z[:, :2 * LRU_W]
    zw_ref[...] = z[:, 2 * LRU_W:]


def _const_spec(shape):
    nd = len(shape)
    return pl.BlockSpec(shape, lambda i, t, _n=nd: (0,) * _n)


def _inproj(x, tab, mods, w_in, wc=None):
    t, d = x.shape
    n = w_in.shape[1]
    tm = ROW_TILE
    even = wc is not None
    in_specs = [pl.BlockSpec((tm, d), lambda i, t_: (i, 0)),
                pl.BlockSpec((1, 6, d), lambda i, t_: (t_[0, i], 0, 0)),
                _const_spec((d, n))]
    if even:
        in_specs.append(_const_spec(wc.shape))
        widths = (n - FNET_W, 2 * FNET_W)
        dtypes = (F32, BF16)
        kern = _inproj_even_kernel
        args = (tab, x, mods, w_in, wc)
    else:
        widths = (2 * LRU_W, n - 2 * LRU_W)
        dtypes = (F32, F32)
        kern = _inproj_odd_kernel
        args = (tab, x, mods, w_in)
    return pl.pallas_call(
        kern,
        grid_spec=pltpu.PrefetchScalarGridSpec(
            num_scalar_prefetch=1, grid=(t // tm,), in_specs=in_specs,
            out_specs=[pl.BlockSpec((tm, w), lambda i, t_: (i, 0)) for w in widths]),
        out_shape=[jax.ShapeDtypeStruct((t, w), dt) for w, dt in zip(widths, dtypes)],
        compiler_params=_cparams(("arbitrary",)),
        name="inproj_even" if even else "inproj_odd",
    )(*args)


def _seqdft_kernel(c_ref, s_ref, p_ref, q_ref, y_ref):
    y_ref[...] = (_dot(c_ref[...], p_ref[...]) + _dot(s_ref[...], q_ref[...])).astype(BF16)


def _seqdft(pq, y_rows, row0, n_seq, s_len, cmat, smat):
    tm = min(s_len, ROW_TILE)
    nt = s_len // tm
    blk0 = row0 // s_len
    return pl.pallas_call(
        _seqdft_kernel,
        grid=(n_seq, nt),
        in_specs=[pl.BlockSpec((tm, s_len), lambda b, i: (i, 0)),
                  pl.BlockSpec((tm, s_len), lambda b, i: (i, 0)),
                  pl.BlockSpec((s_len, FNET_W), lambda b, i: (blk0 + b, 0)),
                  pl.BlockSpec((s_len, FNET_W), lambda b, i: (blk0 + b, 1))],
        out_specs=pl.BlockSpec((tm, FNET_W), lambda b, i: (b * nt + i, 0)),
        out_shape=jax.ShapeDtypeStruct((y_rows, FNET_W), BF16),
        compiler_params=_cparams(("arbitrary", "arbitrary")),
        name="seqdft_%d" % s_len,
    )(cmat, smat, pq, pq)


def _post_and_query(x, mix, m, g1, b1, wq_ref, keys_ref, x1_ref, h2t_ref, st_ref):
    x1 = _ln(ALPHA * x + m[2:3] * mix) * g1 + b1
    x1_ref[...] = x1
    h2 = _ln(x1) * (1.0 + m[4:5]) + m[3:4]
    h2b = h2.astype(BF16)
    h2t_ref[...] = h2.T.astype(BF16)
    q = _dot(h2b, wq_ref[...])
    for hp in range(2 * PEER_HEADS):
        qb = q[:, hp * PEER_HALF:(hp + 1) * PEER_HALF].astype(BF16)
        st_ref[hp * N_KEYS:(hp + 1) * N_KEYS, :] = _dot_nt(keys_ref[hp], qb)


def _post_even_kernel(tab_ref, x_ref, bg_ref, cg_ref, xi_ref, cgp_ref, xip_ref, cgn_ref, xin_ref, y_ref,
                      m_ref, cw_ref, cb_ref, wo_ref, g1_ref, b1_ref, wq_ref, keys_ref,
                      x1_ref, h2t_ref, st_ref, ext_ref):
    i = pl.program_id(0)
    tm = ROW_TILE
    first = tab_ref[1, i] == 1
    last = tab_ref[2, i] == 1
    u = cg_ref[...] * xi_ref[...]
    ext_ref[0:HALO] = jnp.where(first, 0.0, cgp_ref[...] * xip_ref[...])
    ext_ref[HALO:HALO + tm] = u
    ext_ref[HALO + tm:2 * HALO + tm] = jnp.where(last, 0.0, cgn_ref[...] * xin_ref[...])
    cw = cw_ref[...]
    conv = (ext_ref[HALO - 1:HALO - 1 + tm] * cw[0:1] + u * cw[1:2]
            + ext_ref[HALO + 1:HALO + 1 + tm] * cw[2:3] + cb_ref[...])
    ymix = (bg_ref[...] * conv).astype(BF16)
    mix = _dot(y_ref[...], wo_ref[0:FNET_W]) + _dot(ymix, wo_ref[FNET_W:])
    _post_and_query(x_ref[...], mix, m_ref[0], g1_ref[...], b1_ref[...], wq_ref, keys_ref,
                    x1_ref, h2t_ref, st_ref)


def _post_out_specs(t, d, tm):
    nsc = 2 * PEER_HEADS * N_KEYS
    specs = [pl.BlockSpec((tm, d), lambda i, t_: (i, 0)),
             pl.BlockSpec((d, tm), lambda i, t_: (0, i)),
             pl.BlockSpec((nsc, tm), lambda i, t_: (0, i))]
    shapes = [jax.ShapeDtypeStruct((t, d), F32),
              jax.ShapeDtypeStruct((d, t), BF16),
              jax.ShapeDtypeStruct((nsc, t), F32)]
    return specs, shapes


def _post_even(x, tab, zr, y, mods, cw, cb, wo, g1, b1, wq, keys):
    t, d = x.shape
    tm = ROW_TILE
    hb = tm // HALO
    nblk = t // HALO
    prev = lambda c: pl.BlockSpec((HALO, CONV_W), lambda i, t_, _c=c: (jnp.maximum(i * hb - 1, 0), _c))
    nxt = lambda c: pl.BlockSpec((HALO, CONV_W), lambda i, t_, _c=c: (jnp.minimum((i + 1) * hb, nblk - 1), _c))
    col = lambda c: pl.BlockSpec((tm, CONV_W), lambda i, t_, _c=c: (i, _c))
    in_specs = [pl.BlockSpec((tm, d), lambda i, t_: (i, 0)),
                col(0), col(1), col(2), prev(1), prev(2), nxt(1), nxt(2),
                pl.BlockSpec((tm, FNET_W), lambda i, t_: (i, 0)),
                pl.BlockSpec((1, 6, d), lambda i, t_: (t_[0, i], 0, 0)),
                _const_spec(cw.shape), _const_spec(cb.shape), _const_spec(wo.shape),
                _const_spec(g1.shape), _const_spec(b1.shape), _const_spec(wq.shape), _const_spec(keys.shape)]
    out_specs, out_shape = _post_out_specs(t, d, tm)
    return pl.pallas_call(
        _post_even_kernel,
        grid_spec=pltpu.PrefetchScalarGridSpec(
            num_scalar_prefetch=1, grid=(t // tm,), in_specs=in_specs, out_specs=out_specs,
            scratch_shapes=[pltpu.VMEM((tm + 2 * HALO, CONV_W), F32)]),
        out_shape=out_shape,
        compiler_params=_cparams(("arbitrary",)),
        name="post_even",
    )(tab, x, zr, zr, zr, zr, zr, zr, zr, y, mods, cw, cb, wo, g1, b1, wq, keys)


def _lru_kernel(xf_ref, xfh_ref, xb_ref, xbh_ref, cw_ref, cb_ref, wa_ref, ba_ref, wx_ref, bx_ref, lam_ref,
                h0_ref, hf_ref, hb_ref, hfin_ref, ext_ref, carry_ref):
    i = pl.program_id(1)
    nt = pl.num_programs(1)
    tm = ROW_TILE

    @pl.when(i == 0)
    def _():
        carry_ref[...] = h0_ref[0]

    row = lax.broadcasted_iota(jnp.int32, (tm, 1), 0)
    for d in range(2):
        x = (xf_ref if d == 0 else xb_ref)[...]
        halo = jnp.where(i == 0, 0.0, (xfh_ref if d == 0 else xbh_ref)[...])
        ext_ref[HALO:HALO + tm] = x
        if d == 0:
            ext_ref[0:HALO] = halo
        else:
            ext_ref[HALO + tm:2 * HALO + tm] = halo
        cw = cw_ref[d]
        xc = cb_ref[d:d + 1] + x * cw[LRU_CONV_K - 1:LRU_CONV_K]
        for j in range(LRU_CONV_K - 1):
            k = LRU_CONV_K - 1 - j
            off = HALO - k if d == 0 else HALO + k
            xc = xc + ext_ref[off:off + tm] * cw[j:j + 1]
        xcb = xc.astype(BF16)
        gate_r = _sigmoid(_dot(xcb, wa_ref[d]) + ba_ref[d:d + 1])
        gate_i = _sigmoid(_dot(xcb, wx_ref[d]) + bx_ref[d:d + 1])
        nl = -lam_ref[d:d + 1]
        softplus = jnp.maximum(nl, 0.0) + jnp.log1p(jnp.exp(-jnp.abs(nl)))
        log_a = -LRU_C * gate_r * softplus
        a = jnp.exp(log_a)
        b = jnp.sqrt(-jnp.tanh(log_a) * (a * a + 1.0)) * (gate_i * xc)
        s = 1
        while s < tm:
            if d == 0:
                keep = row >= s
                sh = s
            else:
                keep = row < tm - s
                sh = tm - s
            a_sh = jnp.where(keep, pltpu.roll(a, sh, 0), 1.0)
            b_sh = jnp.where(keep, pltpu.roll(b, sh, 0), 0.0)
            b = a * b_sh + b
            a = a * a_sh
            s *= 2
        h = a * carry_ref[d:d + 1] + b
        if d == 0:
            hf_ref[...] = h
            carry_ref[0:1] = h[tm - 1:tm]
        else:
            hb_ref[...] = h
            carry_ref[1:2] = h[0:1]

    @pl.when(i == nt - 1)
    def _():
        hfin_ref[0] = carry_ref[...]


def _lru(zl, row0, n_seq, s_len, h0, cw, cb, wa, ba, wx, bx, lam):
    tm = ROW_TILE
    nt = s_len // tm
    t0 = row0 // tm
    hb = tm // HALO
    nblk = zl.shape[0] // HALO
    rows = n_seq * s_len
    fwd = lambda b, i: (t0 + b * nt + i, 0)
    bwd = lambda b, i: (t0 + b * nt + nt - 1 - i, 0)
    fwd_h = lambda b, i: (jnp.maximum((t0 + b * nt + i) * hb - 1, 0), 0)
    bwd_h = lambda b, i: (jnp.minimum((t0 + b * nt + nt - i) * hb, nblk - 1), 0)
    out_f = lambda b, i: (b * nt + i, 0)
    out_b = lambda b, i: (b * nt + nt - 1 - i, 0)
    cst = lambda a: pl.BlockSpec(a.shape, lambda b, i, _n=a.ndim: (0,) * _n)
    return pl.pallas_call(
        _lru_kernel,
        grid=(n_seq, nt),
        in_specs=[pl.BlockSpec((tm, LRU_W), fwd), pl.BlockSpec((HALO, LRU_W), fwd_h),
                  pl.BlockSpec((tm, LRU_W), bwd), pl.BlockSpec((HALO, LRU_W), bwd_h),
                  cst(cw), cst(cb), cst(wa), cst(ba), cst(wx), cst(bx), cst(lam),
                  pl.BlockSpec((1, 2, LRU_W), lambda b, i: (b, 0, 0))],
        out_specs=[pl.BlockSpec((tm, LRU_W), out_f), pl.BlockSpec((tm, LRU_W), out_b),
                   pl.BlockSpec((1, 2, LRU_W), lambda b, i: (b, 0, 0))],
        out_shape=[jax.ShapeDtypeStruct((rows, LRU_W), F32), jax.ShapeDtypeStruct((rows, LRU_W), F32),
                   jax.ShapeDtypeStruct((n_seq, 2, LRU_W), F32)],
        scratch_shapes=[pltpu.VMEM((tm + 2 * HALO, LRU_W), F32), pltpu.VMEM((2, LRU_W), F32)],
        compiler_params=_cparams(("arbitrary", "arbitrary")),
        name="lru_%d" % s_len,
    )(zl, zl, zl, zl, cw, cb, wa, ba, wx, bx, lam, h0)


def _wkvprep_kernel(tab_ref, z_ref, zp_ref, zn_ref, mu_ref, w0_ref, w2_ref, a0_ref, a2_ref, kk_ref, ka_ref,
                    rk_ref, g2_ref, seg_ref, tri_ref,
                    v_ref, bg_ref, g_ref, rf_ref, kf_ref, bf_ref, qf_ref, cf_ref,
                    rb_ref, kb_ref, bb_ref, qb_ref, cb_ref, ext_ref):
    i = pl.program_id(0)
    tm = ROW_TILE
    w = WKV_W
    first = tab_ref[1, i] == 1
    last = tab_ref[2, i] == 1
    z = z_ref[...]
    ext_ref[0:HALO] = jnp.where(first, 0.0, zp_ref[...])
    ext_ref[HALO:HALO + tm] = z
    ext_ref[HALO + tm:2 * HALO + tm] = jnp.where(last, 0.0, zn_ref[...])
    z = z + mu_ref[...] * (0.5 * (ext_ref[HALO - 1:HALO - 1 + tm] + ext_ref[HALO + 1:HALO + 1 + tm]) - z)
    r = z[:, 0:w]
    k = z[:, w:2 * w]
    v = z[:, 2 * w:3 * w]
    wd = jnp.tanh(z[:, 3 * w:3 * w + 128]).astype(BF16)
    ad = z[:, 3 * w + 128:3 * w + 256].astype(BF16)
    gd = _sigmoid(z[:, 3 * w + 256:3 * w + 384]).astype(BF16)
    v_ref[...] = v
    g_ref[...] = _dot(gd, g2_ref[...])
    seg = seg_ref[...]
    rk = rk_ref[...]
    bonus = jnp.zeros((tm, w), F32)
    outs = ((rf_ref, kf_ref, bf_ref, qf_ref, cf_ref), (rb_ref, kb_ref, bb_ref, qb_ref, cb_ref))
    nch = tm // CHUNK
    for d in range(2):
        r_ref, k_ref, b_ref, q_ref, c_ref = outs[d]
        wz = w0_ref[d:d + 1] + _dot(wd, w2_ref[d])
        logw = -DECAY_SCALE * _sigmoid(wz)
        iclr = _sigmoid(a0_ref[d:d + 1] + _dot(ad, a2_ref[d]))
        kk = k * kk_ref[d:d + 1]
        kk = kk * lax.rsqrt(jnp.maximum(_mm01(kk * kk, seg), 1e-24))
        km = k * (1.0 + (iclr - 1.0) * ka_ref[d:d + 1])
        bonus = bonus + _mm01(r * km * rk, seg) * v
        lw_h = logw.astype(BF16)
        lw_r = logw - lw_h.astype(F32)
        lw_m = lw_r.astype(BF16)
        lw_l = (lw_r - lw_m.astype(F32)).astype(BF16)
        tri = tri_ref[d]
        cl = _dot(tri, lw_h) + (_dot(tri, lw_m) + _dot(tri, lw_l))
        c = jnp.exp(cl)
        cinv = jnp.exp(-cl)
        r_ref[...] = r * c
        k_ref[...] = km * cinv
        b_ref[...] = kk * iclr * cinv
        q_ref[...] = kk * jnp.exp(cl - logw)
        for j in range(nch):
            edge = (j + 1) * CHUNK - 1 if d == 0 else j * CHUNK
            c_ref[j] = c[edge:edge + 1]
    bg_ref[...] = bonus


def _wkvprep(zw, tab, mu, w0, w2p, a0, a2p, kk, ka, rk, g2, seg, tri):
    t, n = zw.shape
    tm = ROW_TILE
    hb = tm // HALO
    nblk = t // HALO
    nch = tm // CHUNK
    w = WKV_W
    in_specs = [pl.BlockSpec((tm, n), lambda i, t_: (i, 0)),
                pl.BlockSpec((HALO, n), lambda i, t_: (jnp.maximum(i * hb - 1, 0), 0)),
                pl.BlockSpec((HALO, n), lambda i, t_: (jnp.minimum((i + 1) * hb, nblk - 1), 0))]
    in_specs += [_const_spec(a.shape) for a in (mu, w0, w2p, a0, a2p, kk, ka, rk, g2, seg, tri)]
    row = pl.BlockSpec((tm, w), lambda i, t_: (i, 0))
    cspec = pl.BlockSpec((nch, 1, w), lambda i, t_: (i, 0, 0))
    rshape = jax.ShapeDtypeStruct((t, w), F32)
    cshape = jax.ShapeDtypeStruct((t // CHUNK, 1, w), F32)
    out_specs = [row, row, row] + [row, row, row, row, cspec] * 2
    out_shape = [rshape, rshape, rshape] + [rshape, rshape, rshape, rshape, cshape] * 2
    return pl.pallas_call(
        _wkvprep_kernel,
        grid_spec=pltpu.PrefetchScalarGridSpec(
            num_scalar_prefetch=1, grid=(t // tm,), in_specs=in_specs, out_specs=out_specs,
            scratch_shapes=[pltpu.VMEM((tm + 2 * HALO, n), F32)]),
        out_shape=out_shape,
        compiler_params=_cparams(("arbitrary",)),
        name="wkv_prep",
    )(tab, zw, zw, zw, mu, w0, w2p, a0, a2p, kk, ka, rk, g2, seg, tri)


def _wkv_chunk(rh, kh, bh, kq, v, c_last, m_state, strict, incl, eye):
    lane = lax.broadcasted_iota(jnp.int32, (1, PAIR), 1)
    m0 = (lane < WKV_N).astype(F32)
    m1 = 1.0 - m0
    stack = lambda x: jnp.concatenate([x * m0, x * m1], axis=0)
    rh_s, kh_s, bh_s, kq_s, v_s = stack(rh), stack(kh), stack(bh), stack(kq), stack(v)
    bh_t = bh_s.T
    kh_t = kh_s.T
    att = _mm3(jnp.concatenate([kq_s, rh_s], axis=0), jnp.concatenate([bh_t, kh_t], axis=1))
    n1 = att[0:PAIR, 0:PAIR] * strict
    ak = att[0:PAIR, PAIR:] * strict
    gb = att[PAIR:, 0:PAIR] * incl
    gk = att[PAIR:, PAIR:] * incl
    t_inv = eye - n1
    npow = n1
    for _ in range(5):
        npow = _mm3(npow, npow)
        t_inv = t_inv + _mm3(t_inv, npow)
    kv = _mm3(jnp.concatenate([kh_t, gk, ak], axis=0), v_s)
    ktv, gkv, akv = kv[0:PAIR], kv[PAIR:2 * PAIR], kv[2 * PAIR:]
    x = _mm3(t_inv, jnp.concatenate([kq_s, akv], axis=1))
    bx = _mm3(jnp.concatenate([bh_t, gb], axis=0), x)
    p_mat = eye - bx[0:PAIR, 0:PAIR]
    q_mat = ktv - bx[0:PAIR, PAIR:]
    r_til = rh_s - bx[PAIR:, 0:PAIR]
    y0 = gkv - bx[PAIR:, PAIR:]
    sm = _mm3(jnp.concatenate([r_til, p_mat], axis=0), m_state)
    y_st = sm[0:PAIR] + y0
    y = y_st[0:CHUNK] + y_st[CHUNK:]
    c_col = jnp.broadcast_to(c_last, (PAIR, PAIR)).T
    m_new = (sm[PAIR:] + q_mat) * c_col
    return y, m_new


def _wkvscan_kernel(rf_ref, kf_ref, bf_ref, qf_ref, vf_ref, cf_ref, rb_ref, kb_ref, bb_ref, qb_ref, vb_ref, cb_ref,
                    m0_ref, msk_ref, yf_ref, yb_ref, mfin_ref, m_ref):
    i = pl.program_id(1)
    nc = pl.num_programs(1)

    @pl.when(i == 0)
    def _():
        m_ref[...] = m0_ref[0]

    eye = msk_ref[4]
    ins = ((rf_ref, kf_ref, bf_ref, qf_ref, vf_ref, cf_ref, yf_ref),
           (rb_ref, kb_ref, bb_ref, qb_ref, vb_ref, cb_ref, yb_ref))
    for d in range(2):
        r_ref, k_ref, b_ref, q_ref, v_ref, c_ref, y_ref = ins[d]
        strict = msk_ref[2 * d]
        incl = msk_ref[2 * d + 1]
        for p in range(N_PAIRS):
            sl = slice(p * PAIR, (p + 1) * PAIR)
            y, m_new = _wkv_chunk(r_ref[:, sl], k_ref[:, sl], b_ref[:, sl], q_ref[:, sl], v_ref[:, sl],
                                  c_ref[0, :, sl], m_ref[d, p], strict, incl, eye)
            y_ref[:, sl] = y
            m_ref[d, p] = m_new

    @pl.when(i == nc - 1)
    def _():
        mfin_ref[0] = m_ref[...]


def _wkvscan(prep, v, row0, n_seq, s_len, m0, masks):
    rf, kf, bf, qf, cf, rb, kb, bb, qb, cb = prep
    nc = s_len // CHUNK
    c0 = row0 // CHUNK
    rows = n_seq * s_len
    w = WKV_W
    fwd = lambda b, i: (c0 + b * nc + i, 0)
    bwd = lambda b, i: (c0 + b * nc + nc - 1 - i, 0)
    fwd3 = lambda b, i: (c0 + b * nc + i, 0, 0)
    bwd3 = lambda b, i: (c0 + b * nc + nc - 1 - i, 0, 0)
    blk = lambda im: pl.BlockSpec((CHUNK, w), im)
    cblk = lambda im: pl.BlockSpec((1, 1, w), im)
    mspec = pl.BlockSpec((1, 2, N_PAIRS, PAIR, PAIR), lambda b, i: (b, 0, 0, 0, 0))
    return pl.pallas_call(
        _wkvscan_kernel,
        grid=(n_seq, nc),
        in_specs=[blk(fwd)] * 5 + [cblk(fwd3)] + [blk(bwd)] * 5 + [cblk(bwd3)]
        + [mspec, pl.BlockSpec(masks.shape, lambda b, i: (0, 0, 0))],
        out_specs=[pl.BlockSpec((CHUNK, w), lambda b, i: (b * nc + i, 0)),
                   pl.BlockSpec((CHUNK, w), lambda b, i: (b * nc + nc - 1 - i, 0)),
                   mspec],
        out_shape=[jax.ShapeDtypeStruct((rows, w), F32), jax.ShapeDtypeStruct((rows, w), F32),
                   jax.ShapeDtypeStruct(m0.shape, F32)],
        scratch_shapes=[pltpu.VMEM((2, N_PAIRS, PAIR, PAIR), F32)],
        compiler_params=_cparams(("arbitrary", "arbitrary")),
        name="wkv_scan_%d" % s_len,
    )(rf, kf, bf, qf, v, cf, rb, kb, bb, qb, v, cb, m0, masks)


def _post_odd_kernel(tab_ref, x_ref, hf_ref, hb_ref, gb_ref, yf_ref, yb_ref, bon_ref, g_ref,
                     m_ref, seg_ref, gng_ref, gnb_ref, wo_ref, g1_ref, b1_ref, wq_ref, keys_ref,
                     x1_ref, h2t_ref, st_ref):
    y_lru = ((hf_ref[...] + hb_ref[...]) * _gelu(gb_ref[...])).astype(BF16)
    seg = seg_ref[...]
    ys = yf_ref[...] + yb_ref[...]
    mean = _mm01(ys, seg) * (1.0 / WKV_N)
    yc = ys - mean
    var = _mm01(yc * yc, seg) * (1.0 / WKV_N)
    yn = yc * lax.rsqrt(var + WKV_GN_EPS) * gng_ref[...] + gnb_ref[...]
    y_wkv = ((yn + bon_ref[...]) * g_ref[...]).astype(BF16)
    mix = _dot(y_lru, wo_ref[0:LRU_W]) + _dot(y_wkv, wo_ref[LRU_W:])
    _post_and_query(x_ref[...], mix, m_ref[0], g1_ref[...], b1_ref[...], wq_ref, keys_ref,
                    x1_ref, h2t_ref, st_ref)


def _post_odd(x, tab, hf, hb, zl, yf, yb, bonus, g, mods, seg, gng, gnb, wo, g1, b1, wq, keys):
    t, d = x.shape
    tm = ROW_TILE
    w = WKV_W
    row = pl.BlockSpec((tm, w), lambda i, t_: (i, 0))
    in_specs = [pl.BlockSpec((tm, d), lambda i, t_: (i, 0)), row, row,
                pl.BlockSpec((tm, LRU_W), lambda i, t_: (i, 1)), row, row, row, row,
                pl.BlockSpec((1, 6, d), lambda i, t_: (t_[0, i], 0, 0))]
    in_specs += [_const_spec(a.shape) for a in (seg, gng, gnb, wo, g1, b1, wq, keys)]
    out_specs, out_shape = _post_out_specs(t, d, tm)
    return pl.pallas_call(
        _post_odd_kernel,
        grid_spec=pltpu.PrefetchScalarGridSpec(
            num_scalar_prefetch=1, grid=(t // tm,), in_specs=in_specs, out_specs=out_specs),
        out_shape=out_shape,
        compiler_params=_cparams(("arbitrary",)),
        name="post_odd",
    )(tab, x, hf, hb, zl, yf, yb, bonus, g, mods, seg, gng, gnb, wo, g1, b1, wq, keys)


N_TOP = PEER_TOPK + 1
TOP_ROWS = 24


def _topk_kernel(s_ref, th1_ref, p1_ref, p2_ref, s2_ref, v1_ref, v2_ref):
    neg = -jnp.inf

    def extract(x, n, store):
        vals = []
        for it in range(n):
            m = jnp.max(x, axis=0, keepdims=True)
            vals.append(m)
            if store is not None:
                store[it:it + 1, :] = m
            x = jnp.where(x == m, neg, x)
        return vals

    def head(h, carry):
        s1 = s_ref[h, 0]
        s2 = s_ref[h, 1]
        v1_ref[...] = jnp.full(v1_ref.shape, neg, F32)
        v2_ref[...] = jnp.full(v2_ref.shape, neg, F32)
        extract(s1, N_TOP, v1_ref)
        extract(s2, N_TOP, v2_ref)
        pieces = [v1_ref[0:1, :] + v2_ref[0:TOP_ROWS, :]]
        for a in range(1, N_TOP):
            pieces.append(v1_ref[a:a + 1, :] + v2_ref[0:8, :])
        cand = jnp.concatenate(pieces, axis=0)
        top = extract(cand, N_TOP, None)
        tau = 0.5 * (top[PEER_TOPK - 1] + top[PEER_TOPK])
        mx1 = v1_ref[0:1, :]
        mx2 = v2_ref[0:1, :]
        zsum = jnp.sum(jnp.where(cand >= tau, jnp.exp(cand - (mx1 + mx2)), 0.0), axis=0, keepdims=True)
        th1 = tau - s1
        p1 = jnp.exp(s1 - mx1) / zsum
        p2 = jnp.exp(s2 - mx2)
        for cb in range(TOPK_TL // LANES):
            cs = slice(cb * LANES, (cb + 1) * LANES)
            th1_ref[cb, h] = th1[:, cs]
            p1_ref[cb, h] = p1[:, cs]
            p2_ref[cb, h] = p2[:, cs]
            s2_ref[cb, h] = s2[:, cs]
        return carry

    lax.fori_loop(0, PEER_HEADS, head, 0)


def _topk(st4):
    t = st4.shape[-1]
    tl = TOPK_TL
    out = jax.ShapeDtypeStruct((t // LANES, PEER_HEADS, N_KEYS, LANES), F32)
    ospec = pl.BlockSpec((tl // LANES, PEER_HEADS, N_KEYS, LANES), lambda i: (i, 0, 0, 0))
    return pl.pallas_call(
        _topk_kernel,
        grid=(t // tl,),
        in_specs=[pl.BlockSpec((PEER_HEADS, 2, N_KEYS, tl), lambda i: (0, 0, 0, i))],
        out_specs=[ospec, ospec, ospec, ospec],
        out_shape=[out, out, out, out],
        scratch_shapes=[pltpu.VMEM((TOP_ROWS, tl), F32), pltpu.VMEM((TOP_ROWS, tl), F32)],
        compiler_params=_cparams(("arbitrary",)),
        name="peer_topk",
    )(st4)


def _peer_kernel(tab_ref, h2t_ref, s2_ref, p2_ref, th1_ref, p1_ref, u_ref, vt_ref, x1_ref, m_ref, g2_ref, b2_ref,
                 o_ref, acc_ref, act_ref, ga_ref):
    j = pl.program_id(1)
    nj = pl.num_programs(1)
    n_blk = PEER_TM // LANES
    rows_per_step = PEER_TE // N_KEYS
    r0 = pl.multiple_of(j * rows_per_step, rows_per_step)

    @pl.when(j == 0)
    def _():
        acc_ref[...] = jnp.zeros(acc_ref.shape, F32)

    act = _dot(u_ref[...], h2t_ref[...])
    for cb in range(n_blk):
        act_ref[cb] = act[:, cb * LANES:(cb + 1) * LANES]

    def blk_body(cb, carry):
        th = [th1_ref[cb, h, pl.ds(r0, rows_per_step), :] for h in range(PEER_HEADS)]
        p1 = [p1_ref[cb, h, pl.ds(r0, rows_per_step), :] for h in range(PEER_HEADS)]
        for r in range(rows_per_step):
            gate = jnp.zeros((N_KEYS, LANES), F32)
            for h in range(PEER_HEADS):
                sel = jnp.where(s2_ref[cb, h] >= th[h][r:r + 1], p2_ref[cb, h], 0.0)
                gate = gate + sel * p1[h][r:r + 1]
            rs = slice(r * N_KEYS, (r + 1) * N_KEYS)
            ga_ref[cb, rs, :] = (gate * _gelu(act_ref[cb, rs, :])).astype(BF16)
        return carry

    lax.fori_loop(0, n_blk, blk_body, 0)
    ga = jnp.concatenate([ga_ref[cb] for cb in range(n_blk)], axis=1)
    acc_ref[...] += _dot(vt_ref[...], ga)

    @pl.when(j == nj - 1)
    def _():
        m = m_ref[0]
        ffn = acc_ref[...].T
        o_ref[...] = _ln(ALPHA * x1_ref[...] + m[5:6] * ffn) * g2_ref[...] + b2_ref[...]


def _peer(x1, tab, h2t, s2, p2, th1, p1, u_bf, vt_bf, mods, g2, b2):
    t, d = x1.shape
    tm = PEER_TM
    te = PEER_TE
    n_exp = u_bf.shape[0]
    per = tm // ROW_TILE
    n_blk = tm // LANES
    hk = pl.BlockSpec((n_blk, PEER_HEADS, N_KEYS, LANES), lambda i, j, t_: (i, 0, 0, 0))
    in_specs = [pl.BlockSpec((d, tm), lambda i, j, t_: (0, i)),
                hk, hk, hk, hk,
                pl.BlockSpec((te, d), lambda i, j, t_: (j, 0)),
                pl.BlockSpec((d, te), lambda i, j, t_: (0, j)),
                pl.BlockSpec((tm, d), lambda i, j, t_: (i, 0)),
                pl.BlockSpec((1, 6, d), lambda i, j, t_: (t_[0, i * per], 0, 0)),
                pl.BlockSpec(g2.shape, lambda i, j, t_: (0, 0)),
                pl.BlockSpec(b2.shape, lambda i, j, t_: (0, 0))]
    return pl.pallas_call(
        _peer_kernel,
        grid_spec=pltpu.PrefetchScalarGridSpec(
            num_scalar_prefetch=1, grid=(t // tm, n_exp // te), in_specs=in_specs,
            out_specs=pl.BlockSpec((tm, d), lambda i, j, t_: (i, 0)),
            scratch_shapes=[pltpu.VMEM((d, tm), F32), pltpu.VMEM((n_blk, te, LANES), F32),
                            pltpu.VMEM((n_blk, te, LANES), BF16)]),
        out_shape=jax.ShapeDtypeStruct((t, d), F32),
        compiler_params=_cparams(("arbitrary", "arbitrary")),
        name="peer_mix",
    )(tab, h2t, s2, p2, th1, p1, u_bf, vt_bf, x1, mods, g2, b2)


def _sincos(pos, dim):
    omega = 1.0 / (10000.0 ** (jnp.arange(dim // 2, dtype=F32) / (dim // 2)))
    ang = pos.astype(F32)[:, None] * omega[None, :]
    return jnp.concatenate([jnp.sin(ang), jnp.cos(ang)], -1)


def _grid_pos_embed(n_tok):
    rows = n_tok // GRID_W
    half = D_MODEL // 2
    er = _sincos(jnp.arange(rows), half)
    ec = _sincos(jnp.arange(GRID_W), half)
    emb = jnp.concatenate([jnp.broadcast_to(er[:, None, :], (rows, GRID_W, half)),
                           jnp.broadcast_to(ec[None, :, :], (rows, GRID_W, half))], -1)
    return emb.reshape(rows * GRID_W, D_MODEL)


def _seq_dft_tables(s_len):
    n = jnp.arange(s_len, dtype=jnp.int32)
    ang = ((n[:, None] * n[None, :]) % s_len).astype(F32) * (2.0 * math.pi / s_len)
    scale = 1.0 / math.sqrt(s_len * FNET_GW)
    return (jnp.cos(ang) * scale).astype(BF16), (-jnp.sin(ang) * scale).astype(BF16)


def _channel_dft_table():
    n = np.arange(FNET_GW)
    ang = ((n[:, None] * n[None, :]) % FNET_GW) * (2.0 * np.pi / FNET_GW)
    wc = np.zeros((FNET_W, 2 * FNET_W), np.float32)
    for g in range(FNET_GROUPS):
        sl = slice(g * FNET_GW, (g + 1) * FNET_GW)
        wc[sl, sl] = np.cos(ang)
        wc[sl, FNET_W + g * FNET_GW:FNET_W + (g + 1) * FNET_GW] = np.sin(ang)
    return jnp.asarray(wc, BF16)


def _segment_ones():
    idx = np.arange(WKV_W) // WKV_N
    return jnp.asarray((idx[:, None] == idx[None, :]).astype(np.float32), BF16)


def _chunk_tri():
    i = np.arange(ROW_TILE)
    same = (i[:, None] // CHUNK) == (i[None, :] // CHUNK)
    lower = same & (i[None, :] <= i[:, None])
    upper = same & (i[None, :] >= i[:, None])
    return jnp.asarray(np.stack([lower, upper]).astype(np.float32), BF16)


def _scan_masks():
    i = np.arange(PAIR)
    same = (i[:, None] // CHUNK) == (i[None, :] // CHUNK)
    t_row = i[:, None] % CHUNK
    t_col = i[None, :] % CHUNK
    masks = [same & (t_col < t_row), same & (t_col <= t_row), same & (t_col > t_row), same & (t_col >= t_row),
             np.eye(PAIR, dtype=bool)]
    return jnp.asarray(np.stack(masks).astype(np.float32))


def _block_diag(w):
    h, n, _ = w.shape
    eye = jnp.eye(h, dtype=w.dtype)
    return jnp.einsum("hij,hg->higj", w, eye).reshape(h * n, h * n)


def _pad_rank(w, d):
    r = w.shape[1]
    z = jnp.zeros_like(w[d])
    return jnp.concatenate([w[d], z] if d == 0 else [z, w[d]], axis=0)


def _state_to_pairs(s0):
    b = s0.shape[0]
    st = jnp.swapaxes(s0, -1, -2).reshape(b, 2, N_PAIRS, 2, WKV_N, WKV_N)
    eye = jnp.eye(2, dtype=s0.dtype)
    return jnp.einsum("bdpeji,ef->bdpejfi", st, eye).reshape(b, 2, N_PAIRS, PAIR, PAIR)


def _pairs_to_state(m):
    b = m.shape[0]
    m7 = m.reshape(b, 2, N_PAIRS, 2, WKV_N, 2, WKV_N)
    st = jnp.stack([m7[:, :, :, 0, :, 0, :], m7[:, :, :, 1, :, 1, :]], axis=3)
    return jnp.swapaxes(st.reshape(b, 2, WKV_H, WKV_N, WKV_N), -1, -2)


def _tile_table(groups):
    cv, first, last = [], [], []
    row = 0
    for gi, (n_seq, s_len) in enumerate(groups):
        nt = s_len // ROW_TILE
        for b in range(n_seq):
            for i in range(nt):
                cv.append(0 if gi == 0 else 1 + b)
                first.append(int(i == 0))
                last.append(int(i == nt - 1))
        row += n_seq * s_len
    return jnp.asarray(np.array([cv, first, last], np.int32))


def kernel(x_prompt, x_sample, state_lru, state_wkv, c, c_ctx, w_mod, b_mod, ln1_g, ln1_b, ln2_g, ln2_b,
           w_in_e, w_out_e, sconv_w, sconv_b, w_in_o, w_out_o, lru_conv_w, lru_conv_b, lru_wa, lru_ba,
           lru_wx, lru_bx, lru_lambda, wkv_mu, wkv_w0, wkv_w2, wkv_a0, wkv_a2, wkv_kk, wkv_ka, wkv_rk,
           wkv_g2, wkv_gn_g, wkv_gn_b, peer_wq, peer_keys, peer_u, peer_v):
    bp, sp, d = x_prompt.shape
    bs, ss, _ = x_sample.shape
    depth = w_mod.shape[0]
    assert sp % ROW_TILE == 0 and ss % PEER_TM == 0 and (bp * sp) % ss == 0
    assert bs + 1 <= 8
    groups = ((bp, sp), (bs, ss))
    tp = bp * sp
    tab = _tile_table(groups)

    cv8 = jnp.concatenate([c_ctx[None, :], c, jnp.zeros((8 - 1 - bs, d), F32)], axis=0)
    mods = _modulation(cv8, w_mod, b_mod).reshape(depth, 8, 6, d)

    xs = _add_pos(x_sample, _grid_pos_embed(ss).astype(x_sample.dtype))
    x = jnp.concatenate([x_prompt.reshape(tp, d), xs.reshape(bs * ss, d)], axis=0)

    wc = _channel_dft_table()
    dft = {s: _seq_dft_tables(s) for s in sorted({sp, ss})}
    seg = _segment_ones()
    tri = _chunk_tri()
    masks = _scan_masks()
    row2 = lambda a: a.reshape(1, -1)

    lru_fin = []
    wkv_fin = []
    for l in range(depth):
        j = l // 2
        m_l = mods[l]
        wq = peer_wq[l].astype(BF16)
        keys = peer_keys[l].reshape(2 * PEER_HEADS, N_KEYS, PEER_HALF).astype(BF16)
        if l % 2 == 0:
            zr, pq = _inproj(x, tab, m_l, w_in_e[j].astype(BF16), wc)
            y = jnp.concatenate([_seqdft(pq, bp * sp, 0, bp, sp, *dft[sp]),
                                 _seqdft(pq, bs * ss, tp, bs, ss, *dft[ss])], axis=0)
            x1, h2t, st = _post_even(x, tab, zr, y, m_l, sconv_w[j], row2(sconv_b[j]), w_out_e[j].astype(BF16),
                                     row2(ln1_g[l]), row2(ln1_b[l]), wq, keys)
        else:
            zl, zw = _inproj(x, tab, m_l, w_in_o[j].astype(BF16))
            lru_args = (lru_conv_w[j], lru_conv_b[j],
                        jnp.stack([_block_diag(lru_wa[j, dd]) for dd in range(2)]).astype(BF16), lru_ba[j],
                        jnp.stack([_block_diag(lru_wx[j, dd]) for dd in range(2)]).astype(BF16), lru_bx[j],
                        lru_lambda[j])
            hf_p, hb_p, hfin_p = _lru(zl, 0, bp, sp, jnp.zeros((bp, 2, LRU_W), F32), *lru_args)
            hf_s, hb_s, _ = _lru(zl, tp, bs, ss, state_lru[:, j], *lru_args)
            lru_fin.append(hfin_p)
            prep = _wkvprep(zw, tab, row2(wkv_mu[j]), wkv_w0[j],
                            jnp.stack([_pad_rank(wkv_w2[j], dd) for dd in range(2)]).astype(BF16), wkv_a0[j],
                            jnp.stack([_pad_rank(wkv_a2[j], dd) for dd in range(2)]).astype(BF16),
                            wkv_kk[j], wkv_ka[j], row2(wkv_rk[j]), wkv_g2[j].astype(BF16), seg, tri)
            v, bonus, g = prep[0], prep[1], prep[2]
            yf_p, yb_p, mfin_p = _wkvscan(prep[3:], v, 0, bp, sp,
                                          jnp.zeros((bp, 2, N_PAIRS, PAIR, PAIR), F32), masks)
            yf_s, yb_s, _ = _wkvscan(prep[3:], v, tp, bs, ss, _state_to_pairs(state_wkv[:, j]), masks)
            wkv_fin.append(_pairs_to_state(mfin_p))
            cat = lambda a, b: jnp.concatenate([a, b], axis=0)
            x1, h2t, st = _post_odd(x, tab, cat(hf_p, hf_s), cat(hb_p, hb_s), zl, cat(yf_p, yf_s), cat(yb_p, yb_s),
                                    bonus, g, m_l, seg, row2(wkv_gn_g[j]), row2(wkv_gn_b[j]),
                                    w_out_o[j].astype(BF16), row2(ln1_g[l]), row2(ln1_b[l]), wq, keys)
        st4 = st.reshape(PEER_HEADS, 2, N_KEYS, st.shape[-1])
        th1, p1, p2, s2 = _topk(st4)
        x = _peer(x1, tab, h2t, s2, p2, th1, p1, peer_u[l].astype(BF16), peer_v[l].T.astype(BF16), m_l,
                  row2(ln2_g[l]), row2(ln2_b[l]))

    y_prompt = x[:tp].reshape(bp, sp, d)
    y_sample = x[tp:].reshape(bs, ss, d)
    return (y_prompt, y_sample, jnp.stack(lru_fin, 1).astype(x_prompt.dtype),
            jnp.stack(wkv_fin, 1).astype(x_prompt.dtype))
```

```python
import functools
import math

import numpy as np
import jax
import jax.numpy as jnp
from jax import lax
from jax.experimental import pallas as pl
from jax.experimental.pallas import tpu as pltpu

F32 = jnp.float32
BF16 = jnp.bfloat16

D_MODEL = 1024
GRID_W = 64
FNET_W = 512
FNET_GROUPS = 4
FNET_GW = FNET_W // FNET_GROUPS
CONV_W = 512
LRU_W = 512
LRU_HEADS = 8
LRU_CONV_K = 4
LRU_C = 8.0
WKV_W = 512
WKV_N = 64
WKV_H = 8
WKV_IN = 1920
DECAY_SCALE = math.exp(-0.5)
WKV_GN_EPS = 64e-5
PEER_HEADS = 8
N_KEYS = 128
PEER_TOPK = 16
PEER_HALF = 128
DEPTH = 4
ALPHA = (2 * DEPTH) ** 0.25
LN_EPS = 1e-6

ROW_TILE = 256
LANES = 128
HALO = 8
CHUNK = 64
PAIR = 2 * WKV_N
N_PAIRS = WKV_H // 2
PEER_TM = 512
PEER_TE = 1024
TOPK_TL = 256
VMEM_LIMIT = 56 * 1024 * 1024


def _cparams(sem):
    return pltpu.CompilerParams(dimension_semantics=sem, vmem_limit_bytes=VMEM_LIMIT)


def _dot(a, b):
    return jnp.dot(a, b, preferred_element_type=F32)


def _dot_nt(a, b):
    return lax.dot_general(a, b, (((1,), (1,)), ((), ())), preferred_element_type=F32)


def _mm3(a, b):
    ah = a.astype(BF16)
    al = (a - ah.astype(F32)).astype(BF16)
    bh = b.astype(BF16)
    bl = (b - bh.astype(F32)).astype(BF16)
    return _dot(ah, bh) + (_dot(ah, bl) + _dot(al, bh))


def _mm1(a, b):
    return _dot(a.astype(BF16), b.astype(BF16))


def _mm01(a, b01):
    h = a.astype(BF16)
    r = a - h.astype(F32)
    m = r.astype(BF16)
    lo = (r - m.astype(F32)).astype(BF16)
    return _dot(h, b01) + (_dot(m, b01) + _dot(lo, b01))


def _ln(x):
    mu = jnp.mean(x, axis=-1, keepdims=True)
    xc = x - mu
    var = jnp.mean(xc * xc, axis=-1, keepdims=True)
    return xc * lax.rsqrt(var + LN_EPS)


def _gelu(x):
    return 0.5 * x * (1.0 + jnp.tanh(0.7978845608028654 * (x + 0.044715 * (x * x * x))))


def _sigmoid(x):
    return 1.0 / (1.0 + jnp.exp(-x))


def _mod_kernel(c_ref, w_ref, b_ref, o_ref):
    c = c_ref[...]
    sc = c * _sigmoid(c)
    o_ref[0] = _dot(sc.astype(BF16), w_ref[0].astype(BF16)) + b_ref[0]


def _modulation(cv8, w_mod, b_mod):
    depth, d, n = w_mod.shape
    tn = 1536
    return pl.pallas_call(
        _mod_kernel,
        grid=(depth, n // tn),
        in_specs=[pl.BlockSpec((8, d), lambda l, j: (0, 0)),
                  pl.BlockSpec((1, d, tn), lambda l, j: (l, 0, j)),
                  pl.BlockSpec((1, 1, tn), lambda l, j: (l, 0, j))],
        out_specs=pl.BlockSpec((1, 8, tn), lambda l, j: (l, 0, j)),
        out_shape=jax.ShapeDtypeStruct((depth, 8, n), F32),
        compiler_params=_cparams(("arbitrary", "arbitrary")),
        name="modulation",
    )(cv8, w_mod, b_mod.reshape(depth, 1, n))


def _addpos_kernel(x_ref, p_ref, o_ref):
    o_ref[0] = x_ref[0] + p_ref[...]


def _add_pos(x, pos):
    b, s, d = x.shape
    ts = 512
    return pl.pallas_call(
        _addpos_kernel,
        grid=(s // ts, b),
        in_specs=[pl.BlockSpec((1, ts, d), lambda i, j: (j, i, 0)),
                  pl.BlockSpec((ts, d), lambda i, j: (i, 0))],
        out_specs=pl.BlockSpec((1, ts, d), lambda i, j: (j, i, 0)),
        out_shape=jax.ShapeDtypeStruct(x.shape, x.dtype),
        compiler_params=_cparams(("arbitrary", "arbitrary")),
        name="add_pos",
    )(x, pos)


def _inproj_even_kernel(tab_ref, x_ref, m_ref, w_ref, wc_ref, zr_ref, pq_ref):
    m = m_ref[0]
    h = _ln(x_ref[...]) * (1.0 + m[1:2]) + m[0:1]
    z = _dot(h.astype(BF16), w_ref[...])
    zr_ref[...] = z[:, FNET_W:]
    pq_ref[...] = _dot(z[:, :FNET_W].astype(BF16), wc_ref[...]).astype(BF16)


def _inproj_odd_kernel(tab_ref, x_ref, m_ref, w_ref, zl_ref, zw_ref):
    m = m_ref[0]
    h = _ln(x_ref[...]) * (1.0 + m[1:2]) + m[0:1]
    z = _dot(h.astype(BF16), w_ref[...])
    zl_ref[...] = z[:, :2 * LRU_W]
    zw_ref[...] = z[:, 2 * LRU_W:]


def _const_spec(shape):
    nd = len(shape)
    return pl.BlockSpec(shape, lambda i, t, _n=nd: (0,) * _n)


def _inproj(x, tab, mods, w_in, wc=None):
    t, d = x.shape
    n = w_in.shape[1]
    tm = ROW_TILE
    even = wc is not None
    in_specs = [pl.BlockSpec((tm, d), lambda i, t_: (i, 0)),
                pl.BlockSpec((1, 6, d), lambda i, t_: (t_[0, i], 0, 0)),
                _const_spec((d, n))]
    if even:
        in_specs.append(_const_spec(wc.shape))
        widths = (n - FNET_W, 2 * FNET_W)
        dtypes = (F32, BF16)
        kern = _inproj_even_kernel
        args = (tab, x, mods, w_in, wc)
    else:
        widths = (2 * LRU_W, n - 2 * LRU_W)
        dtypes = (F32, F32)
        kern = _inproj_odd_kernel
        args = (tab, x, mods, w_in)
    return pl.pallas_call(
        kern,
        grid_spec=pltpu.PrefetchScalarGridSpec(
            num_scalar_prefetch=1, grid=(t // tm,), in_specs=in_specs,
            out_specs=[pl.BlockSpec((tm, w), lambda i, t_: (i, 0)) for w in widths]),
        out_shape=[jax.ShapeDtypeStruct((t, w), dt) for w, dt in zip(widths, dtypes)],
        compiler_params=_cparams(("arbitrary",)),
        name="inproj_even" if even else "inproj_odd",
    )(*args)


def _seqdft_kernel(c_ref, s_ref, p_ref, q_ref, y_ref):
    y_ref[...] = (_dot(c_ref[...], p_ref[...]) + _dot(s_ref[...], q_ref[...])).astype(BF16)


def _seqdft(pq, y_rows, row0, n_seq, s_len, cmat, smat):
    tm = min(s_len, ROW_TILE)
    nt = s_len // tm
    blk0 = row0 // s_len
    return pl.pallas_call(
        _seqdft_kernel,
        grid=(n_seq, nt),
        in_specs=[pl.BlockSpec((tm, s_len), lambda b, i: (i, 0)),
                  pl.BlockSpec((tm, s_len), lambda b, i: (i, 0)),
                  pl.BlockSpec((s_len, FNET_W), lambda b, i: (blk0 + b, 0)),
                  pl.BlockSpec((s_len, FNET_W), lambda b, i: (blk0 + b, 1))],
        out_specs=pl.BlockSpec((tm, FNET_W), lambda b, i: (b * nt + i, 0)),
        out_shape=jax.ShapeDtypeStruct((y_rows, FNET_W), BF16),
        compiler_params=_cparams(("arbitrary", "arbitrary")),
        name="seqdft_%d" % s_len,
    )(cmat, smat, pq, pq)


def _post_and_query(x, mix, m, g1, b1, wq_ref, keys_ref, x1_ref, h2t_ref, st_ref):
    x1 = _ln(ALPHA * x + m[2:3] * mix) * g1 + b1
    x1_ref[...] = x1
    h2 = _ln(x1) * (1.0 + m[4:5]) + m[3:4]
    h2b = h2.astype(BF16)
    h2t_ref[...] = h2.T.astype(BF16)
    q = _dot(h2b, wq_ref[...])
    for hp in range(2 * PEER_HEADS):
        qb = q[:, hp * PEER_HALF:(hp + 1) * PEER_HALF].astype(BF16)
        st_ref[hp * N_KEYS:(hp + 1) * N_KEYS, :] = _dot_nt(keys_ref[hp], qb)


def _post_even_kernel(tab_ref, x_ref, bg_ref, cg_ref, xi_ref, cgp_ref, xip_ref, cgn_ref, xin_ref, y_ref,
                      m_ref, cw_ref, cb_ref, wo_ref, g1_ref, b1_ref, wq_ref, keys_ref,
                      x1_ref, h2t_ref, st_ref, ext_ref):
    i = pl.program_id(0)
    tm = ROW_TILE
    first = tab_ref[1, i] == 1
    last = tab_ref[2, i] == 1
    u = cg_ref[...] * xi_ref[...]
    ext_ref[0:HALO] = jnp.where(first, 0.0, cgp_ref[...] * xip_ref[...])
    ext_ref[HALO:HALO + tm] = u
    ext_ref[HALO + tm:2 * HALO + tm] = jnp.where(last, 0.0, cgn_ref[...] * xin_ref[...])
    cw = cw_ref[...]
    conv = (ext_ref[HALO - 1:HALO - 1 + tm] * cw[0:1] + u * cw[1:2]
            + ext_ref[HALO + 1:HALO + 1 + tm] * cw[2:3] + cb_ref[...])
    ymix = (bg_ref[...] * conv).astype(BF16)
    mix = _dot(y_ref[...], wo_ref[0:FNET_W]) + _dot(ymix, wo_ref[FNET_W:])
    _post_and_query(x_ref[...], mix, m_ref[0], g1_ref[...], b1_ref[...], wq_ref, keys_ref,
                    x1_ref, h2t_ref, st_ref)


def _post_out_specs(t, d, tm):
    nsc = 2 * PEER_HEADS * N_KEYS
    specs = [pl.BlockSpec((tm, d), lambda i, t_: (i, 0)),
             pl.BlockSpec((d, tm), lambda i, t_: (0, i)),
             pl.BlockSpec((nsc, tm), lambda i, t_: (0, i))]
    shapes = [jax.ShapeDtypeStruct((t, d), F32),
              jax.ShapeDtypeStruct((d, t), BF16),
              jax.ShapeDtypeStruct((nsc, t), F32)]
    return specs, shapes


def _post_even(x, tab, zr, y, mods, cw, cb, wo, g1, b1, wq, keys):
    t, d = x.shape
    tm = ROW_TILE
    hb = tm // HALO
    nblk = t // HALO
    prev = lambda c: pl.BlockSpec((HALO, CONV_W), lambda i, t_, _c=c: (jnp.maximum(i * hb - 1, 0), _c))
    nxt = lambda c: pl.BlockSpec((HALO, CONV_W), lambda i, t_, _c=c: (jnp.minimum((i + 1) * hb, nblk - 1), _c))
    col = lambda c: pl.BlockSpec((tm, CONV_W), lambda i, t_, _c=c: (i, _c))
    in_specs = [pl.BlockSpec((tm, d), lambda i, t_: (i, 0)),
                col(0), col(1), col(2), prev(1), prev(2), nxt(1), nxt(2),
                pl.BlockSpec((tm, FNET_W), lambda i, t_: (i, 0)),
                pl.BlockSpec((1, 6, d), lambda i, t_: (t_[0, i], 0, 0)),
                _const_spec(cw.shape), _const_spec(cb.shape), _const_spec(wo.shape),
                _const_spec(g1.shape), _const_spec(b1.shape), _const_spec(wq.shape), _const_spec(keys.shape)]
    out_specs, out_shape = _post_out_specs(t, d, tm)
    return pl.pallas_call(
        _post_even_kernel,
        grid_spec=pltpu.PrefetchScalarGridSpec(
            num_scalar_prefetch=1, grid=(t // tm,), in_specs=in_specs, out_specs=out_specs,
            scratch_shapes=[pltpu.VMEM((tm + 2 * HALO, CONV_W), F32)]),
        out_shape=out_shape,
        compiler_params=_cparams(("arbitrary",)),
        name="post_even",
    )(tab, x, zr, zr, zr, zr, zr, zr, zr, y, mods, cw, cb, wo, g1, b1, wq, keys)


def _lru_kernel(xf_ref, xfh_ref, xb_ref, xbh_ref, cw_ref, cb_ref, wa_ref, ba_ref, wx_ref, bx_ref, lam_ref,
                h0_ref, hf_ref, hb_ref, hfin_ref, ext_ref, carry_ref):
    i = pl.program_id(1)
    nt = pl.num_programs(1)
    tm = ROW_TILE

    @pl.when(i == 0)
    def _():
        carry_ref[...] = h0_ref[0]

    row = lax.broadcasted_iota(jnp.int32, (tm, 1), 0)
    for d in range(2):
        x = (xf_ref if d == 0 else xb_ref)[...]
        halo = jnp.where(i == 0, 0.0, (xfh_ref if d == 0 else xbh_ref)[...])
        ext_ref[HALO:HALO + tm] = x
        if d == 0:
            ext_ref[0:HALO] = halo
        else:
            ext_ref[HALO + tm:2 * HALO + tm] = halo
        cw = cw_ref[d]
        xc = cb_ref[d:d + 1] + x * cw[LRU_CONV_K - 1:LRU_CONV_K]
        for j in range(LRU_CONV_K - 1):
            k = LRU_CONV_K - 1 - j
            off = HALO - k if d == 0 else HALO + k
            xc = xc + ext_ref[off:off + tm] * cw[j:j + 1]
        xcb = xc.astype(BF16)
        gate_r = _sigmoid(_dot(xcb, wa_ref[d]) + ba_ref[d:d + 1])
        gate_i = _sigmoid(_dot(xcb, wx_ref[d]) + bx_ref[d:d + 1])
        nl = -lam_ref[d:d + 1]
        softplus = jnp.maximum(nl, 0.0) + jnp.log1p(jnp.exp(-jnp.abs(nl)))
        log_a = -LRU_C * gate_r * softplus
        a = jnp.exp(log_a)
        b = jnp.sqrt(-jnp.tanh(log_a) * (a * a + 1.0)) * (gate_i * xc)
        s = 1
        while s < tm:
            if d == 0:
                keep = row >= s
                sh = s
            else:
                keep = row < tm - s
                sh = tm - s
            a_sh = jnp.where(keep, pltpu.roll(a, sh, 0), 1.0)
            b_sh = jnp.where(keep, pltpu.roll(b, sh, 0), 0.0)
            b = a * b_sh + b
            a = a * a_sh
            s *= 2
        h = a * carry_ref[d:d + 1] + b
        if d == 0:
            hf_ref[...] = h
            carry_ref[0:1] = h[tm - 1:tm]
        else:
            hb_ref[...] = h
            carry_ref[1:2] = h[0:1]

    @pl.when(i == nt - 1)
    def _():
        hfin_ref[0] = carry_ref[...]


def _lru(zl, row0, n_seq, s_len, h0, cw, cb, wa, ba, wx, bx, lam):
    tm = ROW_TILE
    nt = s_len // tm
    t0 = row0 // tm
    hb = tm // HALO
    nblk = zl.shape[0] // HALO
    rows = n_seq * s_len
    fwd = lambda b, i: (t0 + b * nt + i, 0)
    bwd = lambda b, i: (t0 + b * nt + nt - 1 - i, 0)
    fwd_h = lambda b, i: (jnp.maximum((t0 + b * nt + i) * hb - 1, 0), 0)
    bwd_h = lambda b, i: (jnp.minimum((t0 + b * nt + nt - i) * hb, nblk - 1), 0)
    out_f = lambda b, i: (b * nt + i, 0)
    out_b = lambda b, i: (b * nt + nt - 1 - i, 0)
    cst = lambda a: pl.BlockSpec(a.shape, lambda b, i, _n=a.ndim: (0,) * _n)
    return pl.pallas_call(
        _lru_kernel,
        grid=(n_seq, nt),
        in_specs=[pl.BlockSpec((tm, LRU_W), fwd), pl.BlockSpec((HALO, LRU_W), fwd_h),
                  pl.BlockSpec((tm, LRU_W), bwd), pl.BlockSpec((HALO, LRU_W), bwd_h),
                  cst(cw), cst(cb), cst(wa), cst(ba), cst(wx), cst(bx), cst(lam),
                  pl.BlockSpec((1, 2, LRU_W), lambda b, i: (b, 0, 0))],
        out_specs=[pl.BlockSpec((tm, LRU_W), out_f), pl.BlockSpec((tm, LRU_W), out_b),
                   pl.BlockSpec((1, 2, LRU_W), lambda b, i: (b, 0, 0))],
        out_shape=[jax.ShapeDtypeStruct((rows, LRU_W), F32), jax.ShapeDtypeStruct((rows, LRU_W), F32),
                   jax.ShapeDtypeStruct((n_seq, 2, LRU_W), F32)],
        scratch_shapes=[pltpu.VMEM((tm + 2 * HALO, LRU_W), F32), pltpu.VMEM((2, LRU_W), F32)],
        compiler_params=_cparams(("arbitrary", "arbitrary")),
        name="lru_%d" % s_len,
    )(zl, zl, zl, zl, cw, cb, wa, ba, wx, bx, lam, h0)


def _wkvprep_kernel(tab_ref, z_ref, zp_ref, zn_ref, mu_ref, w0_ref, w2_ref, a0_ref, a2_ref, kk_ref, ka_ref,
                    rk_ref, g2_ref, seg_ref, tri_ref,
                    v_ref, bg_ref, g_ref, rf_ref, kf_ref, bf_ref, qf_ref, cf_ref,
                    rb_ref, kb_ref, bb_ref, qb_ref, cb_ref, ext_ref):
    i = pl.program_id(0)
    tm = ROW_TILE
    w = WKV_W
    first = tab_ref[1, i] == 1
    last = tab_ref[2, i] == 1
    z = z_ref[...]
    ext_ref[0:HALO] = jnp.where(first, 0.0, zp_ref[...])
    ext_ref[HALO:HALO + tm] = z
    ext_ref[HALO + tm:2 * HALO + tm] = jnp.where(last, 0.0, zn_ref[...])
    z = z + mu_ref[...] * (0.5 * (ext_ref[HALO - 1:HALO - 1 + tm] + ext_ref[HALO + 1:HALO + 1 + tm]) - z)
    r = z[:, 0:w]
    k = z[:, w:2 * w]
    v = z[:, 2 * w:3 * w]
    wd = jnp.tanh(z[:, 3 * w:3 * w + 128]).astype(BF16)
    ad = z[:, 3 * w + 128:3 * w + 256].astype(BF16)
    gd = _sigmoid(z[:, 3 * w + 256:3 * w + 384]).astype(BF16)
    v_ref[...] = v
    g_ref[...] = _dot(gd, g2_ref[...])
    seg = seg_ref[...]
    rk = rk_ref[...]
    bonus = jnp.zeros((tm, w), F32)
    outs = ((rf_ref, kf_ref, bf_ref, qf_ref, cf_ref), (rb_ref, kb_ref, bb_ref, qb_ref, cb_ref))
    nch = tm // CHUNK
    for d in range(2):
        r_ref, k_ref, b_ref, q_ref, c_ref = outs[d]
        wz = w0_ref[d:d + 1] + _dot(wd, w2_ref[d])
        logw = -DECAY_SCALE * _sigmoid(wz)
        iclr = _sigmoid(a0_ref[d:d + 1] + _dot(ad, a2_ref[d]))
        kk = k * kk_ref[d:d + 1]
        kk = kk * lax.rsqrt(jnp.maximum(_mm01(kk * kk, seg), 1e-24))
        km = k * (1.0 + (iclr - 1.0) * ka_ref[d:d + 1])
        bonus = bonus + _mm01(r * km * rk, seg) * v
        lw_h = logw.astype(BF16)
        lw_r = logw - lw_h.astype(F32)
        lw_m = lw_r.astype(BF16)
        lw_l = (lw_r - lw_m.astype(F32)).astype(BF16)
        tri = tri_ref[d]
        cl = _dot(tri, lw_h) + (_dot(tri, lw_m) + _dot(tri, lw_l))
        c = jnp.exp(cl)
        cinv = jnp.exp(-cl)
        r_ref[...] = r * c
        k_ref[...] = km * cinv
        b_ref[...] = kk * iclr * cinv
        q_ref[...] = kk * jnp.exp(cl - logw)
        for j in range(nch):
            edge = (j + 1) * CHUNK - 1 if d == 0 else j * CHUNK
            c_ref[j] = c[edge:edge + 1]
    bg_ref[...] = bonus


def _wkvprep(zw, tab, mu, w0, w2p, a0, a2p, kk, ka, rk, g2, seg, tri):
    t, n = zw.shape
    tm = ROW_TILE
    hb = tm // HALO
    nblk = t // HALO
    nch = tm // CHUNK
    w = WKV_W
    in_specs = [pl.BlockSpec((tm, n), lambda i, t_: (i, 0)),
                pl.BlockSpec((HALO, n), lambda i, t_: (jnp.maximum(i * hb - 1, 0), 0)),
                pl.BlockSpec((HALO, n), lambda i, t_: (jnp.minimum((i + 1) * hb, nblk - 1), 0))]
    in_specs += [_const_spec(a.shape) for a in (mu, w0, w2p, a0, a2p, kk, ka, rk, g2, seg, tri)]
    row = pl.BlockSpec((tm, w), lambda i, t_: (i, 0))
    cspec = pl.BlockSpec((nch, 1, w), lambda i, t_: (i, 0, 0))
    rshape = jax.ShapeDtypeStruct((t, w), F32)
    cshape = jax.ShapeDtypeStruct((t // CHUNK, 1, w), F32)
    out_specs = [row, row, row] + [row, row, row, row, cspec] * 2
    out_shape = [rshape, rshape, rshape] + [rshape, rshape, rshape, rshape, cshape] * 2
    return pl.pallas_call(
        _wkvprep_kernel,
        grid_spec=pltpu.PrefetchScalarGridSpec(
            num_scalar_prefetch=1, grid=(t // tm,), in_specs=in_specs, out_specs=out_specs,
            scratch_shapes=[pltpu.VMEM((tm + 2 * HALO, n), F32)]),
        out_shape=out_shape,
        compiler_params=_cparams(("arbitrary",)),
        name="wkv_prep",
    )(tab, zw, zw, zw, mu, w0, w2p, a0, a2p, kk, ka, rk, g2, seg, tri)


def _wkv_chunks(items, eye):
    lane = lax.broadcasted_iota(jnp.int32, (1, PAIR), 1)
    m0 = (lane < WKV_N).astype(F32)
    m1 = 1.0 - m0
    stack = lambda x: jnp.concatenate([x * m0, x * m1], axis=0)
    n = range(len(items))
    rh_s = [stack(it[0]) for it in items]
    kh_s = [stack(it[1]) for it in items]
    bh_s = [stack(it[2]) for it in items]
    kq_s = [stack(it[3]) for it in items]
    v_s = [stack(it[4]) for it in items]
    bh_t = [x.T for x in bh_s]
    kh_t = [x.T for x in kh_s]
    bk_t = [jnp.concatenate([bh_t[i], kh_t[i]], axis=1) for i in n]
    att_s = [_mm3(kq_s[i], bk_t[i]) for i in n]
    att_y = [_mm1(rh_s[i], bk_t[i]) for i in n]
    n1 = [att_s[i][:, 0:PAIR] * items[i][7] for i in n]
    ak = [att_s[i][:, PAIR:] * items[i][7] for i in n]
    gb = [att_y[i][:, 0:PAIR] * items[i][8] for i in n]
    gk = [att_y[i][:, PAIR:] * items[i][8] for i in n]
    t_inv = [eye - n1[i] for i in n]
    npow = n1
    for _ in range(5):
        npow = [_mm3(npow[i], npow[i]) for i in n]
        t_inv = [t_inv[i] + _mm3(t_inv[i], npow[i]) for i in n]
    kv = [_mm3(jnp.concatenate([kh_t[i], ak[i]], axis=0), v_s[i]) for i in n]
    gkv = [_mm1(gk[i], v_s[i]) for i in n]
    x = [_mm3(t_inv[i], jnp.concatenate([kq_s[i], kv[i][PAIR:]], axis=1)) for i in n]
    bx = [_mm3(bh_t[i], x[i]) for i in n]
    gx = [_mm1(gb[i], x[i]) for i in n]
    p_mat = [eye - bx[i][:, 0:PAIR] for i in n]
    q_mat = [kv[i][0:PAIR] - bx[i][:, PAIR:] for i in n]
    r_til = [rh_s[i] - gx[i][:, 0:PAIR] for i in n]
    y0 = [gkv[i] - gx[i][:, PAIR:] for i in n]
    pm = [_mm3(p_mat[i], items[i][6]) for i in n]
    y_st = [_mm1(r_til[i], items[i][6]) + y0[i] for i in n]
    ys = [y_st[i][0:CHUNK] + y_st[i][CHUNK:] for i in n]
    c_col = [jnp.broadcast_to(items[i][5], (PAIR, PAIR)).T for i in n]
    m_new = [(pm[i] + q_mat[i]) * c_col[i] for i in n]
    return ys, m_new


def _wkvscan_kernel(rf_ref, kf_ref, bf_ref, qf_ref, vf_ref, cf_ref, rb_ref, kb_ref, bb_ref, qb_ref, vb_ref, cb_ref,
                    m0_ref, msk_ref, yf_ref, yb_ref, mfin_ref, m_ref):
    i = pl.program_id(1)
    nc = pl.num_programs(1)

    @pl.when(i == 0)
    def _():
        m_ref[...] = m0_ref[0]

    eye = msk_ref[4]
    ins = ((rf_ref, kf_ref, bf_ref, qf_ref, vf_ref, cf_ref, yf_ref),
           (rb_ref, kb_ref, bb_ref, qb_ref, vb_ref, cb_ref, yb_ref))
    items = []
    for d in range(2):
        r_ref, k_ref, b_ref, q_ref, v_ref, c_ref, _ = ins[d]
        for p in range(N_PAIRS):
            sl = slice(p * PAIR, (p + 1) * PAIR)
            items.append((r_ref[:, sl], k_ref[:, sl], b_ref[:, sl], q_ref[:, sl], v_ref[:, sl],
                          c_ref[0, :, sl], m_ref[d, p], msk_ref[2 * d], msk_ref[2 * d + 1]))
    ys, m_new = _wkv_chunks(items, eye)
    for d in range(2):
        for p in range(N_PAIRS):
            sl = slice(p * PAIR, (p + 1) * PAIR)
            ins[d][6][:, sl] = ys[d * N_PAIRS + p]
            m_ref[d, p] = m_new[d * N_PAIRS + p]

    @pl.when(i == nc - 1)
    def _():
        mfin_ref[0] = m_ref[...]


def _wkvscan(prep, v, row0, n_seq, s_len, m0, masks):
    rf, kf, bf, qf, cf, rb, kb, bb, qb, cb = prep
    nc = s_len // CHUNK
    c0 = row0 // CHUNK
    rows = n_seq * s_len
    w = WKV_W
    fwd = lambda b, i: (c0 + b * nc + i, 0)
    bwd = lambda b, i: (c0 + b * nc + nc - 1 - i, 0)
    fwd3 = lambda b, i: (c0 + b * nc + i, 0, 0)
    bwd3 = lambda b, i: (c0 + b * nc + nc - 1 - i, 0, 0)
    blk = lambda im: pl.BlockSpec((CHUNK, w), im)
    cblk = lambda im: pl.BlockSpec((1, 1, w), im)
    mspec = pl.BlockSpec((1, 2, N_PAIRS, PAIR, PAIR), lambda b, i: (b, 0, 0, 0, 0))
    return pl.pallas_call(
        _wkvscan_kernel,
        grid=(n_seq, nc),
        in_specs=[blk(fwd)] * 5 + [cblk(fwd3)] + [blk(bwd)] * 5 + [cblk(bwd3)]
        + [mspec, pl.BlockSpec(masks.shape, lambda b, i: (0, 0, 0))],
        out_specs=[pl.BlockSpec((CHUNK, w), lambda b, i: (b * nc + i, 0)),
                   pl.BlockSpec((CHUNK, w), lambda b, i: (b * nc + nc - 1 - i, 0)),
                   mspec],
        out_shape=[jax.ShapeDtypeStruct((rows, w), F32), jax.ShapeDtypeStruct((rows, w), F32),
                   jax.ShapeDtypeStruct(m0.shape, F32)],
        scratch_shapes=[pltpu.VMEM((2, N_PAIRS, PAIR, PAIR), F32)],
        compiler_params=_cparams(("arbitrary", "arbitrary")),
        name="wkv_scan_%d" % s_len,
    )(rf, kf, bf, qf, v, cf, rb, kb, bb, qb, v, cb, m0, masks)


def _post_odd_kernel(tab_ref, x_ref, hf_ref, hb_ref, gb_ref, yf_ref, yb_ref, bon_ref, g_ref,
                     m_ref, seg_ref, gng_ref, gnb_ref, wo_ref, g1_ref, b1_ref, wq_ref, keys_ref,
                     x1_ref, h2t_ref, st_ref):
    y_lru = ((hf_ref[...] + hb_ref[...]) * _gelu(gb_ref[...])).astype(BF16)
    seg = seg_ref[...]
    ys = yf_ref[...] + yb_ref[...]
    mean = _mm01(ys, seg) * (1.0 / WKV_N)
    yc = ys - mean
    var = _mm01(yc * yc, seg) * (1.0 / WKV_N)
    yn = yc * lax.rsqrt(var + WKV_GN_EPS) * gng_ref[...] + gnb_ref[...]
    y_wkv = ((yn + bon_ref[...]) * g_ref[...]).astype(BF16)
    mix = _dot(y_lru, wo_ref[0:LRU_W]) + _dot(y_wkv, wo_ref[LRU_W:])
    _post_and_query(x_ref[...], mix, m_ref[0], g1_ref[...], b1_ref[...], wq_ref, keys_ref,
                    x1_ref, h2t_ref, st_ref)


def _post_odd(x, tab, hf, hb, zl, yf, yb, bonus, g, mods, seg, gng, gnb, wo, g1, b1, wq, keys):
    t, d = x.shape
    tm = ROW_TILE
    w = WKV_W
    row = pl.BlockSpec((tm, w), lambda i, t_: (i, 0))
    in_specs = [pl.BlockSpec((tm, d), lambda i, t_: (i, 0)), row, row,
                pl.BlockSpec((tm, LRU_W), lambda i, t_: (i, 1)), row, row, row, row,
                pl.BlockSpec((1, 6, d), lambda i, t_: (t_[0, i], 0, 0))]
    in_specs += [_const_spec(a.shape) for a in (seg, gng, gnb, wo, g1, b1, wq, keys)]
    out_specs, out_shape = _post_out_specs(t, d, tm)
    return pl.pallas_call(
        _post_odd_kernel,
        grid_spec=pltpu.PrefetchScalarGridSpec(
            num_scalar_prefetch=1, grid=(t // tm,), in_specs=in_specs, out_specs=out_specs),
        out_shape=out_shape,
        compiler_params=_cparams(("arbitrary",)),
        name="post_odd",
    )(tab, x, hf, hb, zl, yf, yb, bonus, g, mods, seg, gng, gnb, wo, g1, b1, wq, keys)


N_TOP = PEER_TOPK + 1
TOP_ROWS = 24


def _topk_kernel(s_ref, th1_ref, p1_ref, p2_ref, s2_ref, v1_ref, v2_ref):
    neg = -jnp.inf

    def extract(x, n, store):
        vals = []
        for it in range(n):
            m = jnp.max(x, axis=0, keepdims=True)
            vals.append(m)
            if store is not None:
                store[it:it + 1, :] = m
            x = jnp.where(x == m, neg, x)
        return vals

    def head(h, carry):
        s1 = s_ref[h, 0]
        s2 = s_ref[h, 1]
        v1_ref[...] = jnp.full(v1_ref.shape, neg, F32)
        v2_ref[...] = jnp.full(v2_ref.shape, neg, F32)
        extract(s1, N_TOP, v1_ref)
        extract(s2, N_TOP, v2_ref)
        pieces = [v1_ref[0:1, :] + v2_ref[0:TOP_ROWS, :]]
        for a in range(1, N_TOP):
            pieces.append(v1_ref[a:a + 1, :] + v2_ref[0:8, :])
        cand = jnp.concatenate(pieces, axis=0)
        top = extract(cand, N_TOP, None)
        tau = 0.5 * (top[PEER_TOPK - 1] + top[PEER_TOPK])
        mx1 = v1_ref[0:1, :]
        mx2 = v2_ref[0:1, :]
        zsum = jnp.sum(jnp.where(cand >= tau, jnp.exp(cand - (mx1 + mx2)), 0.0), axis=0, keepdims=True)
        th1 = tau - s1
        p1 = jnp.exp(s1 - mx1) / zsum
        p2 = jnp.exp(s2 - mx2)
        for cb in range(TOPK_TL // LANES):
            cs = slice(cb * LANES, (cb + 1) * LANES)
            th1_ref[cb, h] = th1[:, cs]
            p1_ref[cb, h] = p1[:, cs]
            p2_ref[cb, h] = p2[:, cs]
            s2_ref[cb, h] = s2[:, cs]
        return carry

    lax.fori_loop(0, PEER_HEADS, head, 0)


def _topk(st4):
    t = st4.shape[-1]
    tl = TOPK_TL
    out = jax.ShapeDtypeStruct((t // LANES, PEER_HEADS, N_KEYS, LANES), F32)
    ospec = pl.BlockSpec((tl // LANES, PEER_HEADS, N_KEYS, LANES), lambda i: (i, 0, 0, 0))
    return pl.pallas_call(
        _topk_kernel,
        grid=(t // tl,),
        in_specs=[pl.BlockSpec((PEER_HEADS, 2, N_KEYS, tl), lambda i: (0, 0, 0, i))],
        out_specs=[ospec, ospec, ospec, ospec],
        out_shape=[out, out, out, out],
        scratch_shapes=[pltpu.VMEM((TOP_ROWS, tl), F32), pltpu.VMEM((TOP_ROWS, tl), F32)],
        compiler_params=_cparams(("arbitrary",)),
        name="peer_topk",
    )(st4)


def _peer_kernel(tab_ref, h2t_ref, s2_ref, p2_ref, th1_ref, p1_ref, u_ref, vt_ref, x1_ref, m_ref, g2_ref, b2_ref,
                 o_ref, acc_ref, act_ref, ga_ref):
    s = pl.program_id(1)
    n_steps = pl.num_programs(1)
    n_blk = PEER_TM // LANES
    rows_per_tile = PEER_TE // N_KEYS
    d_rows = acc_ref.shape[0] // n_blk
    u_rows = PEER_TE // n_blk

    def project(half, q):
        ro = pl.multiple_of(q * u_rows, u_rows)
        return _dot(u_ref[pl.ds(half * PEER_TE + ro, u_rows), :], h2t_ref[...])

    def store_act(slot, q, a):
        ro = pl.multiple_of(q * u_rows, u_rows)
        for k in range(n_blk):
            act_ref[slot, k, pl.ds(ro, u_rows), :] = a[:, k * LANES:(k + 1) * LANES]

    def apply_v(slot, half, q):
        ro = pl.multiple_of(q * d_rows, d_rows)
        ga = jnp.concatenate([ga_ref[slot, k] for k in range(n_blk)], axis=1)
        return _dot(vt_ref[pl.ds(ro, d_rows), half * PEER_TE:(half + 1) * PEER_TE], ga)

    def stage(cur, tile):
        nxt = 1 - cur
        r0 = pl.multiple_of(tile * rows_per_tile, rows_per_tile)

        def body(cb, carry):
            a_next = project(nxt, cb)
            v_prev = apply_v(nxt, nxt, cb)
            th = [th1_ref[cb, h, pl.ds(r0, rows_per_tile), :] for h in range(PEER_HEADS)]
            p1 = [p1_ref[cb, h, pl.ds(r0, rows_per_tile), :] for h in range(PEER_HEADS)]
            for r in range(rows_per_tile):
                gate = jnp.zeros((N_KEYS, LANES), F32)
                for h in range(PEER_HEADS):
                    sel = jnp.where(s2_ref[cb, h] >= th[h][r:r + 1], p2_ref[cb, h], 0.0)
                    gate = gate + sel * p1[h][r:r + 1]
                rs = slice(r * N_KEYS, (r + 1) * N_KEYS)
                ga_ref[cur, cb, rs, :] = (gate * _gelu(act_ref[cur, cb, rs, :])).astype(BF16)
            store_act(nxt, cb, a_next)
            ro = pl.multiple_of(cb * d_rows, d_rows)
            acc_ref[pl.ds(ro, d_rows), :] += v_prev
            return carry

        lax.fori_loop(0, n_blk, body, 0)

    @pl.when(s == 0)
    def _():
        acc_ref[...] = jnp.zeros(acc_ref.shape, F32)
        ga_ref[...] = jnp.zeros(ga_ref.shape, BF16)
        for q in range(n_blk):
            store_act(0, q, project(0, q))

    @pl.when(s > 0)
    def _():
        stage(1, 2 * s - 1)

    @pl.when(s < n_steps - 1)
    def _():
        stage(0, 2 * s)

    @pl.when(s == n_steps - 1)
    def _():
        m = m_ref[0]
        for q in range(n_blk):
            acc_ref[q * d_rows:(q + 1) * d_rows, :] += apply_v(1, 1, q)
        ffn = acc_ref[...].T
        o_ref[...] = _ln(ALPHA * x1_ref[...] + m[5:6] * ffn) * g2_ref[...] + b2_ref[...]


def _peer(x1, tab, h2t, s2, p2, th1, p1, u_bf, vt_bf, mods, g2, b2):
    t, d = x1.shape
    tm = PEER_TM
    te = PEER_TE
    n_tiles = u_bf.shape[0] // te
    per = tm // ROW_TILE
    n_blk = tm // LANES
    hk = pl.BlockSpec((n_blk, PEER_HEADS, N_KEYS, LANES), lambda i, j, t_: (i, 0, 0, 0))
    in_specs = [pl.BlockSpec((d, tm), lambda i, j, t_: (0, i)),
                hk, hk, hk, hk,
                pl.BlockSpec((2 * te, d), lambda i, j, t_: (jnp.minimum(j, n_tiles // 2 - 1), 0)),
                pl.BlockSpec((d, 2 * te), lambda i, j, t_: (0, jnp.maximum(j - 1, 0))),
                pl.BlockSpec((tm, d), lambda i, j, t_: (i, 0)),
                pl.BlockSpec((1, 6, d), lambda i, j, t_: (t_[0, i * per], 0, 0)),
                pl.BlockSpec(g2.shape, lambda i, j, t_: (0, 0)),
                pl.BlockSpec(b2.shape, lambda i, j, t_: (0, 0))]
    return pl.pallas_call(
        _peer_kernel,
        grid_spec=pltpu.PrefetchScalarGridSpec(
            num_scalar_prefetch=1, grid=(t // tm, n_tiles // 2 + 1), in_specs=in_specs,
            out_specs=pl.BlockSpec((tm, d), lambda i, j, t_: (i, 0)),
            scratch_shapes=[pltpu.VMEM((d, tm), F32), pltpu.VMEM((2, n_blk, te, LANES), F32),
                            pltpu.VMEM((2, n_blk, te, LANES), BF16)]),
        out_shape=jax.ShapeDtypeStruct((t, d), F32),
        compiler_params=_cparams(("arbitrary", "arbitrary")),
        name="peer_mix",
    )(tab, h2t, s2, p2, th1, p1, u_bf, vt_bf, x1, mods, g2, b2)


def _sincos(pos, dim):
    omega = 1.0 / (10000.0 ** (jnp.arange(dim // 2, dtype=F32) / (dim // 2)))
    ang = pos.astype(F32)[:, None] * omega[None, :]
    return jnp.concatenate([jnp.sin(ang), jnp.cos(ang)], -1)


def _grid_pos_embed(n_tok):
    rows = n_tok // GRID_W
    half = D_MODEL // 2
    er = _sincos(jnp.arange(rows), half)
    ec = _sincos(jnp.arange(GRID_W), half)
    emb = jnp.concatenate([jnp.broadcast_to(er[:, None, :], (rows, GRID_W, half)),
                           jnp.broadcast_to(ec[None, :, :], (rows, GRID_W, half))], -1)
    return emb.reshape(rows * GRID_W, D_MODEL)


def _seq_dft_tables(s_len):
    n = jnp.arange(s_len, dtype=jnp.int32)
    ang = ((n[:, None] * n[None, :]) % s_len).astype(F32) * (2.0 * math.pi / s_len)
    scale = 1.0 / math.sqrt(s_len * FNET_GW)
    return (jnp.cos(ang) * scale).astype(BF16), (-jnp.sin(ang) * scale).astype(BF16)


def _channel_dft_table():
    n = np.arange(FNET_GW)
    ang = ((n[:, None] * n[None, :]) % FNET_GW) * (2.0 * np.pi / FNET_GW)
    wc = np.zeros((FNET_W, 2 * FNET_W), np.float32)
    for g in range(FNET_GROUPS):
        sl = slice(g * FNET_GW, (g + 1) * FNET_GW)
        wc[sl, sl] = np.cos(ang)
        wc[sl, FNET_W + g * FNET_GW:FNET_W + (g + 1) * FNET_GW] = np.sin(ang)
    return jnp.asarray(wc, BF16)


def _segment_ones():
    idx = np.arange(WKV_W) // WKV_N
    return jnp.asarray((idx[:, None] == idx[None, :]).astype(np.float32), BF16)


def _chunk_tri():
    i = np.arange(ROW_TILE)
    same = (i[:, None] // CHUNK) == (i[None, :] // CHUNK)
    lower = same & (i[None, :] <= i[:, None])
    upper = same & (i[None, :] >= i[:, None])
    return jnp.asarray(np.stack([lower, upper]).astype(np.float32), BF16)


def _scan_masks():
    i = np.arange(PAIR)
    same = (i[:, None] // CHUNK) == (i[None, :] // CHUNK)
    t_row = i[:, None] % CHUNK
    t_col = i[None, :] % CHUNK
    masks = [same & (t_col < t_row), same & (t_col <= t_row), same & (t_col > t_row), same & (t_col >= t_row),
             np.eye(PAIR, dtype=bool)]
    return jnp.asarray(np.stack(masks).astype(np.float32))


def _block_diag(w):
    h, n, _ = w.shape
    eye = jnp.eye(h, dtype=w.dtype)
    return jnp.einsum("hij,hg->higj", w, eye).reshape(h * n, h * n)


def _pad_rank(w, d):
    r = w.shape[1]
    z = jnp.zeros_like(w[d])
    return jnp.concatenate([w[d], z] if d == 0 else [z, w[d]], axis=0)


def _state_to_pairs(s0):
    b = s0.shape[0]
    st = jnp.swapaxes(s0, -1, -2).reshape(b, 2, N_PAIRS, 2, WKV_N, WKV_N)
    eye = jnp.eye(2, dtype=s0.dtype)
    return jnp.einsum("bdpeji,ef->bdpejfi", st, eye).reshape(b, 2, N_PAIRS, PAIR, PAIR)


def _pairs_to_state(m):
    b = m.shape[0]
    m7 = m.reshape(b, 2, N_PAIRS, 2, WKV_N, 2, WKV_N)
    st = jnp.stack([m7[:, :, :, 0, :, 0, :], m7[:, :, :, 1, :, 1, :]], axis=3)
    return jnp.swapaxes(st.reshape(b, 2, WKV_H, WKV_N, WKV_N), -1, -2)


def _tile_table(groups):
    cv, first, last = [], [], []
    row = 0
    for gi, (n_seq, s_len) in enumerate(groups):
        nt = s_len // ROW_TILE
        for b in range(n_seq):
            for i in range(nt):
                cv.append(0 if gi == 0 else 1 + b)
                first.append(int(i == 0))
                last.append(int(i == nt - 1))
        row += n_seq * s_len
    return jnp.asarray(np.array([cv, first, last], np.int32))


def kernel(x_prompt, x_sample, state_lru, state_wkv, c, c_ctx, w_mod, b_mod, ln1_g, ln1_b, ln2_g, ln2_b,
           w_in_e, w_out_e, sconv_w, sconv_b, w_in_o, w_out_o, lru_conv_w, lru_conv_b, lru_wa, lru_ba,
           lru_wx, lru_bx, lru_lambda, wkv_mu, wkv_w0, wkv_w2, wkv_a0, wkv_a2, wkv_kk, wkv_ka, wkv_rk,
           wkv_g2, wkv_gn_g, wkv_gn_b, peer_wq, peer_keys, peer_u, peer_v):
    bp, sp, d = x_prompt.shape
    bs, ss, _ = x_sample.shape
    depth = w_mod.shape[0]
    assert sp % ROW_TILE == 0 and ss % PEER_TM == 0 and (bp * sp) % ss == 0
    assert bs + 1 <= 8
    groups = ((bp, sp), (bs, ss))
    tp = bp * sp
    tab = _tile_table(groups)

    cv8 = jnp.concatenate([c_ctx[None, :], c, jnp.zeros((8 - 1 - bs, d), F32)], axis=0)
    mods = _modulation(cv8, w_mod, b_mod).reshape(depth, 8, 6, d)

    xs = _add_pos(x_sample, _grid_pos_embed(ss).astype(x_sample.dtype))
    x = jnp.concatenate([x_prompt.reshape(tp, d), xs.reshape(bs * ss, d)], axis=0)

    wc = _channel_dft_table()
    dft = {s: _seq_dft_tables(s) for s in sorted({sp, ss})}
    seg = _segment_ones()
    tri = _chunk_tri()
    masks = _scan_masks()
    row2 = lambda a: a.reshape(1, -1)

    lru_fin = []
    wkv_fin = []
    for l in range(depth):
        j = l // 2
        m_l = mods[l]
        wq = peer_wq[l].astype(BF16)
        keys = peer_keys[l].reshape(2 * PEER_HEADS, N_KEYS, PEER_HALF).astype(BF16)
        if l % 2 == 0:
            zr, pq = _inproj(x, tab, m_l, w_in_e[j].astype(BF16), wc)
            y = jnp.concatenate([_seqdft(pq, bp * sp, 0, bp, sp, *dft[sp]),
                                 _seqdft(pq, bs * ss, tp, bs, ss, *dft[ss])], axis=0)
            x1, h2t, st = _post_even(x, tab, zr, y, m_l, sconv_w[j], row2(sconv_b[j]), w_out_e[j].astype(BF16),
                                     row2(ln1_g[l]), row2(ln1_b[l]), wq, keys)
        else:
            zl, zw = _inproj(x, tab, m_l, w_in_o[j].astype(BF16))
            lru_args = (lru_conv_w[j], lru_conv_b[j],
                        jnp.stack([_block_diag(lru_wa[j, dd]) for dd in range(2)]).astype(BF16), lru_ba[j],
                        jnp.stack([_block_diag(lru_wx[j, dd]) for dd in range(2)]).astype(BF16), lru_bx[j],
                        lru_lambda[j])
            hf_p, hb_p, hfin_p = _lru(zl, 0, bp, sp, jnp.zeros((bp, 2, LRU_W), F32), *lru_args)
            hf_s, hb_s, _ = _lru(zl, tp, bs, ss, state_lru[:, j], *lru_args)
            lru_fin.append(hfin_p)
            prep = _wkvprep(zw, tab, row2(wkv_mu[j]), wkv_w0[j],
                            jnp.stack([_pad_rank(wkv_w2[j], dd) for dd in range(2)]).astype(BF16), wkv_a0[j],
                            jnp.stack([_pad_rank(wkv_a2[j], dd) for dd in range(2)]).astype(BF16),
                            wkv_kk[j], wkv_ka[j], row2(wkv_rk[j]), wkv_g2[j].astype(BF16), seg, tri)
            v, bonus, g = prep[0], prep[1], prep[2]
            yf_p, yb_p, mfin_p = _wkvscan(prep[3:], v, 0, bp, sp,
                                          jnp.zeros((bp, 2, N_PAIRS, PAIR, PAIR), F32), masks)
            yf_s, yb_s, _ = _wkvscan(prep[3:], v, tp, bs, ss, _state_to_pairs(state_wkv[:, j]), masks)
            wkv_fin.append(_pairs_to_state(mfin_p))
            cat = lambda a, b: jnp.concatenate([a, b], axis=0)
            x1, h2t, st = _post_odd(x, tab, cat(hf_p, hf_s), cat(hb_p, hb_s), zl, cat(yf_p, yf_s), cat(yb_p, yb_s),
                                    bonus, g, m_l, seg, row2(wkv_gn_g[j]), row2(wkv_gn_b[j]),
                                    w_out_o[j].astype(BF16), row2(ln1_g[l]), row2(ln1_b[l]), wq, keys)
        st4 = st.reshape(PEER_HEADS, 2, N_KEYS, st.shape[-1])
        th1, p1, p2, s2 = _topk(st4)
        x = _peer(x1, tab, h2t, s2, p2, th1, p1, peer_u[l].astype(BF16), peer_v[l].T.astype(BF16), m_l,
                  row2(ln2_g[l]), row2(ln2_b[l]))

    y_prompt = x[:tp].reshape(bp, sp, d)
    y_sample = x[tp:].reshape(bs, ss, d)
    return (y_prompt, y_sample, jnp.stack(lru_fin, 1).astype(x_prompt.dtype),
            jnp.stack(wkv_fin, 1).astype(x_prompt.dtype))
```

```python
import functools
import math

import numpy as np
import jax
import jax.numpy as jnp
from jax import lax
from jax.experimental import pallas as pl
from jax.experimental.pallas import tpu as pltpu

F32 = jnp.float32
BF16 = jnp.bfloat16

D_MODEL = 1024
GRID_W = 64
FNET_W = 512
FNET_GROUPS = 4
FNET_GW = FNET_W // FNET_GROUPS
CONV_W = 512
LRU_W = 512
LRU_HEADS = 8
LRU_CONV_K = 4
LRU_C = 8.0
WKV_W = 512
WKV_N = 64
WKV_H = 8
WKV_IN = 1920
DECAY_SCALE = math.exp(-0.5)
WKV_GN_EPS = 64e-5
PEER_HEADS = 8
N_KEYS = 128
PEER_TOPK = 16
PEER_HALF = 128
DEPTH = 4
ALPHA = (2 * DEPTH) ** 0.25
LN_EPS = 1e-6

ROW_TILE = 256
LANES = 128
HALO = 8
CHUNK = 64
PAIR = 2 * WKV_N
N_PAIRS = WKV_H // 2
PEER_TM = 512
PEER_TE = 1024
TOPK_TL = 256
VMEM_LIMIT = 56 * 1024 * 1024


def _cparams(sem):
    return pltpu.CompilerParams(dimension_semantics=sem, vmem_limit_bytes=VMEM_LIMIT)


def _dot(a, b):
    return jnp.dot(a, b, preferred_element_type=F32)


def _dot_nt(a, b):
    return lax.dot_general(a, b, (((1,), (1,)), ((), ())), preferred_element_type=F32)


def _mm3(a, b):
    ah = a.astype(BF16)
    al = (a - ah.astype(F32)).astype(BF16)
    bh = b.astype(BF16)
    bl = (b - bh.astype(F32)).astype(BF16)
    return _dot(ah, bh) + (_dot(ah, bl) + _dot(al, bh))


def _mm1(a, b):
    return _dot(a.astype(BF16), b.astype(BF16))


def _mm01(a, b01):
    h = a.astype(BF16)
    r = a - h.astype(F32)
    m = r.astype(BF16)
    lo = (r - m.astype(F32)).astype(BF16)
    return _dot(h, b01) + (_dot(m, b01) + _dot(lo, b01))


def _ln(x):
    mu = jnp.mean(x, axis=-1, keepdims=True)
    xc = x - mu
    var = jnp.mean(xc * xc, axis=-1, keepdims=True)
    return xc * lax.rsqrt(var + LN_EPS)


def _gelu(x):
    z = x * (0.7978845608028654 + 0.035677408136300125 * (x * x))
    hx = 0.5 * x
    return hx + hx * jnp.tanh(z)


def _sigmoid(x):
    return 1.0 / (1.0 + jnp.exp(-x))


def _mod_kernel(c_ref, w_ref, b_ref, o_ref):
    c = c_ref[...]
    sc = c * _sigmoid(c)
    o_ref[0] = _dot(sc.astype(BF16), w_ref[0].astype(BF16)) + b_ref[0]


def _modulation(cv8, w_mod, b_mod):
    depth, d, n = w_mod.shape
    tn = 1536
    return pl.pallas_call(
        _mod_kernel,
        grid=(depth, n // tn),
        in_specs=[pl.BlockSpec((8, d), lambda l, j: (0, 0)),
                  pl.BlockSpec((1, d, tn), lambda l, j: (l, 0, j)),
                  pl.BlockSpec((1, 1, tn), lambda l, j: (l, 0, j))],
        out_specs=pl.BlockSpec((1, 8, tn), lambda l, j: (l, 0, j)),
        out_shape=jax.ShapeDtypeStruct((depth, 8, n), F32),
        compiler_params=_cparams(("arbitrary", "arbitrary")),
        name="modulation",
    )(cv8, w_mod, b_mod.reshape(depth, 1, n))


def _addpos_kernel(x_ref, p_ref, o_ref):
    o_ref[0] = x_ref[0] + p_ref[...]


def _add_pos(x, pos):
    b, s, d = x.shape
    ts = 512
    return pl.pallas_call(
        _addpos_kernel,
        grid=(s // ts, b),
        in_specs=[pl.BlockSpec((1, ts, d), lambda i, j: (j, i, 0)),
                  pl.BlockSpec((ts, d), lambda i, j: (i, 0))],
        out_specs=pl.BlockSpec((1, ts, d), lambda i, j: (j, i, 0)),
        out_shape=jax.ShapeDtypeStruct(x.shape, x.dtype),
        compiler_params=_cparams(("arbitrary", "arbitrary")),
        name="add_pos",
    )(x, pos)


def _inproj_even_kernel(tab_ref, x_ref, m_ref, w_ref, wc_ref, zr_ref, pq_ref):
    m = m_ref[0]
    h = _ln(x_ref[...]) * (1.0 + m[1:2]) + m[0:1]
    z = _dot(h.astype(BF16), w_ref[...])
    zr_ref[...] = z[:, FNET_W:]
    pq_ref[...] = _dot(z[:, :FNET_W].astype(BF16), wc_ref[...]).astype(BF16)


def _inproj_odd_kernel(tab_ref, x_ref, m_ref, w_ref, zl_ref, zw_ref):
    m = m_ref[0]
    h = _ln(x_ref[...]) * (1.0 + m[1:2]) + m[0:1]
    z = _dot(h.astype(BF16), w_ref[...])
    zl_ref[...] = z[:, :2 * LRU_W]
    zw_ref[...] = z[:, 2 * LRU_W:]


def _const_spec(shape):
    nd = len(shape)
    return pl.BlockSpec(shape, lambda i, t, _n=nd: (0,) * _n)


def _inproj(x, tab, mods, w_in, wc=None):
    t, d = x.shape
    n = w_in.shape[1]
    tm = ROW_TILE
    even = wc is not None
    in_specs = [pl.BlockSpec((tm, d), lambda i, t_: (i, 0)),
                pl.BlockSpec((1, 6, d), lambda i, t_: (t_[0, i], 0, 0)),
                _const_spec((d, n))]
    if even:
        in_specs.append(_const_spec(wc.shape))
        widths = (n - FNET_W, 2 * FNET_W)
        dtypes = (F32, BF16)
        kern = _inproj_even_kernel
        args = (tab, x, mods, w_in, wc)
    else:
        widths = (2 * LRU_W, n - 2 * LRU_W)
        dtypes = (F32, F32)
        kern = _inproj_odd_kernel
        args = (tab, x, mods, w_in)
    return pl.pallas_call(
        kern,
        grid_spec=pltpu.PrefetchScalarGridSpec(
            num_scalar_prefetch=1, grid=(t // tm,), in_specs=in_specs,
            out_specs=[pl.BlockSpec((tm, w), lambda i, t_: (i, 0)) for w in widths]),
        out_shape=[jax.ShapeDtypeStruct((t, w), dt) for w, dt in zip(widths, dtypes)],
        compiler_params=_cparams(("arbitrary",)),
        name="inproj_even" if even else "inproj_odd",
    )(*args)


def _seqdft_kernel(c_ref, s_ref, p_ref, q_ref, y_ref):
    y_ref[...] = (_dot(c_ref[...], p_ref[...]) + _dot(s_ref[...], q_ref[...])).astype(BF16)


def _seqdft(pq, y_rows, row0, n_seq, s_len, cmat, smat):
    tm = min(s_len, ROW_TILE)
    nt = s_len // tm
    blk0 = row0 // s_len
    return pl.pallas_call(
        _seqdft_kernel,
        grid=(n_seq, nt),
        in_specs=[pl.BlockSpec((tm, s_len), lambda b, i: (i, 0)),
                  pl.BlockSpec((tm, s_len), lambda b, i: (i, 0)),
                  pl.BlockSpec((s_len, FNET_W), lambda b, i: (blk0 + b, 0)),
                  pl.BlockSpec((s_len, FNET_W), lambda b, i: (blk0 + b, 1))],
        out_specs=pl.BlockSpec((tm, FNET_W), lambda b, i: (b * nt + i, 0)),
        out_shape=jax.ShapeDtypeStruct((y_rows, FNET_W), BF16),
        compiler_params=_cparams(("arbitrary", "arbitrary")),
        name="seqdft_%d" % s_len,
    )(cmat, smat, pq, pq)


def _post_and_query(x, mix, m, g1, b1, wq_ref, keys_ref, x1_ref, h2t_ref, st_ref):
    x1 = _ln(ALPHA * x + m[2:3] * mix) * g1 + b1
    x1_ref[...] = x1
    h2 = _ln(x1) * (1.0 + m[4:5]) + m[3:4]
    h2b = h2.astype(BF16)
    h2t_ref[...] = h2.T.astype(BF16)
    q = _dot(h2b, wq_ref[...])
    for hp in range(2 * PEER_HEADS):
        qb = q[:, hp * PEER_HALF:(hp + 1) * PEER_HALF].astype(BF16)
        st_ref[hp * N_KEYS:(hp + 1) * N_KEYS, :] = _dot_nt(keys_ref[hp], qb)


def _post_even_kernel(tab_ref, x_ref, bg_ref, cg_ref, xi_ref, cgp_ref, xip_ref, cgn_ref, xin_ref, y_ref,
                      m_ref, cw_ref, cb_ref, wo_ref, g1_ref, b1_ref, wq_ref, keys_ref,
                      x1_ref, h2t_ref, st_ref, ext_ref):
    i = pl.program_id(0)
    tm = ROW_TILE
    first = tab_ref[1, i] == 1
    last = tab_ref[2, i] == 1
    u = cg_ref[...] * xi_ref[...]
    ext_ref[0:HALO] = jnp.where(first, 0.0, cgp_ref[...] * xip_ref[...])
    ext_ref[HALO:HALO + tm] = u
    ext_ref[HALO + tm:2 * HALO + tm] = jnp.where(last, 0.0, cgn_ref[...] * xin_ref[...])
    cw = cw_ref[...]
    conv = (ext_ref[HALO - 1:HALO - 1 + tm] * cw[0:1] + u * cw[1:2]
            + ext_ref[HALO + 1:HALO + 1 + tm] * cw[2:3] + cb_ref[...])
    ymix = (bg_ref[...] * conv).astype(BF16)
    mix = _dot(y_ref[...], wo_ref[0:FNET_W]) + _dot(ymix, wo_ref[FNET_W:])
    _post_and_query(x_ref[...], mix, m_ref[0], g1_ref[...], b1_ref[...], wq_ref, keys_ref,
                    x1_ref, h2t_ref, st_ref)


def _post_out_specs(t, d, tm):
    nsc = 2 * PEER_HEADS * N_KEYS
    specs = [pl.BlockSpec((tm, d), lambda i, t_: (i, 0)),
             pl.BlockSpec((d, tm), lambda i, t_: (0, i)),
             pl.BlockSpec((nsc, tm), lambda i, t_: (0, i))]
    shapes = [jax.ShapeDtypeStruct((t, d), F32),
              jax.ShapeDtypeStruct((d, t), BF16),
              jax.ShapeDtypeStruct((nsc, t), F32)]
    return specs, shapes


def _post_even(x, tab, zr, y, mods, cw, cb, wo, g1, b1, wq, keys):
    t, d = x.shape
    tm = ROW_TILE
    hb = tm // HALO
    nblk = t // HALO
    prev = lambda c: pl.BlockSpec((HALO, CONV_W), lambda i, t_, _c=c: (jnp.maximum(i * hb - 1, 0), _c))
    nxt = lambda c: pl.BlockSpec((HALO, CONV_W), lambda i, t_, _c=c: (jnp.minimum((i + 1) * hb, nblk - 1), _c))
    col = lambda c: pl.BlockSpec((tm, CONV_W), lambda i, t_, _c=c: (i, _c))
    in_specs = [pl.BlockSpec((tm, d), lambda i, t_: (i, 0)),
                col(0), col(1), col(2), prev(1), prev(2), nxt(1), nxt(2),
                pl.BlockSpec((tm, FNET_W), lambda i, t_: (i, 0)),
                pl.BlockSpec((1, 6, d), lambda i, t_: (t_[0, i], 0, 0)),
                _const_spec(cw.shape), _const_spec(cb.shape), _const_spec(wo.shape),
                _const_spec(g1.shape), _const_spec(b1.shape), _const_spec(wq.shape), _const_spec(keys.shape)]
    out_specs, out_shape = _post_out_specs(t, d, tm)
    return pl.pallas_call(
        _post_even_kernel,
        grid_spec=pltpu.PrefetchScalarGridSpec(
            num_scalar_prefetch=1, grid=(t // tm,), in_specs=in_specs, out_specs=out_specs,
            scratch_shapes=[pltpu.VMEM((tm + 2 * HALO, CONV_W), F32)]),
        out_shape=out_shape,
        compiler_params=_cparams(("arbitrary",)),
        name="post_even",
    )(tab, x, zr, zr, zr, zr, zr, zr, zr, y, mods, cw, cb, wo, g1, b1, wq, keys)


def _lru_kernel(xf_ref, xfh_ref, xb_ref, xbh_ref, cw_ref, cb_ref, wa_ref, ba_ref, wx_ref, bx_ref, lam_ref,
                h0_ref, hf_ref, hb_ref, hfin_ref, ext_ref, carry_ref):
    i = pl.program_id(1)
    nt = pl.num_programs(1)
    tm = ROW_TILE

    @pl.when(i == 0)
    def _():
        carry_ref[...] = h0_ref[0]

    row = lax.broadcasted_iota(jnp.int32, (tm, 1), 0)
    for d in range(2):
        x = (xf_ref if d == 0 else xb_ref)[...]
        halo = jnp.where(i == 0, 0.0, (xfh_ref if d == 0 else xbh_ref)[...])
        ext_ref[HALO:HALO + tm] = x
        if d == 0:
            ext_ref[0:HALO] = halo
        else:
            ext_ref[HALO + tm:2 * HALO + tm] = halo
        cw = cw_ref[d]
        xc = cb_ref[d:d + 1] + x * cw[LRU_CONV_K - 1:LRU_CONV_K]
        for j in range(LRU_CONV_K - 1):
            k = LRU_CONV_K - 1 - j
            off = HALO - k if d == 0 else HALO + k
            xc = xc + ext_ref[off:off + tm] * cw[j:j + 1]
        xcb = xc.astype(BF16)
        gate_r = _sigmoid(_dot(xcb, wa_ref[d]) + ba_ref[d:d + 1])
        gate_i = _sigmoid(_dot(xcb, wx_ref[d]) + bx_ref[d:d + 1])
        nl = -lam_ref[d:d + 1]
        softplus = jnp.maximum(nl, 0.0) + jnp.log1p(jnp.exp(-jnp.abs(nl)))
        log_a = -LRU_C * gate_r * softplus
        a = jnp.exp(log_a)
        b = jnp.sqrt(-jnp.tanh(log_a) * (a * a + 1.0)) * (gate_i * xc)
        s = 1
        while s < tm:
            if d == 0:
                keep = row >= s
                sh = s
            else:
                keep = row < tm - s
                sh = tm - s
            a_sh = jnp.where(keep, pltpu.roll(a, sh, 0), 1.0)
            b_sh = jnp.where(keep, pltpu.roll(b, sh, 0), 0.0)
            b = a * b_sh + b
            a = a * a_sh
            s *= 2
        h = a * carry_ref[d:d + 1] + b
        if d == 0:
            hf_ref[...] = h
            carry_ref[0:1] = h[tm - 1:tm]
        else:
            hb_ref[...] = h
            carry_ref[1:2] = h[0:1]

    @pl.when(i == nt - 1)
    def _():
        hfin_ref[0] = carry_ref[...]


def _lru(zl, row0, n_seq, s_len, h0, cw, cb, wa, ba, wx, bx, lam):
    tm = ROW_TILE
    nt = s_len // tm
    t0 = row0 // tm
    hb = tm // HALO
    nblk = zl.shape[0] // HALO
    rows = n_seq * s_len
    fwd = lambda b, i: (t0 + b * nt + i, 0)
    bwd = lambda b, i: (t0 + b * nt + nt - 1 - i, 0)
    fwd_h = lambda b, i: (jnp.maximum((t0 + b * nt + i) * hb - 1, 0), 0)
    bwd_h = lambda b, i: (jnp.minimum((t0 + b * nt + nt - i) * hb, nblk - 1), 0)
    out_f = lambda b, i: (b * nt + i, 0)
    out_b = lambda b, i: (b * nt + nt - 1 - i, 0)
    cst = lambda a: pl.BlockSpec(a.shape, lambda b, i, _n=a.ndim: (0,) * _n)
    return pl.pallas_call(
        _lru_kernel,
        grid=(n_seq, nt),
        in_specs=[pl.BlockSpec((tm, LRU_W), fwd), pl.BlockSpec((HALO, LRU_W), fwd_h),
                  pl.BlockSpec((tm, LRU_W), bwd), pl.BlockSpec((HALO, LRU_W), bwd_h),
                  cst(cw), cst(cb), cst(wa), cst(ba), cst(wx), cst(bx), cst(lam),
                  pl.BlockSpec((1, 2, LRU_W), lambda b, i: (b, 0, 0))],
        out_specs=[pl.BlockSpec((tm, LRU_W), out_f), pl.BlockSpec((tm, LRU_W), out_b),
                   pl.BlockSpec((1, 2, LRU_W), lambda b, i: (b, 0, 0))],
        out_shape=[jax.ShapeDtypeStruct((rows, LRU_W), F32), jax.ShapeDtypeStruct((rows, LRU_W), F32),
                   jax.ShapeDtypeStruct((n_seq, 2, LRU_W), F32)],
        scratch_shapes=[pltpu.VMEM((tm + 2 * HALO, LRU_W), F32), pltpu.VMEM((2, LRU_W), F32)],
        compiler_params=_cparams(("arbitrary", "arbitrary")),
        name="lru_%d" % s_len,
    )(zl, zl, zl, zl, cw, cb, wa, ba, wx, bx, lam, h0)


def _wkvprep_kernel(tab_ref, z_ref, zp_ref, zn_ref, mu_ref, w0_ref, w2_ref, a0_ref, a2_ref, kk_ref, ka_ref,
                    rk_ref, g2_ref, seg_ref, tri_ref,
                    v_ref, bg_ref, g_ref, rf_ref, kf_ref, bf_ref, qf_ref, cf_ref,
                    rb_ref, kb_ref, bb_ref, qb_ref, cb_ref, ext_ref):
    i = pl.program_id(0)
    tm = ROW_TILE
    w = WKV_W
    first = tab_ref[1, i] == 1
    last = tab_ref[2, i] == 1
    z = z_ref[...]
    ext_ref[0:HALO] = jnp.where(first, 0.0, zp_ref[...])
    ext_ref[HALO:HALO + tm] = z
    ext_ref[HALO + tm:2 * HALO + tm] = jnp.where(last, 0.0, zn_ref[...])
    z = z + mu_ref[...] * (0.5 * (ext_ref[HALO - 1:HALO - 1 + tm] + ext_ref[HALO + 1:HALO + 1 + tm]) - z)
    r = z[:, 0:w]
    k = z[:, w:2 * w]
    v = z[:, 2 * w:3 * w]
    wd = jnp.tanh(z[:, 3 * w:3 * w + 128]).astype(BF16)
    ad = z[:, 3 * w + 128:3 * w + 256].astype(BF16)
    gd = _sigmoid(z[:, 3 * w + 256:3 * w + 384]).astype(BF16)
    v_ref[...] = v
    g_ref[...] = _dot(gd, g2_ref[...])
    seg = seg_ref[...]
    rk = rk_ref[...]
    bonus = jnp.zeros((tm, w), F32)
    outs = ((rf_ref, kf_ref, bf_ref, qf_ref, cf_ref), (rb_ref, kb_ref, bb_ref, qb_ref, cb_ref))
    nch = tm // CHUNK
    for d in range(2):
        r_ref, k_ref, b_ref, q_ref, c_ref = outs[d]
        wz = w0_ref[d:d + 1] + _dot(wd, w2_ref[d])
        logw = -DECAY_SCALE * _sigmoid(wz)
        iclr = _sigmoid(a0_ref[d:d + 1] + _dot(ad, a2_ref[d]))
        kk = k * kk_ref[d:d + 1]
        kk = kk * lax.rsqrt(jnp.maximum(_mm01(kk * kk, seg), 1e-24))
        km = k * (1.0 + (iclr - 1.0) * ka_ref[d:d + 1])
        bonus = bonus + _mm01(r * km * rk, seg) * v
        lw_h = logw.astype(BF16)
        lw_r = logw - lw_h.astype(F32)
        lw_m = lw_r.astype(BF16)
        lw_l = (lw_r - lw_m.astype(F32)).astype(BF16)
        tri = tri_ref[d]
        cl = _dot(tri, lw_h) + (_dot(tri, lw_m) + _dot(tri, lw_l))
        c = jnp.exp(cl)
        cinv = jnp.exp(-cl)
        r_ref[...] = r * c
        k_ref[...] = km * cinv
        b_ref[...] = kk * iclr * cinv
        q_ref[...] = kk * jnp.exp(cl - logw)
        for j in range(nch):
            edge = (j + 1) * CHUNK - 1 if d == 0 else j * CHUNK
            c_ref[j] = c[edge:edge + 1]
    bg_ref[...] = bonus


def _wkvprep(zw, tab, mu, w0, w2p, a0, a2p, kk, ka, rk, g2, seg, tri):
    t, n = zw.shape
    tm = ROW_TILE
    hb = tm // HALO
    nblk = t // HALO
    nch = tm // CHUNK
    w = WKV_W
    in_specs = [pl.BlockSpec((tm, n), lambda i, t_: (i, 0)),
                pl.BlockSpec((HALO, n), lambda i, t_: (jnp.maximum(i * hb - 1, 0), 0)),
                pl.BlockSpec((HALO, n), lambda i, t_: (jnp.minimum((i + 1) * hb, nblk - 1), 0))]
    in_specs += [_const_spec(a.shape) for a in (mu, w0, w2p, a0, a2p, kk, ka, rk, g2, seg, tri)]
    row = pl.BlockSpec((tm, w), lambda i, t_: (i, 0))
    cspec = pl.BlockSpec((nch, 1, w), lambda i, t_: (i, 0, 0))
    rshape = jax.ShapeDtypeStruct((t, w), F32)
    cshape = jax.ShapeDtypeStruct((t // CHUNK, 1, w), F32)
    out_specs = [row, row, row] + [row, row, row, row, cspec] * 2
    out_shape = [rshape, rshape, rshape] + [rshape, rshape, rshape, rshape, cshape] * 2
    return pl.pallas_call(
        _wkvprep_kernel,
        grid_spec=pltpu.PrefetchScalarGridSpec(
            num_scalar_prefetch=1, grid=(t // tm,), in_specs=in_specs, out_specs=out_specs,
            scratch_shapes=[pltpu.VMEM((tm + 2 * HALO, n), F32)]),
        out_shape=out_shape,
        compiler_params=_cparams(("arbitrary",)),
        name="wkv_prep",
    )(tab, zw, zw, zw, mu, w0, w2p, a0, a2p, kk, ka, rk, g2, seg, tri)


def _wkv_chunks(items, eye):
    lane = lax.broadcasted_iota(jnp.int32, (1, PAIR), 1)
    m0 = (lane < WKV_N).astype(F32)
    m1 = 1.0 - m0
    stack = lambda x: jnp.concatenate([x * m0, x * m1], axis=0)
    n = range(len(items))
    rh_s = [stack(it[0]) for it in items]
    kh_s = [stack(it[1]) for it in items]
    bh_s = [stack(it[2]) for it in items]
    kq_s = [stack(it[3]) for it in items]
    v_s = [stack(it[4]) for it in items]
    bh_t = [x.T for x in bh_s]
    kh_t = [x.T for x in kh_s]
    bk_t = [jnp.concatenate([bh_t[i], kh_t[i]], axis=1) for i in n]
    att_s = [_mm3(kq_s[i], bk_t[i]) for i in n]
    att_y = [_mm1(rh_s[i], bk_t[i]) for i in n]
    n1 = [att_s[i][:, 0:PAIR] * items[i][7] for i in n]
    ak = [att_s[i][:, PAIR:] * items[i][7] for i in n]
    gb = [att_y[i][:, 0:PAIR] * items[i][8] for i in n]
    gk = [att_y[i][:, PAIR:] * items[i][8] for i in n]
    t_inv = [eye - n1[i] for i in n]
    npow = n1
    for _ in range(5):
        npow = [_mm3(npow[i], npow[i]) for i in n]
        t_inv = [t_inv[i] + _mm3(t_inv[i], npow[i]) for i in n]
    kv = [_mm3(jnp.concatenate([kh_t[i], ak[i]], axis=0), v_s[i]) for i in n]
    gkv = [_mm1(gk[i], v_s[i]) for i in n]
    x = [_mm3(t_inv[i], jnp.concatenate([kq_s[i], kv[i][PAIR:]], axis=1)) for i in n]
    bx = [_mm3(bh_t[i], x[i]) for i in n]
    gx = [_mm1(gb[i], x[i]) for i in n]
    p_mat = [eye - bx[i][:, 0:PAIR] for i in n]
    q_mat = [kv[i][0:PAIR] - bx[i][:, PAIR:] for i in n]
    r_til = [rh_s[i] - gx[i][:, 0:PAIR] for i in n]
    y0 = [gkv[i] - gx[i][:, PAIR:] for i in n]
    pm = [_mm3(p_mat[i], items[i][6]) for i in n]
    y_st = [_mm1(r_til[i], items[i][6]) + y0[i] for i in n]
    ys = [y_st[i][0:CHUNK] + y_st[i][CHUNK:] for i in n]
    c_col = [jnp.broadcast_to(items[i][5], (PAIR, PAIR)).T for i in n]
    m_new = [(pm[i] + q_mat[i]) * c_col[i] for i in n]
    return ys, m_new


def _wkvscan_kernel(rf_ref, kf_ref, bf_ref, qf_ref, vf_ref, cf_ref, rb_ref, kb_ref, bb_ref, qb_ref, vb_ref, cb_ref,
                    m0_ref, msk_ref, yf_ref, yb_ref, mfin_ref, m_ref):
    i = pl.program_id(1)
    nc = pl.num_programs(1)

    @pl.when(i == 0)
    def _():
        m_ref[...] = m0_ref[0]

    eye = msk_ref[4]
    ins = ((rf_ref, kf_ref, bf_ref, qf_ref, vf_ref, cf_ref, yf_ref),
           (rb_ref, kb_ref, bb_ref, qb_ref, vb_ref, cb_ref, yb_ref))
    items = []
    for d in range(2):
        r_ref, k_ref, b_ref, q_ref, v_ref, c_ref, _ = ins[d]
        for p in range(N_PAIRS):
            sl = slice(p * PAIR, (p + 1) * PAIR)
            items.append((r_ref[:, sl], k_ref[:, sl], b_ref[:, sl], q_ref[:, sl], v_ref[:, sl],
                          c_ref[0, :, sl], m_ref[d, p], msk_ref[2 * d], msk_ref[2 * d + 1]))
    ys, m_new = _wkv_chunks(items, eye)
    for d in range(2):
        for p in range(N_PAIRS):
            sl = slice(p * PAIR, (p + 1) * PAIR)
            ins[d][6][:, sl] = ys[d * N_PAIRS + p]
            m_ref[d, p] = m_new[d * N_PAIRS + p]

    @pl.when(i == nc - 1)
    def _():
        mfin_ref[0] = m_ref[...]


def _wkvscan(prep, v, row0, n_seq, s_len, m0, masks):
    rf, kf, bf, qf, cf, rb, kb, bb, qb, cb = prep
    nc = s_len // CHUNK
    c0 = row0 // CHUNK
    rows = n_seq * s_len
    w = WKV_W
    fwd = lambda b, i: (c0 + b * nc + i, 0)
    bwd = lambda b, i: (c0 + b * nc + nc - 1 - i, 0)
    fwd3 = lambda b, i: (c0 + b * nc + i, 0, 0)
    bwd3 = lambda b, i: (c0 + b * nc + nc - 1 - i, 0, 0)
    blk = lambda im: pl.BlockSpec((CHUNK, w), im)
    cblk = lambda im: pl.BlockSpec((1, 1, w), im)
    mspec = pl.BlockSpec((1, 2, N_PAIRS, PAIR, PAIR), lambda b, i: (b, 0, 0, 0, 0))
    return pl.pallas_call(
        _wkvscan_kernel,
        grid=(n_seq, nc),
        in_specs=[blk(fwd)] * 5 + [cblk(fwd3)] + [blk(bwd)] * 5 + [cblk(bwd3)]
        + [mspec, pl.BlockSpec(masks.shape, lambda b, i: (0, 0, 0))],
        out_specs=[pl.BlockSpec((CHUNK, w), lambda b, i: (b * nc + i, 0)),
                   pl.BlockSpec((CHUNK, w), lambda b, i: (b * nc + nc - 1 - i, 0)),
                   mspec],
        out_shape=[jax.ShapeDtypeStruct((rows, w), F32), jax.ShapeDtypeStruct((rows, w), F32),
                   jax.ShapeDtypeStruct(m0.shape, F32)],
        scratch_shapes=[pltpu.VMEM((2, N_PAIRS, PAIR, PAIR), F32)],
        compiler_params=_cparams(("arbitrary", "arbitrary")),
        name="wkv_scan_%d" % s_len,
    )(rf, kf, bf, qf, v, cf, rb, kb, bb, qb, v, cb, m0, masks)


def _post_odd_kernel(tab_ref, x_ref, hf_ref, hb_ref, gb_ref, yf_ref, yb_ref, bon_ref, g_ref,
                     m_ref, seg_ref, gng_ref, gnb_ref, wo_ref, g1_ref, b1_ref, wq_ref, keys_ref,
                     x1_ref, h2t_ref, st_ref):
    y_lru = ((hf_ref[...] + hb_ref[...]) * _gelu(gb_ref[...])).astype(BF16)
    seg = seg_ref[...]
    ys = yf_ref[...] + yb_ref[...]
    mean = _mm01(ys, seg) * (1.0 / WKV_N)
    yc = ys - mean
    var = _mm01(yc * yc, seg) * (1.0 / WKV_N)
    yn = yc * lax.rsqrt(var + WKV_GN_EPS) * gng_ref[...] + gnb_ref[...]
    y_wkv = ((yn + bon_ref[...]) * g_ref[...]).astype(BF16)
    mix = _dot(y_lru, wo_ref[0:LRU_W]) + _dot(y_wkv, wo_ref[LRU_W:])
    _post_and_query(x_ref[...], mix, m_ref[0], g1_ref[...], b1_ref[...], wq_ref, keys_ref,
                    x1_ref, h2t_ref, st_ref)


def _post_odd(x, tab, hf, hb, zl, yf, yb, bonus, g, mods, seg, gng, gnb, wo, g1, b1, wq, keys):
    t, d = x.shape
    tm = ROW_TILE
    w = WKV_W
    row = pl.BlockSpec((tm, w), lambda i, t_: (i, 0))
    in_specs = [pl.BlockSpec((tm, d), lambda i, t_: (i, 0)), row, row,
                pl.BlockSpec((tm, LRU_W), lambda i, t_: (i, 1)), row, row, row, row,
                pl.BlockSpec((1, 6, d), lambda i, t_: (t_[0, i], 0, 0))]
    in_specs += [_const_spec(a.shape) for a in (seg, gng, gnb, wo, g1, b1, wq, keys)]
    out_specs, out_shape = _post_out_specs(t, d, tm)
    return pl.pallas_call(
        _post_odd_kernel,
        grid_spec=pltpu.PrefetchScalarGridSpec(
            num_scalar_prefetch=1, grid=(t // tm,), in_specs=in_specs, out_specs=out_specs),
        out_shape=out_shape,
        compiler_params=_cparams(("arbitrary",)),
        name="post_odd",
    )(tab, x, hf, hb, zl, yf, yb, bonus, g, mods, seg, gng, gnb, wo, g1, b1, wq, keys)


N_TOP = PEER_TOPK + 1
TOP_ROWS = 24
SUBLANES = 8


def _batcher_network(n):
    def merge(lo, hi, r):
        step = r * 2
        if step < hi - lo:
            yield from merge(lo, hi, step)
            yield from merge(lo + r, hi, step)
            yield from [(i, i + r) for i in range(lo + r, hi - r, step)]
        else:
            yield (lo, lo + r)

    def sort(lo, hi):
        if hi - lo >= 1:
            mid = lo + (hi - lo) // 2
            yield from sort(lo, mid)
            yield from sort(mid + 1, hi)
            yield from merge(lo, hi, 1)

    return tuple(sort(0, n - 1))


def _sort_levels(levels, net):
    lv = list(levels)
    for i, j in net:
        a, b = lv[i], lv[j]
        lv[i] = jnp.maximum(a, b)
        lv[j] = jnp.minimum(a, b)
    return lv


def _pop_sorted(levels, n_top, emit):
    lv = list(levels)
    for it in range(n_top):
        m = jnp.max(lv[0], axis=0, keepdims=True)
        emit(it, m)
        hit = lv[0] == m
        live = min(len(lv), n_top - it - 1)
        for k in range(live):
            below = lv[k + 1] if k + 1 < len(lv) else -jnp.inf
            lv[k] = jnp.where(hit, below, lv[k])


def _topk_kernel(s_ref, th1_ref, p1_ref, p2_ref, s2_ref, v1_ref, v2_ref):
    neg = -jnp.inf
    net16 = _batcher_network(N_KEYS // SUBLANES)
    net8 = _batcher_network(SUBLANES)
    row = lax.broadcasted_iota(jnp.int32, (SUBLANES, TOPK_TL), 0)
    roll = lambda x, sh: pltpu.roll(x, sh, 0)

    def top_values(x, store):
        store[...] = jnp.full(store.shape, neg, F32)
        lv = _sort_levels([x[SUBLANES * k:SUBLANES * (k + 1)] for k in range(N_KEYS // SUBLANES)], net16)

        def emit(it, m):
            store[it:it + 1, :] = m

        _pop_sorted(lv, N_TOP, emit)

    def head(h, carry):
        s1 = s_ref[h, 0]
        s2 = s_ref[h, 1]
        top_values(s1, v1_ref)
        top_values(s2, v2_ref)
        one = lambda ref, a: ref[a:a + 1, :]
        v2a = v2_ref[0:8, :]
        v1b2 = roll(v1_ref[8:16, :], 2)
        cands = [one(v1_ref, 0) + v2a, one(v1_ref, 0) + v2_ref[8:16, :], one(v1_ref, 0) + v2_ref[16:24, :],
                 one(v1_ref, 1) + v2a,
                 jnp.where(row < 5, one(v1_ref, 2), one(v1_ref, 4)) + jnp.where(row < 5, v2a, roll(v2a, 5)),
                 jnp.where(row < 4, one(v1_ref, 3), jnp.where(row < 6, one(v1_ref, 5), one(v1_ref, 6)))
                 + jnp.where(row < 4, v2a, jnp.where(row < 6, roll(v2a, 4), roll(v2a, 6))),
                 jnp.where(row < 2, one(v1_ref, 7), v1b2) + jnp.where(row < 2, v2a, one(v2_ref, 0)),
                 jnp.where(row < 2, v1b2, roll(v1_ref[16:24, :], 2)) + one(v2_ref, 0)]
        top = []
        _pop_sorted(_sort_levels(cands, net8), N_TOP, lambda it, m: top.append(m))
        tau = 0.5 * (top[PEER_TOPK - 1] + top[PEER_TOPK])
        mx1 = v1_ref[0:1, :]
        mx2 = v2_ref[0:1, :]
        zacc = jnp.zeros((SUBLANES, TOPK_TL), F32)
        for cnd in cands:
            zacc = zacc + jnp.where(cnd >= tau, jnp.exp(cnd - (mx1 + mx2)), 0.0)
        zsum = jnp.sum(zacc, axis=0, keepdims=True)
        th1 = tau - s1
        p1 = jnp.exp(s1 - mx1) / zsum
        p2 = jnp.exp(s2 - mx2)
        for cb in range(TOPK_TL // LANES):
            cs = slice(cb * LANES, (cb + 1) * LANES)
            th1_ref[cb, h] = th1[:, cs]
            p1_ref[cb, h] = p1[:, cs]
            p2_ref[cb, h] = p2[:, cs]
            s2_ref[cb, h] = s2[:, cs]
        return carry

    lax.fori_loop(0, PEER_HEADS, head, 0)


def _topk(st4):
    t = st4.shape[-1]
    tl = TOPK_TL
    out = jax.ShapeDtypeStruct((t // LANES, PEER_HEADS, N_KEYS, LANES), F32)
    ospec = pl.BlockSpec((tl // LANES, PEER_HEADS, N_KEYS, LANES), lambda i: (i, 0, 0, 0))
    return pl.pallas_call(
        _topk_kernel,
        grid=(t // tl,),
        in_specs=[pl.BlockSpec((PEER_HEADS, 2, N_KEYS, tl), lambda i: (0, 0, 0, i))],
        out_specs=[ospec, ospec, ospec, ospec],
        out_shape=[out, out, out, out],
        scratch_shapes=[pltpu.VMEM((TOP_ROWS, tl), F32), pltpu.VMEM((TOP_ROWS, tl), F32)],
        compiler_params=_cparams(("arbitrary",)),
        name="peer_topk",
    )(st4)


def _peer_kernel(tab_ref, h2t_ref, s2_ref, p2_ref, th1_ref, p1_ref, u_ref, vt_ref, x1_ref, m_ref, g2_ref, b2_ref,
                 o_ref, acc_ref, act_ref, ga_ref):
    s = pl.program_id(1)
    n_steps = pl.num_programs(1)
    n_blk = PEER_TM // LANES
    rows_per_tile = PEER_TE // N_KEYS
    d_rows = acc_ref.shape[0] // n_blk
    u_rows = PEER_TE // n_blk

    def project(half, q):
        ro = pl.multiple_of(q * u_rows, u_rows)
        return _dot(u_ref[pl.ds(half * PEER_TE + ro, u_rows), :], h2t_ref[...])

    def store_act(slot, q, a):
        ro = pl.multiple_of(q * u_rows, u_rows)
        for k in range(n_blk):
            act_ref[slot, k, pl.ds(ro, u_rows), :] = a[:, k * LANES:(k + 1) * LANES]

    def apply_v(slot, half, q):
        ro = pl.multiple_of(q * d_rows, d_rows)
        ga = jnp.concatenate([ga_ref[slot, k] for k in range(n_blk)], axis=1)
        return _dot(vt_ref[pl.ds(ro, d_rows), half * PEER_TE:(half + 1) * PEER_TE], ga)

    def stage(cur, tile):
        nxt = 1 - cur
        r0 = pl.multiple_of(tile * rows_per_tile, rows_per_tile)

        def body(cb, carry):
            a_next = project(nxt, cb)
            v_prev = apply_v(nxt, nxt, cb)
            th = [th1_ref[cb, h, pl.ds(r0, rows_per_tile), :] for h in range(PEER_HEADS)]
            p1 = [p1_ref[cb, h, pl.ds(r0, rows_per_tile), :] for h in range(PEER_HEADS)]
            for r in range(rows_per_tile):
                gate = jnp.zeros((N_KEYS, LANES), F32)
                for h in range(PEER_HEADS):
                    sel = jnp.where(s2_ref[cb, h] >= th[h][r:r + 1], p2_ref[cb, h], 0.0)
                    gate = gate + sel * p1[h][r:r + 1]
                rs = slice(r * N_KEYS, (r + 1) * N_KEYS)
                ga_ref[cur, cb, rs, :] = (gate * _gelu(act_ref[cur, cb, rs, :])).astype(BF16)
            store_act(nxt, cb, a_next)
            ro = pl.multiple_of(cb * d_rows, d_rows)
            acc_ref[pl.ds(ro, d_rows), :] += v_prev
            return carry

        lax.fori_loop(0, n_blk, body, 0)

    @pl.when(s == 0)
    def _():
        acc_ref[...] = jnp.zeros(acc_ref.shape, F32)
        ga_ref[...] = jnp.zeros(ga_ref.shape, BF16)
        for q in range(n_blk):
            store_act(0, q, project(0, q))

    @pl.when(s > 0)
    def _():
        stage(1, 2 * s - 1)

    @pl.when(s < n_steps - 1)
    def _():
        stage(0, 2 * s)

    @pl.when(s == n_steps - 1)
    def _():
        m = m_ref[0]
        for q in range(n_blk):
            acc_ref[q * d_rows:(q + 1) * d_rows, :] += apply_v(1, 1, q)
        ffn = acc_ref[...].T
        o_ref[...] = _ln(ALPHA * x1_ref[...] + m[5:6] * ffn) * g2_ref[...] + b2_ref[...]


def _peer(x1, tab, h2t, s2, p2, th1, p1, u_bf, vt_bf, mods, g2, b2):
    t, d = x1.shape
    tm = PEER_TM
    te = PEER_TE
    n_tiles = u_bf.shape[0] // te
    per = tm // ROW_TILE
    n_blk = tm // LANES
    hk = pl.BlockSpec((n_blk, PEER_HEADS, N_KEYS, LANES), lambda i, j, t_: (i, 0, 0, 0))
    in_specs = [pl.BlockSpec((d, tm), lambda i, j, t_: (0, i)),
                hk, hk, hk, hk,
                pl.BlockSpec((2 * te, d), lambda i, j, t_: (jnp.minimum(j, n_tiles // 2 - 1), 0)),
                pl.BlockSpec((d, 2 * te), lambda i, j, t_: (0, jnp.maximum(j - 1, 0))),
                pl.BlockSpec((tm, d), lambda i, j, t_: (i, 0)),
                pl.BlockSpec((1, 6, d), lambda i, j, t_: (t_[0, i * per], 0, 0)),
                pl.BlockSpec(g2.shape, lambda i, j, t_: (0, 0)),
                pl.BlockSpec(b2.shape, lambda i, j, t_: (0, 0))]
    return pl.pallas_call(
        _peer_kernel,
        grid_spec=pltpu.PrefetchScalarGridSpec(
            num_scalar_prefetch=1, grid=(t // tm, n_tiles // 2 + 1), in_specs=in_specs,
            out_specs=pl.BlockSpec((tm, d), lambda i, j, t_: (i, 0)),
            scratch_shapes=[pltpu.VMEM((d, tm), F32), pltpu.VMEM((2, n_blk, te, LANES), F32),
                            pltpu.VMEM((2, n_blk, te, LANES), BF16)]),
        out_shape=jax.ShapeDtypeStruct((t, d), F32),
        compiler_params=_cparams(("arbitrary", "arbitrary")),
        name="peer_mix",
    )(tab, h2t, s2, p2, th1, p1, u_bf, vt_bf, x1, mods, g2, b2)


def _sincos(pos, dim):
    omega = 1.0 / (10000.0 ** (jnp.arange(dim // 2, dtype=F32) / (dim // 2)))
    ang = pos.astype(F32)[:, None] * omega[None, :]
    return jnp.concatenate([jnp.sin(ang), jnp.cos(ang)], -1)


def _grid_pos_embed(n_tok):
    rows = n_tok // GRID_W
    half = D_MODEL // 2
    er = _sincos(jnp.arange(rows), half)
    ec = _sincos(jnp.arange(GRID_W), half)
    emb = jnp.concatenate([jnp.broadcast_to(er[:, None, :], (rows, GRID_W, half)),
                           jnp.broadcast_to(ec[None, :, :], (rows, GRID_W, half))], -1)
    return emb.reshape(rows * GRID_W, D_MODEL)


def _seq_dft_tables(s_len):
    n = jnp.arange(s_len, dtype=jnp.int32)
    ang = ((n[:, None] * n[None, :]) % s_len).astype(F32) * (2.0 * math.pi / s_len)
    scale = 1.0 / math.sqrt(s_len * FNET_GW)
    return (jnp.cos(ang) * scale).astype(BF16), (-jnp.sin(ang) * scale).astype(BF16)


def _channel_dft_table():
    n = np.arange(FNET_GW)
    ang = ((n[:, None] * n[None, :]) % FNET_GW) * (2.0 * np.pi / FNET_GW)
    wc = np.zeros((FNET_W, 2 * FNET_W), np.float32)
    for g in range(FNET_GROUPS):
        sl = slice(g * FNET_GW, (g + 1) * FNET_GW)
        wc[sl, sl] = np.cos(ang)
        wc[sl, FNET_W + g * FNET_GW:FNET_W + (g + 1) * FNET_GW] = np.sin(ang)
    return jnp.asarray(wc, BF16)


def _segment_ones():
    idx = np.arange(WKV_W) // WKV_N
    return jnp.asarray((idx[:, None] == idx[None, :]).astype(np.float32), BF16)


def _chunk_tri():
    i = np.arange(ROW_TILE)
    same = (i[:, None] // CHUNK) == (i[None, :] // CHUNK)
    lower = same & (i[None, :] <= i[:, None])
    upper = same & (i[None, :] >= i[:, None])
    return jnp.asarray(np.stack([lower, upper]).astype(np.float32), BF16)


def _scan_masks():
    i = np.arange(PAIR)
    same = (i[:, None] // CHUNK) == (i[None, :] // CHUNK)
    t_row = i[:, None] % CHUNK
    t_col = i[None, :] % CHUNK
    masks = [same & (t_col < t_row), same & (t_col <= t_row), same & (t_col > t_row), same & (t_col >= t_row),
             np.eye(PAIR, dtype=bool)]
    return jnp.asarray(np.stack(masks).astype(np.float32))


def _block_diag(w):
    h, n, _ = w.shape
    eye = jnp.eye(h, dtype=w.dtype)
    return jnp.einsum("hij,hg->higj", w, eye).reshape(h * n, h * n)


def _pad_rank(w, d):
    r = w.shape[1]
    z = jnp.zeros_like(w[d])
    return jnp.concatenate([w[d], z] if d == 0 else [z, w[d]], axis=0)


def _state_to_pairs(s0):
    b = s0.shape[0]
    st = jnp.swapaxes(s0, -1, -2).reshape(b, 2, N_PAIRS, 2, WKV_N, WKV_N)
    eye = jnp.eye(2, dtype=s0.dtype)
    return jnp.einsum("bdpeji,ef->bdpejfi", st, eye).reshape(b, 2, N_PAIRS, PAIR, PAIR)


def _pairs_to_state(m):
    b = m.shape[0]
    m7 = m.reshape(b, 2, N_PAIRS, 2, WKV_N, 2, WKV_N)
    st = jnp.stack([m7[:, :, :, 0, :, 0, :], m7[:, :, :, 1, :, 1, :]], axis=3)
    return jnp.swapaxes(st.reshape(b, 2, WKV_H, WKV_N, WKV_N), -1, -2)


def _tile_table(groups):
    cv, first, last = [], [], []
    row = 0
    for gi, (n_seq, s_len) in enumerate(groups):
        nt = s_len // ROW_TILE
        for b in range(n_seq):
            for i in range(nt):
                cv.append(0 if gi == 0 else 1 + b)
                first.append(int(i == 0))
                last.append(int(i == nt - 1))
        row += n_seq * s_len
    return jnp.asarray(np.array([cv, first, last], np.int32))


def kernel(x_prompt, x_sample, state_lru, state_wkv, c, c_ctx, w_mod, b_mod, ln1_g, ln1_b, ln2_g, ln2_b,
           w_in_e, w_out_e, sconv_w, sconv_b, w_in_o, w_out_o, lru_conv_w, lru_conv_b, lru_wa, lru_ba,
           lru_wx, lru_bx, lru_lambda, wkv_mu, wkv_w0, wkv_w2, wkv_a0, wkv_a2, wkv_kk, wkv_ka, wkv_rk,
           wkv_g2, wkv_gn_g, wkv_gn_b, peer_wq, peer_keys, peer_u, peer_v):
    bp, sp, d = x_prompt.shape
    bs, ss, _ = x_sample.shape
    depth = w_mod.shape[0]
    assert sp % ROW_TILE == 0 and ss % PEER_TM == 0 and (bp * sp) % ss == 0
    assert bs + 1 <= 8
    groups = ((bp, sp), (bs, ss))
    tp = bp * sp
    tab = _tile_table(groups)

    cv8 = jnp.concatenate([c_ctx[None, :], c, jnp.zeros((8 - 1 - bs, d), F32)], axis=0)
    mods = _modulation(cv8, w_mod, b_mod).reshape(depth, 8, 6, d)

    xs = _add_pos(x_sample, _grid_pos_embed(ss).astype(x_sample.dtype))
    x = jnp.concatenate([x_prompt.reshape(tp, d), xs.reshape(bs * ss, d)], axis=0)

    wc = _channel_dft_table()
    dft = {s: _seq_dft_tables(s) for s in sorted({sp, ss})}
    seg = _segment_ones()
    tri = _chunk_tri()
    masks = _scan_masks()
    row2 = lambda a: a.reshape(1, -1)

    lru_fin = []
    wkv_fin = []
    for l in range(depth):
        j = l // 2
        m_l = mods[l]
        wq = peer_wq[l].astype(BF16)
        keys = peer_keys[l].reshape(2 * PEER_HEADS, N_KEYS, PEER_HALF).astype(BF16)
        if l % 2 == 0:
            zr, pq = _inproj(x, tab, m_l, w_in_e[j].astype(BF16), wc)
            y = jnp.concatenate([_seqdft(pq, bp * sp, 0, bp, sp, *dft[sp]),
                                 _seqdft(pq, bs * ss, tp, bs, ss, *dft[ss])], axis=0)
            x1, h2t, st = _post_even(x, tab, zr, y, m_l, sconv_w[j], row2(sconv_b[j]), w_out_e[j].astype(BF16),
                                     row2(ln1_g[l]), row2(ln1_b[l]), wq, keys)
        else:
            zl, zw = _inproj(x, tab, m_l, w_in_o[j].astype(BF16))
            lru_args = (lru_conv_w[j], lru_conv_b[j],
                        jnp.stack([_block_diag(lru_wa[j, dd]) for dd in range(2)]).astype(BF16), lru_ba[j],
                        jnp.stack([_block_diag(lru_wx[j, dd]) for dd in range(2)]).astype(BF16), lru_bx[j],
                        lru_lambda[j])
            hf_p, hb_p, hfin_p = _lru(zl, 0, bp, sp, jnp.zeros((bp, 2, LRU_W), F32), *lru_args)
            hf_s, hb_s, _ = _lru(zl, tp, bs, ss, state_lru[:, j], *lru_args)
            lru_fin.append(hfin_p)
            prep = _wkvprep(zw, tab, row2(wkv_mu[j]), wkv_w0[j],
                            jnp.stack([_pad_rank(wkv_w2[j], dd) for dd in range(2)]).astype(BF16), wkv_a0[j],
                            jnp.stack([_pad_rank(wkv_a2[j], dd) for dd in range(2)]).astype(BF16),
                            wkv_kk[j], wkv_ka[j], row2(wkv_rk[j]), wkv_g2[j].astype(BF16), seg, tri)
            v, bonus, g = prep[0], prep[1], prep[2]
            yf_p, yb_p, mfin_p = _wkvscan(prep[3:], v, 0, bp, sp,
                                          jnp.zeros((bp, 2, N_PAIRS, PAIR, PAIR), F32), masks)
            yf_s, yb_s, _ = _wkvscan(prep[3:], v, tp, bs, ss, _state_to_pairs(state_wkv[:, j]), masks)
            wkv_fin.append(_pairs_to_state(mfin_p))
            cat = lambda a, b: jnp.concatenate([a, b], axis=0)
            x1, h2t, st = _post_odd(x, tab, cat(hf_p, hf_s), cat(hb_p, hb_s), zl, cat(yf_p, yf_s), cat(yb_p, yb_s),
                                    bonus, g, m_l, seg, row2(wkv_gn_g[j]), row2(wkv_gn_b[j]),
                                    w_out_o[j].astype(BF16), row2(ln1_g[l]), row2(ln1_b[l]), wq, keys)
        st4 = st.reshape(PEER_HEADS, 2, N_KEYS, st.shape[-1])
        th1, p1, p2, s2 = _topk(st4)
        x = _peer(x1, tab, h2t, s2, p2, th1, p1, peer_u[l].astype(BF16), peer_v[l].T.astype(BF16), m_l,
                  row2(ln2_g[l]), row2(ln2_b[l]))

    y_prompt = x[:tp].reshape(bp, sp, d)
    y_sample = x[tp:].reshape(bs, ss, d)
    return (y_prompt, y_sample, jnp.stack(lru_fin, 1).astype(x_prompt.dtype),
            jnp.stack(wkv_fin, 1).astype(x_prompt.dtype))
```

```python
import functools
import math

import numpy as np
import jax
import jax.numpy as jnp
from jax import lax
from jax.experimental import pallas as pl
from jax.experimental.pallas import tpu as pltpu

F32 = jnp.float32
BF16 = jnp.bfloat16

D_MODEL = 1024
GRID_W = 64
FNET_W = 512
FNET_GROUPS = 4
FNET_GW = FNET_W // FNET_GROUPS
CONV_W = 512
LRU_W = 512
LRU_HEADS = 8
LRU_CONV_K = 4
LRU_C = 8.0
WKV_W = 512
WKV_N = 64
WKV_H = 8
WKV_IN = 1920
DECAY_SCALE = math.exp(-0.5)
WKV_GN_EPS = 64e-5
PEER_HEADS = 8
N_KEYS = 128
PEER_TOPK = 16
PEER_HALF = 128
DEPTH = 4
ALPHA = (2 * DEPTH) ** 0.25
LN_EPS = 1e-6

ROW_TILE = 256
LANES = 128
HALO = 8
CHUNK = 64
PAIR = 2 * WKV_N
N_PAIRS = WKV_H // 2
PEER_TM = 512
PEER_TE = 1024
TOPK_TL = 256
VMEM_LIMIT = 56 * 1024 * 1024


def _cparams(sem):
    return pltpu.CompilerParams(dimension_semantics=sem, vmem_limit_bytes=VMEM_LIMIT)


def _dot(a, b):
    return jnp.dot(a, b, preferred_element_type=F32)


def _dot_nt(a, b):
    return lax.dot_general(a, b, (((1,), (1,)), ((), ())), preferred_element_type=F32)


def _mm3(a, b):
    ah = a.astype(BF16)
    al = (a - ah.astype(F32)).astype(BF16)
    bh = b.astype(BF16)
    bl = (b - bh.astype(F32)).astype(BF16)
    return _dot(ah, bh) + (_dot(ah, bl) + _dot(al, bh))


def _mm1(a, b):
    return _dot(a.astype(BF16), b.astype(BF16))


def _mm01(a, b01):
    h = a.astype(BF16)
    r = a - h.astype(F32)
    m = r.astype(BF16)
    lo = (r - m.astype(F32)).astype(BF16)
    return _dot(h, b01) + (_dot(m, b01) + _dot(lo, b01))


def _ln(x):
    mu = jnp.mean(x, axis=-1, keepdims=True)
    xc = x - mu
    var = jnp.mean(xc * xc, axis=-1, keepdims=True)
    return xc * lax.rsqrt(var + LN_EPS)


def _gelu(x):
    z = x * (0.7978845608028654 + 0.035677408136300125 * (x * x))
    hx = 0.5 * x
    return hx + hx * jnp.tanh(z)


def _sigmoid(x):
    return 1.0 / (1.0 + jnp.exp(-x))


def _mod_kernel(c_ref, w_ref, b_ref, o_ref):
    c = c_ref[...]
    sc = c * _sigmoid(c)
    o_ref[0] = _dot(sc.astype(BF16), w_ref[0].astype(BF16)) + b_ref[0]


def _modulation(cv8, w_mod, b_mod):
    depth, d, n = w_mod.shape
    tn = 1536
    return pl.pallas_call(
        _mod_kernel,
        grid=(depth, n // tn),
        in_specs=[pl.BlockSpec((8, d), lambda l, j: (0, 0)),
                  pl.BlockSpec((1, d, tn), lambda l, j: (l, 0, j)),
                  pl.BlockSpec((1, 1, tn), lambda l, j: (l, 0, j))],
        out_specs=pl.BlockSpec((1, 8, tn), lambda l, j: (l, 0, j)),
        out_shape=jax.ShapeDtypeStruct((depth, 8, n), F32),
        compiler_params=_cparams(("arbitrary", "arbitrary")),
        name="modulation",
    )(cv8, w_mod, b_mod.reshape(depth, 1, n))


def _addpos_kernel(x_ref, p_ref, o_ref):
    o_ref[0] = x_ref[0] + p_ref[...]


def _add_pos(x, pos):
    b, s, d = x.shape
    ts = 512
    return pl.pallas_call(
        _addpos_kernel,
        grid=(s // ts, b),
        in_specs=[pl.BlockSpec((1, ts, d), lambda i, j: (j, i, 0)),
                  pl.BlockSpec((ts, d), lambda i, j: (i, 0))],
        out_specs=pl.BlockSpec((1, ts, d), lambda i, j: (j, i, 0)),
        out_shape=jax.ShapeDtypeStruct(x.shape, x.dtype),
        compiler_params=_cparams(("arbitrary", "arbitrary")),
        name="add_pos",
    )(x, pos)


def _inproj_even_kernel(tab_ref, x_ref, m_ref, w_ref, wc_ref, zr_ref, pq_ref):
    m = m_ref[0]
    h = _ln(x_ref[...]) * (1.0 + m[1:2]) + m[0:1]
    z = _dot(h.astype(BF16), w_ref[...])
    zr_ref[...] = z[:, FNET_W:]
    pq_ref[...] = _dot(z[:, :FNET_W].astype(BF16), wc_ref[...]).astype(BF16)


def _inproj_odd_kernel(tab_ref, x_ref, m_ref, w_ref, zl_ref, zw_ref):
    m = m_ref[0]
    h = _ln(x_ref[...]) * (1.0 + m[1:2]) + m[0:1]
    z = _dot(h.astype(BF16), w_ref[...])
    zl_ref[...] = z[:, :2 * LRU_W]
    zw_ref[...] = z[:, 2 * LRU_W:]


def _const_spec(shape):
    nd = len(shape)
    return pl.BlockSpec(shape, lambda i, t, _n=nd: (0,) * _n)


def _inproj(x, tab, mods, w_in, wc=None):
    t, d = x.shape
    n = w_in.shape[1]
    tm = ROW_TILE
    even = wc is not None
    in_specs = [pl.BlockSpec((tm, d), lambda i, t_: (i, 0)),
                pl.BlockSpec((1, 6, d), lambda i, t_: (t_[0, i], 0, 0)),
                _const_spec((d, n))]
    if even:
        in_specs.append(_const_spec(wc.shape))
        widths = (n - FNET_W, 2 * FNET_W)
        dtypes = (F32, BF16)
        kern = _inproj_even_kernel
        args = (tab, x, mods, w_in, wc)
    else:
        widths = (2 * LRU_W, n - 2 * LRU_W)
        dtypes = (F32, F32)
        kern = _inproj_odd_kernel
        args = (tab, x, mods, w_in)
    return pl.pallas_call(
        kern,
        grid_spec=pltpu.PrefetchScalarGridSpec(
            num_scalar_prefetch=1, grid=(t // tm,), in_specs=in_specs,
            out_specs=[pl.BlockSpec((tm, w), lambda i, t_: (i, 0)) for w in widths]),
        out_shape=[jax.ShapeDtypeStruct((t, w), dt) for w, dt in zip(widths, dtypes)],
        compiler_params=_cparams(("arbitrary",)),
        name="inproj_even" if even else "inproj_odd",
    )(*args)


def _seqdft_kernel(c_ref, s_ref, p_ref, q_ref, y_ref):
    y_ref[...] = (_dot(c_ref[...], p_ref[...]) + _dot(s_ref[...], q_ref[...])).astype(BF16)


def _seqdft(pq, y_rows, row0, n_seq, s_len, cmat, smat):
    tm = min(s_len, ROW_TILE)
    nt = s_len // tm
    blk0 = row0 // s_len
    return pl.pallas_call(
        _seqdft_kernel,
        grid=(n_seq, nt),
        in_specs=[pl.BlockSpec((tm, s_len), lambda b, i: (i, 0)),
                  pl.BlockSpec((tm, s_len), lambda b, i: (i, 0)),
                  pl.BlockSpec((s_len, FNET_W), lambda b, i: (blk0 + b, 0)),
                  pl.BlockSpec((s_len, FNET_W), lambda b, i: (blk0 + b, 1))],
        out_specs=pl.BlockSpec((tm, FNET_W), lambda b, i: (b * nt + i, 0)),
        out_shape=jax.ShapeDtypeStruct((y_rows, FNET_W), BF16),
        compiler_params=_cparams(("arbitrary", "arbitrary")),
        name="seqdft_%d" % s_len,
    )(cmat, smat, pq, pq)


def _post_and_query(x, mix, m, g1, b1, wq_ref, keys_ref, x1_ref, h2t_ref, st_ref):
    x1 = _ln(ALPHA * x + m[2:3] * mix) * g1 + b1
    x1_ref[...] = x1
    h2 = _ln(x1) * (1.0 + m[4:5]) + m[3:4]
    h2b = h2.astype(BF16)
    h2t_ref[...] = h2.T.astype(BF16)
    q = _dot(h2b, wq_ref[...])
    for hp in range(2 * PEER_HEADS):
        qb = q[:, hp * PEER_HALF:(hp + 1) * PEER_HALF].astype(BF16)
        st_ref[hp * N_KEYS:(hp + 1) * N_KEYS, :] = _dot_nt(keys_ref[hp], qb)


def _post_even_kernel(tab_ref, x_ref, bg_ref, cg_ref, xi_ref, cgp_ref, xip_ref, cgn_ref, xin_ref, y_ref,
                      m_ref, cw_ref, cb_ref, wo_ref, g1_ref, b1_ref, wq_ref, keys_ref,
                      x1_ref, h2t_ref, st_ref, ext_ref):
    i = pl.program_id(0)
    tm = ROW_TILE
    first = tab_ref[1, i] == 1
    last = tab_ref[2, i] == 1
    u = cg_ref[...] * xi_ref[...]
    ext_ref[0:HALO] = jnp.where(first, 0.0, cgp_ref[...] * xip_ref[...])
    ext_ref[HALO:HALO + tm] = u
    ext_ref[HALO + tm:2 * HALO + tm] = jnp.where(last, 0.0, cgn_ref[...] * xin_ref[...])
    cw = cw_ref[...]
    conv = (ext_ref[HALO - 1:HALO - 1 + tm] * cw[0:1] + u * cw[1:2]
            + ext_ref[HALO + 1:HALO + 1 + tm] * cw[2:3] + cb_ref[...])
    ymix = (bg_ref[...] * conv).astype(BF16)
    mix = _dot(y_ref[...], wo_ref[0:FNET_W]) + _dot(ymix, wo_ref[FNET_W:])
    _post_and_query(x_ref[...], mix, m_ref[0], g1_ref[...], b1_ref[...], wq_ref, keys_ref,
                    x1_ref, h2t_ref, st_ref)


def _post_out_specs(t, d, tm):
    nsc = 2 * PEER_HEADS * N_KEYS
    specs = [pl.BlockSpec((tm, d), lambda i, t_: (i, 0)),
             pl.BlockSpec((d, tm), lambda i, t_: (0, i)),
             pl.BlockSpec((nsc, tm), lambda i, t_: (0, i))]
    shapes = [jax.ShapeDtypeStruct((t, d), F32),
              jax.ShapeDtypeStruct((d, t), BF16),
              jax.ShapeDtypeStruct((nsc, t), F32)]
    return specs, shapes


def _post_even(x, tab, zr, y, mods, cw, cb, wo, g1, b1, wq, keys):
    t, d = x.shape
    tm = ROW_TILE
    hb = tm // HALO
    nblk = t // HALO
    prev = lambda c: pl.BlockSpec((HALO, CONV_W), lambda i, t_, _c=c: (jnp.maximum(i * hb - 1, 0), _c))
    nxt = lambda c: pl.BlockSpec((HALO, CONV_W), lambda i, t_, _c=c: (jnp.minimum((i + 1) * hb, nblk - 1), _c))
    col = lambda c: pl.BlockSpec((tm, CONV_W), lambda i, t_, _c=c: (i, _c))
    in_specs = [pl.BlockSpec((tm, d), lambda i, t_: (i, 0)),
                col(0), col(1), col(2), prev(1), prev(2), nxt(1), nxt(2),
                pl.BlockSpec((tm, FNET_W), lambda i, t_: (i, 0)),
                pl.BlockSpec((1, 6, d), lambda i, t_: (t_[0, i], 0, 0)),
                _const_spec(cw.shape), _const_spec(cb.shape), _const_spec(wo.shape),
                _const_spec(g1.shape), _const_spec(b1.shape), _const_spec(wq.shape), _const_spec(keys.shape)]
    out_specs, out_shape = _post_out_specs(t, d, tm)
    return pl.pallas_call(
        _post_even_kernel,
        grid_spec=pltpu.PrefetchScalarGridSpec(
            num_scalar_prefetch=1, grid=(t // tm,), in_specs=in_specs, out_specs=out_specs,
            scratch_shapes=[pltpu.VMEM((tm + 2 * HALO, CONV_W), F32)]),
        out_shape=out_shape,
        compiler_params=_cparams(("arbitrary",)),
        name="post_even",
    )(tab, x, zr, zr, zr, zr, zr, zr, zr, y, mods, cw, cb, wo, g1, b1, wq, keys)


def _lru_kernel(xf_ref, xfh_ref, xb_ref, xbh_ref, cw_ref, cb_ref, wa_ref, ba_ref, wx_ref, bx_ref, lam_ref,
                h0_ref, hf_ref, hb_ref, hfin_ref, ext_ref, carry_ref):
    i = pl.program_id(1)
    nt = pl.num_programs(1)
    tm = ROW_TILE

    @pl.when(i == 0)
    def _():
        carry_ref[...] = h0_ref[0]

    row = lax.broadcasted_iota(jnp.int32, (tm, 1), 0)
    for d in range(2):
        x = (xf_ref if d == 0 else xb_ref)[...]
        halo = jnp.where(i == 0, 0.0, (xfh_ref if d == 0 else xbh_ref)[...])
        ext_ref[HALO:HALO + tm] = x
        if d == 0:
            ext_ref[0:HALO] = halo
        else:
            ext_ref[HALO + tm:2 * HALO + tm] = halo
        cw = cw_ref[d]
        xc = cb_ref[d:d + 1] + x * cw[LRU_CONV_K - 1:LRU_CONV_K]
        for j in range(LRU_CONV_K - 1):
            k = LRU_CONV_K - 1 - j
            off = HALO - k if d == 0 else HALO + k
            xc = xc + ext_ref[off:off + tm] * cw[j:j + 1]
        xcb = xc.astype(BF16)
        gate_r = _sigmoid(_dot(xcb, wa_ref[d]) + ba_ref[d:d + 1])
        gate_i = _sigmoid(_dot(xcb, wx_ref[d]) + bx_ref[d:d + 1])
        nl = -lam_ref[d:d + 1]
        softplus = jnp.maximum(nl, 0.0) + jnp.log1p(jnp.exp(-jnp.abs(nl)))
        log_a = -LRU_C * gate_r * softplus
        a = jnp.exp(log_a)
        b = jnp.sqrt(-jnp.tanh(log_a) * (a * a + 1.0)) * (gate_i * xc)
        s = 1
        while s < tm:
            if d == 0:
                keep = row >= s
                sh = s
            else:
                keep = row < tm - s
                sh = tm - s
            a_sh = jnp.where(keep, pltpu.roll(a, sh, 0), 1.0)
            b_sh = jnp.where(keep, pltpu.roll(b, sh, 0), 0.0)
            b = a * b_sh + b
            a = a * a_sh
            s *= 2
        h = a * carry_ref[d:d + 1] + b
        if d == 0:
            hf_ref[...] = h
            carry_ref[0:1] = h[tm - 1:tm]
        else:
            hb_ref[...] = h
            carry_ref[1:2] = h[0:1]

    @pl.when(i == nt - 1)
    def _():
        hfin_ref[0] = carry_ref[...]


def _lru(zl, row0, n_seq, s_len, h0, cw, cb, wa, ba, wx, bx, lam):
    tm = ROW_TILE
    nt = s_len // tm
    t0 = row0 // tm
    hb = tm // HALO
    nblk = zl.shape[0] // HALO
    rows = n_seq * s_len
    fwd = lambda b, i: (t0 + b * nt + i, 0)
    bwd = lambda b, i: (t0 + b * nt + nt - 1 - i, 0)
    fwd_h = lambda b, i: (jnp.maximum((t0 + b * nt + i) * hb - 1, 0), 0)
    bwd_h = lambda b, i: (jnp.minimum((t0 + b * nt + nt - i) * hb, nblk - 1), 0)
    out_f = lambda b, i: (b * nt + i, 0)
    out_b = lambda b, i: (b * nt + nt - 1 - i, 0)
    cst = lambda a: pl.BlockSpec(a.shape, lambda b, i, _n=a.ndim: (0,) * _n)
    return pl.pallas_call(
        _lru_kernel,
        grid=(n_seq, nt),
        in_specs=[pl.BlockSpec((tm, LRU_W), fwd), pl.BlockSpec((HALO, LRU_W), fwd_h),
                  pl.BlockSpec((tm, LRU_W), bwd), pl.BlockSpec((HALO, LRU_W), bwd_h),
                  cst(cw), cst(cb), cst(wa), cst(ba), cst(wx), cst(bx), cst(lam),
                  pl.BlockSpec((1, 2, LRU_W), lambda b, i: (b, 0, 0))],
        out_specs=[pl.BlockSpec((tm, LRU_W), out_f), pl.BlockSpec((tm, LRU_W), out_b),
                   pl.BlockSpec((1, 2, LRU_W), lambda b, i: (b, 0, 0))],
        out_shape=[jax.ShapeDtypeStruct((rows, LRU_W), F32), jax.ShapeDtypeStruct((rows, LRU_W), F32),
                   jax.ShapeDtypeStruct((n_seq, 2, LRU_W), F32)],
        scratch_shapes=[pltpu.VMEM((tm + 2 * HALO, LRU_W), F32), pltpu.VMEM((2, LRU_W), F32)],
        compiler_params=_cparams(("arbitrary", "arbitrary")),
        name="lru_%d" % s_len,
    )(zl, zl, zl, zl, cw, cb, wa, ba, wx, bx, lam, h0)


def _wkvprep_kernel(tab_ref, z_ref, zp_ref, zn_ref, mu_ref, w0_ref, w2_ref, a0_ref, a2_ref, kk_ref, ka_ref,
                    rk_ref, g2_ref, seg_ref, tri_ref,
                    v_ref, bg_ref, g_ref, rf_ref, kf_ref, bf_ref, qf_ref, cf_ref,
                    rb_ref, kb_ref, bb_ref, qb_ref, cb_ref, ext_ref):
    i = pl.program_id(0)
    tm = ROW_TILE
    w = WKV_W
    first = tab_ref[1, i] == 1
    last = tab_ref[2, i] == 1
    z = z_ref[...]
    ext_ref[0:HALO] = jnp.where(first, 0.0, zp_ref[...])
    ext_ref[HALO:HALO + tm] = z
    ext_ref[HALO + tm:2 * HALO + tm] = jnp.where(last, 0.0, zn_ref[...])
    z = z + mu_ref[...] * (0.5 * (ext_ref[HALO - 1:HALO - 1 + tm] + ext_ref[HALO + 1:HALO + 1 + tm]) - z)
    r = z[:, 0:w]
    k = z[:, w:2 * w]
    v = z[:, 2 * w:3 * w]
    wd = jnp.tanh(z[:, 3 * w:3 * w + 128]).astype(BF16)
    ad = z[:, 3 * w + 128:3 * w + 256].astype(BF16)
    gd = _sigmoid(z[:, 3 * w + 256:3 * w + 384]).astype(BF16)
    v_ref[...] = v
    g_ref[...] = _dot(gd, g2_ref[...])
    seg = seg_ref[...]
    rk = rk_ref[...]
    bonus = jnp.zeros((tm, w), F32)
    outs = ((rf_ref, kf_ref, bf_ref, qf_ref, cf_ref), (rb_ref, kb_ref, bb_ref, qb_ref, cb_ref))
    nch = tm // CHUNK
    for d in range(2):
        r_ref, k_ref, b_ref, q_ref, c_ref = outs[d]
        wz = w0_ref[d:d + 1] + _dot(wd, w2_ref[d])
        logw = -DECAY_SCALE * _sigmoid(wz)
        iclr = _sigmoid(a0_ref[d:d + 1] + _dot(ad, a2_ref[d]))
        kk = k * kk_ref[d:d + 1]
        kk = kk * lax.rsqrt(jnp.maximum(_mm01(kk * kk, seg), 1e-24))
        km = k * (1.0 + (iclr - 1.0) * ka_ref[d:d + 1])
        bonus = bonus + _mm01(r * km * rk, seg) * v
        lw_h = logw.astype(BF16)
        lw_r = logw - lw_h.astype(F32)
        lw_m = lw_r.astype(BF16)
        lw_l = (lw_r - lw_m.astype(F32)).astype(BF16)
        tri = tri_ref[d]
        cl = _dot(tri, lw_h) + (_dot(tri, lw_m) + _dot(tri, lw_l))
        c = jnp.exp(cl)
        cinv = jnp.exp(-cl)
        r_ref[...] = r * c
        k_ref[...] = km * cinv
        b_ref[...] = kk * iclr * cinv
        q_ref[...] = kk * jnp.exp(cl - logw)
        for j in range(nch):
            edge = (j + 1) * CHUNK - 1 if d == 0 else j * CHUNK
            c_ref[j] = c[edge:edge + 1]
    bg_ref[...] = bonus


def _wkvprep(zw, tab, mu, w0, w2p, a0, a2p, kk, ka, rk, g2, seg, tri):
    t, n = zw.shape
    tm = ROW_TILE
    hb = tm // HALO
    nblk = t // HALO
    nch = tm // CHUNK
    w = WKV_W
    in_specs = [pl.BlockSpec((tm, n), lambda i, t_: (i, 0)),
                pl.BlockSpec((HALO, n), lambda i, t_: (jnp.maximum(i * hb - 1, 0), 0)),
                pl.BlockSpec((HALO, n), lambda i, t_: (jnp.minimum((i + 1) * hb, nblk - 1), 0))]
    in_specs += [_const_spec(a.shape) for a in (mu, w0, w2p, a0, a2p, kk, ka, rk, g2, seg, tri)]
    row = pl.BlockSpec((tm, w), lambda i, t_: (i, 0))
    cspec = pl.BlockSpec((nch, 1, w), lambda i, t_: (i, 0, 0))
    rshape = jax.ShapeDtypeStruct((t, w), F32)
    cshape = jax.ShapeDtypeStruct((t // CHUNK, 1, w), F32)
    out_specs = [row, row, row] + [row, row, row, row, cspec] * 2
    out_shape = [rshape, rshape, rshape] + [rshape, rshape, rshape, rshape, cshape] * 2
    return pl.pallas_call(
        _wkvprep_kernel,
        grid_spec=pltpu.PrefetchScalarGridSpec(
            num_scalar_prefetch=1, grid=(t // tm,), in_specs=in_specs, out_specs=out_specs,
            scratch_shapes=[pltpu.VMEM((tm + 2 * HALO, n), F32)]),
        out_shape=out_shape,
        compiler_params=_cparams(("arbitrary",)),
        name="wkv_prep",
    )(tab, zw, zw, zw, mu, w0, w2p, a0, a2p, kk, ka, rk, g2, seg, tri)


def _wkv_chunks(items, eye):
    lane = lax.broadcasted_iota(jnp.int32, (1, PAIR), 1)
    m0 = (lane < WKV_N).astype(F32)
    m1 = 1.0 - m0
    stack = lambda x: jnp.concatenate([x * m0, x * m1], axis=0)
    n = range(len(items))
    rh_s = [stack(it[0]) for it in items]
    kh_s = [stack(it[1]) for it in items]
    bh_s = [stack(it[2]) for it in items]
    kq_s = [stack(it[3]) for it in items]
    v_s = [stack(it[4]) for it in items]
    bh_t = [x.T for x in bh_s]
    kh_t = [x.T for x in kh_s]
    bk_t = [jnp.concatenate([bh_t[i], kh_t[i]], axis=1) for i in n]
    att_s = [_mm3(kq_s[i], bk_t[i]) for i in n]
    att_y = [_mm1(rh_s[i], bk_t[i]) for i in n]
    n1 = [att_s[i][:, 0:PAIR] * items[i][7] for i in n]
    ak = [att_s[i][:, PAIR:] * items[i][7] for i in n]
    gb = [att_y[i][:, 0:PAIR] * items[i][8] for i in n]
    gk = [att_y[i][:, PAIR:] * items[i][8] for i in n]
    t_inv = [eye - n1[i] for i in n]
    npow = n1
    for _ in range(5):
        npow = [_mm1(npow[i], npow[i]) for i in n]
        t_inv = [t_inv[i] + _mm1(t_inv[i], npow[i]) for i in n]
    kv = [_mm1(jnp.concatenate([kh_t[i], gk[i], ak[i]], axis=0), v_s[i]) for i in n]
    x = [_mm1(t_inv[i], jnp.concatenate([kq_s[i], kv[i][2 * PAIR:]], axis=1)) for i in n]
    bx = [_mm1(jnp.concatenate([bh_t[i], gb[i]], axis=0), x[i]) for i in n]
    p_mat = [eye - bx[i][0:PAIR, 0:PAIR] for i in n]
    q_mat = [kv[i][0:PAIR] - bx[i][0:PAIR, PAIR:] for i in n]
    r_til = [rh_s[i] - bx[i][PAIR:, 0:PAIR] for i in n]
    y0 = [kv[i][PAIR:2 * PAIR] - bx[i][PAIR:, PAIR:] for i in n]
    pm = [_mm3(p_mat[i], items[i][6]) for i in n]
    y_st = [_mm1(r_til[i], items[i][6]) + y0[i] for i in n]
    ys = [y_st[i][0:CHUNK] + y_st[i][CHUNK:] for i in n]
    c_col = [jnp.broadcast_to(items[i][5], (PAIR, PAIR)).T for i in n]
    m_new = [(pm[i] + q_mat[i]) * c_col[i] for i in n]
    return ys, m_new


def _wkvscan_kernel(rf_ref, kf_ref, bf_ref, qf_ref, vf_ref, cf_ref, rb_ref, kb_ref, bb_ref, qb_ref, vb_ref, cb_ref,
                    m0_ref, msk_ref, yf_ref, yb_ref, mfin_ref, m_ref):
    i = pl.program_id(1)
    nc = pl.num_programs(1)

    @pl.when(i == 0)
    def _():
        m_ref[...] = m0_ref[0]

    eye = msk_ref[4]
    ins = ((rf_ref, kf_ref, bf_ref, qf_ref, vf_ref, cf_ref, yf_ref),
           (rb_ref, kb_ref, bb_ref, qb_ref, vb_ref, cb_ref, yb_ref))
    items = []
    for d in range(2):
        r_ref, k_ref, b_ref, q_ref, v_ref, c_ref, _ = ins[d]
        for p in range(N_PAIRS):
            sl = slice(p * PAIR, (p + 1) * PAIR)
            items.append((r_ref[:, sl], k_ref[:, sl], b_ref[:, sl], q_ref[:, sl], v_ref[:, sl],
                          c_ref[0, :, sl], m_ref[d, p], msk_ref[2 * d], msk_ref[2 * d + 1]))
    ys, m_new = _wkv_chunks(items, eye)
    for d in range(2):
        for p in range(N_PAIRS):
            sl = slice(p * PAIR, (p + 1) * PAIR)
            ins[d][6][:, sl] = ys[d * N_PAIRS + p]
            m_ref[d, p] = m_new[d * N_PAIRS + p]

    @pl.when(i == nc - 1)
    def _():
        mfin_ref[0] = m_ref[...]


def _wkvscan(prep, v, row0, n_seq, s_len, m0, masks):
    rf, kf, bf, qf, cf, rb, kb, bb, qb, cb = prep
    nc = s_len // CHUNK
    c0 = row0 // CHUNK
    rows = n_seq * s_len
    w = WKV_W
    fwd = lambda b, i: (c0 + b * nc + i, 0)
    bwd = lambda b, i: (c0 + b * nc + nc - 1 - i, 0)
    fwd3 = lambda b, i: (c0 + b * nc + i, 0, 0)
    bwd3 = lambda b, i: (c0 + b * nc + nc - 1 - i, 0, 0)
    blk = lambda im: pl.BlockSpec((CHUNK, w), im)
    cblk = lambda im: pl.BlockSpec((1, 1, w), im)
    mspec = pl.BlockSpec((1, 2, N_PAIRS, PAIR, PAIR), lambda b, i: (b, 0, 0, 0, 0))
    return pl.pallas_call(
        _wkvscan_kernel,
        grid=(n_seq, nc),
        in_specs=[blk(fwd)] * 5 + [cblk(fwd3)] + [blk(bwd)] * 5 + [cblk(bwd3)]
        + [mspec, pl.BlockSpec(masks.shape, lambda b, i: (0, 0, 0))],
        out_specs=[pl.BlockSpec((CHUNK, w), lambda b, i: (b * nc + i, 0)),
                   pl.BlockSpec((CHUNK, w), lambda b, i: (b * nc + nc - 1 - i, 0)),
                   mspec],
        out_shape=[jax.ShapeDtypeStruct((rows, w), F32), jax.ShapeDtypeStruct((rows, w), F32),
                   jax.ShapeDtypeStruct(m0.shape, F32)],
        scratch_shapes=[pltpu.VMEM((2, N_PAIRS, PAIR, PAIR), F32)],
        compiler_params=_cparams(("arbitrary", "arbitrary")),
        name="wkv_scan_%d" % s_len,
    )(rf, kf, bf, qf, v, cf, rb, kb, bb, qb, v, cb, m0, masks)


def _post_odd_kernel(tab_ref, x_ref, hf_ref, hb_ref, gb_ref, yf_ref, yb_ref, bon_ref, g_ref,
                     m_ref, seg_ref, gng_ref, gnb_ref, wo_ref, g1_ref, b1_ref, wq_ref, keys_ref,
                     x1_ref, h2t_ref, st_ref):
    y_lru = ((hf_ref[...] + hb_ref[...]) * _gelu(gb_ref[...])).astype(BF16)
    seg = seg_ref[...]
    ys = yf_ref[...] + yb_ref[...]
    mean = _mm01(ys, seg) * (1.0 / WKV_N)
    yc = ys - mean
    var = _mm01(yc * yc, seg) * (1.0 / WKV_N)
    yn = yc * lax.rsqrt(var + WKV_GN_EPS) * gng_ref[...] + gnb_ref[...]
    y_wkv = ((yn + bon_ref[...]) * g_ref[...]).astype(BF16)
    mix = _dot(y_lru, wo_ref[0:LRU_W]) + _dot(y_wkv, wo_ref[LRU_W:])
    _post_and_query(x_ref[...], mix, m_ref[0], g1_ref[...], b1_ref[...], wq_ref, keys_ref,
                    x1_ref, h2t_ref, st_ref)


def _post_odd(x, tab, hf, hb, zl, yf, yb, bonus, g, mods, seg, gng, gnb, wo, g1, b1, wq, keys):
    t, d = x.shape
    tm = ROW_TILE
    w = WKV_W
    row = pl.BlockSpec((tm, w), lambda i, t_: (i, 0))
    in_specs = [pl.BlockSpec((tm, d), lambda i, t_: (i, 0)), row, row,
                pl.BlockSpec((tm, LRU_W), lambda i, t_: (i, 1)), row, row, row, row,
                pl.BlockSpec((1, 6, d), lambda i, t_: (t_[0, i], 0, 0))]
    in_specs += [_const_spec(a.shape) for a in (seg, gng, gnb, wo, g1, b1, wq, keys)]
    out_specs, out_shape = _post_out_specs(t, d, tm)
    return pl.pallas_call(
        _post_odd_kernel,
        grid_spec=pltpu.PrefetchScalarGridSpec(
            num_scalar_prefetch=1, grid=(t // tm,), in_specs=in_specs, out_specs=out_specs),
        out_shape=out_shape,
        compiler_params=_cparams(("arbitrary",)),
        name="post_odd",
    )(tab, x, hf, hb, zl, yf, yb, bonus, g, mods, seg, gng, gnb, wo, g1, b1, wq, keys)


N_TOP = PEER_TOPK + 1
TOP_ROWS = 24
SUBLANES = 8


def _batcher_network(n):
    def merge(lo, hi, r):
        step = r * 2
        if step < hi - lo:
            yield from merge(lo, hi, step)
            yield from merge(lo + r, hi, step)
            yield from [(i, i + r) for i in range(lo + r, hi - r, step)]
        else:
            yield (lo, lo + r)

    def sort(lo, hi):
        if hi - lo >= 1:
            mid = lo + (hi - lo) // 2
            yield from sort(lo, mid)
            yield from sort(mid + 1, hi)
            yield from merge(lo, hi, 1)

    return tuple(sort(0, n - 1))


def _sort_levels(levels, net):
    lv = list(levels)
    for i, j in net:
        a, b = lv[i], lv[j]
        lv[i] = jnp.maximum(a, b)
        lv[j] = jnp.minimum(a, b)
    return lv


def _pop_sorted(levels, n_top, emit):
    lv = list(levels)
    for it in range(n_top):
        m = jnp.max(lv[0], axis=0, keepdims=True)
        emit(it, m)
        hit = lv[0] == m
        live = min(len(lv), n_top - it - 1)
        for k in range(live):
            below = lv[k + 1] if k + 1 < len(lv) else -jnp.inf
            lv[k] = jnp.where(hit, below, lv[k])


def _topk_kernel(s_ref, th1_ref, p1_ref, p2_ref, s2_ref, v1_ref, v2_ref):
    neg = -jnp.inf
    net16 = _batcher_network(N_KEYS // SUBLANES)
    net8 = _batcher_network(SUBLANES)
    row = lax.broadcasted_iota(jnp.int32, (SUBLANES, TOPK_TL), 0)
    roll = lambda x, sh: pltpu.roll(x, sh, 0)

    def top_values(x, store):
        store[...] = jnp.full(store.shape, neg, F32)
        lv = _sort_levels([x[SUBLANES * k:SUBLANES * (k + 1)] for k in range(N_KEYS // SUBLANES)], net16)

        def emit(it, m):
            store[it:it + 1, :] = m

        _pop_sorted(lv, N_TOP, emit)

    def head(h, carry):
        s1 = s_ref[h, 0]
        s2 = s_ref[h, 1]
        top_values(s1, v1_ref)
        top_values(s2, v2_ref)
        one = lambda ref, a: ref[a:a + 1, :]
        v2a = v2_ref[0:8, :]
        v1b2 = roll(v1_ref[8:16, :], 2)
        cands = [one(v1_ref, 0) + v2a, one(v1_ref, 0) + v2_ref[8:16, :], one(v1_ref, 0) + v2_ref[16:24, :],
                 one(v1_ref, 1) + v2a,
                 jnp.where(row < 5, one(v1_ref, 2), one(v1_ref, 4)) + jnp.where(row < 5, v2a, roll(v2a, 5)),
                 jnp.where(row < 4, one(v1_ref, 3), jnp.where(row < 6, one(v1_ref, 5), one(v1_ref, 6)))
                 + jnp.where(row < 4, v2a, jnp.where(row < 6, roll(v2a, 4), roll(v2a, 6))),
                 jnp.where(row < 2, one(v1_ref, 7), v1b2) + jnp.where(row < 2, v2a, one(v2_ref, 0)),
                 jnp.where(row < 2, v1b2, roll(v1_ref[16:24, :], 2)) + one(v2_ref, 0)]
        top = []
        _pop_sorted(_sort_levels(cands, net8), N_TOP, lambda it, m: top.append(m))
        tau = 0.5 * (top[PEER_TOPK - 1] + top[PEER_TOPK])
        mx1 = v1_ref[0:1, :]
        mx2 = v2_ref[0:1, :]
        zacc = jnp.zeros((SUBLANES, TOPK_TL), F32)
        for cnd in cands:
            zacc = zacc + jnp.where(cnd >= tau, jnp.exp(cnd - (mx1 + mx2)), 0.0)
        zsum = jnp.sum(zacc, axis=0, keepdims=True)
        th1 = tau - s1
        p1 = jnp.exp(s1 - mx1) / zsum
        p2 = jnp.exp(s2 - mx2)
        for cb in range(TOPK_TL // LANES):
            cs = slice(cb * LANES, (cb + 1) * LANES)
            th1_ref[cb, h] = th1[:, cs]
            p1_ref[cb, h] = p1[:, cs]
            p2_ref[cb, h] = p2[:, cs]
            s2_ref[cb, h] = s2[:, cs]
        return carry

    lax.fori_loop(0, PEER_HEADS, head, 0)


def _topk(st4):
    t = st4.shape[-1]
    tl = TOPK_TL
    out = jax.ShapeDtypeStruct((t // LANES, PEER_HEADS, N_KEYS, LANES), F32)
    ospec = pl.BlockSpec((tl // LANES, PEER_HEADS, N_KEYS, LANES), lambda i: (i, 0, 0, 0))
    return pl.pallas_call(
        _topk_kernel,
        grid=(t // tl,),
        in_specs=[pl.BlockSpec((PEER_HEADS, 2, N_KEYS, tl), lambda i: (0, 0, 0, i))],
        out_specs=[ospec, ospec, ospec, ospec],
        out_shape=[out, out, out, out],
        scratch_shapes=[pltpu.VMEM((TOP_ROWS, tl), F32), pltpu.VMEM((TOP_ROWS, tl), F32)],
        compiler_params=_cparams(("arbitrary",)),
        name="peer_topk",
    )(st4)


def _peer_kernel(tab_ref, h2t_ref, s2_ref, p2_ref, th1_ref, p1_ref, u_ref, vt_ref, x1_ref, m_ref, g2_ref, b2_ref,
                 o_ref, acc_ref, act_ref, ga_ref):
    s = pl.program_id(1)
    n_steps = pl.num_programs(1)
    n_blk = PEER_TM // LANES
    rows_per_tile = PEER_TE // N_KEYS
    d_rows = acc_ref.shape[0] // n_blk
    u_rows = PEER_TE // n_blk

    def project(half, q):
        ro = pl.multiple_of(q * u_rows, u_rows)
        return _dot(u_ref[pl.ds(half * PEER_TE + ro, u_rows), :], h2t_ref[...])

    def store_act(slot, q, a):
        ro = pl.multiple_of(q * u_rows, u_rows)
        for k in range(n_blk):
            act_ref[slot, k, pl.ds(ro, u_rows), :] = a[:, k * LANES:(k + 1) * LANES]

    def apply_v(slot, half, q):
        ro = pl.multiple_of(q * d_rows, d_rows)
        ga = jnp.concatenate([ga_ref[slot, k] for k in range(n_blk)], axis=1)
        return _dot(vt_ref[pl.ds(ro, d_rows), half * PEER_TE:(half + 1) * PEER_TE], ga)

    def stage(cur, tile):
        nxt = 1 - cur
        r0 = pl.multiple_of(tile * rows_per_tile, rows_per_tile)

        def body(cb, carry):
            a_next = project(nxt, cb)
            v_prev = apply_v(nxt, nxt, cb)
            th = [th1_ref[cb, h, pl.ds(r0, rows_per_tile), :] for h in range(PEER_HEADS)]
            p1 = [p1_ref[cb, h, pl.ds(r0, rows_per_tile), :] for h in range(PEER_HEADS)]
            for r in range(rows_per_tile):
                gate = None
                for h in range(PEER_HEADS):
                    term = jnp.where(s2_ref[cb, h] >= th[h][r:r + 1], p2_ref[cb, h], 0.0) * p1[h][r:r + 1]
                    gate = term if gate is None else gate + term
                rs = slice(r * N_KEYS, (r + 1) * N_KEYS)
                ga_ref[cur, cb, rs, :] = (gate * _gelu(act_ref[cur, cb, rs, :])).astype(BF16)
            store_act(nxt, cb, a_next)
            ro = pl.multiple_of(cb * d_rows, d_rows)
            acc_ref[pl.ds(ro, d_rows), :] += v_prev
            return carry

        lax.fori_loop(0, n_blk, body, 0)

    @pl.when(s == 0)
    def _():
        acc_ref[...] = jnp.zeros(acc_ref.shape, F32)
        ga_ref[...] = jnp.zeros(ga_ref.shape, BF16)
        for q in range(n_blk):
            store_act(0, q, project(0, q))

    @pl.when(s > 0)
    def _():
        stage(1, 2 * s - 1)

    @pl.when(s < n_steps - 1)
    def _():
        stage(0, 2 * s)

    @pl.when(s == n_steps - 1)
    def _():
        m = m_ref[0]
        for q in range(n_blk):
            acc_ref[q * d_rows:(q + 1) * d_rows, :] += apply_v(1, 1, q)
        ffn = acc_ref[...].T
        o_ref[...] = _ln(ALPHA * x1_ref[...] + m[5:6] * ffn) * g2_ref[...] + b2_ref[...]


def _peer(x1, tab, h2t, s2, p2, th1, p1, u_bf, vt_bf, mods, g2, b2):
    t, d = x1.shape
    tm = PEER_TM
    te = PEER_TE
    n_tiles = u_bf.shape[0] // te
    per = tm // ROW_TILE
    n_blk = tm // LANES
    hk = pl.BlockSpec((n_blk, PEER_HEADS, N_KEYS, LANES), lambda i, j, t_: (i, 0, 0, 0))
    in_specs = [pl.BlockSpec((d, tm), lambda i, j, t_: (0, i)),
                hk, hk, hk, hk,
                pl.BlockSpec((2 * te, d), lambda i, j, t_: (jnp.minimum(j, n_tiles // 2 - 1), 0)),
                pl.BlockSpec((d, 2 * te), lambda i, j, t_: (0, jnp.maximum(j - 1, 0))),
                pl.BlockSpec((tm, d), lambda i, j, t_: (i, 0)),
                pl.BlockSpec((1, 6, d), lambda i, j, t_: (t_[0, i * per], 0, 0)),
                pl.BlockSpec(g2.shape, lambda i, j, t_: (0, 0)),
                pl.BlockSpec(b2.shape, lambda i, j, t_: (0, 0))]
    return pl.pallas_call(
        _peer_kernel,
        grid_spec=pltpu.PrefetchScalarGridSpec(
            num_scalar_prefetch=1, grid=(t // tm, n_tiles // 2 + 1), in_specs=in_specs,
            out_specs=pl.BlockSpec((tm, d), lambda i, j, t_: (i, 0)),
            scratch_shapes=[pltpu.VMEM((d, tm), F32), pltpu.VMEM((2, n_blk, te, LANES), F32),
                            pltpu.VMEM((2, n_blk, te, LANES), BF16)]),
        out_shape=jax.ShapeDtypeStruct((t, d), F32),
        compiler_params=_cparams(("arbitrary", "arbitrary")),
        name="peer_mix",
    )(tab, h2t, s2, p2, th1, p1, u_bf, vt_bf, x1, mods, g2, b2)


def _sincos(pos, dim):
    omega = 1.0 / (10000.0 ** (jnp.arange(dim // 2, dtype=F32) / (dim // 2)))
    ang = pos.astype(F32)[:, None] * omega[None, :]
    return jnp.concatenate([jnp.sin(ang), jnp.cos(ang)], -1)


def _grid_pos_embed(n_tok):
    rows = n_tok // GRID_W
    half = D_MODEL // 2
    er = _sincos(jnp.arange(rows), half)
    ec = _sincos(jnp.arange(GRID_W), half)
    emb = jnp.concatenate([jnp.broadcast_to(er[:, None, :], (rows, GRID_W, half)),
                           jnp.broadcast_to(ec[None, :, :], (rows, GRID_W, half))], -1)
    return emb.reshape(rows * GRID_W, D_MODEL)


def _seq_dft_tables(s_len):
    n = jnp.arange(s_len, dtype=jnp.int32)
    ang = ((n[:, None] * n[None, :]) % s_len).astype(F32) * (2.0 * math.pi / s_len)
    scale = 1.0 / math.sqrt(s_len * FNET_GW)
    return (jnp.cos(ang) * scale).astype(BF16), (-jnp.sin(ang) * scale).astype(BF16)


def _channel_dft_table():
    n = np.arange(FNET_GW)
    ang = ((n[:, None] * n[None, :]) % FNET_GW) * (2.0 * np.pi / FNET_GW)
    wc = np.zeros((FNET_W, 2 * FNET_W), np.float32)
    for g in range(FNET_GROUPS):
        sl = slice(g * FNET_GW, (g + 1) * FNET_GW)
        wc[sl, sl] = np.cos(ang)
        wc[sl, FNET_W + g * FNET_GW:FNET_W + (g + 1) * FNET_GW] = np.sin(ang)
    return jnp.asarray(wc, BF16)


def _segment_ones():
    idx = np.arange(WKV_W) // WKV_N
    return jnp.asarray((idx[:, None] == idx[None, :]).astype(np.float32), BF16)


def _chunk_tri():
    i = np.arange(ROW_TILE)
    same = (i[:, None] // CHUNK) == (i[None, :] // CHUNK)
    lower = same & (i[None, :] <= i[:, None])
    upper = same & (i[None, :] >= i[:, None])
    return jnp.asarray(np.stack([lower, upper]).astype(np.float32), BF16)


def _scan_masks():
    i = np.arange(PAIR)
    same = (i[:, None] // CHUNK) == (i[None, :] // CHUNK)
    t_row = i[:, None] % CHUNK
    t_col = i[None, :] % CHUNK
    masks = [same & (t_col < t_row), same & (t_col <= t_row), same & (t_col > t_row), same & (t_col >= t_row),
             np.eye(PAIR, dtype=bool)]
    return jnp.asarray(np.stack(masks).astype(np.float32))


def _block_diag(w):
    h, n, _ = w.shape
    eye = jnp.eye(h, dtype=w.dtype)
    return jnp.einsum("hij,hg->higj", w, eye).reshape(h * n, h * n)


def _pad_rank(w, d):
    r = w.shape[1]
    z = jnp.zeros_like(w[d])
    return jnp.concatenate([w[d], z] if d == 0 else [z, w[d]], axis=0)


def _state_to_pairs(s0):
    b = s0.shape[0]
    st = jnp.swapaxes(s0, -1, -2).reshape(b, 2, N_PAIRS, 2, WKV_N, WKV_N)
    eye = jnp.eye(2, dtype=s0.dtype)
    return jnp.einsum("bdpeji,ef->bdpejfi", st, eye).reshape(b, 2, N_PAIRS, PAIR, PAIR)


def _pairs_to_state(m):
    b = m.shape[0]
    m7 = m.reshape(b, 2, N_PAIRS, 2, WKV_N, 2, WKV_N)
    st = jnp.stack([m7[:, :, :, 0, :, 0, :], m7[:, :, :, 1, :, 1, :]], axis=3)
    return jnp.swapaxes(st.reshape(b, 2, WKV_H, WKV_N, WKV_N), -1, -2)


def _tile_table(groups):
    cv, first, last = [], [], []
    row = 0
    for gi, (n_seq, s_len) in enumerate(groups):
        nt = s_len // ROW_TILE
        for b in range(n_seq):
            for i in range(nt):
                cv.append(0 if gi == 0 else 1 + b)
                first.append(int(i == 0))
                last.append(int(i == nt - 1))
        row += n_seq * s_len
    return jnp.asarray(np.array([cv, first, last], np.int32))


def kernel(x_prompt, x_sample, state_lru, state_wkv, c, c_ctx, w_mod, b_mod, ln1_g, ln1_b, ln2_g, ln2_b,
           w_in_e, w_out_e, sconv_w, sconv_b, w_in_o, w_out_o, lru_conv_w, lru_conv_b, lru_wa, lru_ba,
           lru_wx, lru_bx, lru_lambda, wkv_mu, wkv_w0, wkv_w2, wkv_a0, wkv_a2, wkv_kk, wkv_ka, wkv_rk,
           wkv_g2, wkv_gn_g, wkv_gn_b, peer_wq, peer_keys, peer_u, peer_v):
    bp, sp, d = x_prompt.shape
    bs, ss, _ = x_sample.shape
    depth = w_mod.shape[0]
    assert sp % ROW_TILE == 0 and ss % PEER_TM == 0 and (bp * sp) % ss == 0
    assert bs + 1 <= 8
    groups = ((bp, sp), (bs, ss))
    tp = bp * sp
    tab = _tile_table(groups)

    cv8 = jnp.concatenate([c_ctx[None, :], c, jnp.zeros((8 - 1 - bs, d), F32)], axis=0)
    mods = _modulation(cv8, w_mod, b_mod).reshape(depth, 8, 6, d)

    xs = _add_pos(x_sample, _grid_pos_embed(ss).astype(x_sample.dtype))
    x = jnp.concatenate([x_prompt.reshape(tp, d), xs.reshape(bs * ss, d)], axis=0)

    wc = _channel_dft_table()
    dft = {s: _seq_dft_tables(s) for s in sorted({sp, ss})}
    seg = _segment_ones()
    tri = _chunk_tri()
    masks = _scan_masks()
    row2 = lambda a: a.reshape(1, -1)

    lru_fin = []
    wkv_fin = []
    for l in range(depth):
        j = l // 2
        m_l = mods[l]
        wq = peer_wq[l].astype(BF16)
        keys = peer_keys[l].reshape(2 * PEER_HEADS, N_KEYS, PEER_HALF).astype(BF16)
        if l % 2 == 0:
            zr, pq = _inproj(x, tab, m_l, w_in_e[j].astype(BF16), wc)
            y = jnp.concatenate([_seqdft(pq, bp * sp, 0, bp, sp, *dft[sp]),
                                 _seqdft(pq, bs * ss, tp, bs, ss, *dft[ss])], axis=0)
            x1, h2t, st = _post_even(x, tab, zr, y, m_l, sconv_w[j], row2(sconv_b[j]), w_out_e[j].astype(BF16),
                                     row2(ln1_g[l]), row2(ln1_b[l]), wq, keys)
        else:
            zl, zw = _inproj(x, tab, m_l, w_in_o[j].astype(BF16))
            lru_args = (lru_conv_w[j], lru_conv_b[j],
                        jnp.stack([_block_diag(lru_wa[j, dd]) for dd in range(2)]).astype(BF16), lru_ba[j],
                        jnp.stack([_block_diag(lru_wx[j, dd]) for dd in range(2)]).astype(BF16), lru_bx[j],
                        lru_lambda[j])
            hf_p, hb_p, hfin_p = _lru(zl, 0, bp, sp, jnp.zeros((bp, 2, LRU_W), F32), *lru_args)
            hf_s, hb_s, _ = _lru(zl, tp, bs, ss, state_lru[:, j], *lru_args)
            lru_fin.append(hfin_p)
            prep = _wkvprep(zw, tab, row2(wkv_mu[j]), wkv_w0[j],
                            jnp.stack([_pad_rank(wkv_w2[j], dd) for dd in range(2)]).astype(BF16), wkv_a0[j],
                            jnp.stack([_pad_rank(wkv_a2[j], dd) for dd in range(2)]).astype(BF16),
                            wkv_kk[j], wkv_ka[j], row2(wkv_rk[j]), wkv_g2[j].astype(BF16), seg, tri)
            v, bonus, g = prep[0], prep[1], prep[2]
            yf_p, yb_p, mfin_p = _wkvscan(prep[3:], v, 0, bp, sp,
                                          jnp.zeros((bp, 2, N_PAIRS, PAIR, PAIR), F32), masks)
            yf_s, yb_s, _ = _wkvscan(prep[3:], v, tp, bs, ss, _state_to_pairs(state_wkv[:, j]), masks)
            wkv_fin.append(_pairs_to_state(mfin_p))
            cat = lambda a, b: jnp.concatenate([a, b], axis=0)
            x1, h2t, st = _post_odd(x, tab, cat(hf_p, hf_s), cat(hb_p, hb_s), zl, cat(yf_p, yf_s), cat(yb_p, yb_s),
                                    bonus, g, m_l, seg, row2(wkv_gn_g[j]), row2(wkv_gn_b[j]),
                                    w_out_o[j].astype(BF16), row2(ln1_g[l]), row2(ln1_b[l]), wq, keys)
        st4 = st.reshape(PEER_HEADS, 2, N_KEYS, st.shape[-1])
        th1, p1, p2, s2 = _topk(st4)
        x = _peer(x1, tab, h2t, s2, p2, th1, p1, peer_u[l].astype(BF16), peer_v[l].T.astype(BF16), m_l,
                  row2(ln2_g[l]), row2(ln2_b[l]))

    y_prompt = x[:tp].reshape(bp, sp, d)
    y_sample = x[tp:].reshape(bs, ss, d)
    return (y_prompt, y_sample, jnp.stack(lru_fin, 1).astype(x_prompt.dtype),
            jnp.stack(wkv_fin, 1).astype(x_prompt.dtype))
```

```python
import functools
import math

import numpy as np
import jax
import jax.numpy as jnp
from jax import lax
from jax.experimental import pallas as pl
from jax.experimental.pallas import tpu as pltpu

F32 = jnp.float32
BF16 = jnp.bfloat16

D_MODEL = 1024
GRID_W = 64
FNET_W = 512
FNET_GROUPS = 4
FNET_GW = FNET_W // FNET_GROUPS
CONV_W = 512
LRU_W = 512
LRU_HEADS = 8
LRU_CONV_K = 4
LRU_C = 8.0
WKV_W = 512
WKV_N = 64
WKV_H = 8
WKV_IN = 1920
DECAY_SCALE = math.exp(-0.5)
WKV_GN_EPS = 64e-5
PEER_HEADS = 8
N_KEYS = 128
PEER_TOPK = 16
PEER_HALF = 128
DEPTH = 4
ALPHA = (2 * DEPTH) ** 0.25
LN_EPS = 1e-6

ROW_TILE = 256
LANES = 128
HALO = 8
CHUNK = 64
PAIR = 2 * WKV_N
N_PAIRS = WKV_H // 2
PEER_TM = 512
PEER_TE = 1024
VMEM_LIMIT = 56 * 1024 * 1024


def _cparams(sem):
    return pltpu.CompilerParams(dimension_semantics=sem, vmem_limit_bytes=VMEM_LIMIT)


def _dot(a, b):
    return jnp.dot(a, b, preferred_element_type=F32)


def _dot_nt(a, b):
    return lax.dot_general(a, b, (((1,), (1,)), ((), ())), preferred_element_type=F32)


def _mm3(a, b):
    ah = a.astype(BF16)
    al = (a - ah.astype(F32)).astype(BF16)
    bh = b.astype(BF16)
    bl = (b - bh.astype(F32)).astype(BF16)
    return _dot(ah, bh) + (_dot(ah, bl) + _dot(al, bh))


def _mm1(a, b):
    return _dot(a.astype(BF16), b.astype(BF16))


def _mm01(a, b01):
    h = a.astype(BF16)
    r = a - h.astype(F32)
    m = r.astype(BF16)
    lo = (r - m.astype(F32)).astype(BF16)
    return _dot(h, b01) + (_dot(m, b01) + _dot(lo, b01))


def _ln(x):
    mu = jnp.mean(x, axis=-1, keepdims=True)
    xc = x - mu
    var = jnp.mean(xc * xc, axis=-1, keepdims=True)
    return xc * lax.rsqrt(var + LN_EPS)


def _gelu(x):
    z = x * (0.7978845608028654 + 0.035677408136300125 * (x * x))
    hx = 0.5 * x
    return hx + hx * jnp.tanh(z)


def _sigmoid(x):
    return 1.0 / (1.0 + jnp.exp(-x))


def _mod_kernel(c_ref, w_ref, b_ref, o_ref):
    c = c_ref[...]
    sc = c * _sigmoid(c)
    o_ref[0] = _dot(sc.astype(BF16), w_ref[0].astype(BF16)) + b_ref[0]


def _modulation(cv8, w_mod, b_mod):
    depth, d, n = w_mod.shape
    tn = 1536
    return pl.pallas_call(
        _mod_kernel,
        grid=(depth, n // tn),
        in_specs=[pl.BlockSpec((8, d), lambda l, j: (0, 0)),
                  pl.BlockSpec((1, d, tn), lambda l, j: (l, 0, j)),
                  pl.BlockSpec((1, 1, tn), lambda l, j: (l, 0, j))],
        out_specs=pl.BlockSpec((1, 8, tn), lambda l, j: (l, 0, j)),
        out_shape=jax.ShapeDtypeStruct((depth, 8, n), F32),
        compiler_params=_cparams(("arbitrary", "arbitrary")),
        name="modulation",
    )(cv8, w_mod, b_mod.reshape(depth, 1, n))


def _embed_kernel(xp_ref, xs_ref, p_ref, o_ref, *, n_prompt):
    i = pl.program_id(0)

    @pl.when(i < n_prompt)
    def _():
        o_ref[...] = xp_ref[...]

    @pl.when(i >= n_prompt)
    def _():
        o_ref[...] = xs_ref[...] + p_ref[...]


def _embed(xp, xs, pos):
    tp, d = xp.shape
    ts = xs.shape[0]
    s_len = pos.shape[0]
    tm = 512
    n_prompt = tp // tm
    per_seq = s_len // tm
    return pl.pallas_call(
        functools.partial(_embed_kernel, n_prompt=n_prompt),
        grid=((tp + ts) // tm,),
        in_specs=[pl.BlockSpec((tm, d), lambda i: (jnp.minimum(i, n_prompt - 1), 0)),
                  pl.BlockSpec((tm, d), lambda i: (jnp.maximum(i - n_prompt, 0), 0)),
                  pl.BlockSpec((tm, d), lambda i: (jnp.maximum(i - n_prompt, 0) % per_seq, 0))],
        out_specs=pl.BlockSpec((tm, d), lambda i: (i, 0)),
        out_shape=jax.ShapeDtypeStruct((tp + ts, d), xp.dtype),
        compiler_params=_cparams(("arbitrary",)),
        name="embed",
    )(xp, xs, pos)


def _inproj_even_kernel(tab_ref, x_ref, m_ref, w_ref, wc_ref, zr_ref, pq_ref):
    m = m_ref[0]
    h = _ln(x_ref[...]) * (1.0 + m[1:2]) + m[0:1]
    z = _dot(h.astype(BF16), w_ref[...])
    zr_ref[...] = z[:, FNET_W:]
    pq_ref[...] = _dot(z[:, :FNET_W].astype(BF16), wc_ref[...]).astype(BF16)


def _inproj_odd_kernel(tab_ref, x_ref, m_ref, w_ref, zl_ref, zw_ref):
    m = m_ref[0]
    h = _ln(x_ref[...]) * (1.0 + m[1:2]) + m[0:1]
    z = _dot(h.astype(BF16), w_ref[...])
    zl_ref[...] = z[:, :2 * LRU_W]
    zw_ref[...] = z[:, 2 * LRU_W:]


def _const_spec(shape):
    nd = len(shape)
    return pl.BlockSpec(shape, lambda i, t, _n=nd: (0,) * _n)


def _inproj(x, tab, mods, w_in, wc=None):
    t, d = x.shape
    n = w_in.shape[1]
    tm = ROW_TILE
    even = wc is not None
    in_specs = [pl.BlockSpec((tm, d), lambda i, t_: (i, 0)),
                pl.BlockSpec((1, 6, d), lambda i, t_: (t_[0, i], 0, 0)),
                _const_spec((d, n))]
    if even:
        in_specs.append(_const_spec(wc.shape))
        widths = (n - FNET_W, 2 * FNET_W)
        dtypes = (F32, BF16)
        kern = _inproj_even_kernel
        args = (tab, x, mods, w_in, wc)
    else:
        widths = (2 * LRU_W, n - 2 * LRU_W)
        dtypes = (F32, F32)
        kern = _inproj_odd_kernel
        args = (tab, x, mods, w_in)
    return pl.pallas_call(
        kern,
        grid_spec=pltpu.PrefetchScalarGridSpec(
            num_scalar_prefetch=1, grid=(t // tm,), in_specs=in_specs,
            out_specs=[pl.BlockSpec((tm, w), lambda i, t_: (i, 0)) for w in widths]),
        out_shape=[jax.ShapeDtypeStruct((t, w), dt) for w, dt in zip(widths, dtypes)],
        compiler_params=_cparams(("arbitrary",)),
        name="inproj_even" if even else "inproj_odd",
    )(*args)


def _seqdft_kernel(c_ref, s_ref, p_ref, q_ref, y_ref):
    y_ref[...] = (_dot(c_ref[...], p_ref[...]) + _dot(s_ref[...], q_ref[...])).astype(BF16)


def _seqdft(pq, y_rows, row0, n_seq, s_len, cmat, smat):
    tm = min(s_len, ROW_TILE)
    nt = s_len // tm
    blk0 = row0 // s_len
    return pl.pallas_call(
        _seqdft_kernel,
        grid=(n_seq, nt),
        in_specs=[pl.BlockSpec((tm, s_len), lambda b, i: (i, 0)),
                  pl.BlockSpec((tm, s_len), lambda b, i: (i, 0)),
                  pl.BlockSpec((s_len, FNET_W), lambda b, i: (blk0 + b, 0)),
                  pl.BlockSpec((s_len, FNET_W), lambda b, i: (blk0 + b, 1))],
        out_specs=pl.BlockSpec((tm, FNET_W), lambda b, i: (b * nt + i, 0)),
        out_shape=jax.ShapeDtypeStruct((y_rows, FNET_W), BF16),
        compiler_params=_cparams(("arbitrary", "arbitrary")),
        name="seqdft_%d" % s_len,
    )(cmat, smat, pq, pq)


def _post_and_query(x, mix, m, g1, b1, wq_ref, keys_ref, x1_ref, h2t_ref, topk_refs, st_ref, v1_ref, v2_ref):
    x1 = _ln(ALPHA * x + m[2:3] * mix) * g1 + b1
    x1_ref[...] = x1
    h2 = _ln(x1) * (1.0 + m[4:5]) + m[3:4]
    h2b = h2.astype(BF16)
    h2t_ref[...] = h2.T.astype(BF16)
    q = _dot(h2b, wq_ref[...])
    for hp in range(2 * PEER_HEADS):
        qb = q[:, hp * PEER_HALF:(hp + 1) * PEER_HALF].astype(BF16)
        st_ref[hp * N_KEYS:(hp + 1) * N_KEYS, :] = _dot_nt(keys_ref[hp], qb)
    _topk_heads(st_ref, *topk_refs, v1_ref, v2_ref)


def _post_even_kernel(tab_ref, x_ref, bg_ref, cg_ref, xi_ref, cgp_ref, xip_ref, cgn_ref, xin_ref, y_ref,
                      m_ref, cw_ref, cb_ref, wo_ref, g1_ref, b1_ref, wq_ref, keys_ref,
                      x1_ref, h2t_ref, th1_ref, p1_ref, p2_ref, s2_ref, ext_ref, st_ref, v1_ref, v2_ref):
    i = pl.program_id(0)
    tm = ROW_TILE
    first = tab_ref[1, i] == 1
    last = tab_ref[2, i] == 1
    u = cg_ref[...] * xi_ref[...]
    ext_ref[0:HALO] = jnp.where(first, 0.0, cgp_ref[...] * xip_ref[...])
    ext_ref[HALO:HALO + tm] = u
    ext_ref[HALO + tm:2 * HALO + tm] = jnp.where(last, 0.0, cgn_ref[...] * xin_ref[...])
    cw = cw_ref[...]
    conv = (ext_ref[HALO - 1:HALO - 1 + tm] * cw[0:1] + u * cw[1:2]
            + ext_ref[HALO + 1:HALO + 1 + tm] * cw[2:3] + cb_ref[...])
    ymix = (bg_ref[...] * conv).astype(BF16)
    mix = _dot(y_ref[...], wo_ref[0:FNET_W]) + _dot(ymix, wo_ref[FNET_W:])
    _post_and_query(x_ref[...], mix, m_ref[0], g1_ref[...], b1_ref[...], wq_ref, keys_ref,
                    x1_ref, h2t_ref, (th1_ref, p1_ref, p2_ref, s2_ref), st_ref, v1_ref, v2_ref)


def _post_out_specs(t, d, tm):
    rshape = jax.ShapeDtypeStruct((t // LANES, PEER_HEADS, N_KEYS, LANES), F32)
    rspec = pl.BlockSpec((tm // LANES, PEER_HEADS, N_KEYS, LANES), lambda i, t_: (i, 0, 0, 0))
    specs = [pl.BlockSpec((tm, d), lambda i, t_: (i, 0)),
             pl.BlockSpec((d, tm), lambda i, t_: (0, i))] + [rspec] * 4
    shapes = [jax.ShapeDtypeStruct((t, d), F32),
              jax.ShapeDtypeStruct((d, t), BF16)] + [rshape] * 4
    return specs, shapes


def _post_scratch(tm):
    return [pltpu.VMEM((2 * PEER_HEADS * N_KEYS, tm), F32), pltpu.VMEM((TOP_ROWS, tm), F32),
            pltpu.VMEM((TOP_ROWS, tm), F32)]


def _post_even(x, tab, zr, y, mods, cw, cb, wo, g1, b1, wq, keys):
    t, d = x.shape
    tm = ROW_TILE
    hb = tm // HALO
    nblk = t // HALO
    prev = lambda c: pl.BlockSpec((HALO, CONV_W), lambda i, t_, _c=c: (jnp.maximum(i * hb - 1, 0), _c))
    nxt = lambda c: pl.BlockSpec((HALO, CONV_W), lambda i, t_, _c=c: (jnp.minimum((i + 1) * hb, nblk - 1), _c))
    col = lambda c: pl.BlockSpec((tm, CONV_W), lambda i, t_, _c=c: (i, _c))
    in_specs = [pl.BlockSpec((tm, d), lambda i, t_: (i, 0)),
                col(0), col(1), col(2), prev(1), prev(2), nxt(1), nxt(2),
                pl.BlockSpec((tm, FNET_W), lambda i, t_: (i, 0)),
                pl.BlockSpec((1, 6, d), lambda i, t_: (t_[0, i], 0, 0)),
                _const_spec(cw.shape), _const_spec(cb.shape), _const_spec(wo.shape),
                _const_spec(g1.shape), _const_spec(b1.shape), _const_spec(wq.shape), _const_spec(keys.shape)]
    out_specs, out_shape = _post_out_specs(t, d, tm)
    return pl.pallas_call(
        _post_even_kernel,
        grid_spec=pltpu.PrefetchScalarGridSpec(
            num_scalar_prefetch=1, grid=(t // tm,), in_specs=in_specs, out_specs=out_specs,
            scratch_shapes=[pltpu.VMEM((tm + 2 * HALO, CONV_W), F32)] + _post_scratch(tm)),
        out_shape=out_shape,
        compiler_params=_cparams(("arbitrary",)),
        name="post_even",
    )(tab, x, zr, zr, zr, zr, zr, zr, zr, y, mods, cw, cb, wo, g1, b1, wq, keys)


def _lru_kernel(xf_ref, xfh_ref, xb_ref, xbh_ref, cw_ref, cb_ref, wa_ref, ba_ref, wx_ref, bx_ref, lam_ref,
                h0_ref, hf_ref, hb_ref, hfin_ref, ext_ref, carry_ref):
    i = pl.program_id(1)
    nt = pl.num_programs(1)
    tm = ROW_TILE

    @pl.when(i == 0)
    def _():
        carry_ref[...] = h0_ref[0]

    row = lax.broadcasted_iota(jnp.int32, (tm, 1), 0)
    for d in range(2):
        x = (xf_ref if d == 0 else xb_ref)[...]
        halo = jnp.where(i == 0, 0.0, (xfh_ref if d == 0 else xbh_ref)[...])
        ext_ref[HALO:HALO + tm] = x
        if d == 0:
            ext_ref[0:HALO] = halo
        else:
            ext_ref[HALO + tm:2 * HALO + tm] = halo
        cw = cw_ref[d]
        xc = cb_ref[d:d + 1] + x * cw[LRU_CONV_K - 1:LRU_CONV_K]
        for j in range(LRU_CONV_K - 1):
            k = LRU_CONV_K - 1 - j
            off = HALO - k if d == 0 else HALO + k
            xc = xc + ext_ref[off:off + tm] * cw[j:j + 1]
        xcb = xc.astype(BF16)
        gate_r = _sigmoid(_dot(xcb, wa_ref[d]) + ba_ref[d:d + 1])
        gate_i = _sigmoid(_dot(xcb, wx_ref[d]) + bx_ref[d:d + 1])
        nl = -lam_ref[d:d + 1]
        softplus = jnp.maximum(nl, 0.0) + jnp.log1p(jnp.exp(-jnp.abs(nl)))
        log_a = -LRU_C * gate_r * softplus
        a = jnp.exp(log_a)
        b = jnp.sqrt(-jnp.tanh(log_a) * (a * a + 1.0)) * (gate_i * xc)
        s = 1
        while s < tm:
            if d == 0:
                keep = row >= s
                sh = s
            else:
                keep = row < tm - s
                sh = tm - s
            a_sh = jnp.where(keep, pltpu.roll(a, sh, 0), 1.0)
            b_sh = jnp.where(keep, pltpu.roll(b, sh, 0), 0.0)
            b = a * b_sh + b
            a = a * a_sh
            s *= 2
        h = a * carry_ref[d:d + 1] + b
        if d == 0:
            hf_ref[...] = h
            carry_ref[0:1] = h[tm - 1:tm]
        else:
            hb_ref[...] = h
            carry_ref[1:2] = h[0:1]

    @pl.when(i == nt - 1)
    def _():
        hfin_ref[0] = carry_ref[...]


def _lru(zl, row0, n_seq, s_len, h0, cw, cb, wa, ba, wx, bx, lam):
    tm = ROW_TILE
    nt = s_len // tm
    t0 = row0 // tm
    hb = tm // HALO
    nblk = zl.shape[0] // HALO
    rows = n_seq * s_len
    fwd = lambda b, i: (t0 + b * nt + i, 0)
    bwd = lambda b, i: (t0 + b * nt + nt - 1 - i, 0)
    fwd_h = lambda b, i: (jnp.maximum((t0 + b * nt + i) * hb - 1, 0), 0)
    bwd_h = lambda b, i: (jnp.minimum((t0 + b * nt + nt - i) * hb, nblk - 1), 0)
    out_f = lambda b, i: (b * nt + i, 0)
    out_b = lambda b, i: (b * nt + nt - 1 - i, 0)
    cst = lambda a: pl.BlockSpec(a.shape, lambda b, i, _n=a.ndim: (0,) * _n)
    return pl.pallas_call(
        _lru_kernel,
        grid=(n_seq, nt),
        in_specs=[pl.BlockSpec((tm, LRU_W), fwd), pl.BlockSpec((HALO, LRU_W), fwd_h),
                  pl.BlockSpec((tm, LRU_W), bwd), pl.BlockSpec((HALO, LRU_W), bwd_h),
                  cst(cw), cst(cb), cst(wa), cst(ba), cst(wx), cst(bx), cst(lam),
                  pl.BlockSpec((1, 2, LRU_W), lambda b, i: (b, 0, 0))],
        out_specs=[pl.BlockSpec((tm, LRU_W), out_f), pl.BlockSpec((tm, LRU_W), out_b),
                   pl.BlockSpec((1, 2, LRU_W), lambda b, i: (b, 0, 0))],
        out_shape=[jax.ShapeDtypeStruct((rows, LRU_W), F32), jax.ShapeDtypeStruct((rows, LRU_W), F32),
                   jax.ShapeDtypeStruct((n_seq, 2, LRU_W), F32)],
        scratch_shapes=[pltpu.VMEM((tm + 2 * HALO, LRU_W), F32), pltpu.VMEM((2, LRU_W), F32)],
        compiler_params=_cparams(("arbitrary", "arbitrary")),
        name="lru_%d" % s_len,
    )(zl, zl, zl, zl, cw, cb, wa, ba, wx, bx, lam, h0)


def _wkvprep_kernel(tab_ref, z_ref, zp_ref, zn_ref, mu_ref, w0_ref, w2_ref, a0_ref, a2_ref, kk_ref, ka_ref,
                    rk_ref, g2_ref, seg_ref, tri_ref,
                    v_ref, bg_ref, g_ref, rf_ref, kf_ref, bf_ref, qf_ref, cf_ref,
                    rb_ref, kb_ref, bb_ref, qb_ref, cb_ref, ext_ref):
    i = pl.program_id(0)
    tm = ROW_TILE
    w = WKV_W
    first = tab_ref[1, i] == 1
    last = tab_ref[2, i] == 1
    z = z_ref[...]
    ext_ref[0:HALO] = jnp.where(first, 0.0, zp_ref[...])
    ext_ref[HALO:HALO + tm] = z
    ext_ref[HALO + tm:2 * HALO + tm] = jnp.where(last, 0.0, zn_ref[...])
    z = z + mu_ref[...] * (0.5 * (ext_ref[HALO - 1:HALO - 1 + tm] + ext_ref[HALO + 1:HALO + 1 + tm]) - z)
    r = z[:, 0:w]
    k = z[:, w:2 * w]
    v = z[:, 2 * w:3 * w]
    wd = jnp.tanh(z[:, 3 * w:3 * w + 128]).astype(BF16)
    ad = z[:, 3 * w + 128:3 * w + 256].astype(BF16)
    gd = _sigmoid(z[:, 3 * w + 256:3 * w + 384]).astype(BF16)
    v_ref[...] = v
    g_ref[...] = _dot(gd, g2_ref[...])
    seg = seg_ref[...]
    rk = rk_ref[...]
    bonus = jnp.zeros((tm, w), F32)
    outs = ((rf_ref, kf_ref, bf_ref, qf_ref, cf_ref), (rb_ref, kb_ref, bb_ref, qb_ref, cb_ref))
    nch = tm // CHUNK
    for d in range(2):
        r_ref, k_ref, b_ref, q_ref, c_ref = outs[d]
        wz = w0_ref[d:d + 1] + _dot(wd, w2_ref[d])
        logw = -DECAY_SCALE * _sigmoid(wz)
        iclr = _sigmoid(a0_ref[d:d + 1] + _dot(ad, a2_ref[d]))
        kk = k * kk_ref[d:d + 1]
        kk = kk * lax.rsqrt(jnp.maximum(_mm01(kk * kk, seg), 1e-24))
        km = k * (1.0 + (iclr - 1.0) * ka_ref[d:d + 1])
        bonus = bonus + _mm01(r * km * rk, seg) * v
        lw_h = logw.astype(BF16)
        lw_r = logw - lw_h.astype(F32)
        lw_m = lw_r.astype(BF16)
        lw_l = (lw_r - lw_m.astype(F32)).astype(BF16)
        tri = tri_ref[d]
        cl = _dot(tri, lw_h) + (_dot(tri, lw_m) + _dot(tri, lw_l))
        c = jnp.exp(cl)
        cinv = jnp.exp(-cl)
        r_ref[...] = r * c
        k_ref[...] = km * cinv
        b_ref[...] = kk * iclr * cinv
        q_ref[...] = kk * jnp.exp(cl - logw)
        for j in range(nch):
            edge = (j + 1) * CHUNK - 1 if d == 0 else j * CHUNK
            c_ref[j] = c[edge:edge + 1]
    bg_ref[...] = bonus


def _wkvprep(zw, tab, mu, w0, w2p, a0, a2p, kk, ka, rk, g2, seg, tri):
    t, n = zw.shape
    tm = ROW_TILE
    hb = tm // HALO
    nblk = t // HALO
    nch = tm // CHUNK
    w = WKV_W
    in_specs = [pl.BlockSpec((tm, n), lambda i, t_: (i, 0)),
                pl.BlockSpec((HALO, n), lambda i, t_: (jnp.maximum(i * hb - 1, 0), 0)),
                pl.BlockSpec((HALO, n), lambda i, t_: (jnp.minimum((i + 1) * hb, nblk - 1), 0))]
    in_specs += [_const_spec(a.shape) for a in (mu, w0, w2p, a0, a2p, kk, ka, rk, g2, seg, tri)]
    row = pl.BlockSpec((tm, w), lambda i, t_: (i, 0))
    cspec = pl.BlockSpec((nch, 1, w), lambda i, t_: (i, 0, 0))
    rshape = jax.ShapeDtypeStruct((t, w), F32)
    cshape = jax.ShapeDtypeStruct((t // CHUNK, 1, w), F32)
    out_specs = [row, row, row] + [row, row, row, row, cspec] * 2
    out_shape = [rshape, rshape, rshape] + [rshape, rshape, rshape, rshape, cshape] * 2
    return pl.pallas_call(
        _wkvprep_kernel,
        grid_spec=pltpu.PrefetchScalarGridSpec(
            num_scalar_prefetch=1, grid=(t // tm,), in_specs=in_specs, out_specs=out_specs,
            scratch_shapes=[pltpu.VMEM((tm + 2 * HALO, n), F32)]),
        out_shape=out_shape,
        compiler_params=_cparams(("arbitrary",)),
        name="wkv_prep",
    )(tab, zw, zw, zw, mu, w0, w2p, a0, a2p, kk, ka, rk, g2, seg, tri)


def _wkv_chunks(items, eye):
    lane = lax.broadcasted_iota(jnp.int32, (1, PAIR), 1)
    m0 = (lane < WKV_N).astype(F32)
    m1 = 1.0 - m0
    stack = lambda x: jnp.concatenate([x * m0, x * m1], axis=0)
    n = range(len(items))
    rh_s = [stack(it[0]) for it in items]
    kh_s = [stack(it[1]) for it in items]
    bh_s = [stack(it[2]) for it in items]
    kq_s = [stack(it[3]) for it in items]
    v_s = [stack(it[4]) for it in items]
    bh_t = [x.T for x in bh_s]
    kh_t = [x.T for x in kh_s]
    bk_t = [jnp.concatenate([bh_t[i], kh_t[i]], axis=1) for i in n]
    att_s = [_mm3(kq_s[i], bk_t[i]) for i in n]
    att_y = [_mm1(rh_s[i], bk_t[i]) for i in n]
    n1 = [att_s[i][:, 0:PAIR] * items[i][7] for i in n]
    ak = [att_s[i][:, PAIR:] * items[i][7] for i in n]
    gb = [att_y[i][:, 0:PAIR] * items[i][8] for i in n]
    gk = [att_y[i][:, PAIR:] * items[i][8] for i in n]
    t_inv = [eye - n1[i] for i in n]
    npow = n1
    for _ in range(5):
        npow = [_mm1(npow[i], npow[i]) for i in n]
        t_inv = [t_inv[i] + _mm1(t_inv[i], npow[i]) for i in n]
    kv = [_mm1(jnp.concatenate([kh_t[i], gk[i], ak[i]], axis=0), v_s[i]) for i in n]
    x = [_mm1(t_inv[i], jnp.concatenate([kq_s[i], kv[i][2 * PAIR:]], axis=1)) for i in n]
    bx = [_mm1(jnp.concatenate([bh_t[i], gb[i]], axis=0), x[i]) for i in n]
    p_mat = [eye - bx[i][0:PAIR, 0:PAIR] for i in n]
    q_mat = [kv[i][0:PAIR] - bx[i][0:PAIR, PAIR:] for i in n]
    r_til = [rh_s[i] - bx[i][PAIR:, 0:PAIR] for i in n]
    y0 = [kv[i][PAIR:2 * PAIR] - bx[i][PAIR:, PAIR:] for i in n]
    pm = [_mm3(p_mat[i], items[i][6]) for i in n]
    y_st = [_mm1(r_til[i], items[i][6]) + y0[i] for i in n]
    ys = [y_st[i][0:CHUNK] + y_st[i][CHUNK:] for i in n]
    c_col = [jnp.broadcast_to(items[i][5], (PAIR, PAIR)).T for i in n]
    m_new = [(pm[i] + q_mat[i]) * c_col[i] for i in n]
    return ys, m_new


def _wkvscan_kernel(rf_ref, kf_ref, bf_ref, qf_ref, vf_ref, cf_ref, rb_ref, kb_ref, bb_ref, qb_ref, vb_ref, cb_ref,
                    m0_ref, msk_ref, yf_ref, yb_ref, mfin_ref, m_ref):
    i = pl.program_id(1)
    nc = pl.num_programs(1)

    @pl.when(i == 0)
    def _():
        m_ref[...] = m0_ref[0]

    eye = msk_ref[4]
    ins = ((rf_ref, kf_ref, bf_ref, qf_ref, vf_ref, cf_ref, yf_ref),
           (rb_ref, kb_ref, bb_ref, qb_ref, vb_ref, cb_ref, yb_ref))
    items = []
    for d in range(2):
        r_ref, k_ref, b_ref, q_ref, v_ref, c_ref, _ = ins[d]
        for p in range(N_PAIRS):
            sl = slice(p * PAIR, (p + 1) * PAIR)
            items.append((r_ref[:, sl], k_ref[:, sl], b_ref[:, sl], q_ref[:, sl], v_ref[:, sl],
                          c_ref[0, :, sl], m_ref[d, p], msk_ref[2 * d], msk_ref[2 * d + 1]))
    ys, m_new = _wkv_chunks(items, eye)
    for d in range(2):
        for p in range(N_PAIRS):
            sl = slice(p * PAIR, (p + 1) * PAIR)
            ins[d][6][:, sl] = ys[d * N_PAIRS + p]
            m_ref[d, p] = m_new[d * N_PAIRS + p]

    @pl.when(i == nc - 1)
    def _():
        mfin_ref[0] = m_ref[...]


def _wkvscan(prep, v, row0, n_seq, s_len, m0, masks):
    rf, kf, bf, qf, cf, rb, kb, bb, qb, cb = prep
    nc = s_len // CHUNK
    c0 = row0 // CHUNK
    rows = n_seq * s_len
    w = WKV_W
    fwd = lambda b, i: (c0 + b * nc + i, 0)
    bwd = lambda b, i: (c0 + b * nc + nc - 1 - i, 0)
    fwd3 = lambda b, i: (c0 + b * nc + i, 0, 0)
    bwd3 = lambda b, i: (c0 + b * nc + nc - 1 - i, 0, 0)
    blk = lambda im: pl.BlockSpec((CHUNK, w), im)
    cblk = lambda im: pl.BlockSpec((1, 1, w), im)
    mspec = pl.BlockSpec((1, 2, N_PAIRS, PAIR, PAIR), lambda b, i: (b, 0, 0, 0, 0))
    return pl.pallas_call(
        _wkvscan_kernel,
        grid=(n_seq, nc),
        in_specs=[blk(fwd)] * 5 + [cblk(fwd3)] + [blk(bwd)] * 5 + [cblk(bwd3)]
        + [mspec, pl.BlockSpec(masks.shape, lambda b, i: (0, 0, 0))],
        out_specs=[pl.BlockSpec((CHUNK, w), lambda b, i: (b * nc + i, 0)),
                   pl.BlockSpec((CHUNK, w), lambda b, i: (b * nc + nc - 1 - i, 0)),
                   mspec],
        out_shape=[jax.ShapeDtypeStruct((rows, w), F32), jax.ShapeDtypeStruct((rows, w), F32),
                   jax.ShapeDtypeStruct(m0.shape, F32)],
        scratch_shapes=[pltpu.VMEM((2, N_PAIRS, PAIR, PAIR), F32)],
        compiler_params=_cparams(("arbitrary", "arbitrary")),
        name="wkv_scan_%d" % s_len,
    )(rf, kf, bf, qf, v, cf, rb, kb, bb, qb, v, cb, m0, masks)


def _post_odd_kernel(tab_ref, x_ref, hfp_ref, hfs_ref, hbp_ref, hbs_ref, gb_ref, yfp_ref, yfs_ref, ybp_ref, ybs_ref,
                     bon_ref, g_ref, m_ref, seg_ref, gng_ref, gnb_ref, wo_ref, g1_ref, b1_ref, wq_ref, keys_ref,
                     x1_ref, h2t_ref, th1_ref, p1_ref, p2_ref, s2_ref, st_ref, v1_ref, v2_ref, *, n_prompt_tiles):
    is_prompt = pl.program_id(0) < n_prompt_tiles
    pick = lambda p_ref, s_ref: jnp.where(is_prompt, p_ref[...], s_ref[...])
    y_lru = ((pick(hfp_ref, hfs_ref) + pick(hbp_ref, hbs_ref)) * _gelu(gb_ref[...])).astype(BF16)
    seg = seg_ref[...]
    ys = pick(yfp_ref, yfs_ref) + pick(ybp_ref, ybs_ref)
    mean = _mm01(ys, seg) * (1.0 / WKV_N)
    yc = ys - mean
    var = _mm01(yc * yc, seg) * (1.0 / WKV_N)
    yn = yc * lax.rsqrt(var + WKV_GN_EPS) * gng_ref[...] + gnb_ref[...]
    y_wkv = ((yn + bon_ref[...]) * g_ref[...]).astype(BF16)
    mix = _dot(y_lru, wo_ref[0:LRU_W]) + _dot(y_wkv, wo_ref[LRU_W:])
    _post_and_query(x_ref[...], mix, m_ref[0], g1_ref[...], b1_ref[...], wq_ref, keys_ref,
                    x1_ref, h2t_ref, (th1_ref, p1_ref, p2_ref, s2_ref), st_ref, v1_ref, v2_ref)


def _post_odd(x, tab, hf, hb, zl, yf, yb, bonus, g, mods, seg, gng, gnb, wo, g1, b1, wq, keys):
    t, d = x.shape
    tm = ROW_TILE
    w = WKV_W
    n_p = hf[0].shape[0] // tm
    n_s = hf[1].shape[0] // tm
    row = pl.BlockSpec((tm, w), lambda i, t_: (i, 0))
    row_p = pl.BlockSpec((tm, w), lambda i, t_: (jnp.minimum(i, n_p - 1), 0))
    row_s = pl.BlockSpec((tm, w), lambda i, t_: (jnp.clip(i - n_p, 0, n_s - 1), 0))
    in_specs = [pl.BlockSpec((tm, d), lambda i, t_: (i, 0)), row_p, row_s, row_p, row_s,
                pl.BlockSpec((tm, LRU_W), lambda i, t_: (i, 1)), row_p, row_s, row_p, row_s, row, row,
                pl.BlockSpec((1, 6, d), lambda i, t_: (t_[0, i], 0, 0))]
    in_specs += [_const_spec(a.shape) for a in (seg, gng, gnb, wo, g1, b1, wq, keys)]
    out_specs, out_shape = _post_out_specs(t, d, tm)
    return pl.pallas_call(
        functools.partial(_post_odd_kernel, n_prompt_tiles=n_p),
        grid_spec=pltpu.PrefetchScalarGridSpec(
            num_scalar_prefetch=1, grid=(t // tm,), in_specs=in_specs, out_specs=out_specs,
            scratch_shapes=_post_scratch(tm)),
        out_shape=out_shape,
        compiler_params=_cparams(("arbitrary",)),
        name="post_odd",
    )(tab, x, hf[0], hf[1], hb[0], hb[1], zl, yf[0], yf[1], yb[0], yb[1], bonus, g, mods, seg, gng, gnb, wo,
      g1, b1, wq, keys)


N_TOP = PEER_TOPK + 1
TOP_ROWS = 24
SUBLANES = 8


def _batcher_network(n):
    def merge(lo, hi, r):
        step = r * 2
        if step < hi - lo:
            yield from merge(lo, hi, step)
            yield from merge(lo + r, hi, step)
            yield from [(i, i + r) for i in range(lo + r, hi - r, step)]
        else:
            yield (lo, lo + r)

    def sort(lo, hi):
        if hi - lo >= 1:
            mid = lo + (hi - lo) // 2
            yield from sort(lo, mid)
            yield from sort(mid + 1, hi)
            yield from merge(lo, hi, 1)

    return tuple(sort(0, n - 1))


def _sort_levels(levels, net):
    lv = list(levels)
    for i, j in net:
        a, b = lv[i], lv[j]
        lv[i] = jnp.maximum(a, b)
        lv[j] = jnp.minimum(a, b)
    return lv


def _pop_sorted(levels, n_top, emit):
    lv = list(levels)
    for it in range(n_top):
        m = jnp.max(lv[0], axis=0, keepdims=True)
        emit(it, m)
        hit = lv[0] == m
        live = min(len(lv), n_top - it - 1)
        for k in range(live):
            below = lv[k + 1] if k + 1 < len(lv) else -jnp.inf
            lv[k] = jnp.where(hit, below, lv[k])


def _topk_heads(st_ref, th1_ref, p1_ref, p2_ref, s2_ref, v1_ref, v2_ref):
    tl = st_ref.shape[-1]
    neg = -jnp.inf
    net16 = _batcher_network(N_KEYS // SUBLANES)
    net8 = _batcher_network(SUBLANES)
    row = lax.broadcasted_iota(jnp.int32, (SUBLANES, tl), 0)
    roll = lambda x, sh: pltpu.roll(x, sh, 0)

    def top_values(x, store):
        store[...] = jnp.full(store.shape, neg, F32)
        lv = _sort_levels([x[SUBLANES * k:SUBLANES * (k + 1)] for k in range(N_KEYS // SUBLANES)], net16)

        def emit(it, m):
            store[it:it + 1, :] = m

        _pop_sorted(lv, N_TOP, emit)

    def head(h, carry):
        s1 = st_ref[pl.ds(pl.multiple_of(2 * h * N_KEYS, N_KEYS), N_KEYS), :]
        s2 = st_ref[pl.ds(pl.multiple_of((2 * h + 1) * N_KEYS, N_KEYS), N_KEYS), :]
        top_values(s1, v1_ref)
        top_values(s2, v2_ref)
        one = lambda ref, a: ref[a:a + 1, :]
        v2a = v2_ref[0:8, :]
        v1b2 = roll(v1_ref[8:16, :], 2)
        cands = [one(v1_ref, 0) + v2a, one(v1_ref, 0) + v2_ref[8:16, :], one(v1_ref, 0) + v2_ref[16:24, :],
                 one(v1_ref, 1) + v2a,
                 jnp.where(row < 5, one(v1_ref, 2), one(v1_ref, 4)) + jnp.where(row < 5, v2a, roll(v2a, 5)),
                 jnp.where(row < 4, one(v1_ref, 3), jnp.where(row < 6, one(v1_ref, 5), one(v1_ref, 6)))
                 + jnp.where(row < 4, v2a, jnp.where(row < 6, roll(v2a, 4), roll(v2a, 6))),
                 jnp.where(row < 2, one(v1_ref, 7), v1b2) + jnp.where(row < 2, v2a, one(v2_ref, 0)),
                 jnp.where(row < 2, v1b2, roll(v1_ref[16:24, :], 2)) + one(v2_ref, 0)]
        top = []
        _pop_sorted(_sort_levels(cands, net8), N_TOP, lambda it, m: top.append(m))
        tau = 0.5 * (top[PEER_TOPK - 1] + top[PEER_TOPK])
        mx1 = v1_ref[0:1, :]
        mx2 = v2_ref[0:1, :]
        zacc = jnp.zeros((SUBLANES, tl), F32)
        for cnd in cands:
            zacc = zacc + jnp.where(cnd >= tau, jnp.exp(cnd - (mx1 + mx2)), 0.0)
        zsum = jnp.sum(zacc, axis=0, keepdims=True)
        th1 = tau - s1
        p1 = jnp.exp(s1 - mx1) / zsum
        p2 = jnp.exp(s2 - mx2)
        for cb in range(tl // LANES):
            cs = slice(cb * LANES, (cb + 1) * LANES)
            th1_ref[cb, h] = th1[:, cs]
            p1_ref[cb, h] = p1[:, cs]
            p2_ref[cb, h] = p2[:, cs]
            s2_ref[cb, h] = s2[:, cs]
        return carry

    lax.fori_loop(0, PEER_HEADS, head, 0)


def _peer_kernel(tab_ref, h2t_ref, s2_ref, p2_ref, th1_ref, p1_ref, u_ref, vt_ref, x1_ref, m_ref, g2_ref, b2_ref,
                 o_ref, acc_ref, act_ref, ga_ref):
    s = pl.program_id(1)
    n_steps = pl.num_programs(1)
    n_blk = PEER_TM // LANES
    rows_per_tile = PEER_TE // N_KEYS
    d_rows = acc_ref.shape[0] // n_blk
    u_rows = PEER_TE // n_blk

    def project(half, q):
        ro = pl.multiple_of(q * u_rows, u_rows)
        return _dot(u_ref[pl.ds(half * PEER_TE + ro, u_rows), :], h2t_ref[...])

    def store_act(slot, q, a):
        ro = pl.multiple_of(q * u_rows, u_rows)
        for k in range(n_blk):
            act_ref[slot, k, pl.ds(ro, u_rows), :] = a[:, k * LANES:(k + 1) * LANES]

    def apply_v(slot, half, q):
        ro = pl.multiple_of(q * d_rows, d_rows)
        ga = jnp.concatenate([ga_ref[slot, k] for k in range(n_blk)], axis=1)
        return _dot(vt_ref[pl.ds(ro, d_rows), half * PEER_TE:(half + 1) * PEER_TE], ga)

    def stage(cur, tile):
        nxt = 1 - cur
        r0 = pl.multiple_of(tile * rows_per_tile, rows_per_tile)

        def body(cb, carry):
            a_next = project(nxt, cb)
            v_prev = apply_v(nxt, nxt, cb)
            th = [th1_ref[cb, h, pl.ds(r0, rows_per_tile), :] for h in range(PEER_HEADS)]
            p1 = [p1_ref[cb, h, pl.ds(r0, rows_per_tile), :] for h in range(PEER_HEADS)]
            for r in range(rows_per_tile):
                gate = None
                for h in range(PEER_HEADS):
                    term = jnp.where(s2_ref[cb, h] >= th[h][r:r + 1], p2_ref[cb, h], 0.0) * p1[h][r:r + 1]
                    gate = term if gate is None else gate + term
                rs = slice(r * N_KEYS, (r + 1) * N_KEYS)
                ga_ref[cur, cb, rs, :] = (gate * _gelu(act_ref[cur, cb, rs, :])).astype(BF16)
            store_act(nxt, cb, a_next)
            ro = pl.multiple_of(cb * d_rows, d_rows)
            acc_ref[pl.ds(ro, d_rows), :] += v_prev
            return carry

        lax.fori_loop(0, n_blk, body, 0)

    @pl.when(s == 0)
    def _():
        acc_ref[...] = jnp.zeros(acc_ref.shape, F32)
        ga_ref[...] = jnp.zeros(ga_ref.shape, BF16)
        for q in range(n_blk):
            store_act(0, q, project(0, q))

    @pl.when(s > 0)
    def _():
        stage(1, 2 * s - 1)

    @pl.when(s < n_steps - 1)
    def _():
        stage(0, 2 * s)

    @pl.when(s == n_steps - 1)
    def _():
        m = m_ref[0]
        for q in range(n_blk):
            acc_ref[q * d_rows:(q + 1) * d_rows, :] += apply_v(1, 1, q)
        ffn = acc_ref[...].T
        o_ref[...] = _ln(ALPHA * x1_ref[...] + m[5:6] * ffn) * g2_ref[...] + b2_ref[...]


def _peer(x1, tab, h2t, s2, p2, th1, p1, u_bf, vt_bf, mods, g2, b2):
    t, d = x1.shape
    tm = PEER_TM
    te = PEER_TE
    n_tiles = u_bf.shape[0] // te
    per = tm // ROW_TILE
    n_blk = tm // LANES
    hk = pl.BlockSpec((n_blk, PEER_HEADS, N_KEYS, LANES), lambda i, j, t_: (i, 0, 0, 0))
    in_specs = [pl.BlockSpec((d, tm), lambda i, j, t_: (0, i)),
                hk, hk, hk, hk,
                pl.BlockSpec((2 * te, d), lambda i, j, t_: (jnp.minimum(j, n_tiles // 2 - 1), 0)),
                pl.BlockSpec((d, 2 * te), lambda i, j, t_: (0, jnp.maximum(j - 1, 0))),
                pl.BlockSpec((tm, d), lambda i, j, t_: (i, 0)),
                pl.BlockSpec((1, 6, d), lambda i, j, t_: (t_[0, i * per], 0, 0)),
                pl.BlockSpec(g2.shape, lambda i, j, t_: (0, 0)),
                pl.BlockSpec(b2.shape, lambda i, j, t_: (0, 0))]
    return pl.pallas_call(
        _peer_kernel,
        grid_spec=pltpu.PrefetchScalarGridSpec(
            num_scalar_prefetch=1, grid=(t // tm, n_tiles // 2 + 1), in_specs=in_specs,
            out_specs=pl.BlockSpec((tm, d), lambda i, j, t_: (i, 0)),
            scratch_shapes=[pltpu.VMEM((d, tm), F32), pltpu.VMEM((2, n_blk, te, LANES), F32),
                            pltpu.VMEM((2, n_blk, te, LANES), BF16)]),
        out_shape=jax.ShapeDtypeStruct((t, d), F32),
        compiler_params=_cparams(("arbitrary", "arbitrary")),
        name="peer_mix",
    )(tab, h2t, s2, p2, th1, p1, u_bf, vt_bf, x1, mods, g2, b2)


def _sincos(pos, dim):
    omega = 1.0 / (10000.0 ** (jnp.arange(dim // 2, dtype=F32) / (dim // 2)))
    ang = pos.astype(F32)[:, None] * omega[None, :]
    return jnp.concatenate([jnp.sin(ang), jnp.cos(ang)], -1)


def _grid_pos_embed(n_tok):
    rows = n_tok // GRID_W
    half = D_MODEL // 2
    er = _sincos(jnp.arange(rows), half)
    ec = _sincos(jnp.arange(GRID_W), half)
    emb = jnp.concatenate([jnp.broadcast_to(er[:, None, :], (rows, GRID_W, half)),
                           jnp.broadcast_to(ec[None, :, :], (rows, GRID_W, half))], -1)
    return emb.reshape(rows * GRID_W, D_MODEL)


def _seq_dft_tables(s_len):
    r = 1
    while r * r < s_len:
        r *= 2
    k = jnp.arange(s_len, dtype=jnp.int32)
    w = 2.0 * math.pi / s_len
    ang_a = ((jnp.arange(s_len // r, dtype=jnp.int32)[:, None] * r * k[None, :]) % s_len).astype(F32) * w
    ang_b = ((jnp.arange(r, dtype=jnp.int32)[:, None] * k[None, :]) % s_len).astype(F32) * w
    ca, sa = jnp.cos(ang_a)[:, None, :], jnp.sin(ang_a)[:, None, :]
    cb, sb = jnp.cos(ang_b)[None, :, :], jnp.sin(ang_b)[None, :, :]
    scale = 1.0 / math.sqrt(s_len * FNET_GW)
    cmat = ((ca * cb - sa * sb) * scale).reshape(s_len, s_len)
    smat = ((sa * cb + ca * sb) * (-scale)).reshape(s_len, s_len)
    return cmat.astype(BF16), smat.astype(BF16)


def _channel_dft_table():
    n = np.arange(FNET_GW)
    ang = ((n[:, None] * n[None, :]) % FNET_GW) * (2.0 * np.pi / FNET_GW)
    wc = np.zeros((FNET_W, 2 * FNET_W), np.float32)
    for g in range(FNET_GROUPS):
        sl = slice(g * FNET_GW, (g + 1) * FNET_GW)
        wc[sl, sl] = np.cos(ang)
        wc[sl, FNET_W + g * FNET_GW:FNET_W + (g + 1) * FNET_GW] = np.sin(ang)
    return jnp.asarray(wc, BF16)


def _segment_ones():
    idx = np.arange(WKV_W) // WKV_N
    return jnp.asarray((idx[:, None] == idx[None, :]).astype(np.float32), BF16)


def _chunk_tri():
    i = np.arange(ROW_TILE)
    same = (i[:, None] // CHUNK) == (i[None, :] // CHUNK)
    lower = same & (i[None, :] <= i[:, None])
    upper = same & (i[None, :] >= i[:, None])
    return jnp.asarray(np.stack([lower, upper]).astype(np.float32), BF16)


def _scan_masks():
    i = np.arange(PAIR)
    same = (i[:, None] // CHUNK) == (i[None, :] // CHUNK)
    t_row = i[:, None] % CHUNK
    t_col = i[None, :] % CHUNK
    masks = [same & (t_col < t_row), same & (t_col <= t_row), same & (t_col > t_row), same & (t_col >= t_row),
             np.eye(PAIR, dtype=bool)]
    return jnp.asarray(np.stack(masks).astype(np.float32))


def _block_diag(w):
    h, n, _ = w.shape
    eye = jnp.eye(h, dtype=w.dtype)
    return jnp.einsum("hij,hg->higj", w, eye).reshape(h * n, h * n)


def _pad_rank(w, d):
    r = w.shape[1]
    z = jnp.zeros_like(w[d])
    return jnp.concatenate([w[d], z] if d == 0 else [z, w[d]], axis=0)


def _state_to_pairs(s0):
    b = s0.shape[0]
    st = jnp.swapaxes(s0, -1, -2).reshape(b, 2, N_PAIRS, 2, WKV_N, WKV_N)
    eye = jnp.eye(2, dtype=s0.dtype)
    return jnp.einsum("bdpeji,ef->bdpejfi", st, eye).reshape(b, 2, N_PAIRS, PAIR, PAIR)


def _pairs_to_state(m):
    b = m.shape[0]
    m7 = m.reshape(b, 2, N_PAIRS, 2, WKV_N, 2, WKV_N)
    st = jnp.stack([m7[:, :, :, 0, :, 0, :], m7[:, :, :, 1, :, 1, :]], axis=3)
    return jnp.swapaxes(st.reshape(b, 2, WKV_H, WKV_N, WKV_N), -1, -2)


def _tile_table(groups):
    cv, first, last = [], [], []
    row = 0
    for gi, (n_seq, s_len) in enumerate(groups):
        nt = s_len // ROW_TILE
        for b in range(n_seq):
            for i in range(nt):
                cv.append(0 if gi == 0 else 1 + b)
                first.append(int(i == 0))
                last.append(int(i == nt - 1))
        row += n_seq * s_len
    return jnp.asarray(np.array([cv, first, last], np.int32))


def kernel(x_prompt, x_sample, state_lru, state_wkv, c, c_ctx, w_mod, b_mod, ln1_g, ln1_b, ln2_g, ln2_b,
           w_in_e, w_out_e, sconv_w, sconv_b, w_in_o, w_out_o, lru_conv_w, lru_conv_b, lru_wa, lru_ba,
           lru_wx, lru_bx, lru_lambda, wkv_mu, wkv_w0, wkv_w2, wkv_a0, wkv_a2, wkv_kk, wkv_ka, wkv_rk,
           wkv_g2, wkv_gn_g, wkv_gn_b, peer_wq, peer_keys, peer_u, peer_v):
    bp, sp, d = x_prompt.shape
    bs, ss, _ = x_sample.shape
    depth = w_mod.shape[0]
    assert sp % ROW_TILE == 0 and ss % PEER_TM == 0 and (bp * sp) % ss == 0
    assert bs + 1 <= 8
    groups = ((bp, sp), (bs, ss))
    tp = bp * sp
    tab = _tile_table(groups)

    cv8 = jnp.concatenate([c_ctx[None, :], c, jnp.zeros((8 - 1 - bs, d), F32)], axis=0)
    mods = _modulation(cv8, w_mod, b_mod).reshape(depth, 8, 6, d)

    x = _embed(x_prompt.reshape(tp, d), x_sample.reshape(bs * ss, d), _grid_pos_embed(ss).astype(x_sample.dtype))

    wc = _channel_dft_table()
    dft = {s: _seq_dft_tables(s) for s in sorted({sp, ss})}
    seg = _segment_ones()
    tri = _chunk_tri()
    masks = _scan_masks()
    row2 = lambda a: a.reshape(1, -1)

    lru_fin = []
    wkv_fin = []
    for l in range(depth):
        j = l // 2
        m_l = mods[l]
        wq = peer_wq[l].astype(BF16)
        keys = peer_keys[l].reshape(2 * PEER_HEADS, N_KEYS, PEER_HALF).astype(BF16)
        if l % 2 == 0:
            zr, pq = _inproj(x, tab, m_l, w_in_e[j].astype(BF16), wc)
            y = jnp.concatenate([_seqdft(pq, bp * sp, 0, bp, sp, *dft[sp]),
                                 _seqdft(pq, bs * ss, tp, bs, ss, *dft[ss])], axis=0)
            x1, h2t, th1, p1, p2, s2 = _post_even(x, tab, zr, y, m_l, sconv_w[j], row2(sconv_b[j]), w_out_e[j].astype(BF16),
                                     row2(ln1_g[l]), row2(ln1_b[l]), wq, keys)
        else:
            zl, zw = _inproj(x, tab, m_l, w_in_o[j].astype(BF16))
            lru_args = (lru_conv_w[j], lru_conv_b[j],
                        jnp.stack([_block_diag(lru_wa[j, dd]) for dd in range(2)]).astype(BF16), lru_ba[j],
                        jnp.stack([_block_diag(lru_wx[j, dd]) for dd in range(2)]).astype(BF16), lru_bx[j],
                        lru_lambda[j])
            hf_p, hb_p, hfin_p = _lru(zl, 0, bp, sp, jnp.zeros((bp, 2, LRU_W), F32), *lru_args)
            hf_s, hb_s, _ = _lru(zl, tp, bs, ss, state_lru[:, j], *lru_args)
            lru_fin.append(hfin_p)
            prep = _wkvprep(zw, tab, row2(wkv_mu[j]), wkv_w0[j],
                            jnp.stack([_pad_rank(wkv_w2[j], dd) for dd in range(2)]).astype(BF16), wkv_a0[j],
                            jnp.stack([_pad_rank(wkv_a2[j], dd) for dd in range(2)]).astype(BF16),
                            wkv_kk[j], wkv_ka[j], row2(wkv_rk[j]), wkv_g2[j].astype(BF16), seg, tri)
            v, bonus, g = prep[0], prep[1], prep[2]
            yf_p, yb_p, mfin_p = _wkvscan(prep[3:], v, 0, bp, sp,
                                          jnp.zeros((bp, 2, N_PAIRS, PAIR, PAIR), F32), masks)
            yf_s, yb_s, _ = _wkvscan(prep[3:], v, tp, bs, ss, _state_to_pairs(state_wkv[:, j]), masks)
            wkv_fin.append(_pairs_to_state(mfin_p))
            x1, h2t, th1, p1, p2, s2 = _post_odd(x, tab, (hf_p, hf_s), (hb_p, hb_s), zl, (yf_p, yf_s), (yb_p, yb_s),
                                    bonus, g, m_l, seg, row2(wkv_gn_g[j]), row2(wkv_gn_b[j]),
                                    w_out_o[j].astype(BF16), row2(ln1_g[l]), row2(ln1_b[l]), wq, keys)
        x = _peer(x1, tab, h2t, s2, p2, th1, p1, peer_u[l].astype(BF16), peer_v[l].T.astype(BF16), m_l,
                  row2(ln2_g[l]), row2(ln2_b[l]))

    y_prompt = x[:tp].reshape(bp, sp, d)
    y_sample = x[tp:].reshape(bs, ss, d)
    return (y_prompt, y_sample, jnp.stack(lru_fin, 1).astype(x_prompt.dtype),
            jnp.stack(wkv_fin, 1).astype(x_prompt.dtype))
```

```python
import functools
import math

import numpy as np
import jax
import jax.numpy as jnp
from jax import lax
from jax.experimental import pallas as pl
from jax.experimental.pallas import tpu as pltpu

F32 = jnp.float32
BF16 = jnp.bfloat16

D_MODEL = 1024
GRID_W = 64
FNET_W = 512
FNET_GROUPS = 4
FNET_GW = FNET_W // FNET_GROUPS
CONV_W = 512
LRU_W = 512
LRU_HEADS = 8
LRU_CONV_K = 4
LRU_C = 8.0
WKV_W = 512
WKV_N = 64
WKV_H = 8
WKV_IN = 1920
DECAY_SCALE = math.exp(-0.5)
WKV_GN_EPS = 64e-5
PEER_HEADS = 8
N_KEYS = 128
PEER_TOPK = 16
PEER_HALF = 128
DEPTH = 4
ALPHA = (2 * DEPTH) ** 0.25
LN_EPS = 1e-6

ROW_TILE = 256
LANES = 128
HALO = 8
CHUNK = 64
PAIR = 2 * WKV_N
N_PAIRS = WKV_H // 2
PEER_TM = 512
PEER_TE = 1024
ROW_GROUP = 4
KEY_CHUNK = 64
VMEM_LIMIT = 56 * 1024 * 1024


def _cparams(sem):
    return pltpu.CompilerParams(dimension_semantics=sem, vmem_limit_bytes=VMEM_LIMIT)


def _dot(a, b):
    return jnp.dot(a, b, preferred_element_type=F32)


def _dot_nt(a, b):
    return lax.dot_general(a, b, (((1,), (1,)), ((), ())), preferred_element_type=F32)


def _mm3(a, b):
    ah = a.astype(BF16)
    al = (a - ah.astype(F32)).astype(BF16)
    bh = b.astype(BF16)
    bl = (b - bh.astype(F32)).astype(BF16)
    return _dot(ah, bh) + (_dot(ah, bl) + _dot(al, bh))


def _mm1(a, b):
    return _dot(a.astype(BF16), b.astype(BF16))


def _mm01(a, b01):
    h = a.astype(BF16)
    r = a - h.astype(F32)
    m = r.astype(BF16)
    lo = (r - m.astype(F32)).astype(BF16)
    return _dot(h, b01) + (_dot(m, b01) + _dot(lo, b01))


def _ln(x):
    mu = jnp.mean(x, axis=-1, keepdims=True)
    xc = x - mu
    var = jnp.mean(xc * xc, axis=-1, keepdims=True)
    return xc * lax.rsqrt(var + LN_EPS)


def _gelu(x):
    z = x * (0.7978845608028654 + 0.035677408136300125 * (x * x))
    hx = 0.5 * x
    return hx + hx * jnp.tanh(z)


def _sigmoid(x):
    return 1.0 / (1.0 + jnp.exp(-x))


def _mod_kernel(c_ref, w_ref, b_ref, o_ref):
    c = c_ref[...]
    sc = c * _sigmoid(c)
    o_ref[0] = _dot(sc.astype(BF16), w_ref[0].astype(BF16)) + b_ref[0]


def _modulation(cv8, w_mod, b_mod):
    depth, d, n = w_mod.shape
    tn = 1536
    return pl.pallas_call(
        _mod_kernel,
        grid=(depth, n // tn),
        in_specs=[pl.BlockSpec((8, d), lambda l, j: (0, 0)),
                  pl.BlockSpec((1, d, tn), lambda l, j: (l, 0, j)),
                  pl.BlockSpec((1, 1, tn), lambda l, j: (l, 0, j))],
        out_specs=pl.BlockSpec((1, 8, tn), lambda l, j: (l, 0, j)),
        out_shape=jax.ShapeDtypeStruct((depth, 8, n), F32),
        compiler_params=_cparams(("arbitrary", "arbitrary")),
        name="modulation",
    )(cv8, w_mod, b_mod.reshape(depth, 1, n))


def _embed_kernel(xp_ref, xs_ref, p_ref, o_ref, *, n_prompt):
    i = pl.program_id(0)

    @pl.when(i < n_prompt)
    def _():
        o_ref[...] = xp_ref[...]

    @pl.when(i >= n_prompt)
    def _():
        o_ref[...] = xs_ref[...] + p_ref[...]


def _embed(xp, xs, pos):
    tp, d = xp.shape
    ts = xs.shape[0]
    s_len = pos.shape[0]
    tm = 512
    n_prompt = tp // tm
    per_seq = s_len // tm
    return pl.pallas_call(
        functools.partial(_embed_kernel, n_prompt=n_prompt),
        grid=((tp + ts) // tm,),
        in_specs=[pl.BlockSpec((tm, d), lambda i: (jnp.minimum(i, n_prompt - 1), 0)),
                  pl.BlockSpec((tm, d), lambda i: (jnp.maximum(i - n_prompt, 0), 0)),
                  pl.BlockSpec((tm, d), lambda i: (jnp.maximum(i - n_prompt, 0) % per_seq, 0))],
        out_specs=pl.BlockSpec((tm, d), lambda i: (i, 0)),
        out_shape=jax.ShapeDtypeStruct((tp + ts, d), xp.dtype),
        compiler_params=_cparams(("arbitrary",)),
        name="embed",
    )(xp, xs, pos)


def _inproj_even_kernel(tab_ref, x_ref, m_ref, w_ref, wc_ref, zr_ref, pq_ref):
    m = m_ref[0]
    h = _ln(x_ref[...]) * (1.0 + m[1:2]) + m[0:1]
    z = _dot(h.astype(BF16), w_ref[...])
    zr_ref[...] = z[:, FNET_W:]
    pq_ref[...] = _dot(z[:, :FNET_W].astype(BF16), wc_ref[...]).astype(BF16)


def _inproj_odd_kernel(tab_ref, x_ref, m_ref, w_ref, zl_ref, zw_ref):
    m = m_ref[0]
    h = _ln(x_ref[...]) * (1.0 + m[1:2]) + m[0:1]
    z = _dot(h.astype(BF16), w_ref[...])
    zl_ref[...] = z[:, :2 * LRU_W]
    zw_ref[...] = z[:, 2 * LRU_W:]


def _const_spec(shape):
    nd = len(shape)
    return pl.BlockSpec(shape, lambda i, t, _n=nd: (0,) * _n)


def _inproj(x, tab, mods, w_in, wc=None):
    t, d = x.shape
    n = w_in.shape[1]
    tm = ROW_TILE
    even = wc is not None
    in_specs = [pl.BlockSpec((tm, d), lambda i, t_: (i, 0)),
                pl.BlockSpec((1, 6, d), lambda i, t_: (t_[0, i], 0, 0)),
                _const_spec((d, n))]
    if even:
        in_specs.append(_const_spec(wc.shape))
        widths = (n - FNET_W, 2 * FNET_W)
        dtypes = (F32, BF16)
        kern = _inproj_even_kernel
        args = (tab, x, mods, w_in, wc)
    else:
        widths = (2 * LRU_W, n - 2 * LRU_W)
        dtypes = (F32, F32)
        kern = _inproj_odd_kernel
        args = (tab, x, mods, w_in)
    return pl.pallas_call(
        kern,
        grid_spec=pltpu.PrefetchScalarGridSpec(
            num_scalar_prefetch=1, grid=(t // tm,), in_specs=in_specs,
            out_specs=[pl.BlockSpec((tm, w), lambda i, t_: (i, 0)) for w in widths]),
        out_shape=[jax.ShapeDtypeStruct((t, w), dt) for w, dt in zip(widths, dtypes)],
        compiler_params=_cparams(("arbitrary",)),
        name="inproj_even" if even else "inproj_odd",
    )(*args)


def _seqdft_kernel(c_ref, s_ref, p_ref, q_ref, y_ref):
    y_ref[...] = (_dot(c_ref[...], p_ref[...]) + _dot(s_ref[...], q_ref[...])).astype(BF16)


def _seqdft(pq, y_rows, row0, n_seq, s_len, cmat, smat):
    tm = min(s_len, ROW_TILE)
    nt = s_len // tm
    blk0 = row0 // s_len
    return pl.pallas_call(
        _seqdft_kernel,
        grid=(n_seq, nt),
        in_specs=[pl.BlockSpec((tm, s_len), lambda b, i: (i, 0)),
                  pl.BlockSpec((tm, s_len), lambda b, i: (i, 0)),
                  pl.BlockSpec((s_len, FNET_W), lambda b, i: (blk0 + b, 0)),
                  pl.BlockSpec((s_len, FNET_W), lambda b, i: (blk0 + b, 1))],
        out_specs=pl.BlockSpec((tm, FNET_W), lambda b, i: (b * nt + i, 0)),
        out_shape=jax.ShapeDtypeStruct((y_rows, FNET_W), BF16),
        compiler_params=_cparams(("arbitrary", "arbitrary")),
        name="seqdft_%d" % s_len,
    )(cmat, smat, pq, pq)


def _post_and_query(x, mix, m, g1, b1, wq_ref, keys_ref, x1_ref, h2t_ref, topk_refs, st_ref, v1_ref, v2_ref):
    x1 = _ln(ALPHA * x + m[2:3] * mix) * g1 + b1
    x1_ref[...] = x1
    h2 = _ln(x1) * (1.0 + m[4:5]) + m[3:4]
    h2b = h2.astype(BF16)
    h2t_ref[...] = h2.T.astype(BF16)
    q = _dot(h2b, wq_ref[...])
    for hp in range(2 * PEER_HEADS):
        qb = q[:, hp * PEER_HALF:(hp + 1) * PEER_HALF].astype(BF16)
        st_ref[hp * N_KEYS:(hp + 1) * N_KEYS, :] = _dot_nt(keys_ref[hp], qb)
    _topk_heads(st_ref, *topk_refs, v1_ref, v2_ref)


def _post_even_kernel(tab_ref, x_ref, bg_ref, cg_ref, xi_ref, cgp_ref, xip_ref, cgn_ref, xin_ref, y_ref,
                      m_ref, cw_ref, cb_ref, wo_ref, g1_ref, b1_ref, wq_ref, keys_ref,
                      x1_ref, h2t_ref, th1_ref, p1_ref, p2_ref, s2_ref, ext_ref, st_ref, v1_ref, v2_ref):
    i = pl.program_id(0)
    tm = ROW_TILE
    first = tab_ref[1, i] == 1
    last = tab_ref[2, i] == 1
    u = cg_ref[...] * xi_ref[...]
    ext_ref[0:HALO] = jnp.where(first, 0.0, cgp_ref[...] * xip_ref[...])
    ext_ref[HALO:HALO + tm] = u
    ext_ref[HALO + tm:2 * HALO + tm] = jnp.where(last, 0.0, cgn_ref[...] * xin_ref[...])
    cw = cw_ref[...]
    conv = (ext_ref[HALO - 1:HALO - 1 + tm] * cw[0:1] + u * cw[1:2]
            + ext_ref[HALO + 1:HALO + 1 + tm] * cw[2:3] + cb_ref[...])
    ymix = (bg_ref[...] * conv).astype(BF16)
    mix = _dot(y_ref[...], wo_ref[0:FNET_W]) + _dot(ymix, wo_ref[FNET_W:])
    _post_and_query(x_ref[...], mix, m_ref[0], g1_ref[...], b1_ref[...], wq_ref, keys_ref,
                    x1_ref, h2t_ref, (th1_ref, p1_ref, p2_ref, s2_ref), st_ref, v1_ref, v2_ref)


def _post_out_specs(t, d, tm):
    rshape = jax.ShapeDtypeStruct((t // LANES, PEER_HEADS, N_KEYS, LANES), F32)
    rspec = pl.BlockSpec((tm // LANES, PEER_HEADS, N_KEYS, LANES), lambda i, t_: (i, 0, 0, 0))
    specs = [pl.BlockSpec((tm, d), lambda i, t_: (i, 0)),
             pl.BlockSpec((d, tm), lambda i, t_: (0, i))] + [rspec] * 4
    shapes = [jax.ShapeDtypeStruct((t, d), F32),
              jax.ShapeDtypeStruct((d, t), BF16)] + [rshape] * 4
    return specs, shapes


def _post_scratch(tm):
    return [pltpu.VMEM((2 * PEER_HEADS * N_KEYS, tm), F32), pltpu.VMEM((TOP_ROWS, tm), F32),
            pltpu.VMEM((TOP_ROWS, tm), F32)]


def _post_even(x, tab, zr, y, mods, cw, cb, wo, g1, b1, wq, keys):
    t, d = x.shape
    tm = ROW_TILE
    hb = tm // HALO
    nblk = t // HALO
    prev = lambda c: pl.BlockSpec((HALO, CONV_W), lambda i, t_, _c=c: (jnp.maximum(i * hb - 1, 0), _c))
    nxt = lambda c: pl.BlockSpec((HALO, CONV_W), lambda i, t_, _c=c: (jnp.minimum((i + 1) * hb, nblk - 1), _c))
    col = lambda c: pl.BlockSpec((tm, CONV_W), lambda i, t_, _c=c: (i, _c))
    in_specs = [pl.BlockSpec((tm, d), lambda i, t_: (i, 0)),
                col(0), col(1), col(2), prev(1), prev(2), nxt(1), nxt(2),
                pl.BlockSpec((tm, FNET_W), lambda i, t_: (i, 0)),
                pl.BlockSpec((1, 6, d), lambda i, t_: (t_[0, i], 0, 0)),
                _const_spec(cw.shape), _const_spec(cb.shape), _const_spec(wo.shape),
                _const_spec(g1.shape), _const_spec(b1.shape), _const_spec(wq.shape), _const_spec(keys.shape)]
    out_specs, out_shape = _post_out_specs(t, d, tm)
    return pl.pallas_call(
        _post_even_kernel,
        grid_spec=pltpu.PrefetchScalarGridSpec(
            num_scalar_prefetch=1, grid=(t // tm,), in_specs=in_specs, out_specs=out_specs,
            scratch_shapes=[pltpu.VMEM((tm + 2 * HALO, CONV_W), F32)] + _post_scratch(tm)),
        out_shape=out_shape,
        compiler_params=_cparams(("arbitrary",)),
        name="post_even",
    )(tab, x, zr, zr, zr, zr, zr, zr, zr, y, mods, cw, cb, wo, g1, b1, wq, keys)


def _lru_kernel(xf_ref, xfh_ref, xb_ref, xbh_ref, cw_ref, cb_ref, wa_ref, ba_ref, wx_ref, bx_ref, lam_ref,
                h0_ref, hf_ref, hb_ref, hfin_ref, ext_ref, carry_ref):
    i = pl.program_id(1)
    nt = pl.num_programs(1)
    tm = ROW_TILE

    @pl.when(i == 0)
    def _():
        carry_ref[...] = h0_ref[0]

    row = lax.broadcasted_iota(jnp.int32, (tm, 1), 0)
    for d in range(2):
        x = (xf_ref if d == 0 else xb_ref)[...]
        halo = jnp.where(i == 0, 0.0, (xfh_ref if d == 0 else xbh_ref)[...])
        ext_ref[HALO:HALO + tm] = x
        if d == 0:
            ext_ref[0:HALO] = halo
        else:
            ext_ref[HALO + tm:2 * HALO + tm] = halo
        cw = cw_ref[d]
        xc = cb_ref[d:d + 1] + x * cw[LRU_CONV_K - 1:LRU_CONV_K]
        for j in range(LRU_CONV_K - 1):
            k = LRU_CONV_K - 1 - j
            off = HALO - k if d == 0 else HALO + k
            xc = xc + ext_ref[off:off + tm] * cw[j:j + 1]
        xcb = xc.astype(BF16)
        gate_r = _sigmoid(_dot(xcb, wa_ref[d]) + ba_ref[d:d + 1])
        gate_i = _sigmoid(_dot(xcb, wx_ref[d]) + bx_ref[d:d + 1])
        nl = -lam_ref[d:d + 1]
        softplus = jnp.maximum(nl, 0.0) + jnp.log1p(jnp.exp(-jnp.abs(nl)))
        log_a = -LRU_C * gate_r * softplus
        a = jnp.exp(log_a)
        b = jnp.sqrt(-jnp.tanh(log_a) * (a * a + 1.0)) * (gate_i * xc)
        s = 1
        while s < tm:
            if d == 0:
                keep = row >= s
                sh = s
            else:
                keep = row < tm - s
                sh = tm - s
            a_sh = jnp.where(keep, pltpu.roll(a, sh, 0), 1.0)
            b_sh = jnp.where(keep, pltpu.roll(b, sh, 0), 0.0)
            b = a * b_sh + b
            a = a * a_sh
            s *= 2
        h = a * carry_ref[d:d + 1] + b
        if d == 0:
            hf_ref[...] = h
            carry_ref[0:1] = h[tm - 1:tm]
        else:
            hb_ref[...] = h
            carry_ref[1:2] = h[0:1]

    @pl.when(i == nt - 1)
    def _():
        hfin_ref[0] = carry_ref[...]


def _lru(zl, row0, n_seq, s_len, h0, cw, cb, wa, ba, wx, bx, lam):
    tm = ROW_TILE
    nt = s_len // tm
    t0 = row0 // tm
    hb = tm // HALO
    nblk = zl.shape[0] // HALO
    rows = n_seq * s_len
    fwd = lambda b, i: (t0 + b * nt + i, 0)
    bwd = lambda b, i: (t0 + b * nt + nt - 1 - i, 0)
    fwd_h = lambda b, i: (jnp.maximum((t0 + b * nt + i) * hb - 1, 0), 0)
    bwd_h = lambda b, i: (jnp.minimum((t0 + b * nt + nt - i) * hb, nblk - 1), 0)
    out_f = lambda b, i: (b * nt + i, 0)
    out_b = lambda b, i: (b * nt + nt - 1 - i, 0)
    cst = lambda a: pl.BlockSpec(a.shape, lambda b, i, _n=a.ndim: (0,) * _n)
    return pl.pallas_call(
        _lru_kernel,
        grid=(n_seq, nt),
        in_specs=[pl.BlockSpec((tm, LRU_W), fwd), pl.BlockSpec((HALO, LRU_W), fwd_h),
                  pl.BlockSpec((tm, LRU_W), bwd), pl.BlockSpec((HALO, LRU_W), bwd_h),
                  cst(cw), cst(cb), cst(wa), cst(ba), cst(wx), cst(bx), cst(lam),
                  pl.BlockSpec((1, 2, LRU_W), lambda b, i: (b, 0, 0))],
        out_specs=[pl.BlockSpec((tm, LRU_W), out_f), pl.BlockSpec((tm, LRU_W), out_b),
                   pl.BlockSpec((1, 2, LRU_W), lambda b, i: (b, 0, 0))],
        out_shape=[jax.ShapeDtypeStruct((rows, LRU_W), F32), jax.ShapeDtypeStruct((rows, LRU_W), F32),
                   jax.ShapeDtypeStruct((n_seq, 2, LRU_W), F32)],
        scratch_shapes=[pltpu.VMEM((tm + 2 * HALO, LRU_W), F32), pltpu.VMEM((2, LRU_W), F32)],
        compiler_params=_cparams(("arbitrary", "arbitrary")),
        name="lru_%d" % s_len,
    )(zl, zl, zl, zl, cw, cb, wa, ba, wx, bx, lam, h0)


def _wkvprep_kernel(tab_ref, z_ref, zp_ref, zn_ref, mu_ref, w0_ref, w2_ref, a0_ref, a2_ref, kk_ref, ka_ref,
                    rk_ref, g2_ref, seg_ref, tri_ref,
                    v_ref, bg_ref, g_ref, rf_ref, kf_ref, bf_ref, qf_ref, cf_ref,
                    rb_ref, kb_ref, bb_ref, qb_ref, cb_ref, ext_ref):
    i = pl.program_id(0)
    tm = ROW_TILE
    w = WKV_W
    first = tab_ref[1, i] == 1
    last = tab_ref[2, i] == 1
    z = z_ref[...]
    ext_ref[0:HALO] = jnp.where(first, 0.0, zp_ref[...])
    ext_ref[HALO:HALO + tm] = z
    ext_ref[HALO + tm:2 * HALO + tm] = jnp.where(last, 0.0, zn_ref[...])
    z = z + mu_ref[...] * (0.5 * (ext_ref[HALO - 1:HALO - 1 + tm] + ext_ref[HALO + 1:HALO + 1 + tm]) - z)
    r = z[:, 0:w]
    k = z[:, w:2 * w]
    v = z[:, 2 * w:3 * w]
    wd = jnp.tanh(z[:, 3 * w:3 * w + 128]).astype(BF16)
    ad = z[:, 3 * w + 128:3 * w + 256].astype(BF16)
    gd = _sigmoid(z[:, 3 * w + 256:3 * w + 384]).astype(BF16)
    v_ref[...] = v
    g_ref[...] = _dot(gd, g2_ref[...])
    seg = seg_ref[...]
    rk = rk_ref[...]
    bonus = jnp.zeros((tm, w), F32)
    outs = ((rf_ref, kf_ref, bf_ref, qf_ref, cf_ref), (rb_ref, kb_ref, bb_ref, qb_ref, cb_ref))
    nch = tm // CHUNK
    for d in range(2):
        r_ref, k_ref, b_ref, q_ref, c_ref = outs[d]
        wz = w0_ref[d:d + 1] + _dot(wd, w2_ref[d])
        logw = -DECAY_SCALE * _sigmoid(wz)
        iclr = _sigmoid(a0_ref[d:d + 1] + _dot(ad, a2_ref[d]))
        kk = k * kk_ref[d:d + 1]
        kk = kk * lax.rsqrt(jnp.maximum(_mm01(kk * kk, seg), 1e-24))
        km = k * (1.0 + (iclr - 1.0) * ka_ref[d:d + 1])
        bonus = bonus + _mm01(r * km * rk, seg) * v
        lw_h = logw.astype(BF16)
        lw_r = logw - lw_h.astype(F32)
        lw_m = lw_r.astype(BF16)
        lw_l = (lw_r - lw_m.astype(F32)).astype(BF16)
        tri = tri_ref[d]
        cl = _dot(tri, lw_h) + (_dot(tri, lw_m) + _dot(tri, lw_l))
        c = jnp.exp(cl)
        cinv = jnp.exp(-cl)
        r_ref[...] = r * c
        k_ref[...] = km * cinv
        b_ref[...] = kk * iclr * cinv
        q_ref[...] = kk * jnp.exp(cl - logw)
        for j in range(nch):
            edge = (j + 1) * CHUNK - 1 if d == 0 else j * CHUNK
            c_ref[j] = c[edge:edge + 1]
    bg_ref[...] = bonus


def _wkvprep(zw, tab, mu, w0, w2p, a0, a2p, kk, ka, rk, g2, seg, tri):
    t, n = zw.shape
    tm = ROW_TILE
    hb = tm // HALO
    nblk = t // HALO
    nch = tm // CHUNK
    w = WKV_W
    in_specs = [pl.BlockSpec((tm, n), lambda i, t_: (i, 0)),
                pl.BlockSpec((HALO, n), lambda i, t_: (jnp.maximum(i * hb - 1, 0), 0)),
                pl.BlockSpec((HALO, n), lambda i, t_: (jnp.minimum((i + 1) * hb, nblk - 1), 0))]
    in_specs += [_const_spec(a.shape) for a in (mu, w0, w2p, a0, a2p, kk, ka, rk, g2, seg, tri)]
    row = pl.BlockSpec((tm, w), lambda i, t_: (i, 0))
    cspec = pl.BlockSpec((nch, 1, w), lambda i, t_: (i, 0, 0))
    rshape = jax.ShapeDtypeStruct((t, w), F32)
    cshape = jax.ShapeDtypeStruct((t // CHUNK, 1, w), F32)
    out_specs = [row, row, row] + [row, row, row, row, cspec] * 2
    out_shape = [rshape, rshape, rshape] + [rshape, rshape, rshape, rshape, cshape] * 2
    return pl.pallas_call(
        _wkvprep_kernel,
        grid_spec=pltpu.PrefetchScalarGridSpec(
            num_scalar_prefetch=1, grid=(t // tm,), in_specs=in_specs, out_specs=out_specs,
            scratch_shapes=[pltpu.VMEM((tm + 2 * HALO, n), F32)]),
        out_shape=out_shape,
        compiler_params=_cparams(("arbitrary",)),
        name="wkv_prep",
    )(tab, zw, zw, zw, mu, w0, w2p, a0, a2p, kk, ka, rk, g2, seg, tri)


def _wkv_chunks(items, eye):
    lane = lax.broadcasted_iota(jnp.int32, (1, PAIR), 1)
    m0 = (lane < WKV_N).astype(F32)
    m1 = 1.0 - m0
    stack = lambda x: jnp.concatenate([x * m0, x * m1], axis=0)
    n = range(len(items))
    rh_s = [stack(it[0]) for it in items]
    kh_s = [stack(it[1]) for it in items]
    bh_s = [stack(it[2]) for it in items]
    kq_s = [stack(it[3]) for it in items]
    v_s = [stack(it[4]) for it in items]
    bh_t = [x.T for x in bh_s]
    kh_t = [x.T for x in kh_s]
    bk_t = [jnp.concatenate([bh_t[i], kh_t[i]], axis=1) for i in n]
    att_s = [_mm3(kq_s[i], bk_t[i]) for i in n]
    att_y = [_mm1(rh_s[i], bk_t[i]) for i in n]
    n1 = [att_s[i][:, 0:PAIR] * items[i][7] for i in n]
    ak = [att_s[i][:, PAIR:] * items[i][7] for i in n]
    gb = [att_y[i][:, 0:PAIR] * items[i][8] for i in n]
    gk = [att_y[i][:, PAIR:] * items[i][8] for i in n]
    t_inv = [eye - n1[i] for i in n]
    npow = n1
    for _ in range(5):
        npow = [_mm1(npow[i], npow[i]) for i in n]
        t_inv = [t_inv[i] + _mm1(t_inv[i], npow[i]) for i in n]
    kv = [_mm1(jnp.concatenate([kh_t[i], gk[i], ak[i]], axis=0), v_s[i]) for i in n]
    x = [_mm1(t_inv[i], jnp.concatenate([kq_s[i], kv[i][2 * PAIR:]], axis=1)) for i in n]
    bx = [_mm1(jnp.concatenate([bh_t[i], gb[i]], axis=0), x[i]) for i in n]
    p_mat = [eye - bx[i][0:PAIR, 0:PAIR] for i in n]
    q_mat = [kv[i][0:PAIR] - bx[i][0:PAIR, PAIR:] for i in n]
    r_til = [rh_s[i] - bx[i][PAIR:, 0:PAIR] for i in n]
    y0 = [kv[i][PAIR:2 * PAIR] - bx[i][PAIR:, PAIR:] for i in n]
    pm = [_mm3(p_mat[i], items[i][6]) for i in n]
    y_st = [_mm1(r_til[i], items[i][6]) + y0[i] for i in n]
    ys = [y_st[i][0:CHUNK] + y_st[i][CHUNK:] for i in n]
    c_col = [jnp.broadcast_to(items[i][5], (PAIR, PAIR)).T for i in n]
    m_new = [(pm[i] + q_mat[i]) * c_col[i] for i in n]
    return ys, m_new


def _wkvscan_kernel(rf_ref, kf_ref, bf_ref, qf_ref, vf_ref, cf_ref, rb_ref, kb_ref, bb_ref, qb_ref, vb_ref, cb_ref,
                    m0_ref, msk_ref, yf_ref, yb_ref, mfin_ref, m_ref):
    i = pl.program_id(1)
    nc = pl.num_programs(1)

    @pl.when(i == 0)
    def _():
        m_ref[...] = m0_ref[0]

    eye = msk_ref[4]
    ins = ((rf_ref, kf_ref, bf_ref, qf_ref, vf_ref, cf_ref, yf_ref),
           (rb_ref, kb_ref, bb_ref, qb_ref, vb_ref, cb_ref, yb_ref))
    items = []
    for d in range(2):
        r_ref, k_ref, b_ref, q_ref, v_ref, c_ref, _ = ins[d]
        for p in range(N_PAIRS):
            sl = slice(p * PAIR, (p + 1) * PAIR)
            items.append((r_ref[:, sl], k_ref[:, sl], b_ref[:, sl], q_ref[:, sl], v_ref[:, sl],
                          c_ref[0, :, sl], m_ref[d, p], msk_ref[2 * d], msk_ref[2 * d + 1]))
    ys, m_new = _wkv_chunks(items, eye)
    for d in range(2):
        for p in range(N_PAIRS):
            sl = slice(p * PAIR, (p + 1) * PAIR)
            ins[d][6][:, sl] = ys[d * N_PAIRS + p]
            m_ref[d, p] = m_new[d * N_PAIRS + p]

    @pl.when(i == nc - 1)
    def _():
        mfin_ref[0] = m_ref[...]


def _wkvscan(prep, v, row0, n_seq, s_len, m0, masks):
    rf, kf, bf, qf, cf, rb, kb, bb, qb, cb = prep
    nc = s_len // CHUNK
    c0 = row0 // CHUNK
    rows = n_seq * s_len
    w = WKV_W
    fwd = lambda b, i: (c0 + b * nc + i, 0)
    bwd = lambda b, i: (c0 + b * nc + nc - 1 - i, 0)
    fwd3 = lambda b, i: (c0 + b * nc + i, 0, 0)
    bwd3 = lambda b, i: (c0 + b * nc + nc - 1 - i, 0, 0)
    blk = lambda im: pl.BlockSpec((CHUNK, w), im)
    cblk = lambda im: pl.BlockSpec((1, 1, w), im)
    mspec = pl.BlockSpec((1, 2, N_PAIRS, PAIR, PAIR), lambda b, i: (b, 0, 0, 0, 0))
    return pl.pallas_call(
        _wkvscan_kernel,
        grid=(n_seq, nc),
        in_specs=[blk(fwd)] * 5 + [cblk(fwd3)] + [blk(bwd)] * 5 + [cblk(bwd3)]
        + [mspec, pl.BlockSpec(masks.shape, lambda b, i: (0, 0, 0))],
        out_specs=[pl.BlockSpec((CHUNK, w), lambda b, i: (b * nc + i, 0)),
                   pl.BlockSpec((CHUNK, w), lambda b, i: (b * nc + nc - 1 - i, 0)),
                   mspec],
        out_shape=[jax.ShapeDtypeStruct((rows, w), F32), jax.ShapeDtypeStruct((rows, w), F32),
                   jax.ShapeDtypeStruct(m0.shape, F32)],
        scratch_shapes=[pltpu.VMEM((2, N_PAIRS, PAIR, PAIR), F32)],
        compiler_params=_cparams(("arbitrary", "arbitrary")),
        name="wkv_scan_%d" % s_len,
    )(rf, kf, bf, qf, v, cf, rb, kb, bb, qb, v, cb, m0, masks)


def _post_odd_kernel(tab_ref, x_ref, hfp_ref, hfs_ref, hbp_ref, hbs_ref, gb_ref, yfp_ref, yfs_ref, ybp_ref, ybs_ref,
                     bon_ref, g_ref, m_ref, seg_ref, gng_ref, gnb_ref, wo_ref, g1_ref, b1_ref, wq_ref, keys_ref,
                     x1_ref, h2t_ref, th1_ref, p1_ref, p2_ref, s2_ref, st_ref, v1_ref, v2_ref, *, n_prompt_tiles):
    is_prompt = pl.program_id(0) < n_prompt_tiles
    pick = lambda p_ref, s_ref: jnp.where(is_prompt, p_ref[...], s_ref[...])
    y_lru = ((pick(hfp_ref, hfs_ref) + pick(hbp_ref, hbs_ref)) * _gelu(gb_ref[...])).astype(BF16)
    seg = seg_ref[...]
    ys = pick(yfp_ref, yfs_ref) + pick(ybp_ref, ybs_ref)
    mean = _mm01(ys, seg) * (1.0 / WKV_N)
    yc = ys - mean
    var = _mm01(yc * yc, seg) * (1.0 / WKV_N)
    yn = yc * lax.rsqrt(var + WKV_GN_EPS) * gng_ref[...] + gnb_ref[...]
    y_wkv = ((yn + bon_ref[...]) * g_ref[...]).astype(BF16)
    mix = _dot(y_lru, wo_ref[0:LRU_W]) + _dot(y_wkv, wo_ref[LRU_W:])
    _post_and_query(x_ref[...], mix, m_ref[0], g1_ref[...], b1_ref[...], wq_ref, keys_ref,
                    x1_ref, h2t_ref, (th1_ref, p1_ref, p2_ref, s2_ref), st_ref, v1_ref, v2_ref)


def _post_odd(x, tab, hf, hb, zl, yf, yb, bonus, g, mods, seg, gng, gnb, wo, g1, b1, wq, keys):
    t, d = x.shape
    tm = ROW_TILE
    w = WKV_W
    n_p = hf[0].shape[0] // tm
    n_s = hf[1].shape[0] // tm
    row = pl.BlockSpec((tm, w), lambda i, t_: (i, 0))
    row_p = pl.BlockSpec((tm, w), lambda i, t_: (jnp.minimum(i, n_p - 1), 0))
    row_s = pl.BlockSpec((tm, w), lambda i, t_: (jnp.clip(i - n_p, 0, n_s - 1), 0))
    in_specs = [pl.BlockSpec((tm, d), lambda i, t_: (i, 0)), row_p, row_s, row_p, row_s,
                pl.BlockSpec((tm, LRU_W), lambda i, t_: (i, 1)), row_p, row_s, row_p, row_s, row, row,
                pl.BlockSpec((1, 6, d), lambda i, t_: (t_[0, i], 0, 0))]
    in_specs += [_const_spec(a.shape) for a in (seg, gng, gnb, wo, g1, b1, wq, keys)]
    out_specs, out_shape = _post_out_specs(t, d, tm)
    return pl.pallas_call(
        functools.partial(_post_odd_kernel, n_prompt_tiles=n_p),
        grid_spec=pltpu.PrefetchScalarGridSpec(
            num_scalar_prefetch=1, grid=(t // tm,), in_specs=in_specs, out_specs=out_specs,
            scratch_shapes=_post_scratch(tm)),
        out_shape=out_shape,
        compiler_params=_cparams(("arbitrary",)),
        name="post_odd",
    )(tab, x, hf[0], hf[1], hb[0], hb[1], zl, yf[0], yf[1], yb[0], yb[1], bonus, g, mods, seg, gng, gnb, wo,
      g1, b1, wq, keys)


N_TOP = PEER_TOPK + 1
TOP_ROWS = 24
SUBLANES = 8


def _batcher_network(n):
    def merge(lo, hi, r):
        step = r * 2
        if step < hi - lo:
            yield from merge(lo, hi, step)
            yield from merge(lo + r, hi, step)
            yield from [(i, i + r) for i in range(lo + r, hi - r, step)]
        else:
            yield (lo, lo + r)

    def sort(lo, hi):
        if hi - lo >= 1:
            mid = lo + (hi - lo) // 2
            yield from sort(lo, mid)
            yield from sort(mid + 1, hi)
            yield from merge(lo, hi, 1)

    return tuple(sort(0, n - 1))


def _sort_levels(levels, net):
    lv = list(levels)
    for i, j in net:
        a, b = lv[i], lv[j]
        lv[i] = jnp.maximum(a, b)
        lv[j] = jnp.minimum(a, b)
    return lv


def _pop_sorted(levels, n_top, emit):
    lv = list(levels)
    for it in range(n_top):
        m = jnp.max(lv[0], axis=0, keepdims=True)
        emit(it, m)
        hit = lv[0] == m
        live = min(len(lv), n_top - it - 1)
        for k in range(live):
            below = lv[k + 1] if k + 1 < len(lv) else -jnp.inf
            lv[k] = jnp.where(hit, below, lv[k])


def _topk_heads(st_ref, th1_ref, p1_ref, p2_ref, s2_ref, v1_ref, v2_ref):
    tl = st_ref.shape[-1]
    neg = -jnp.inf
    net16 = _batcher_network(N_KEYS // SUBLANES)
    net8 = _batcher_network(SUBLANES)
    row = lax.broadcasted_iota(jnp.int32, (SUBLANES, tl), 0)
    roll = lambda x, sh: pltpu.roll(x, sh, 0)

    def top_values(x, store):
        store[...] = jnp.full(store.shape, neg, F32)
        lv = _sort_levels([x[SUBLANES * k:SUBLANES * (k + 1)] for k in range(N_KEYS // SUBLANES)], net16)

        def emit(it, m):
            store[it:it + 1, :] = m

        _pop_sorted(lv, N_TOP, emit)

    def head(h, carry):
        s1 = st_ref[pl.ds(pl.multiple_of(2 * h * N_KEYS, N_KEYS), N_KEYS), :]
        s2 = st_ref[pl.ds(pl.multiple_of((2 * h + 1) * N_KEYS, N_KEYS), N_KEYS), :]
        top_values(s1, v1_ref)
        top_values(s2, v2_ref)
        one = lambda ref, a: ref[a:a + 1, :]
        v2a = v2_ref[0:8, :]
        v1b2 = roll(v1_ref[8:16, :], 2)
        cands = [one(v1_ref, 0) + v2a, one(v1_ref, 0) + v2_ref[8:16, :], one(v1_ref, 0) + v2_ref[16:24, :],
                 one(v1_ref, 1) + v2a,
                 jnp.where(row < 5, one(v1_ref, 2), one(v1_ref, 4)) + jnp.where(row < 5, v2a, roll(v2a, 5)),
                 jnp.where(row < 4, one(v1_ref, 3), jnp.where(row < 6, one(v1_ref, 5), one(v1_ref, 6)))
                 + jnp.where(row < 4, v2a, jnp.where(row < 6, roll(v2a, 4), roll(v2a, 6))),
                 jnp.where(row < 2, one(v1_ref, 7), v1b2) + jnp.where(row < 2, v2a, one(v2_ref, 0)),
                 jnp.where(row < 2, v1b2, roll(v1_ref[16:24, :], 2)) + one(v2_ref, 0)]
        top = []
        _pop_sorted(_sort_levels(cands, net8), N_TOP, lambda it, m: top.append(m))
        tau = 0.5 * (top[PEER_TOPK - 1] + top[PEER_TOPK])
        mx1 = v1_ref[0:1, :]
        mx2 = v2_ref[0:1, :]
        zacc = jnp.zeros((SUBLANES, tl), F32)
        for cnd in cands:
            zacc = zacc + jnp.where(cnd >= tau, jnp.exp(cnd - (mx1 + mx2)), 0.0)
        zsum = jnp.sum(zacc, axis=0, keepdims=True)
        th1 = tau - s1
        p1 = jnp.exp(s1 - mx1) / zsum
        p2 = jnp.exp(s2 - mx2)
        for cb in range(tl // LANES):
            cs = slice(cb * LANES, (cb + 1) * LANES)
            th1_ref[cb, h] = th1[:, cs]
            p1_ref[cb, h] = p1[:, cs]
            p2_ref[cb, h] = p2[:, cs]
            s2_ref[cb, h] = s2[:, cs]
        return carry

    lax.fori_loop(0, PEER_HEADS, head, 0)


def _peer_kernel(tab_ref, h2t_ref, s2_ref, p2_ref, th1a_ref, p1a_ref, th1b_ref, p1b_ref, u_ref, vt_ref, x1_ref, m_ref,
                 g2_ref, b2_ref, o_ref, acc_ref, act_ref, ga_ref):
    s = pl.program_id(1)
    n_steps = pl.num_programs(1)
    n_blk = PEER_TM // LANES
    rows_per_tile = PEER_TE // N_KEYS
    d_rows = acc_ref.shape[0] // n_blk
    u_rows = PEER_TE // n_blk

    def project(half, q):
        ro = pl.multiple_of(q * u_rows, u_rows)
        return _dot(u_ref[pl.ds(half * PEER_TE + ro, u_rows), :], h2t_ref[...])

    def store_act(slot, q, a):
        ro = pl.multiple_of(q * u_rows, u_rows)
        for k in range(n_blk):
            act_ref[slot, k, pl.ds(ro, u_rows), :] = a[:, k * LANES:(k + 1) * LANES]

    def apply_v(slot, half, q):
        ro = pl.multiple_of(q * d_rows, d_rows)
        ga = jnp.concatenate([ga_ref[slot, k] for k in range(n_blk)], axis=1)
        return _dot(vt_ref[pl.ds(ro, d_rows), half * PEER_TE:(half + 1) * PEER_TE], ga)

    def stage(cur, th1_ref, p1_ref):
        nxt = 1 - cur

        def body(cb, carry):
            a_next = project(nxt, cb)
            v_prev = apply_v(nxt, nxt, cb)
            th = [th1_ref[cb, h] for h in range(PEER_HEADS)]
            p1 = [p1_ref[cb, h] for h in range(PEER_HEADS)]
            for rg in range(0, rows_per_tile, ROW_GROUP):
                for part in range(N_KEYS // KEY_CHUNK):
                    ks = slice(part * KEY_CHUNK, (part + 1) * KEY_CHUNK)
                    gates = [None] * ROW_GROUP
                    for h in range(PEER_HEADS):
                        s2 = s2_ref[cb, h, ks, :]
                        p2 = p2_ref[cb, h, ks, :]
                        for g in range(ROW_GROUP):
                            r = rg + g
                            term = jnp.where(s2 >= th[h][r:r + 1], p2, 0.0) * p1[h][r:r + 1]
                            gates[g] = term if gates[g] is None else gates[g] + term
                    for g in range(ROW_GROUP):
                        lo = (rg + g) * N_KEYS + part * KEY_CHUNK
                        rs = slice(lo, lo + KEY_CHUNK)
                        ga_ref[cur, cb, rs, :] = (gates[g] * _gelu(act_ref[cur, cb, rs, :])).astype(BF16)
            store_act(nxt, cb, a_next)
            ro = pl.multiple_of(cb * d_rows, d_rows)
            acc_ref[pl.ds(ro, d_rows), :] += v_prev
            return carry

        lax.fori_loop(0, n_blk, body, 0)

    @pl.when(s == 0)
    def _():
        acc_ref[...] = jnp.zeros(acc_ref.shape, F32)
        ga_ref[...] = jnp.zeros(ga_ref.shape, BF16)
        for q in range(n_blk):
            store_act(0, q, project(0, q))

    @pl.when(s > 0)
    def _():
        stage(1, th1a_ref, p1a_ref)

    @pl.when(s < n_steps - 1)
    def _():
        stage(0, th1b_ref, p1b_ref)

    @pl.when(s == n_steps - 1)
    def _():
        m = m_ref[0]
        for q in range(n_blk):
            acc_ref[q * d_rows:(q + 1) * d_rows, :] += apply_v(1, 1, q)
        ffn = acc_ref[...].T
        o_ref[...] = _ln(ALPHA * x1_ref[...] + m[5:6] * ffn) * g2_ref[...] + b2_ref[...]


def _peer(x1, tab, h2t, s2, p2, th1, p1, u_bf, vt_bf, mods, g2, b2):
    t, d = x1.shape
    tm = PEER_TM
    te = PEER_TE
    n_tiles = u_bf.shape[0] // te
    per = tm // ROW_TILE
    n_blk = tm // LANES
    hk = pl.BlockSpec((n_blk, PEER_HEADS, N_KEYS, LANES), lambda i, j, t_: (i, 0, 0, 0))
    rows = te // N_KEYS
    rows_a = pl.BlockSpec((n_blk, PEER_HEADS, rows, LANES), lambda i, j, t_: (i, 0, jnp.maximum(2 * j - 1, 0), 0))
    rows_b = pl.BlockSpec((n_blk, PEER_HEADS, rows, LANES), lambda i, j, t_: (i, 0, jnp.minimum(2 * j, n_tiles - 1), 0))
    in_specs = [pl.BlockSpec((d, tm), lambda i, j, t_: (0, i)),
                hk, hk, rows_a, rows_a, rows_b, rows_b,
                pl.BlockSpec((2 * te, d), lambda i, j, t_: (jnp.minimum(j, n_tiles // 2 - 1), 0)),
                pl.BlockSpec((d, 2 * te), lambda i, j, t_: (0, jnp.maximum(j - 1, 0))),
                pl.BlockSpec((tm, d), lambda i, j, t_: (i, 0)),
                pl.BlockSpec((1, 6, d), lambda i, j, t_: (t_[0, i * per], 0, 0)),
                pl.BlockSpec(g2.shape, lambda i, j, t_: (0, 0)),
                pl.BlockSpec(b2.shape, lambda i, j, t_: (0, 0))]
    return pl.pallas_call(
        _peer_kernel,
        grid_spec=pltpu.PrefetchScalarGridSpec(
            num_scalar_prefetch=1, grid=(t // tm, n_tiles // 2 + 1), in_specs=in_specs,
            out_specs=pl.BlockSpec((tm, d), lambda i, j, t_: (i, 0)),
            scratch_shapes=[pltpu.VMEM((d, tm), F32), pltpu.VMEM((2, n_blk, te, LANES), F32),
                            pltpu.VMEM((2, n_blk, te, LANES), BF16)]),
        out_shape=jax.ShapeDtypeStruct((t, d), F32),
        compiler_params=_cparams(("arbitrary", "arbitrary")),
        name="peer_mix",
    )(tab, h2t, s2, p2, th1, p1, th1, p1, u_bf, vt_bf, x1, mods, g2, b2)


def _sincos(pos, dim):
    omega = 1.0 / (10000.0 ** (jnp.arange(dim // 2, dtype=F32) / (dim // 2)))
    ang = pos.astype(F32)[:, None] * omega[None, :]
    return jnp.concatenate([jnp.sin(ang), jnp.cos(ang)], -1)


def _grid_pos_embed(n_tok):
    rows = n_tok // GRID_W
    half = D_MODEL // 2
    er = _sincos(jnp.arange(rows), half)
    ec = _sincos(jnp.arange(GRID_W), half)
    emb = jnp.concatenate([jnp.broadcast_to(er[:, None, :], (rows, GRID_W, half)),
                           jnp.broadcast_to(ec[None, :, :], (rows, GRID_W, half))], -1)
    return emb.reshape(rows * GRID_W, D_MODEL)


def _seq_dft_tables(s_len):
    r = 1
    while r * r < s_len:
        r *= 2
    k = jnp.arange(s_len, dtype=jnp.int32)
    w = 2.0 * math.pi / s_len
    ang_a = ((jnp.arange(s_len // r, dtype=jnp.int32)[:, None] * r * k[None, :]) % s_len).astype(F32) * w
    ang_b = ((jnp.arange(r, dtype=jnp.int32)[:, None] * k[None, :]) % s_len).astype(F32) * w
    ca, sa = jnp.cos(ang_a)[:, None, :], jnp.sin(ang_a)[:, None, :]
    cb, sb = jnp.cos(ang_b)[None, :, :], jnp.sin(ang_b)[None, :, :]
    scale = 1.0 / math.sqrt(s_len * FNET_GW)
    cmat = ((ca * cb - sa * sb) * scale).reshape(s_len, s_len)
    smat = ((sa * cb + ca * sb) * (-scale)).reshape(s_len, s_len)
    return cmat.astype(BF16), smat.astype(BF16)


def _channel_dft_table():
    n = np.arange(FNET_GW)
    ang = ((n[:, None] * n[None, :]) % FNET_GW) * (2.0 * np.pi / FNET_GW)
    wc = np.zeros((FNET_W, 2 * FNET_W), np.float32)
    for g in range(FNET_GROUPS):
        sl = slice(g * FNET_GW, (g + 1) * FNET_GW)
        wc[sl, sl] = np.cos(ang)
        wc[sl, FNET_W + g * FNET_GW:FNET_W + (g + 1) * FNET_GW] = np.sin(ang)
    return jnp.asarray(wc, BF16)


def _segment_ones():
    idx = np.arange(WKV_W) // WKV_N
    return jnp.asarray((idx[:, None] == idx[None, :]).astype(np.float32), BF16)


def _chunk_tri():
    i = np.arange(ROW_TILE)
    same = (i[:, None] // CHUNK) == (i[None, :] // CHUNK)
    lower = same & (i[None, :] <= i[:, None])
    upper = same & (i[None, :] >= i[:, None])
    return jnp.asarray(np.stack([lower, upper]).astype(np.float32), BF16)


def _scan_masks():
    i = np.arange(PAIR)
    same = (i[:, None] // CHUNK) == (i[None, :] // CHUNK)
    t_row = i[:, None] % CHUNK
    t_col = i[None, :] % CHUNK
    masks = [same & (t_col < t_row), same & (t_col <= t_row), same & (t_col > t_row), same & (t_col >= t_row),
             np.eye(PAIR, dtype=bool)]
    return jnp.asarray(np.stack(masks).astype(np.float32))


def _block_diag(w):
    h, n, _ = w.shape
    eye = jnp.eye(h, dtype=w.dtype)
    return jnp.einsum("hij,hg->higj", w, eye).reshape(h * n, h * n)


def _pad_rank(w, d):
    r = w.shape[1]
    z = jnp.zeros_like(w[d])
    return jnp.concatenate([w[d], z] if d == 0 else [z, w[d]], axis=0)


def _state_to_pairs(s0):
    b = s0.shape[0]
    st = jnp.swapaxes(s0, -1, -2).reshape(b, 2, N_PAIRS, 2, WKV_N, WKV_N)
    eye = jnp.eye(2, dtype=s0.dtype)
    return jnp.einsum("bdpeji,ef->bdpejfi", st, eye).reshape(b, 2, N_PAIRS, PAIR, PAIR)


def _pairs_to_state(m):
    b = m.shape[0]
    m7 = m.reshape(b, 2, N_PAIRS, 2, WKV_N, 2, WKV_N)
    st = jnp.stack([m7[:, :, :, 0, :, 0, :], m7[:, :, :, 1, :, 1, :]], axis=3)
    return jnp.swapaxes(st.reshape(b, 2, WKV_H, WKV_N, WKV_N), -1, -2)


def _tile_table(groups):
    cv, first, last = [], [], []
    row = 0
    for gi, (n_seq, s_len) in enumerate(groups):
        nt = s_len // ROW_TILE
        for b in range(n_seq):
            for i in range(nt):
                cv.append(0 if gi == 0 else 1 + b)
                first.append(int(i == 0))
                last.append(int(i == nt - 1))
        row += n_seq * s_len
    return jnp.asarray(np.array([cv, first, last], np.int32))


def kernel(x_prompt, x_sample, state_lru, state_wkv, c, c_ctx, w_mod, b_mod, ln1_g, ln1_b, ln2_g, ln2_b,
           w_in_e, w_out_e, sconv_w, sconv_b, w_in_o, w_out_o, lru_conv_w, lru_conv_b, lru_wa, lru_ba,
           lru_wx, lru_bx, lru_lambda, wkv_mu, wkv_w0, wkv_w2, wkv_a0, wkv_a2, wkv_kk, wkv_ka, wkv_rk,
           wkv_g2, wkv_gn_g, wkv_gn_b, peer_wq, peer_keys, peer_u, peer_v):
    bp, sp, d = x_prompt.shape
    bs, ss, _ = x_sample.shape
    depth = w_mod.shape[0]
    assert sp % ROW_TILE == 0 and ss % PEER_TM == 0 and (bp * sp) % ss == 0
    assert bs + 1 <= 8
    groups = ((bp, sp), (bs, ss))
    tp = bp * sp
    tab = _tile_table(groups)

    cv8 = jnp.concatenate([c_ctx[None, :], c, jnp.zeros((8 - 1 - bs, d), F32)], axis=0)
    mods = _modulation(cv8, w_mod, b_mod).reshape(depth, 8, 6, d)

    x = _embed(x_prompt.reshape(tp, d), x_sample.reshape(bs * ss, d), _grid_pos_embed(ss).astype(x_sample.dtype))

    wc = _channel_dft_table()
    dft = {s: _seq_dft_tables(s) for s in sorted({sp, ss})}
    seg = _segment_ones()
    tri = _chunk_tri()
    masks = _scan_masks()
    row2 = lambda a: a.reshape(1, -1)

    lru_fin = []
    wkv_fin = []
    for l in range(depth):
        j = l // 2
        m_l = mods[l]
        wq = peer_wq[l].astype(BF16)
        keys = peer_keys[l].reshape(2 * PEER_HEADS, N_KEYS, PEER_HALF).astype(BF16)
        if l % 2 == 0:
            zr, pq = _inproj(x, tab, m_l, w_in_e[j].astype(BF16), wc)
            y = jnp.concatenate([_seqdft(pq, bp * sp, 0, bp, sp, *dft[sp]),
                                 _seqdft(pq, bs * ss, tp, bs, ss, *dft[ss])], axis=0)
            x1, h2t, th1, p1, p2, s2 = _post_even(x, tab, zr, y, m_l, sconv_w[j], row2(sconv_b[j]), w_out_e[j].astype(BF16),
                                     row2(ln1_g[l]), row2(ln1_b[l]), wq, keys)
        else:
            zl, zw = _inproj(x, tab, m_l, w_in_o[j].astype(BF16))
            lru_args = (lru_conv_w[j], lru_conv_b[j],
                        jnp.stack([_block_diag(lru_wa[j, dd]) for dd in range(2)]).astype(BF16), lru_ba[j],
                        jnp.stack([_block_diag(lru_wx[j, dd]) for dd in range(2)]).astype(BF16), lru_bx[j],
                        lru_lambda[j])
            hf_p, hb_p, hfin_p = _lru(zl, 0, bp, sp, jnp.zeros((bp, 2, LRU_W), F32), *lru_args)
            hf_s, hb_s, _ = _lru(zl, tp, bs, ss, state_lru[:, j], *lru_args)
            lru_fin.append(hfin_p)
            prep = _wkvprep(zw, tab, row2(wkv_mu[j]), wkv_w0[j],
                            jnp.stack([_pad_rank(wkv_w2[j], dd) for dd in range(2)]).astype(BF16), wkv_a0[j],
                            jnp.stack([_pad_rank(wkv_a2[j], dd) for dd in range(2)]).astype(BF16),
                            wkv_kk[j], wkv_ka[j], row2(wkv_rk[j]), wkv_g2[j].astype(BF16), seg, tri)
            v, bonus, g = prep[0], prep[1], prep[2]
            yf_p, yb_p, mfin_p = _wkvscan(prep[3:], v, 0, bp, sp,
                                          jnp.zeros((bp, 2, N_PAIRS, PAIR, PAIR), F32), masks)
            yf_s, yb_s, _ = _wkvscan(prep[3:], v, tp, bs, ss, _state_to_pairs(state_wkv[:, j]), masks)
            wkv_fin.append(_pairs_to_state(mfin_p))
            x1, h2t, th1, p1, p2, s2 = _post_odd(x, tab, (hf_p, hf_s), (hb_p, hb_s), zl, (yf_p, yf_s), (yb_p, yb_s),
                                    bonus, g, m_l, seg, row2(wkv_gn_g[j]), row2(wkv_gn_b[j]),
                                    w_out_o[j].astype(BF16), row2(ln1_g[l]), row2(ln1_b[l]), wq, keys)
        x = _peer(x1, tab, h2t, s2, p2, th1, p1, peer_u[l].astype(BF16), peer_v[l].T.astype(BF16), m_l,
                  row2(ln2_g[l]), row2(ln2_b[l]))

    y_prompt = x[:tp].reshape(bp, sp, d)
    y_sample = x[tp:].reshape(bs, ss, d)
    return (y_prompt, y_sample, jnp.stack(lru_fin, 1).astype(x_prompt.dtype),
            jnp.stack(wkv_fin, 1).astype(x_prompt.dtype))
```

```python
import functools
import math

import numpy as np
import jax
import jax.numpy as jnp
from jax import lax
from jax.experimental import pallas as pl
from jax.experimental.pallas import tpu as pltpu

F32 = jnp.float32
BF16 = jnp.bfloat16

D_MODEL = 1024
GRID_W = 64
FNET_W = 512
FNET_GROUPS = 4
FNET_GW = FNET_W // FNET_GROUPS
CONV_W = 512
LRU_W = 512
LRU_HEADS = 8
LRU_CONV_K = 4
LRU_C = 8.0
WKV_W = 512
WKV_N = 64
WKV_H = 8
WKV_IN = 1920
DECAY_SCALE = math.exp(-0.5)
WKV_GN_EPS = 64e-5
PEER_HEADS = 8
N_KEYS = 128
PEER_TOPK = 16
PEER_HALF = 128
DEPTH = 4
ALPHA = (2 * DEPTH) ** 0.25
LN_EPS = 1e-6

ROW_TILE = 256
LANES = 128
HALO = 8
CHUNK = 64
PAIR = 2 * WKV_N
N_PAIRS = WKV_H // 2
PEER_TM = 512
PEER_TE = 1024
ROW_GROUP = 4
KEY_CHUNK = 32
VMEM_LIMIT = 56 * 1024 * 1024


def _cparams(sem):
    return pltpu.CompilerParams(dimension_semantics=sem, vmem_limit_bytes=VMEM_LIMIT)


def _dot(a, b):
    return jnp.dot(a, b, preferred_element_type=F32)


def _dot_nt(a, b):
    return lax.dot_general(a, b, (((1,), (1,)), ((), ())), preferred_element_type=F32)


def _mm3(a, b):
    ah = a.astype(BF16)
    al = (a - ah.astype(F32)).astype(BF16)
    bh = b.astype(BF16)
    bl = (b - bh.astype(F32)).astype(BF16)
    return _dot(ah, bh) + (_dot(ah, bl) + _dot(al, bh))


def _mm1(a, b):
    return _dot(a.astype(BF16), b.astype(BF16))


def _mm01(a, b01):
    h = a.astype(BF16)
    r = a - h.astype(F32)
    m = r.astype(BF16)
    lo = (r - m.astype(F32)).astype(BF16)
    return _dot(h, b01) + (_dot(m, b01) + _dot(lo, b01))


def _ln(x):
    mu = jnp.mean(x, axis=-1, keepdims=True)
    xc = x - mu
    var = jnp.mean(xc * xc, axis=-1, keepdims=True)
    return xc * lax.rsqrt(var + LN_EPS)


def _gelu(x):
    z = x * (0.7978845608028654 + 0.035677408136300125 * (x * x))
    hx = 0.5 * x
    return hx + hx * jnp.tanh(z)


def _sigmoid(x):
    return 1.0 / (1.0 + jnp.exp(-x))


def _mod_kernel(c_ref, w_ref, b_ref, o_ref):
    c = c_ref[...]
    sc = c * _sigmoid(c)
    o_ref[0] = _dot(sc.astype(BF16), w_ref[0].astype(BF16)) + b_ref[0]


def _modulation(cv8, w_mod, b_mod):
    depth, d, n = w_mod.shape
    tn = 1536
    return pl.pallas_call(
        _mod_kernel,
        grid=(depth, n // tn),
        in_specs=[pl.BlockSpec((8, d), lambda l, j: (0, 0)),
                  pl.BlockSpec((1, d, tn), lambda l, j: (l, 0, j)),
                  pl.BlockSpec((1, 1, tn), lambda l, j: (l, 0, j))],
        out_specs=pl.BlockSpec((1, 8, tn), lambda l, j: (l, 0, j)),
        out_shape=jax.ShapeDtypeStruct((depth, 8, n), F32),
        compiler_params=_cparams(("arbitrary", "arbitrary")),
        name="modulation",
    )(cv8, w_mod, b_mod.reshape(depth, 1, n))


def _embed_kernel(xp_ref, xs_ref, p_ref, o_ref, *, n_prompt):
    i = pl.program_id(0)

    @pl.when(i < n_prompt)
    def _():
        o_ref[...] = xp_ref[...]

    @pl.when(i >= n_prompt)
    def _():
        o_ref[...] = xs_ref[...] + p_ref[...]


def _embed(xp, xs, pos):
    tp, d = xp.shape
    ts = xs.shape[0]
    s_len = pos.shape[0]
    tm = 512
    n_prompt = tp // tm
    per_seq = s_len // tm
    return pl.pallas_call(
        functools.partial(_embed_kernel, n_prompt=n_prompt),
        grid=((tp + ts) // tm,),
        in_specs=[pl.BlockSpec((tm, d), lambda i: (jnp.minimum(i, n_prompt - 1), 0)),
                  pl.BlockSpec((tm, d), lambda i: (jnp.maximum(i - n_prompt, 0), 0)),
                  pl.BlockSpec((tm, d), lambda i: (jnp.maximum(i - n_prompt, 0) % per_seq, 0))],
        out_specs=pl.BlockSpec((tm, d), lambda i: (i, 0)),
        out_shape=jax.ShapeDtypeStruct((tp + ts, d), xp.dtype),
        compiler_params=_cparams(("arbitrary",)),
        name="embed",
    )(xp, xs, pos)


def _inproj_even_kernel(tab_ref, x_ref, m_ref, w_ref, wc_ref, zr_ref, pq_ref):
    m = m_ref[0]
    h = _ln(x_ref[...]) * (1.0 + m[1:2]) + m[0:1]
    z = _dot(h.astype(BF16), w_ref[...])
    zr_ref[...] = z[:, FNET_W:]
    pq_ref[...] = _dot(z[:, :FNET_W].astype(BF16), wc_ref[...]).astype(BF16)


def _inproj_odd_kernel(tab_ref, x_ref, m_ref, w_ref, zl_ref, zw_ref):
    m = m_ref[0]
    h = _ln(x_ref[...]) * (1.0 + m[1:2]) + m[0:1]
    z = _dot(h.astype(BF16), w_ref[...])
    zl_ref[...] = z[:, :2 * LRU_W]
    zw_ref[...] = z[:, 2 * LRU_W:]


def _const_spec(shape):
    nd = len(shape)
    return pl.BlockSpec(shape, lambda i, t, _n=nd: (0,) * _n)


def _inproj(x, tab, mods, w_in, wc=None):
    t, d = x.shape
    n = w_in.shape[1]
    tm = ROW_TILE
    even = wc is not None
    in_specs = [pl.BlockSpec((tm, d), lambda i, t_: (i, 0)),
                pl.BlockSpec((1, 6, d), lambda i, t_: (t_[0, i], 0, 0)),
                _const_spec((d, n))]
    if even:
        in_specs.append(_const_spec(wc.shape))
        widths = (n - FNET_W, 2 * FNET_W)
        dtypes = (F32, BF16)
        kern = _inproj_even_kernel
        args = (tab, x, mods, w_in, wc)
    else:
        widths = (2 * LRU_W, n - 2 * LRU_W)
        dtypes = (F32, F32)
        kern = _inproj_odd_kernel
        args = (tab, x, mods, w_in)
    return pl.pallas_call(
        kern,
        grid_spec=pltpu.PrefetchScalarGridSpec(
            num_scalar_prefetch=1, grid=(t // tm,), in_specs=in_specs,
            out_specs=[pl.BlockSpec((tm, w), lambda i, t_: (i, 0)) for w in widths]),
        out_shape=[jax.ShapeDtypeStruct((t, w), dt) for w, dt in zip(widths, dtypes)],
        compiler_params=_cparams(("arbitrary",)),
        name="inproj_even" if even else "inproj_odd",
    )(*args)


def _seqdft_kernel(c_ref, s_ref, p_ref, q_ref, y_ref):
    y_ref[...] = (_dot(c_ref[...], p_ref[...]) + _dot(s_ref[...], q_ref[...])).astype(BF16)


def _seqdft(pq, y_rows, row0, n_seq, s_len, cmat, smat):
    tm = min(s_len, ROW_TILE)
    nt = s_len // tm
    blk0 = row0 // s_len
    return pl.pallas_call(
        _seqdft_kernel,
        grid=(n_seq, nt),
        in_specs=[pl.BlockSpec((tm, s_len), lambda b, i: (i, 0)),
                  pl.BlockSpec((tm, s_len), lambda b, i: (i, 0)),
                  pl.BlockSpec((s_len, FNET_W), lambda b, i: (blk0 + b, 0)),
                  pl.BlockSpec((s_len, FNET_W), lambda b, i: (blk0 + b, 1))],
        out_specs=pl.BlockSpec((tm, FNET_W), lambda b, i: (b * nt + i, 0)),
        out_shape=jax.ShapeDtypeStruct((y_rows, FNET_W), BF16),
        compiler_params=_cparams(("arbitrary", "arbitrary")),
        name="seqdft_%d" % s_len,
    )(cmat, smat, pq, pq)


def _post_and_query(x, mix, m, g1, b1, wq_ref, keys_ref, x1_ref, h2t_ref, topk_refs, st_ref, v1_ref, v2_ref):
    x1 = _ln(ALPHA * x + m[2:3] * mix) * g1 + b1
    x1_ref[...] = x1
    h2 = _ln(x1) * (1.0 + m[4:5]) + m[3:4]
    h2b = h2.astype(BF16)
    h2t_ref[...] = h2.T.astype(BF16)
    q = _dot(h2b, wq_ref[...])
    for hp in range(2 * PEER_HEADS):
        qb = q[:, hp * PEER_HALF:(hp + 1) * PEER_HALF].astype(BF16)
        st_ref[hp * N_KEYS:(hp + 1) * N_KEYS, :] = _dot_nt(keys_ref[hp], qb)
    _topk_heads(st_ref, *topk_refs, v1_ref, v2_ref)


def _post_even_kernel(tab_ref, x_ref, bg_ref, cg_ref, xi_ref, cgp_ref, xip_ref, cgn_ref, xin_ref, y_ref,
                      m_ref, cw_ref, cb_ref, wo_ref, g1_ref, b1_ref, wq_ref, keys_ref,
                      x1_ref, h2t_ref, th1_ref, p1_ref, p2_ref, s2_ref, ext_ref, st_ref, v1_ref, v2_ref):
    i = pl.program_id(0)
    tm = ROW_TILE
    first = tab_ref[1, i] == 1
    last = tab_ref[2, i] == 1
    u = cg_ref[...] * xi_ref[...]
    ext_ref[0:HALO] = jnp.where(first, 0.0, cgp_ref[...] * xip_ref[...])
    ext_ref[HALO:HALO + tm] = u
    ext_ref[HALO + tm:2 * HALO + tm] = jnp.where(last, 0.0, cgn_ref[...] * xin_ref[...])
    cw = cw_ref[...]
    conv = (ext_ref[HALO - 1:HALO - 1 + tm] * cw[0:1] + u * cw[1:2]
            + ext_ref[HALO + 1:HALO + 1 + tm] * cw[2:3] + cb_ref[...])
    ymix = (bg_ref[...] * conv).astype(BF16)
    mix = _dot(y_ref[...], wo_ref[0:FNET_W]) + _dot(ymix, wo_ref[FNET_W:])
    _post_and_query(x_ref[...], mix, m_ref[0], g1_ref[...], b1_ref[...], wq_ref, keys_ref,
                    x1_ref, h2t_ref, (th1_ref, p1_ref, p2_ref, s2_ref), st_ref, v1_ref, v2_ref)


def _post_out_specs(t, d, tm):
    rshape = jax.ShapeDtypeStruct((t // LANES, PEER_HEADS, N_KEYS, LANES), F32)
    rspec = pl.BlockSpec((tm // LANES, PEER_HEADS, N_KEYS, LANES), lambda i, t_: (i, 0, 0, 0))
    specs = [pl.BlockSpec((tm, d), lambda i, t_: (i, 0)),
             pl.BlockSpec((d, tm), lambda i, t_: (0, i))] + [rspec] * 4
    shapes = [jax.ShapeDtypeStruct((t, d), F32),
              jax.ShapeDtypeStruct((d, t), BF16)] + [rshape] * 4
    return specs, shapes


def _post_scratch(tm):
    return [pltpu.VMEM((2 * PEER_HEADS * N_KEYS, tm), F32), pltpu.VMEM((TOP_ROWS, tm), F32),
            pltpu.VMEM((TOP_ROWS, tm), F32)]


def _post_even(x, tab, zr, y, mods, cw, cb, wo, g1, b1, wq, keys):
    t, d = x.shape
    tm = ROW_TILE
    hb = tm // HALO
    nblk = t // HALO
    prev = lambda c: pl.BlockSpec((HALO, CONV_W), lambda i, t_, _c=c: (jnp.maximum(i * hb - 1, 0), _c))
    nxt = lambda c: pl.BlockSpec((HALO, CONV_W), lambda i, t_, _c=c: (jnp.minimum((i + 1) * hb, nblk - 1), _c))
    col = lambda c: pl.BlockSpec((tm, CONV_W), lambda i, t_, _c=c: (i, _c))
    in_specs = [pl.BlockSpec((tm, d), lambda i, t_: (i, 0)),
                col(0), col(1), col(2), prev(1), prev(2), nxt(1), nxt(2),
                pl.BlockSpec((tm, FNET_W), lambda i, t_: (i, 0)),
                pl.BlockSpec((1, 6, d), lambda i, t_: (t_[0, i], 0, 0)),
                _const_spec(cw.shape), _const_spec(cb.shape), _const_spec(wo.shape),
                _const_spec(g1.shape), _const_spec(b1.shape), _const_spec(wq.shape), _const_spec(keys.shape)]
    out_specs, out_shape = _post_out_specs(t, d, tm)
    return pl.pallas_call(
        _post_even_kernel,
        grid_spec=pltpu.PrefetchScalarGridSpec(
            num_scalar_prefetch=1, grid=(t // tm,), in_specs=in_specs, out_specs=out_specs,
            scratch_shapes=[pltpu.VMEM((tm + 2 * HALO, CONV_W), F32)] + _post_scratch(tm)),
        out_shape=out_shape,
        compiler_params=_cparams(("arbitrary",)),
        name="post_even",
    )(tab, x, zr, zr, zr, zr, zr, zr, zr, y, mods, cw, cb, wo, g1, b1, wq, keys)


def _lru_kernel(xf_ref, xfh_ref, xb_ref, xbh_ref, cw_ref, cb_ref, wa_ref, ba_ref, wx_ref, bx_ref, lam_ref,
                h0_ref, hf_ref, hb_ref, hfin_ref, ext_ref, carry_ref):
    i = pl.program_id(1)
    nt = pl.num_programs(1)
    tm = ROW_TILE

    @pl.when(i == 0)
    def _():
        carry_ref[...] = h0_ref[0]

    row = lax.broadcasted_iota(jnp.int32, (tm, 1), 0)
    for d in range(2):
        x = (xf_ref if d == 0 else xb_ref)[...]
        halo = jnp.where(i == 0, 0.0, (xfh_ref if d == 0 else xbh_ref)[...])
        ext_ref[HALO:HALO + tm] = x
        if d == 0:
            ext_ref[0:HALO] = halo
        else:
            ext_ref[HALO + tm:2 * HALO + tm] = halo
        cw = cw_ref[d]
        xc = cb_ref[d:d + 1] + x * cw[LRU_CONV_K - 1:LRU_CONV_K]
        for j in range(LRU_CONV_K - 1):
            k = LRU_CONV_K - 1 - j
            off = HALO - k if d == 0 else HALO + k
            xc = xc + ext_ref[off:off + tm] * cw[j:j + 1]
        xcb = xc.astype(BF16)
        gate_r = _sigmoid(_dot(xcb, wa_ref[d]) + ba_ref[d:d + 1])
        gate_i = _sigmoid(_dot(xcb, wx_ref[d]) + bx_ref[d:d + 1])
        nl = -lam_ref[d:d + 1]
        softplus = jnp.maximum(nl, 0.0) + jnp.log1p(jnp.exp(-jnp.abs(nl)))
        log_a = -LRU_C * gate_r * softplus
        a = jnp.exp(log_a)
        b = jnp.sqrt(-jnp.tanh(log_a) * (a * a + 1.0)) * (gate_i * xc)
        s = 1
        while s < tm:
            if d == 0:
                keep = row >= s
                sh = s
            else:
                keep = row < tm - s
                sh = tm - s
            a_sh = jnp.where(keep, pltpu.roll(a, sh, 0), 1.0)
            b_sh = jnp.where(keep, pltpu.roll(b, sh, 0), 0.0)
            b = a * b_sh + b
            a = a * a_sh
            s *= 2
        h = a * carry_ref[d:d + 1] + b
        if d == 0:
            hf_ref[...] = h
            carry_ref[0:1] = h[tm - 1:tm]
        else:
            hb_ref[...] = h
            carry_ref[1:2] = h[0:1]

    @pl.when(i == nt - 1)
    def _():
        hfin_ref[0] = carry_ref[...]


def _lru(zl, row0, n_seq, s_len, h0, cw, cb, wa, ba, wx, bx, lam):
    tm = ROW_TILE
    nt = s_len // tm
    t0 = row0 // tm
    hb = tm // HALO
    nblk = zl.shape[0] // HALO
    rows = n_seq * s_len
    fwd = lambda b, i: (t0 + b * nt + i, 0)
    bwd = lambda b, i: (t0 + b * nt + nt - 1 - i, 0)
    fwd_h = lambda b, i: (jnp.maximum((t0 + b * nt + i) * hb - 1, 0), 0)
    bwd_h = lambda b, i: (jnp.minimum((t0 + b * nt + nt - i) * hb, nblk - 1), 0)
    out_f = lambda b, i: (b * nt + i, 0)
    out_b = lambda b, i: (b * nt + nt - 1 - i, 0)
    cst = lambda a: pl.BlockSpec(a.shape, lambda b, i, _n=a.ndim: (0,) * _n)
    return pl.pallas_call(
        _lru_kernel,
        grid=(n_seq, nt),
        in_specs=[pl.BlockSpec((tm, LRU_W), fwd), pl.BlockSpec((HALO, LRU_W), fwd_h),
                  pl.BlockSpec((tm, LRU_W), bwd), pl.BlockSpec((HALO, LRU_W), bwd_h),
                  cst(cw), cst(cb), cst(wa), cst(ba), cst(wx), cst(bx), cst(lam),
                  pl.BlockSpec((1, 2, LRU_W), lambda b, i: (b, 0, 0))],
        out_specs=[pl.BlockSpec((tm, LRU_W), out_f), pl.BlockSpec((tm, LRU_W), out_b),
                   pl.BlockSpec((1, 2, LRU_W), lambda b, i: (b, 0, 0))],
        out_shape=[jax.ShapeDtypeStruct((rows, LRU_W), F32), jax.ShapeDtypeStruct((rows, LRU_W), F32),
                   jax.ShapeDtypeStruct((n_seq, 2, LRU_W), F32)],
        scratch_shapes=[pltpu.VMEM((tm + 2 * HALO, LRU_W), F32), pltpu.VMEM((2, LRU_W), F32)],
        compiler_params=_cparams(("arbitrary", "arbitrary")),
        name="lru_%d" % s_len,
    )(zl, zl, zl, zl, cw, cb, wa, ba, wx, bx, lam, h0)


def _wkvprep_kernel(tab_ref, z_ref, zp_ref, zn_ref, mu_ref, w0_ref, w2_ref, a0_ref, a2_ref, kk_ref, ka_ref,
                    rk_ref, g2_ref, seg_ref, tri_ref,
                    v_ref, bg_ref, g_ref, rf_ref, kf_ref, bf_ref, qf_ref, cf_ref,
                    rb_ref, kb_ref, bb_ref, qb_ref, cb_ref, ext_ref):
    i = pl.program_id(0)
    tm = ROW_TILE
    w = WKV_W
    first = tab_ref[1, i] == 1
    last = tab_ref[2, i] == 1
    z = z_ref[...]
    ext_ref[0:HALO] = jnp.where(first, 0.0, zp_ref[...])
    ext_ref[HALO:HALO + tm] = z
    ext_ref[HALO + tm:2 * HALO + tm] = jnp.where(last, 0.0, zn_ref[...])
    z = z + mu_ref[...] * (0.5 * (ext_ref[HALO - 1:HALO - 1 + tm] + ext_ref[HALO + 1:HALO + 1 + tm]) - z)
    r = z[:, 0:w]
    k = z[:, w:2 * w]
    v = z[:, 2 * w:3 * w]
    wd = jnp.tanh(z[:, 3 * w:3 * w + 128]).astype(BF16)
    ad = z[:, 3 * w + 128:3 * w + 256].astype(BF16)
    gd = _sigmoid(z[:, 3 * w + 256:3 * w + 384]).astype(BF16)
    v_ref[...] = v
    g_ref[...] = _dot(gd, g2_ref[...])
    seg = seg_ref[...]
    rk = rk_ref[...]
    bonus = jnp.zeros((tm, w), F32)
    outs = ((rf_ref, kf_ref, bf_ref, qf_ref, cf_ref), (rb_ref, kb_ref, bb_ref, qb_ref, cb_ref))
    nch = tm // CHUNK
    for d in range(2):
        r_ref, k_ref, b_ref, q_ref, c_ref = outs[d]
        wz = w0_ref[d:d + 1] + _dot(wd, w2_ref[d])
        logw = -DECAY_SCALE * _sigmoid(wz)
        iclr = _sigmoid(a0_ref[d:d + 1] + _dot(ad, a2_ref[d]))
        kk = k * kk_ref[d:d + 1]
        kk = kk * lax.rsqrt(jnp.maximum(_mm01(kk * kk, seg), 1e-24))
        km = k * (1.0 + (iclr - 1.0) * ka_ref[d:d + 1])
        bonus = bonus + _mm1(r * km * rk, seg) * v
        lw_h = logw.astype(BF16)
        lw_r = logw - lw_h.astype(F32)
        lw_m = lw_r.astype(BF16)
        lw_l = (lw_r - lw_m.astype(F32)).astype(BF16)
        tri = tri_ref[d]
        cl = _dot(tri, lw_h) + (_dot(tri, lw_m) + _dot(tri, lw_l))
        c = jnp.exp(cl)
        cinv = jnp.exp(-cl)
        r_ref[...] = r * c
        k_ref[...] = km * cinv
        b_ref[...] = kk * iclr * cinv
        q_ref[...] = kk * jnp.exp(cl - logw)
        for j in range(nch):
            edge = (j + 1) * CHUNK - 1 if d == 0 else j * CHUNK
            c_ref[j] = c[edge:edge + 1]
    bg_ref[...] = bonus


def _wkvprep(zw, tab, mu, w0, w2p, a0, a2p, kk, ka, rk, g2, seg, tri):
    t, n = zw.shape
    tm = ROW_TILE
    hb = tm // HALO
    nblk = t // HALO
    nch = tm // CHUNK
    w = WKV_W
    in_specs = [pl.BlockSpec((tm, n), lambda i, t_: (i, 0)),
                pl.BlockSpec((HALO, n), lambda i, t_: (jnp.maximum(i * hb - 1, 0), 0)),
                pl.BlockSpec((HALO, n), lambda i, t_: (jnp.minimum((i + 1) * hb, nblk - 1), 0))]
    in_specs += [_const_spec(a.shape) for a in (mu, w0, w2p, a0, a2p, kk, ka, rk, g2, seg, tri)]
    row = pl.BlockSpec((tm, w), lambda i, t_: (i, 0))
    cspec = pl.BlockSpec((nch, 1, w), lambda i, t_: (i, 0, 0))
    rshape = jax.ShapeDtypeStruct((t, w), F32)
    cshape = jax.ShapeDtypeStruct((t // CHUNK, 1, w), F32)
    out_specs = [row, row, row] + [row, row, row, row, cspec] * 2
    out_shape = [rshape, rshape, rshape] + [rshape, rshape, rshape, rshape, cshape] * 2
    return pl.pallas_call(
        _wkvprep_kernel,
        grid_spec=pltpu.PrefetchScalarGridSpec(
            num_scalar_prefetch=1, grid=(t // tm,), in_specs=in_specs, out_specs=out_specs,
            scratch_shapes=[pltpu.VMEM((tm + 2 * HALO, n), F32)]),
        out_shape=out_shape,
        compiler_params=_cparams(("arbitrary",)),
        name="wkv_prep",
    )(tab, zw, zw, zw, mu, w0, w2p, a0, a2p, kk, ka, rk, g2, seg, tri)


def _wkv_chunks(items, eye):
    lane = lax.broadcasted_iota(jnp.int32, (1, PAIR), 1)
    m0 = (lane < WKV_N).astype(F32)
    m1 = 1.0 - m0
    stack = lambda x: jnp.concatenate([x * m0, x * m1], axis=0)
    n = range(len(items))
    rh_s = [stack(it[0]) for it in items]
    kh_s = [stack(it[1]) for it in items]
    bh_s = [stack(it[2]) for it in items]
    kq_s = [stack(it[3]) for it in items]
    v_s = [stack(it[4]) for it in items]
    bh_t = [x.T for x in bh_s]
    kh_t = [x.T for x in kh_s]
    bk_t = [jnp.concatenate([bh_t[i], kh_t[i]], axis=1) for i in n]
    att_s = [_mm3(kq_s[i], bk_t[i]) for i in n]
    att_y = [_mm1(rh_s[i], bk_t[i]) for i in n]
    n1 = [att_s[i][:, 0:PAIR] * items[i][7] for i in n]
    ak = [att_s[i][:, PAIR:] * items[i][7] for i in n]
    gb = [att_y[i][:, 0:PAIR] * items[i][8] for i in n]
    gk = [att_y[i][:, PAIR:] * items[i][8] for i in n]
    t_inv = [eye - n1[i] for i in n]
    npow = n1
    for _ in range(5):
        npow = [_mm1(npow[i], npow[i]) for i in n]
        t_inv = [t_inv[i] + _mm1(t_inv[i], npow[i]) for i in n]
    kv = [_mm1(jnp.concatenate([kh_t[i], gk[i], ak[i]], axis=0), v_s[i]) for i in n]
    x = [_mm1(t_inv[i], jnp.concatenate([kq_s[i], kv[i][2 * PAIR:]], axis=1)) for i in n]
    bx = [_mm1(jnp.concatenate([bh_t[i], gb[i]], axis=0), x[i]) for i in n]
    p_mat = [eye - bx[i][0:PAIR, 0:PAIR] for i in n]
    q_mat = [kv[i][0:PAIR] - bx[i][0:PAIR, PAIR:] for i in n]
    r_til = [rh_s[i] - bx[i][PAIR:, 0:PAIR] for i in n]
    y0 = [kv[i][PAIR:2 * PAIR] - bx[i][PAIR:, PAIR:] for i in n]
    pm = [_mm3(p_mat[i], items[i][6]) for i in n]
    y_st = [_mm1(r_til[i], items[i][6]) + y0[i] for i in n]
    ys = [y_st[i][0:CHUNK] + y_st[i][CHUNK:] for i in n]
    c_col = [jnp.broadcast_to(items[i][5], (PAIR, PAIR)).T for i in n]
    m_new = [(pm[i] + q_mat[i]) * c_col[i] for i in n]
    return ys, m_new


def _wkvscan_kernel(rf_ref, kf_ref, bf_ref, qf_ref, vf_ref, cf_ref, rb_ref, kb_ref, bb_ref, qb_ref, vb_ref, cb_ref,
                    m0_ref, msk_ref, yf_ref, yb_ref, mfin_ref, m_ref):
    i = pl.program_id(1)
    nc = pl.num_programs(1)

    @pl.when(i == 0)
    def _():
        m_ref[...] = m0_ref[0]

    eye = msk_ref[4]
    ins = ((rf_ref, kf_ref, bf_ref, qf_ref, vf_ref, cf_ref, yf_ref),
           (rb_ref, kb_ref, bb_ref, qb_ref, vb_ref, cb_ref, yb_ref))
    items = []
    for d in range(2):
        r_ref, k_ref, b_ref, q_ref, v_ref, c_ref, _ = ins[d]
        for p in range(N_PAIRS):
            sl = slice(p * PAIR, (p + 1) * PAIR)
            items.append((r_ref[:, sl], k_ref[:, sl], b_ref[:, sl], q_ref[:, sl], v_ref[:, sl],
                          c_ref[0, :, sl], m_ref[d, p], msk_ref[2 * d], msk_ref[2 * d + 1]))
    ys, m_new = _wkv_chunks(items, eye)
    for d in range(2):
        for p in range(N_PAIRS):
            sl = slice(p * PAIR, (p + 1) * PAIR)
            ins[d][6][:, sl] = ys[d * N_PAIRS + p]
            m_ref[d, p] = m_new[d * N_PAIRS + p]

    @pl.when(i == nc - 1)
    def _():
        mfin_ref[0] = m_ref[...]


def _wkvscan(prep, v, row0, n_seq, s_len, m0, masks):
    rf, kf, bf, qf, cf, rb, kb, bb, qb, cb = prep
    nc = s_len // CHUNK
    c0 = row0 // CHUNK
    rows = n_seq * s_len
    w = WKV_W
    fwd = lambda b, i: (c0 + b * nc + i, 0)
    bwd = lambda b, i: (c0 + b * nc + nc - 1 - i, 0)
    fwd3 = lambda b, i: (c0 + b * nc + i, 0, 0)
    bwd3 = lambda b, i: (c0 + b * nc + nc - 1 - i, 0, 0)
    blk = lambda im: pl.BlockSpec((CHUNK, w), im)
    cblk = lambda im: pl.BlockSpec((1, 1, w), im)
    mspec = pl.BlockSpec((1, 2, N_PAIRS, PAIR, PAIR), lambda b, i: (b, 0, 0, 0, 0))
    return pl.pallas_call(
        _wkvscan_kernel,
        grid=(n_seq, nc),
        in_specs=[blk(fwd)] * 5 + [cblk(fwd3)] + [blk(bwd)] * 5 + [cblk(bwd3)]
        + [mspec, pl.BlockSpec(masks.shape, lambda b, i: (0, 0, 0))],
        out_specs=[pl.BlockSpec((CHUNK, w), lambda b, i: (b * nc + i, 0)),
                   pl.BlockSpec((CHUNK, w), lambda b, i: (b * nc + nc - 1 - i, 0)),
                   mspec],
        out_shape=[jax.ShapeDtypeStruct((rows, w), F32), jax.ShapeDtypeStruct((rows, w), F32),
                   jax.ShapeDtypeStruct(m0.shape, F32)],
        scratch_shapes=[pltpu.VMEM((2, N_PAIRS, PAIR, PAIR), F32)],
        compiler_params=_cparams(("arbitrary", "arbitrary")),
        name="wkv_scan_%d" % s_len,
    )(rf, kf, bf, qf, v, cf, rb, kb, bb, qb, v, cb, m0, masks)


def _post_odd_kernel(tab_ref, x_ref, hfp_ref, hfs_ref, hbp_ref, hbs_ref, gb_ref, yfp_ref, yfs_ref, ybp_ref, ybs_ref,
                     bon_ref, g_ref, m_ref, seg_ref, gng_ref, gnb_ref, wo_ref, g1_ref, b1_ref, wq_ref, keys_ref,
                     x1_ref, h2t_ref, th1_ref, p1_ref, p2_ref, s2_ref, st_ref, v1_ref, v2_ref, *, n_prompt_tiles):
    is_prompt = pl.program_id(0) < n_prompt_tiles
    pick = lambda p_ref, s_ref: jnp.where(is_prompt, p_ref[...], s_ref[...])
    y_lru = ((pick(hfp_ref, hfs_ref) + pick(hbp_ref, hbs_ref)) * _gelu(gb_ref[...])).astype(BF16)
    seg = seg_ref[...]
    ys = pick(yfp_ref, yfs_ref) + pick(ybp_ref, ybs_ref)
    mean = _mm1(ys, seg) * (1.0 / WKV_N)
    yc = ys - mean
    var = _mm1(yc * yc, seg) * (1.0 / WKV_N)
    yn = yc * lax.rsqrt(var + WKV_GN_EPS) * gng_ref[...] + gnb_ref[...]
    y_wkv = ((yn + bon_ref[...]) * g_ref[...]).astype(BF16)
    mix = _dot(y_lru, wo_ref[0:LRU_W]) + _dot(y_wkv, wo_ref[LRU_W:])
    _post_and_query(x_ref[...], mix, m_ref[0], g1_ref[...], b1_ref[...], wq_ref, keys_ref,
                    x1_ref, h2t_ref, (th1_ref, p1_ref, p2_ref, s2_ref), st_ref, v1_ref, v2_ref)


def _post_odd(x, tab, hf, hb, zl, yf, yb, bonus, g, mods, seg, gng, gnb, wo, g1, b1, wq, keys):
    t, d = x.shape
    tm = ROW_TILE
    w = WKV_W
    n_p = hf[0].shape[0] // tm
    n_s = hf[1].shape[0] // tm
    row = pl.BlockSpec((tm, w), lambda i, t_: (i, 0))
    row_p = pl.BlockSpec((tm, w), lambda i, t_: (jnp.minimum(i, n_p - 1), 0))
    row_s = pl.BlockSpec((tm, w), lambda i, t_: (jnp.clip(i - n_p, 0, n_s - 1), 0))
    in_specs = [pl.BlockSpec((tm, d), lambda i, t_: (i, 0)), row_p, row_s, row_p, row_s,
                pl.BlockSpec((tm, LRU_W), lambda i, t_: (i, 1)), row_p, row_s, row_p, row_s, row, row,
                pl.BlockSpec((1, 6, d), lambda i, t_: (t_[0, i], 0, 0))]
    in_specs += [_const_spec(a.shape) for a in (seg, gng, gnb, wo, g1, b1, wq, keys)]
    out_specs, out_shape = _post_out_specs(t, d, tm)
    return pl.pallas_call(
        functools.partial(_post_odd_kernel, n_prompt_tiles=n_p),
        grid_spec=pltpu.PrefetchScalarGridSpec(
            num_scalar_prefetch=1, grid=(t // tm,), in_specs=in_specs, out_specs=out_specs,
            scratch_shapes=_post_scratch(tm)),
        out_shape=out_shape,
        compiler_params=_cparams(("arbitrary",)),
        name="post_odd",
    )(tab, x, hf[0], hf[1], hb[0], hb[1], zl, yf[0], yf[1], yb[0], yb[1], bonus, g, mods, seg, gng, gnb, wo,
      g1, b1, wq, keys)


N_TOP = PEER_TOPK + 1
TOP_ROWS = 24
SUBLANES = 8


def _batcher_network(n):
    def merge(lo, hi, r):
        step = r * 2
        if step < hi - lo:
            yield from merge(lo, hi, step)
            yield from merge(lo + r, hi, step)
            yield from [(i, i + r) for i in range(lo + r, hi - r, step)]
        else:
            yield (lo, lo + r)

    def sort(lo, hi):
        if hi - lo >= 1:
            mid = lo + (hi - lo) // 2
            yield from sort(lo, mid)
            yield from sort(mid + 1, hi)
            yield from merge(lo, hi, 1)

    return tuple(sort(0, n - 1))


def _sort_levels(levels, net):
    lv = list(levels)
    for i, j in net:
        a, b = lv[i], lv[j]
        lv[i] = jnp.maximum(a, b)
        lv[j] = jnp.minimum(a, b)
    return lv


def _pop_sorted(levels, n_top, emit):
    lv = list(levels)
    for it in range(n_top):
        m = jnp.max(lv[0], axis=0, keepdims=True)
        emit(it, m)
        hit = lv[0] == m
        live = min(len(lv), n_top - it - 1)
        for k in range(live):
            below = lv[k + 1] if k + 1 < len(lv) else -jnp.inf
            lv[k] = jnp.where(hit, below, lv[k])


def _topk_heads(st_ref, th1_ref, p1_ref, p2_ref, s2_ref, v1_ref, v2_ref):
    tl = st_ref.shape[-1]
    neg = -jnp.inf
    net16 = _batcher_network(N_KEYS // SUBLANES)
    net8 = _batcher_network(SUBLANES)
    row = lax.broadcasted_iota(jnp.int32, (SUBLANES, tl), 0)
    roll = lambda x, sh: pltpu.roll(x, sh, 0)

    def top_values(x, store):
        store[...] = jnp.full(store.shape, neg, F32)
        lv = _sort_levels([x[SUBLANES * k:SUBLANES * (k + 1)] for k in range(N_KEYS // SUBLANES)], net16)

        def emit(it, m):
            store[it:it + 1, :] = m

        _pop_sorted(lv, N_TOP, emit)

    def head(h, carry):
        s1 = st_ref[pl.ds(pl.multiple_of(2 * h * N_KEYS, N_KEYS), N_KEYS), :]
        s2 = st_ref[pl.ds(pl.multiple_of((2 * h + 1) * N_KEYS, N_KEYS), N_KEYS), :]
        top_values(s1, v1_ref)
        top_values(s2, v2_ref)
        one = lambda ref, a: ref[a:a + 1, :]
        v2a = v2_ref[0:8, :]
        v1b2 = roll(v1_ref[8:16, :], 2)
        cands = [one(v1_ref, 0) + v2a, one(v1_ref, 0) + v2_ref[8:16, :], one(v1_ref, 0) + v2_ref[16:24, :],
                 one(v1_ref, 1) + v2a,
                 jnp.where(row < 5, one(v1_ref, 2), one(v1_ref, 4)) + jnp.where(row < 5, v2a, roll(v2a, 5)),
                 jnp.where(row < 4, one(v1_ref, 3), jnp.where(row < 6, one(v1_ref, 5), one(v1_ref, 6)))
                 + jnp.where(row < 4, v2a, jnp.where(row < 6, roll(v2a, 4), roll(v2a, 6))),
                 jnp.where(row < 2, one(v1_ref, 7), v1b2) + jnp.where(row < 2, v2a, one(v2_ref, 0)),
                 jnp.where(row < 2, v1b2, roll(v1_ref[16:24, :], 2)) + one(v2_ref, 0)]
        top = []
        _pop_sorted(_sort_levels(cands, net8), N_TOP, lambda it, m: top.append(m))
        tau = 0.5 * (top[PEER_TOPK - 1] + top[PEER_TOPK])
        mx1 = v1_ref[0:1, :]
        mx2 = v2_ref[0:1, :]
        zacc = jnp.zeros((SUBLANES, tl), F32)
        for cnd in cands:
            zacc = zacc + jnp.where(cnd >= tau, jnp.exp(cnd - (mx1 + mx2)), 0.0)
        zsum = jnp.sum(zacc, axis=0, keepdims=True)
        th1 = tau - s1
        p1 = jnp.exp(s1 - mx1) / zsum
        p2 = jnp.exp(s2 - mx2)
        for cb in range(tl // LANES):
            cs = slice(cb * LANES, (cb + 1) * LANES)
            th1_ref[cb, h] = th1[:, cs]
            p1_ref[cb, h] = p1[:, cs]
            p2_ref[cb, h] = p2[:, cs]
            s2_ref[cb, h] = s2[:, cs]
        return carry

    lax.fori_loop(0, PEER_HEADS, head, 0)


def _peer_kernel(tab_ref, h2t_ref, s2_ref, p2_ref, th1a_ref, p1a_ref, th1b_ref, p1b_ref, u_ref, vt_ref, x1_ref, m_ref,
                 g2_ref, b2_ref, o_ref, acc_ref, act_ref, ga_ref):
    s = pl.program_id(1)
    n_steps = pl.num_programs(1)
    n_blk = PEER_TM // LANES
    rows_per_tile = PEER_TE // N_KEYS
    d_rows = acc_ref.shape[0] // n_blk
    u_rows = PEER_TE // n_blk

    def project(half, q):
        ro = pl.multiple_of(q * u_rows, u_rows)
        return _dot(u_ref[pl.ds(half * PEER_TE + ro, u_rows), :], h2t_ref[...])

    def store_act(slot, q, a):
        ro = pl.multiple_of(q * u_rows, u_rows)
        for k in range(n_blk):
            act_ref[slot, k, pl.ds(ro, u_rows), :] = a[:, k * LANES:(k + 1) * LANES]

    def apply_v(slot, half, q):
        ro = pl.multiple_of(q * d_rows, d_rows)
        ga = jnp.concatenate([ga_ref[slot, k] for k in range(n_blk)], axis=1)
        return _dot(vt_ref[pl.ds(ro, d_rows), half * PEER_TE:(half + 1) * PEER_TE], ga)

    def stage(cur, th1_ref, p1_ref):
        nxt = 1 - cur

        def body(cb, carry):
            a_next = project(nxt, cb)
            v_prev = apply_v(nxt, nxt, cb)
            th = [th1_ref[cb, h] for h in range(PEER_HEADS)]
            p1 = [p1_ref[cb, h] for h in range(PEER_HEADS)]
            for rg in range(0, rows_per_tile, ROW_GROUP):
                for part in range(N_KEYS // KEY_CHUNK):
                    ks = slice(part * KEY_CHUNK, (part + 1) * KEY_CHUNK)
                    gates = [None] * ROW_GROUP
                    for h in range(PEER_HEADS):
                        s2 = s2_ref[cb, h, ks, :]
                        p2 = p2_ref[cb, h, ks, :]
                        for g in range(ROW_GROUP):
                            r = rg + g
                            term = jnp.where(s2 >= th[h][r:r + 1], p2, 0.0) * p1[h][r:r + 1]
                            gates[g] = term if gates[g] is None else gates[g] + term
                    for g in range(ROW_GROUP):
                        lo = (rg + g) * N_KEYS + part * KEY_CHUNK
                        rs = slice(lo, lo + KEY_CHUNK)
                        ga_ref[cur, cb, rs, :] = (gates[g] * _gelu(act_ref[cur, cb, rs, :])).astype(BF16)
            store_act(nxt, cb, a_next)
            ro = pl.multiple_of(cb * d_rows, d_rows)
            acc_ref[pl.ds(ro, d_rows), :] += v_prev
            return carry

        lax.fori_loop(0, n_blk, body, 0)

    @pl.when(s == 0)
    def _():
        acc_ref[...] = jnp.zeros(acc_ref.shape, F32)
        ga_ref[...] = jnp.zeros(ga_ref.shape, BF16)
        for q in range(n_blk):
            store_act(0, q, project(0, q))

    @pl.when(s > 0)
    def _():
        stage(1, th1a_ref, p1a_ref)

    @pl.when(s < n_steps - 1)
    def _():
        stage(0, th1b_ref, p1b_ref)

    @pl.when(s == n_steps - 1)
    def _():
        m = m_ref[0]
        for q in range(n_blk):
            acc_ref[q * d_rows:(q + 1) * d_rows, :] += apply_v(1, 1, q)
        ffn = acc_ref[...].T
        o_ref[...] = _ln(ALPHA * x1_ref[...] + m[5:6] * ffn) * g2_ref[...] + b2_ref[...]


def _peer(x1, tab, h2t, s2, p2, th1, p1, u_bf, vt_bf, mods, g2, b2):
    t, d = x1.shape
    tm = PEER_TM
    te = PEER_TE
    n_tiles = u_bf.shape[0] // te
    per = tm // ROW_TILE
    n_blk = tm // LANES
    hk = pl.BlockSpec((n_blk, PEER_HEADS, N_KEYS, LANES), lambda i, j, t_: (i, 0, 0, 0))
    rows = te // N_KEYS
    rows_a = pl.BlockSpec((n_blk, PEER_HEADS, rows, LANES), lambda i, j, t_: (i, 0, jnp.maximum(2 * j - 1, 0), 0))
    rows_b = pl.BlockSpec((n_blk, PEER_HEADS, rows, LANES), lambda i, j, t_: (i, 0, jnp.minimum(2 * j, n_tiles - 1), 0))
    in_specs = [pl.BlockSpec((d, tm), lambda i, j, t_: (0, i)),
                hk, hk, rows_a, rows_a, rows_b, rows_b,
                pl.BlockSpec((2 * te, d), lambda i, j, t_: (jnp.minimum(j, n_tiles // 2 - 1), 0)),
                pl.BlockSpec((d, 2 * te), lambda i, j, t_: (0, jnp.maximum(j - 1, 0))),
                pl.BlockSpec((tm, d), lambda i, j, t_: (i, 0)),
                pl.BlockSpec((1, 6, d), lambda i, j, t_: (t_[0, i * per], 0, 0)),
                pl.BlockSpec(g2.shape, lambda i, j, t_: (0, 0)),
                pl.BlockSpec(b2.shape, lambda i, j, t_: (0, 0))]
    return pl.pallas_call(
        _peer_kernel,
        grid_spec=pltpu.PrefetchScalarGridSpec(
            num_scalar_prefetch=1, grid=(t // tm, n_tiles // 2 + 1), in_specs=in_specs,
            out_specs=pl.BlockSpec((tm, d), lambda i, j, t_: (i, 0)),
            scratch_shapes=[pltpu.VMEM((d, tm), F32), pltpu.VMEM((2, n_blk, te, LANES), F32),
                            pltpu.VMEM((2, n_blk, te, LANES), BF16)]),
        out_shape=jax.ShapeDtypeStruct((t, d), F32),
        compiler_params=_cparams(("arbitrary", "arbitrary")),
        name="peer_mix",
    )(tab, h2t, s2, p2, th1, p1, th1, p1, u_bf, vt_bf, x1, mods, g2, b2)


def _sincos(pos, dim):
    omega = 1.0 / (10000.0 ** (jnp.arange(dim // 2, dtype=F32) / (dim // 2)))
    ang = pos.astype(F32)[:, None] * omega[None, :]
    return jnp.concatenate([jnp.sin(ang), jnp.cos(ang)], -1)


def _grid_pos_embed(n_tok):
    rows = n_tok // GRID_W
    half = D_MODEL // 2
    er = _sincos(jnp.arange(rows), half)
    ec = _sincos(jnp.arange(GRID_W), half)
    emb = jnp.concatenate([jnp.broadcast_to(er[:, None, :], (rows, GRID_W, half)),
                           jnp.broadcast_to(ec[None, :, :], (rows, GRID_W, half))], -1)
    return emb.reshape(rows * GRID_W, D_MODEL)


def _seq_dft_tables(s_len):
    r = 1
    while r * r < s_len:
        r *= 2
    k = jnp.arange(s_len, dtype=jnp.int32)
    w = 2.0 * math.pi / s_len
    ang_a = ((jnp.arange(s_len // r, dtype=jnp.int32)[:, None] * r * k[None, :]) % s_len).astype(F32) * w
    ang_b = ((jnp.arange(r, dtype=jnp.int32)[:, None] * k[None, :]) % s_len).astype(F32) * w
    ca, sa = jnp.cos(ang_a)[:, None, :], jnp.sin(ang_a)[:, None, :]
    cb, sb = jnp.cos(ang_b)[None, :, :], jnp.sin(ang_b)[None, :, :]
    scale = 1.0 / math.sqrt(s_len * FNET_GW)
    cmat = ((ca * cb - sa * sb) * scale).reshape(s_len, s_len)
    smat = ((sa * cb + ca * sb) * (-scale)).reshape(s_len, s_len)
    return cmat.astype(BF16), smat.astype(BF16)


def _channel_dft_table():
    n = np.arange(FNET_GW)
    ang = ((n[:, None] * n[None, :]) % FNET_GW) * (2.0 * np.pi / FNET_GW)
    wc = np.zeros((FNET_W, 2 * FNET_W), np.float32)
    for g in range(FNET_GROUPS):
        sl = slice(g * FNET_GW, (g + 1) * FNET_GW)
        wc[sl, sl] = np.cos(ang)
        wc[sl, FNET_W + g * FNET_GW:FNET_W + (g + 1) * FNET_GW] = np.sin(ang)
    return jnp.asarray(wc, BF16)


def _segment_ones():
    idx = np.arange(WKV_W) // WKV_N
    return jnp.asarray((idx[:, None] == idx[None, :]).astype(np.float32), BF16)


def _chunk_tri():
    i = np.arange(ROW_TILE)
    same = (i[:, None] // CHUNK) == (i[None, :] // CHUNK)
    lower = same & (i[None, :] <= i[:, None])
    upper = same & (i[None, :] >= i[:, None])
    return jnp.asarray(np.stack([lower, upper]).astype(np.float32), BF16)


def _scan_masks():
    i = np.arange(PAIR)
    same = (i[:, None] // CHUNK) == (i[None, :] // CHUNK)
    t_row = i[:, None] % CHUNK
    t_col = i[None, :] % CHUNK
    masks = [same & (t_col < t_row), same & (t_col <= t_row), same & (t_col > t_row), same & (t_col >= t_row),
             np.eye(PAIR, dtype=bool)]
    return jnp.asarray(np.stack(masks).astype(np.float32))


def _block_diag(w):
    h, n, _ = w.shape
    eye = jnp.eye(h, dtype=w.dtype)
    return jnp.einsum("hij,hg->higj", w, eye).reshape(h * n, h * n)


def _pad_rank(w, d):
    r = w.shape[1]
    z = jnp.zeros_like(w[d])
    return jnp.concatenate([w[d], z] if d == 0 else [z, w[d]], axis=0)


def _state_to_pairs(s0):
    b = s0.shape[0]
    st = jnp.swapaxes(s0, -1, -2).reshape(b, 2, N_PAIRS, 2, WKV_N, WKV_N)
    eye = jnp.eye(2, dtype=s0.dtype)
    return jnp.einsum("bdpeji,ef->bdpejfi", st, eye).reshape(b, 2, N_PAIRS, PAIR, PAIR)


def _pairs_to_state(m):
    b = m.shape[0]
    m7 = m.reshape(b, 2, N_PAIRS, 2, WKV_N, 2, WKV_N)
    st = jnp.stack([m7[:, :, :, 0, :, 0, :], m7[:, :, :, 1, :, 1, :]], axis=3)
    return jnp.swapaxes(st.reshape(b, 2, WKV_H, WKV_N, WKV_N), -1, -2)


def _tile_table(groups):
    cv, first, last = [], [], []
    row = 0
    for gi, (n_seq, s_len) in enumerate(groups):
        nt = s_len // ROW_TILE
        for b in range(n_seq):
            for i in range(nt):
                cv.append(0 if gi == 0 else 1 + b)
                first.append(int(i == 0))
                last.append(int(i == nt - 1))
        row += n_seq * s_len
    return jnp.asarray(np.array([cv, first, last], np.int32))


def kernel(x_prompt, x_sample, state_lru, state_wkv, c, c_ctx, w_mod, b_mod, ln1_g, ln1_b, ln2_g, ln2_b,
           w_in_e, w_out_e, sconv_w, sconv_b, w_in_o, w_out_o, lru_conv_w, lru_conv_b, lru_wa, lru_ba,
           lru_wx, lru_bx, lru_lambda, wkv_mu, wkv_w0, wkv_w2, wkv_a0, wkv_a2, wkv_kk, wkv_ka, wkv_rk,
           wkv_g2, wkv_gn_g, wkv_gn_b, peer_wq, peer_keys, peer_u, peer_v):
    bp, sp, d = x_prompt.shape
    bs, ss, _ = x_sample.shape
    depth = w_mod.shape[0]
    assert sp % ROW_TILE == 0 and ss % PEER_TM == 0 and (bp * sp) % ss == 0
    assert bs + 1 <= 8
    groups = ((bp, sp), (bs, ss))
    tp = bp * sp
    tab = _tile_table(groups)

    cv8 = jnp.concatenate([c_ctx[None, :], c, jnp.zeros((8 - 1 - bs, d), F32)], axis=0)
    mods = _modulation(cv8, w_mod, b_mod).reshape(depth, 8, 6, d)

    x = _embed(x_prompt.reshape(tp, d), x_sample.reshape(bs * ss, d), _grid_pos_embed(ss).astype(x_sample.dtype))

    wc = _channel_dft_table()
    dft = {s: _seq_dft_tables(s) for s in sorted({sp, ss})}
    seg = _segment_ones()
    tri = _chunk_tri()
    masks = _scan_masks()
    row2 = lambda a: a.reshape(1, -1)

    lru_fin = []
    wkv_fin = []
    for l in range(depth):
        j = l // 2
        m_l = mods[l]
        wq = peer_wq[l].astype(BF16)
        keys = peer_keys[l].reshape(2 * PEER_HEADS, N_KEYS, PEER_HALF).astype(BF16)
        if l % 2 == 0:
            zr, pq = _inproj(x, tab, m_l, w_in_e[j].astype(BF16), wc)
            y = jnp.concatenate([_seqdft(pq, bp * sp, 0, bp, sp, *dft[sp]),
                                 _seqdft(pq, bs * ss, tp, bs, ss, *dft[ss])], axis=0)
            x1, h2t, th1, p1, p2, s2 = _post_even(x, tab, zr, y, m_l, sconv_w[j], row2(sconv_b[j]), w_out_e[j].astype(BF16),
                                     row2(ln1_g[l]), row2(ln1_b[l]), wq, keys)
        else:
            zl, zw = _inproj(x, tab, m_l, w_in_o[j].astype(BF16))
            lru_args = (lru_conv_w[j], lru_conv_b[j],
                        jnp.stack([_block_diag(lru_wa[j, dd]) for dd in range(2)]).astype(BF16), lru_ba[j],
                        jnp.stack([_block_diag(lru_wx[j, dd]) for dd in range(2)]).astype(BF16), lru_bx[j],
                        lru_lambda[j])
            hf_p, hb_p, hfin_p = _lru(zl, 0, bp, sp, jnp.zeros((bp, 2, LRU_W), F32), *lru_args)
            hf_s, hb_s, _ = _lru(zl, tp, bs, ss, state_lru[:, j], *lru_args)
            lru_fin.append(hfin_p)
            prep = _wkvprep(zw, tab, row2(wkv_mu[j]), wkv_w0[j],
                            jnp.stack([_pad_rank(wkv_w2[j], dd) for dd in range(2)]).astype(BF16), wkv_a0[j],
                            jnp.stack([_pad_rank(wkv_a2[j], dd) for dd in range(2)]).astype(BF16),
                            wkv_kk[j], wkv_ka[j], row2(wkv_rk[j]), wkv_g2[j].astype(BF16), seg, tri)
            v, bonus, g = prep[0], prep[1], prep[2]
            yf_p, yb_p, mfin_p = _wkvscan(prep[3:], v, 0, bp, sp,
                                          jnp.zeros((bp, 2, N_PAIRS, PAIR, PAIR), F32), masks)
            yf_s, yb_s, _ = _wkvscan(prep[3:], v, tp, bs, ss, _state_to_pairs(state_wkv[:, j]), masks)
            wkv_fin.append(_pairs_to_state(mfin_p))
            x1, h2t, th1, p1, p2, s2 = _post_odd(x, tab, (hf_p, hf_s), (hb_p, hb_s), zl, (yf_p, yf_s), (yb_p, yb_s),
                                    bonus, g, m_l, seg, row2(wkv_gn_g[j]), row2(wkv_gn_b[j]),
                                    w_out_o[j].astype(BF16), row2(ln1_g[l]), row2(ln1_b[l]), wq, keys)
        x = _peer(x1, tab, h2t, s2, p2, th1, p1, peer_u[l].astype(BF16), peer_v[l].T.astype(BF16), m_l,
                  row2(ln2_g[l]), row2(ln2_b[l]))

    y_prompt = x[:tp].reshape(bp, sp, d)
    y_sample = x[tp:].reshape(bs, ss, d)
    return (y_prompt, y_sample, jnp.stack(lru_fin, 1).astype(x_prompt.dtype),
            jnp.stack(wkv_fin, 1).astype(x_prompt.dtype))
```

```python
import functools
import math

import numpy as np
import jax
import jax.numpy as jnp
from jax import lax
from jax.experimental import pallas as pl
from jax.experimental.pallas import tpu as pltpu

F32 = jnp.float32
BF16 = jnp.bfloat16

D_MODEL = 1024
GRID_W = 64
FNET_W = 512
FNET_GROUPS = 4
FNET_GW = FNET_W // FNET_GROUPS
CONV_W = 512
LRU_W = 512
LRU_HEADS = 8
LRU_CONV_K = 4
LRU_C = 8.0
WKV_W = 512
WKV_N = 64
WKV_H = 8
WKV_IN = 1920
DECAY_SCALE = math.exp(-0.5)
WKV_GN_EPS = 64e-5
PEER_HEADS = 8
N_KEYS = 128
PEER_TOPK = 16
PEER_HALF = 128
DEPTH = 4
ALPHA = (2 * DEPTH) ** 0.25
LN_EPS = 1e-6

ROW_TILE = 256
LANES = 128
HALO = 8
CHUNK = 64
PAIR = 2 * WKV_N
N_PAIRS = WKV_H // 2
PEER_TM = 512
PEER_TE = 1024
ROW_GROUP = 4
KEY_CHUNK = 32
BLK_PER_ITER = 2
VMEM_LIMIT = 56 * 1024 * 1024


def _cparams(sem):
    return pltpu.CompilerParams(dimension_semantics=sem, vmem_limit_bytes=VMEM_LIMIT)


def _dot(a, b):
    return jnp.dot(a, b, preferred_element_type=F32)


def _dot_nt(a, b):
    return lax.dot_general(a, b, (((1,), (1,)), ((), ())), preferred_element_type=F32)


def _mm3(a, b):
    ah = a.astype(BF16)
    al = (a - ah.astype(F32)).astype(BF16)
    bh = b.astype(BF16)
    bl = (b - bh.astype(F32)).astype(BF16)
    return _dot(ah, bh) + (_dot(ah, bl) + _dot(al, bh))


def _mm1(a, b):
    return _dot(a.astype(BF16), b.astype(BF16))


def _mm01(a, b01):
    h = a.astype(BF16)
    r = a - h.astype(F32)
    m = r.astype(BF16)
    lo = (r - m.astype(F32)).astype(BF16)
    return _dot(h, b01) + (_dot(m, b01) + _dot(lo, b01))


def _ln(x):
    mu = jnp.mean(x, axis=-1, keepdims=True)
    xc = x - mu
    var = jnp.mean(xc * xc, axis=-1, keepdims=True)
    return xc * lax.rsqrt(var + LN_EPS)


def _gelu(x):
    z = x * (0.7978845608028654 + 0.035677408136300125 * (x * x))
    hx = 0.5 * x
    return hx + hx * jnp.tanh(z)


def _sigmoid(x):
    return 1.0 / (1.0 + jnp.exp(-x))


def _mod_kernel(c_ref, w_ref, b_ref, o_ref):
    c = c_ref[...]
    sc = c * _sigmoid(c)
    o_ref[0] = _dot(sc.astype(BF16), w_ref[0].astype(BF16)) + b_ref[0]


def _modulation(cv8, w_mod, b_mod):
    depth, d, n = w_mod.shape
    tn = 1536
    return pl.pallas_call(
        _mod_kernel,
        grid=(depth, n // tn),
        in_specs=[pl.BlockSpec((8, d), lambda l, j: (0, 0)),
                  pl.BlockSpec((1, d, tn), lambda l, j: (l, 0, j)),
                  pl.BlockSpec((1, 1, tn), lambda l, j: (l, 0, j))],
        out_specs=pl.BlockSpec((1, 8, tn), lambda l, j: (l, 0, j)),
        out_shape=jax.ShapeDtypeStruct((depth, 8, n), F32),
        compiler_params=_cparams(("arbitrary", "arbitrary")),
        name="modulation",
    )(cv8, w_mod, b_mod.reshape(depth, 1, n))


def _embed_kernel(xp_ref, xs_ref, p_ref, o_ref, *, n_prompt):
    i = pl.program_id(0)

    @pl.when(i < n_prompt)
    def _():
        o_ref[...] = xp_ref[...]

    @pl.when(i >= n_prompt)
    def _():
        o_ref[...] = xs_ref[...] + p_ref[...]


def _embed(xp, xs, pos):
    tp, d = xp.shape
    ts = xs.shape[0]
    s_len = pos.shape[0]
    tm = 512
    n_prompt = tp // tm
    per_seq = s_len // tm
    return pl.pallas_call(
        functools.partial(_embed_kernel, n_prompt=n_prompt),
        grid=((tp + ts) // tm,),
        in_specs=[pl.BlockSpec((tm, d), lambda i: (jnp.minimum(i, n_prompt - 1), 0)),
                  pl.BlockSpec((tm, d), lambda i: (jnp.maximum(i - n_prompt, 0), 0)),
                  pl.BlockSpec((tm, d), lambda i: (jnp.maximum(i - n_prompt, 0) % per_seq, 0))],
        out_specs=pl.BlockSpec((tm, d), lambda i: (i, 0)),
        out_shape=jax.ShapeDtypeStruct((tp + ts, d), xp.dtype),
        compiler_params=_cparams(("arbitrary",)),
        name="embed",
    )(xp, xs, pos)


def _inproj_even_kernel(tab_ref, x_ref, m_ref, w_ref, wc_ref, zr_ref, pq_ref):
    m = m_ref[0]
    h = _ln(x_ref[...]) * (1.0 + m[1:2]) + m[0:1]
    z = _dot(h.astype(BF16), w_ref[...])
    zr_ref[...] = z[:, FNET_W:]
    pq_ref[...] = _dot(z[:, :FNET_W].astype(BF16), wc_ref[...]).astype(BF16)


def _inproj_odd_kernel(tab_ref, x_ref, m_ref, w_ref, zl_ref, zw_ref):
    m = m_ref[0]
    h = _ln(x_ref[...]) * (1.0 + m[1:2]) + m[0:1]
    z = _dot(h.astype(BF16), w_ref[...])
    zl_ref[...] = z[:, :2 * LRU_W]
    zw_ref[...] = z[:, 2 * LRU_W:]


def _const_spec(shape):
    nd = len(shape)
    return pl.BlockSpec(shape, lambda i, t, _n=nd: (0,) * _n)


def _inproj(x, tab, mods, w_in, wc=None):
    t, d = x.shape
    n = w_in.shape[1]
    tm = ROW_TILE
    even = wc is not None
    in_specs = [pl.BlockSpec((tm, d), lambda i, t_: (i, 0)),
                pl.BlockSpec((1, 6, d), lambda i, t_: (t_[0, i], 0, 0)),
                _const_spec((d, n))]
    if even:
        in_specs.append(_const_spec(wc.shape))
        widths = (n - FNET_W, 2 * FNET_W)
        dtypes = (F32, BF16)
        kern = _inproj_even_kernel
        args = (tab, x, mods, w_in, wc)
    else:
        widths = (2 * LRU_W, n - 2 * LRU_W)
        dtypes = (F32, F32)
        kern = _inproj_odd_kernel
        args = (tab, x, mods, w_in)
    return pl.pallas_call(
        kern,
        grid_spec=pltpu.PrefetchScalarGridSpec(
            num_scalar_prefetch=1, grid=(t // tm,), in_specs=in_specs,
            out_specs=[pl.BlockSpec((tm, w), lambda i, t_: (i, 0)) for w in widths]),
        out_shape=[jax.ShapeDtypeStruct((t, w), dt) for w, dt in zip(widths, dtypes)],
        compiler_params=_cparams(("arbitrary",)),
        name="inproj_even" if even else "inproj_odd",
    )(*args)


def _seqdft_kernel(c_ref, s_ref, p_ref, q_ref, y_ref):
    y_ref[...] = (_dot(c_ref[...], p_ref[...]) + _dot(s_ref[...], q_ref[...])).astype(BF16)


def _seqdft(pq, y_rows, row0, n_seq, s_len, cmat, smat):
    tm = min(s_len, ROW_TILE)
    nt = s_len // tm
    blk0 = row0 // s_len
    return pl.pallas_call(
        _seqdft_kernel,
        grid=(n_seq, nt),
        in_specs=[pl.BlockSpec((tm, s_len), lambda b, i: (i, 0)),
                  pl.BlockSpec((tm, s_len), lambda b, i: (i, 0)),
                  pl.BlockSpec((s_len, FNET_W), lambda b, i: (blk0 + b, 0)),
                  pl.BlockSpec((s_len, FNET_W), lambda b, i: (blk0 + b, 1))],
        out_specs=pl.BlockSpec((tm, FNET_W), lambda b, i: (b * nt + i, 0)),
        out_shape=jax.ShapeDtypeStruct((y_rows, FNET_W), BF16),
        compiler_params=_cparams(("arbitrary", "arbitrary")),
        name="seqdft_%d" % s_len,
    )(cmat, smat, pq, pq)


def _post_and_query(x, mix, m, g1, b1, wq_ref, keys_ref, x1_ref, h2t_ref, topk_refs, st_ref, v1_ref, v2_ref):
    x1 = _ln(ALPHA * x + m[2:3] * mix) * g1 + b1
    x1_ref[...] = x1
    h2 = _ln(x1) * (1.0 + m[4:5]) + m[3:4]
    h2b = h2.astype(BF16)
    h2t_ref[...] = h2.T.astype(BF16)
    q = _dot(h2b, wq_ref[...])
    for hp in range(2 * PEER_HEADS):
        qb = q[:, hp * PEER_HALF:(hp + 1) * PEER_HALF].astype(BF16)
        st_ref[hp * N_KEYS:(hp + 1) * N_KEYS, :] = _dot_nt(keys_ref[hp], qb)
    _topk_heads(st_ref, *topk_refs, v1_ref, v2_ref)


def _post_even_kernel(tab_ref, x_ref, bg_ref, cg_ref, xi_ref, cgp_ref, xip_ref, cgn_ref, xin_ref, y_ref,
                      m_ref, cw_ref, cb_ref, wo_ref, g1_ref, b1_ref, wq_ref, keys_ref,
                      x1_ref, h2t_ref, th1_ref, p1_ref, p2_ref, s2_ref, ext_ref, st_ref, v1_ref, v2_ref):
    i = pl.program_id(0)
    tm = ROW_TILE
    first = tab_ref[1, i] == 1
    last = tab_ref[2, i] == 1
    u = cg_ref[...] * xi_ref[...]
    ext_ref[0:HALO] = jnp.where(first, 0.0, cgp_ref[...] * xip_ref[...])
    ext_ref[HALO:HALO + tm] = u
    ext_ref[HALO + tm:2 * HALO + tm] = jnp.where(last, 0.0, cgn_ref[...] * xin_ref[...])
    cw = cw_ref[...]
    conv = (ext_ref[HALO - 1:HALO - 1 + tm] * cw[0:1] + u * cw[1:2]
            + ext_ref[HALO + 1:HALO + 1 + tm] * cw[2:3] + cb_ref[...])
    ymix = (bg_ref[...] * conv).astype(BF16)
    mix = _dot(y_ref[...], wo_ref[0:FNET_W]) + _dot(ymix, wo_ref[FNET_W:])
    _post_and_query(x_ref[...], mix, m_ref[0], g1_ref[...], b1_ref[...], wq_ref, keys_ref,
                    x1_ref, h2t_ref, (th1_ref, p1_ref, p2_ref, s2_ref), st_ref, v1_ref, v2_ref)


def _post_out_specs(t, d, tm):
    rshape = jax.ShapeDtypeStruct((t // LANES, PEER_HEADS, N_KEYS, LANES), F32)
    rspec = pl.BlockSpec((tm // LANES, PEER_HEADS, N_KEYS, LANES), lambda i, t_: (i, 0, 0, 0))
    specs = [pl.BlockSpec((tm, d), lambda i, t_: (i, 0)),
             pl.BlockSpec((d, tm), lambda i, t_: (0, i))] + [rspec] * 4
    shapes = [jax.ShapeDtypeStruct((t, d), F32),
              jax.ShapeDtypeStruct((d, t), BF16)] + [rshape] * 4
    return specs, shapes


def _post_scratch(tm):
    return [pltpu.VMEM((2 * PEER_HEADS * N_KEYS, tm), F32), pltpu.VMEM((TOP_ROWS, tm), F32),
            pltpu.VMEM((TOP_ROWS, tm), F32)]


def _post_even(x, tab, zr, y, mods, cw, cb, wo, g1, b1, wq, keys):
    t, d = x.shape
    tm = ROW_TILE
    hb = tm // HALO
    nblk = t // HALO
    prev = lambda c: pl.BlockSpec((HALO, CONV_W), lambda i, t_, _c=c: (jnp.maximum(i * hb - 1, 0), _c))
    nxt = lambda c: pl.BlockSpec((HALO, CONV_W), lambda i, t_, _c=c: (jnp.minimum((i + 1) * hb, nblk - 1), _c))
    col = lambda c: pl.BlockSpec((tm, CONV_W), lambda i, t_, _c=c: (i, _c))
    in_specs = [pl.BlockSpec((tm, d), lambda i, t_: (i, 0)),
                col(0), col(1), col(2), prev(1), prev(2), nxt(1), nxt(2),
                pl.BlockSpec((tm, FNET_W), lambda i, t_: (i, 0)),
                pl.BlockSpec((1, 6, d), lambda i, t_: (t_[0, i], 0, 0)),
                _const_spec(cw.shape), _const_spec(cb.shape), _const_spec(wo.shape),
                _const_spec(g1.shape), _const_spec(b1.shape), _const_spec(wq.shape), _const_spec(keys.shape)]
    out_specs, out_shape = _post_out_specs(t, d, tm)
    return pl.pallas_call(
        _post_even_kernel,
        grid_spec=pltpu.PrefetchScalarGridSpec(
            num_scalar_prefetch=1, grid=(t // tm,), in_specs=in_specs, out_specs=out_specs,
            scratch_shapes=[pltpu.VMEM((tm + 2 * HALO, CONV_W), F32)] + _post_scratch(tm)),
        out_shape=out_shape,
        compiler_params=_cparams(("arbitrary",)),
        name="post_even",
    )(tab, x, zr, zr, zr, zr, zr, zr, zr, y, mods, cw, cb, wo, g1, b1, wq, keys)


def _lru_kernel(xf_ref, xfh_ref, xb_ref, xbh_ref, cw_ref, cb_ref, wa_ref, ba_ref, wx_ref, bx_ref, lam_ref,
                h0_ref, hf_ref, hb_ref, hfin_ref, ext_ref, carry_ref):
    i = pl.program_id(1)
    nt = pl.num_programs(1)
    tm = ROW_TILE

    @pl.when(i == 0)
    def _():
        carry_ref[...] = h0_ref[0]

    row = lax.broadcasted_iota(jnp.int32, (tm, 1), 0)
    for d in range(2):
        x = (xf_ref if d == 0 else xb_ref)[...]
        halo = jnp.where(i == 0, 0.0, (xfh_ref if d == 0 else xbh_ref)[...])
        ext_ref[HALO:HALO + tm] = x
        if d == 0:
            ext_ref[0:HALO] = halo
        else:
            ext_ref[HALO + tm:2 * HALO + tm] = halo
        cw = cw_ref[d]
        xc = cb_ref[d:d + 1] + x * cw[LRU_CONV_K - 1:LRU_CONV_K]
        for j in range(LRU_CONV_K - 1):
            k = LRU_CONV_K - 1 - j
            off = HALO - k if d == 0 else HALO + k
            xc = xc + ext_ref[off:off + tm] * cw[j:j + 1]
        xcb = xc.astype(BF16)
        gate_r = _sigmoid(_dot(xcb, wa_ref[d]) + ba_ref[d:d + 1])
        gate_i = _sigmoid(_dot(xcb, wx_ref[d]) + bx_ref[d:d + 1])
        nl = -lam_ref[d:d + 1]
        softplus = jnp.maximum(nl, 0.0) + jnp.log1p(jnp.exp(-jnp.abs(nl)))
        log_a = -LRU_C * gate_r * softplus
        a = jnp.exp(log_a)
        b = jnp.sqrt(-jnp.tanh(log_a) * (a * a + 1.0)) * (gate_i * xc)
        s = 1
        while s < tm:
            if d == 0:
                keep = row >= s
                sh = s
            else:
                keep = row < tm - s
                sh = tm - s
            a_sh = jnp.where(keep, pltpu.roll(a, sh, 0), 1.0)
            b_sh = jnp.where(keep, pltpu.roll(b, sh, 0), 0.0)
            b = a * b_sh + b
            a = a * a_sh
            s *= 2
        h = a * carry_ref[d:d + 1] + b
        if d == 0:
            hf_ref[...] = h
            carry_ref[0:1] = h[tm - 1:tm]
        else:
            hb_ref[...] = h
            carry_ref[1:2] = h[0:1]

    @pl.when(i == nt - 1)
    def _():
        hfin_ref[0] = carry_ref[...]


def _lru(zl, row0, n_seq, s_len, h0, cw, cb, wa, ba, wx, bx, lam):
    tm = ROW_TILE
    nt = s_len // tm
    t0 = row0 // tm
    hb = tm // HALO
    nblk = zl.shape[0] // HALO
    rows = n_seq * s_len
    fwd = lambda b, i: (t0 + b * nt + i, 0)
    bwd = lambda b, i: (t0 + b * nt + nt - 1 - i, 0)
    fwd_h = lambda b, i: (jnp.maximum((t0 + b * nt + i) * hb - 1, 0), 0)
    bwd_h = lambda b, i: (jnp.minimum((t0 + b * nt + nt - i) * hb, nblk - 1), 0)
    out_f = lambda b, i: (b * nt + i, 0)
    out_b = lambda b, i: (b * nt + nt - 1 - i, 0)
    cst = lambda a: pl.BlockSpec(a.shape, lambda b, i, _n=a.ndim: (0,) * _n)
    return pl.pallas_call(
        _lru_kernel,
        grid=(n_seq, nt),
        in_specs=[pl.BlockSpec((tm, LRU_W), fwd), pl.BlockSpec((HALO, LRU_W), fwd_h),
                  pl.BlockSpec((tm, LRU_W), bwd), pl.BlockSpec((HALO, LRU_W), bwd_h),
                  cst(cw), cst(cb), cst(wa), cst(ba), cst(wx), cst(bx), cst(lam),
                  pl.BlockSpec((1, 2, LRU_W), lambda b, i: (b, 0, 0))],
        out_specs=[pl.BlockSpec((tm, LRU_W), out_f), pl.BlockSpec((tm, LRU_W), out_b),
                   pl.BlockSpec((1, 2, LRU_W), lambda b, i: (b, 0, 0))],
        out_shape=[jax.ShapeDtypeStruct((rows, LRU_W), F32), jax.ShapeDtypeStruct((rows, LRU_W), F32),
                   jax.ShapeDtypeStruct((n_seq, 2, LRU_W), F32)],
        scratch_shapes=[pltpu.VMEM((tm + 2 * HALO, LRU_W), F32), pltpu.VMEM((2, LRU_W), F32)],
        compiler_params=_cparams(("arbitrary", "arbitrary")),
        name="lru_%d" % s_len,
    )(zl, zl, zl, zl, cw, cb, wa, ba, wx, bx, lam, h0)


def _wkvprep_kernel(tab_ref, z_ref, zp_ref, zn_ref, mu_ref, w0_ref, w2_ref, a0_ref, a2_ref, kk_ref, ka_ref,
                    rk_ref, g2_ref, seg_ref, tri_ref,
                    v_ref, bg_ref, g_ref, rf_ref, kf_ref, bf_ref, qf_ref, cf_ref,
                    rb_ref, kb_ref, bb_ref, qb_ref, cb_ref, ext_ref):
    i = pl.program_id(0)
    tm = ROW_TILE
    w = WKV_W
    first = tab_ref[1, i] == 1
    last = tab_ref[2, i] == 1
    z = z_ref[...]
    ext_ref[0:HALO] = jnp.where(first, 0.0, zp_ref[...])
    ext_ref[HALO:HALO + tm] = z
    ext_ref[HALO + tm:2 * HALO + tm] = jnp.where(last, 0.0, zn_ref[...])
    z = z + mu_ref[...] * (0.5 * (ext_ref[HALO - 1:HALO - 1 + tm] + ext_ref[HALO + 1:HALO + 1 + tm]) - z)
    r = z[:, 0:w]
    k = z[:, w:2 * w]
    v = z[:, 2 * w:3 * w]
    wd = jnp.tanh(z[:, 3 * w:3 * w + 128]).astype(BF16)
    ad = z[:, 3 * w + 128:3 * w + 256].astype(BF16)
    gd = _sigmoid(z[:, 3 * w + 256:3 * w + 384]).astype(BF16)
    v_ref[...] = v
    g_ref[...] = _dot(gd, g2_ref[...])
    seg = seg_ref[...]
    rk = rk_ref[...]
    bonus = jnp.zeros((tm, w), F32)
    outs = ((rf_ref, kf_ref, bf_ref, qf_ref, cf_ref), (rb_ref, kb_ref, bb_ref, qb_ref, cb_ref))
    nch = tm // CHUNK
    for d in range(2):
        r_ref, k_ref, b_ref, q_ref, c_ref = outs[d]
        wz = w0_ref[d:d + 1] + _dot(wd, w2_ref[d])
        logw = -DECAY_SCALE * _sigmoid(wz)
        iclr = _sigmoid(a0_ref[d:d + 1] + _dot(ad, a2_ref[d]))
        kk = k * kk_ref[d:d + 1]
        kk = kk * lax.rsqrt(jnp.maximum(_mm01(kk * kk, seg), 1e-24))
        km = k * (1.0 + (iclr - 1.0) * ka_ref[d:d + 1])
        bonus = bonus + _mm1(r * km * rk, seg) * v
        lw_h = logw.astype(BF16)
        lw_r = logw - lw_h.astype(F32)
        lw_m = lw_r.astype(BF16)
        lw_l = (lw_r - lw_m.astype(F32)).astype(BF16)
        tri = tri_ref[d]
        cl = _dot(tri, lw_h) + (_dot(tri, lw_m) + _dot(tri, lw_l))
        c = jnp.exp(cl)
        cinv = jnp.exp(-cl)
        r_ref[...] = r * c
        k_ref[...] = km * cinv
        b_ref[...] = kk * iclr * cinv
        q_ref[...] = kk * jnp.exp(cl - logw)
        for j in range(nch):
            edge = (j + 1) * CHUNK - 1 if d == 0 else j * CHUNK
            c_ref[j] = c[edge:edge + 1]
    bg_ref[...] = bonus


def _wkvprep(zw, tab, mu, w0, w2p, a0, a2p, kk, ka, rk, g2, seg, tri):
    t, n = zw.shape
    tm = ROW_TILE
    hb = tm // HALO
    nblk = t // HALO
    nch = tm // CHUNK
    w = WKV_W
    in_specs = [pl.BlockSpec((tm, n), lambda i, t_: (i, 0)),
                pl.BlockSpec((HALO, n), lambda i, t_: (jnp.maximum(i * hb - 1, 0), 0)),
                pl.BlockSpec((HALO, n), lambda i, t_: (jnp.minimum((i + 1) * hb, nblk - 1), 0))]
    in_specs += [_const_spec(a.shape) for a in (mu, w0, w2p, a0, a2p, kk, ka, rk, g2, seg, tri)]
    row = pl.BlockSpec((tm, w), lambda i, t_: (i, 0))
    cspec = pl.BlockSpec((nch, 1, w), lambda i, t_: (i, 0, 0))
    rshape = jax.ShapeDtypeStruct((t, w), F32)
    cshape = jax.ShapeDtypeStruct((t // CHUNK, 1, w), F32)
    out_specs = [row, row, row] + [row, row, row, row, cspec] * 2
    out_shape = [rshape, rshape, rshape] + [rshape, rshape, rshape, rshape, cshape] * 2
    return pl.pallas_call(
        _wkvprep_kernel,
        grid_spec=pltpu.PrefetchScalarGridSpec(
            num_scalar_prefetch=1, grid=(t // tm,), in_specs=in_specs, out_specs=out_specs,
            scratch_shapes=[pltpu.VMEM((tm + 2 * HALO, n), F32)]),
        out_shape=out_shape,
        compiler_params=_cparams(("arbitrary",)),
        name="wkv_prep",
    )(tab, zw, zw, zw, mu, w0, w2p, a0, a2p, kk, ka, rk, g2, seg, tri)


def _wkv_chunks(items, eye):
    lane = lax.broadcasted_iota(jnp.int32, (1, PAIR), 1)
    m0 = (lane < WKV_N).astype(F32)
    m1 = 1.0 - m0
    stack = lambda x: jnp.concatenate([x * m0, x * m1], axis=0)
    n = range(len(items))
    rh_s = [stack(it[0]) for it in items]
    kh_s = [stack(it[1]) for it in items]
    bh_s = [stack(it[2]) for it in items]
    kq_s = [stack(it[3]) for it in items]
    v_s = [stack(it[4]) for it in items]
    bh_t = [x.T for x in bh_s]
    kh_t = [x.T for x in kh_s]
    bk_t = [jnp.concatenate([bh_t[i], kh_t[i]], axis=1) for i in n]
    att_s = [_mm3(kq_s[i], bk_t[i]) for i in n]
    att_y = [_mm1(rh_s[i], bk_t[i]) for i in n]
    n1 = [att_s[i][:, 0:PAIR] * items[i][7] for i in n]
    ak = [att_s[i][:, PAIR:] * items[i][7] for i in n]
    gb = [att_y[i][:, 0:PAIR] * items[i][8] for i in n]
    gk = [att_y[i][:, PAIR:] * items[i][8] for i in n]
    t_inv = [eye - n1[i] for i in n]
    npow = n1
    for _ in range(5):
        npow = [_mm1(npow[i], npow[i]) for i in n]
        t_inv = [t_inv[i] + _mm1(t_inv[i], npow[i]) for i in n]
    kv = [_mm1(jnp.concatenate([kh_t[i], gk[i], ak[i]], axis=0), v_s[i]) for i in n]
    x = [_mm1(t_inv[i], jnp.concatenate([kq_s[i], kv[i][2 * PAIR:]], axis=1)) for i in n]
    bx = [_mm1(jnp.concatenate([bh_t[i], gb[i]], axis=0), x[i]) for i in n]
    p_mat = [eye - bx[i][0:PAIR, 0:PAIR] for i in n]
    q_mat = [kv[i][0:PAIR] - bx[i][0:PAIR, PAIR:] for i in n]
    r_til = [rh_s[i] - bx[i][PAIR:, 0:PAIR] for i in n]
    y0 = [kv[i][PAIR:2 * PAIR] - bx[i][PAIR:, PAIR:] for i in n]
    pm = [_mm3(p_mat[i], items[i][6]) for i in n]
    y_st = [_mm1(r_til[i], items[i][6]) + y0[i] for i in n]
    ys = [y_st[i][0:CHUNK] + y_st[i][CHUNK:] for i in n]
    c_col = [jnp.broadcast_to(items[i][5], (PAIR, PAIR)).T for i in n]
    m_new = [(pm[i] + q_mat[i]) * c_col[i] for i in n]
    return ys, m_new


def _wkvscan_kernel(rf_ref, kf_ref, bf_ref, qf_ref, vf_ref, cf_ref, rb_ref, kb_ref, bb_ref, qb_ref, vb_ref, cb_ref,
                    m0_ref, msk_ref, yf_ref, yb_ref, mfin_ref, m_ref):
    i = pl.program_id(1)
    nc = pl.num_programs(1)

    @pl.when(i == 0)
    def _():
        m_ref[...] = m0_ref[0]

    eye = msk_ref[4]
    ins = ((rf_ref, kf_ref, bf_ref, qf_ref, vf_ref, cf_ref, yf_ref),
           (rb_ref, kb_ref, bb_ref, qb_ref, vb_ref, cb_ref, yb_ref))
    items = []
    for d in range(2):
        r_ref, k_ref, b_ref, q_ref, v_ref, c_ref, _ = ins[d]
        for p in range(N_PAIRS):
            sl = slice(p * PAIR, (p + 1) * PAIR)
            items.append((r_ref[:, sl], k_ref[:, sl], b_ref[:, sl], q_ref[:, sl], v_ref[:, sl],
                          c_ref[0, :, sl], m_ref[d, p], msk_ref[2 * d], msk_ref[2 * d + 1]))
    ys, m_new = _wkv_chunks(items, eye)
    for d in range(2):
        for p in range(N_PAIRS):
            sl = slice(p * PAIR, (p + 1) * PAIR)
            ins[d][6][:, sl] = ys[d * N_PAIRS + p]
            m_ref[d, p] = m_new[d * N_PAIRS + p]

    @pl.when(i == nc - 1)
    def _():
        mfin_ref[0] = m_ref[...]


def _wkvscan(prep, v, row0, n_seq, s_len, m0, masks):
    rf, kf, bf, qf, cf, rb, kb, bb, qb, cb = prep
    nc = s_len // CHUNK
    c0 = row0 // CHUNK
    rows = n_seq * s_len
    w = WKV_W
    fwd = lambda b, i: (c0 + b * nc + i, 0)
    bwd = lambda b, i: (c0 + b * nc + nc - 1 - i, 0)
    fwd3 = lambda b, i: (c0 + b * nc + i, 0, 0)
    bwd3 = lambda b, i: (c0 + b * nc + nc - 1 - i, 0, 0)
    blk = lambda im: pl.BlockSpec((CHUNK, w), im)
    cblk = lambda im: pl.BlockSpec((1, 1, w), im)
    mspec = pl.BlockSpec((1, 2, N_PAIRS, PAIR, PAIR), lambda b, i: (b, 0, 0, 0, 0))
    return pl.pallas_call(
        _wkvscan_kernel,
        grid=(n_seq, nc),
        in_specs=[blk(fwd)] * 5 + [cblk(fwd3)] + [blk(bwd)] * 5 + [cblk(bwd3)]
        + [mspec, pl.BlockSpec(masks.shape, lambda b, i: (0, 0, 0))],
        out_specs=[pl.BlockSpec((CHUNK, w), lambda b, i: (b * nc + i, 0)),
                   pl.BlockSpec((CHUNK, w), lambda b, i: (b * nc + nc - 1 - i, 0)),
                   mspec],
        out_shape=[jax.ShapeDtypeStruct((rows, w), F32), jax.ShapeDtypeStruct((rows, w), F32),
                   jax.ShapeDtypeStruct(m0.shape, F32)],
        scratch_shapes=[pltpu.VMEM((2, N_PAIRS, PAIR, PAIR), F32)],
        compiler_params=_cparams(("arbitrary", "arbitrary")),
        name="wkv_scan_%d" % s_len,
    )(rf, kf, bf, qf, v, cf, rb, kb, bb, qb, v, cb, m0, masks)


def _post_odd_kernel(tab_ref, x_ref, hfp_ref, hfs_ref, hbp_ref, hbs_ref, gb_ref, yfp_ref, yfs_ref, ybp_ref, ybs_ref,
                     bon_ref, g_ref, m_ref, seg_ref, gng_ref, gnb_ref, wo_ref, g1_ref, b1_ref, wq_ref, keys_ref,
                     x1_ref, h2t_ref, th1_ref, p1_ref, p2_ref, s2_ref, st_ref, v1_ref, v2_ref, *, n_prompt_tiles):
    is_prompt = pl.program_id(0) < n_prompt_tiles
    pick = lambda p_ref, s_ref: jnp.where(is_prompt, p_ref[...], s_ref[...])
    y_lru = ((pick(hfp_ref, hfs_ref) + pick(hbp_ref, hbs_ref)) * _gelu(gb_ref[...])).astype(BF16)
    seg = seg_ref[...]
    ys = pick(yfp_ref, yfs_ref) + pick(ybp_ref, ybs_ref)
    mean = _mm1(ys, seg) * (1.0 / WKV_N)
    yc = ys - mean
    var = _mm1(yc * yc, seg) * (1.0 / WKV_N)
    yn = yc * lax.rsqrt(var + WKV_GN_EPS) * gng_ref[...] + gnb_ref[...]
    y_wkv = ((yn + bon_ref[...]) * g_ref[...]).astype(BF16)
    mix = _dot(y_lru, wo_ref[0:LRU_W]) + _dot(y_wkv, wo_ref[LRU_W:])
    _post_and_query(x_ref[...], mix, m_ref[0], g1_ref[...], b1_ref[...], wq_ref, keys_ref,
                    x1_ref, h2t_ref, (th1_ref, p1_ref, p2_ref, s2_ref), st_ref, v1_ref, v2_ref)


def _post_odd(x, tab, hf, hb, zl, yf, yb, bonus, g, mods, seg, gng, gnb, wo, g1, b1, wq, keys):
    t, d = x.shape
    tm = ROW_TILE
    w = WKV_W
    n_p = hf[0].shape[0] // tm
    n_s = hf[1].shape[0] // tm
    row = pl.BlockSpec((tm, w), lambda i, t_: (i, 0))
    row_p = pl.BlockSpec((tm, w), lambda i, t_: (jnp.minimum(i, n_p - 1), 0))
    row_s = pl.BlockSpec((tm, w), lambda i, t_: (jnp.clip(i - n_p, 0, n_s - 1), 0))
    in_specs = [pl.BlockSpec((tm, d), lambda i, t_: (i, 0)), row_p, row_s, row_p, row_s,
                pl.BlockSpec((tm, LRU_W), lambda i, t_: (i, 1)), row_p, row_s, row_p, row_s, row, row,
                pl.BlockSpec((1, 6, d), lambda i, t_: (t_[0, i], 0, 0))]
    in_specs += [_const_spec(a.shape) for a in (seg, gng, gnb, wo, g1, b1, wq, keys)]
    out_specs, out_shape = _post_out_specs(t, d, tm)
    return pl.pallas_call(
        functools.partial(_post_odd_kernel, n_prompt_tiles=n_p),
        grid_spec=pltpu.PrefetchScalarGridSpec(
            num_scalar_prefetch=1, grid=(t // tm,), in_specs=in_specs, out_specs=out_specs,
            scratch_shapes=_post_scratch(tm)),
        out_shape=out_shape,
        compiler_params=_cparams(("arbitrary",)),
        name="post_odd",
    )(tab, x, hf[0], hf[1], hb[0], hb[1], zl, yf[0], yf[1], yb[0], yb[1], bonus, g, mods, seg, gng, gnb, wo,
      g1, b1, wq, keys)


N_TOP = PEER_TOPK + 1
TOP_ROWS = 24
SUBLANES = 8


def _batcher_network(n):
    def merge(lo, hi, r):
        step = r * 2
        if step < hi - lo:
            yield from merge(lo, hi, step)
            yield from merge(lo + r, hi, step)
            yield from [(i, i + r) for i in range(lo + r, hi - r, step)]
        else:
            yield (lo, lo + r)

    def sort(lo, hi):
        if hi - lo >= 1:
            mid = lo + (hi - lo) // 2
            yield from sort(lo, mid)
            yield from sort(mid + 1, hi)
            yield from merge(lo, hi, 1)

    return tuple(sort(0, n - 1))


def _sort_levels(levels, net):
    lv = list(levels)
    for i, j in net:
        a, b = lv[i], lv[j]
        lv[i] = jnp.maximum(a, b)
        lv[j] = jnp.minimum(a, b)
    return lv


def _pop_sorted(levels, n_top, emit):
    lv = list(levels)
    for it in range(n_top):
        m = jnp.max(lv[0], axis=0, keepdims=True)
        emit(it, m)
        hit = lv[0] == m
        live = min(len(lv), n_top - it - 1)
        for k in range(live):
            below = lv[k + 1] if k + 1 < len(lv) else -jnp.inf
            lv[k] = jnp.where(hit, below, lv[k])


def _topk_heads(st_ref, th1_ref, p1_ref, p2_ref, s2_ref, v1_ref, v2_ref):
    tl = st_ref.shape[-1]
    neg = -jnp.inf
    net16 = _batcher_network(N_KEYS // SUBLANES)
    net8 = _batcher_network(SUBLANES)
    row = lax.broadcasted_iota(jnp.int32, (SUBLANES, tl), 0)
    roll = lambda x, sh: pltpu.roll(x, sh, 0)

    def top_values(x, store):
        store[...] = jnp.full(store.shape, neg, F32)
        lv = _sort_levels([x[SUBLANES * k:SUBLANES * (k + 1)] for k in range(N_KEYS // SUBLANES)], net16)

        def emit(it, m):
            store[it:it + 1, :] = m

        _pop_sorted(lv, N_TOP, emit)

    def head(h, carry):
        s1 = st_ref[pl.ds(pl.multiple_of(2 * h * N_KEYS, N_KEYS), N_KEYS), :]
        s2 = st_ref[pl.ds(pl.multiple_of((2 * h + 1) * N_KEYS, N_KEYS), N_KEYS), :]
        top_values(s1, v1_ref)
        top_values(s2, v2_ref)
        one = lambda ref, a: ref[a:a + 1, :]
        v2a = v2_ref[0:8, :]
        v1b2 = roll(v1_ref[8:16, :], 2)
        cands = [one(v1_ref, 0) + v2a, one(v1_ref, 0) + v2_ref[8:16, :], one(v1_ref, 0) + v2_ref[16:24, :],
                 one(v1_ref, 1) + v2a,
                 jnp.where(row < 5, one(v1_ref, 2), one(v1_ref, 4)) + jnp.where(row < 5, v2a, roll(v2a, 5)),
                 jnp.where(row < 4, one(v1_ref, 3), jnp.where(row < 6, one(v1_ref, 5), one(v1_ref, 6)))
                 + jnp.where(row < 4, v2a, jnp.where(row < 6, roll(v2a, 4), roll(v2a, 6))),
                 jnp.where(row < 2, one(v1_ref, 7), v1b2) + jnp.where(row < 2, v2a, one(v2_ref, 0)),
                 jnp.where(row < 2, v1b2, roll(v1_ref[16:24, :], 2)) + one(v2_ref, 0)]
        top = []
        _pop_sorted(_sort_levels(cands, net8), N_TOP, lambda it, m: top.append(m))
        tau = 0.5 * (top[PEER_TOPK - 1] + top[PEER_TOPK])
        mx1 = v1_ref[0:1, :]
        mx2 = v2_ref[0:1, :]
        zacc = jnp.zeros((SUBLANES, tl), F32)
        for cnd in cands:
            zacc = zacc + jnp.where(cnd >= tau, jnp.exp(cnd - (mx1 + mx2)), 0.0)
        zsum = jnp.sum(zacc, axis=0, keepdims=True)
        th1 = tau - s1
        p1 = jnp.exp(s1 - mx1) / zsum
        p2 = jnp.exp(s2 - mx2)
        for cb in range(tl // LANES):
            cs = slice(cb * LANES, (cb + 1) * LANES)
            th1_ref[cb, h] = th1[:, cs]
            p1_ref[cb, h] = p1[:, cs]
            p2_ref[cb, h] = p2[:, cs]
            s2_ref[cb, h] = s2[:, cs]
        return carry

    lax.fori_loop(0, PEER_HEADS, head, 0)


def _peer_kernel(tab_ref, h2t_ref, s2_ref, p2_ref, th1a_ref, p1a_ref, th1b_ref, p1b_ref, u_ref, vt_ref, x1_ref, m_ref,
                 g2_ref, b2_ref, o_ref, acc_ref, act_ref, ga_ref):
    s = pl.program_id(1)
    n_steps = pl.num_programs(1)
    n_blk = PEER_TM // LANES
    rows_per_tile = PEER_TE // N_KEYS
    n_iter = n_blk // BLK_PER_ITER
    d_rows = acc_ref.shape[0] // n_iter
    u_rows = PEER_TE // n_iter

    def project(half, q):
        ro = pl.multiple_of(q * u_rows, u_rows)
        return _dot(u_ref[pl.ds(half * PEER_TE + ro, u_rows), :], h2t_ref[...])

    def store_act(slot, q, a):
        ro = pl.multiple_of(q * u_rows, u_rows)
        for k in range(n_blk):
            act_ref[slot, k, pl.ds(ro, u_rows), :] = a[:, k * LANES:(k + 1) * LANES]

    def apply_v(slot, half, q):
        ro = pl.multiple_of(q * d_rows, d_rows)
        ga = jnp.concatenate([ga_ref[slot, k] for k in range(n_blk)], axis=1)
        return _dot(vt_ref[pl.ds(ro, d_rows), half * PEER_TE:(half + 1) * PEER_TE], ga)

    def stage(cur, th1_ref, p1_ref):
        nxt = 1 - cur

        def body(it, carry):
            a_next = project(nxt, it)
            v_prev = apply_v(nxt, nxt, it)
            for sub in range(BLK_PER_ITER):
                cb = it * BLK_PER_ITER + sub
                th = [th1_ref[cb, h] for h in range(PEER_HEADS)]
                p1 = [p1_ref[cb, h] for h in range(PEER_HEADS)]
                for rg in range(0, rows_per_tile, ROW_GROUP):
                    for part in range(N_KEYS // KEY_CHUNK):
                        ks = slice(part * KEY_CHUNK, (part + 1) * KEY_CHUNK)
                        gates = [None] * ROW_GROUP
                        for h in range(PEER_HEADS):
                            s2 = s2_ref[cb, h, ks, :]
                            p2 = p2_ref[cb, h, ks, :]
                            for g in range(ROW_GROUP):
                                r = rg + g
                                term = jnp.where(s2 >= th[h][r:r + 1], p2, 0.0) * p1[h][r:r + 1]
                                gates[g] = term if gates[g] is None else gates[g] + term
                        for g in range(ROW_GROUP):
                            lo = (rg + g) * N_KEYS + part * KEY_CHUNK
                            rs = slice(lo, lo + KEY_CHUNK)
                            ga_ref[cur, cb, rs, :] = (gates[g] * _gelu(act_ref[cur, cb, rs, :])).astype(BF16)
            store_act(nxt, it, a_next)
            ro = pl.multiple_of(it * d_rows, d_rows)
            acc_ref[pl.ds(ro, d_rows), :] += v_prev
            return carry

        lax.fori_loop(0, n_iter, body, 0)

    @pl.when(s == 0)
    def _():
        acc_ref[...] = jnp.zeros(acc_ref.shape, F32)
        ga_ref[...] = jnp.zeros(ga_ref.shape, BF16)
        for q in range(n_iter):
            store_act(0, q, project(0, q))

    @pl.when(s > 0)
    def _():
        stage(1, th1a_ref, p1a_ref)

    @pl.when(s < n_steps - 1)
    def _():
        stage(0, th1b_ref, p1b_ref)

    @pl.when(s == n_steps - 1)
    def _():
        m = m_ref[0]
        for q in range(n_iter):
            acc_ref[q * d_rows:(q + 1) * d_rows, :] += apply_v(1, 1, q)
        ffn = acc_ref[...].T
        o_ref[...] = _ln(ALPHA * x1_ref[...] + m[5:6] * ffn) * g2_ref[...] + b2_ref[...]


def _peer(x1, tab, h2t, s2, p2, th1, p1, u_bf, vt_bf, mods, g2, b2):
    t, d = x1.shape
    tm = PEER_TM
    te = PEER_TE
    n_tiles = u_bf.shape[0] // te
    per = tm // ROW_TILE
    n_blk = tm // LANES
    hk = pl.BlockSpec((n_blk, PEER_HEADS, N_KEYS, LANES), lambda i, j, t_: (i, 0, 0, 0))
    rows = te // N_KEYS
    rows_a = pl.BlockSpec((n_blk, PEER_HEADS, rows, LANES), lambda i, j, t_: (i, 0, jnp.maximum(2 * j - 1, 0), 0))
    rows_b = pl.BlockSpec((n_blk, PEER_HEADS, rows, LANES), lambda i, j, t_: (i, 0, jnp.minimum(2 * j, n_tiles - 1), 0))
    in_specs = [pl.BlockSpec((d, tm), lambda i, j, t_: (0, i)),
                hk, hk, rows_a, rows_a, rows_b, rows_b,
                pl.BlockSpec((2 * te, d), lambda i, j, t_: (jnp.minimum(j, n_tiles // 2 - 1), 0)),
                pl.BlockSpec((d, 2 * te), lambda i, j, t_: (0, jnp.maximum(j - 1, 0))),
                pl.BlockSpec((tm, d), lambda i, j, t_: (i, 0)),
                pl.BlockSpec((1, 6, d), lambda i, j, t_: (t_[0, i * per], 0, 0)),
                pl.BlockSpec(g2.shape, lambda i, j, t_: (0, 0)),
                pl.BlockSpec(b2.shape, lambda i, j, t_: (0, 0))]
    return pl.pallas_call(
        _peer_kernel,
        grid_spec=pltpu.PrefetchScalarGridSpec(
            num_scalar_prefetch=1, grid=(t // tm, n_tiles // 2 + 1), in_specs=in_specs,
            out_specs=pl.BlockSpec((tm, d), lambda i, j, t_: (i, 0)),
            scratch_shapes=[pltpu.VMEM((d, tm), F32), pltpu.VMEM((2, n_blk, te, LANES), F32),
                            pltpu.VMEM((2, n_blk, te, LANES), BF16)]),
        out_shape=jax.ShapeDtypeStruct((t, d), F32),
        compiler_params=_cparams(("arbitrary", "arbitrary")),
        name="peer_mix",
    )(tab, h2t, s2, p2, th1, p1, th1, p1, u_bf, vt_bf, x1, mods, g2, b2)


def _sincos(pos, dim):
    omega = 1.0 / (10000.0 ** (jnp.arange(dim // 2, dtype=F32) / (dim // 2)))
    ang = pos.astype(F32)[:, None] * omega[None, :]
    return jnp.concatenate([jnp.sin(ang), jnp.cos(ang)], -1)


def _grid_pos_embed(n_tok):
    rows = n_tok // GRID_W
    half = D_MODEL // 2
    er = _sincos(jnp.arange(rows), half)
    ec = _sincos(jnp.arange(GRID_W), half)
    emb = jnp.concatenate([jnp.broadcast_to(er[:, None, :], (rows, GRID_W, half)),
                           jnp.broadcast_to(ec[None, :, :], (rows, GRID_W, half))], -1)
    return emb.reshape(rows * GRID_W, D_MODEL)


def _seq_dft_tables(s_len):
    r = 1
    while r * r < s_len:
        r *= 2
    k = jnp.arange(s_len, dtype=jnp.int32)
    w = 2.0 * math.pi / s_len
    ang_a = ((jnp.arange(s_len // r, dtype=jnp.int32)[:, None] * r * k[None, :]) % s_len).astype(F32) * w
    ang_b = ((jnp.arange(r, dtype=jnp.int32)[:, None] * k[None, :]) % s_len).astype(F32) * w
    ca, sa = jnp.cos(ang_a)[:, None, :], jnp.sin(ang_a)[:, None, :]
    cb, sb = jnp.cos(ang_b)[None, :, :], jnp.sin(ang_b)[None, :, :]
    scale = 1.0 / math.sqrt(s_len * FNET_GW)
    cmat = ((ca * cb - sa * sb) * scale).reshape(s_len, s_len)
    smat = ((sa * cb + ca * sb) * (-scale)).reshape(s_len, s_len)
    return cmat.astype(BF16), smat.astype(BF16)


def _channel_dft_table():
    n = np.arange(FNET_GW)
    ang = ((n[:, None] * n[None, :]) % FNET_GW) * (2.0 * np.pi / FNET_GW)
    wc = np.zeros((FNET_W, 2 * FNET_W), np.float32)
    for g in range(FNET_GROUPS):
        sl = slice(g * FNET_GW, (g + 1) * FNET_GW)
        wc[sl, sl] = np.cos(ang)
        wc[sl, FNET_W + g * FNET_GW:FNET_W + (g + 1) * FNET_GW] = np.sin(ang)
    return jnp.asarray(wc, BF16)


def _segment_ones():
    idx = np.arange(WKV_W) // WKV_N
    return jnp.asarray((idx[:, None] == idx[None, :]).astype(np.float32), BF16)


def _chunk_tri():
    i = np.arange(ROW_TILE)
    same = (i[:, None] // CHUNK) == (i[None, :] // CHUNK)
    lower = same & (i[None, :] <= i[:, None])
    upper = same & (i[None, :] >= i[:, None])
    return jnp.asarray(np.stack([lower, upper]).astype(np.float32), BF16)


def _scan_masks():
    i = np.arange(PAIR)
    same = (i[:, None] // CHUNK) == (i[None, :] // CHUNK)
    t_row = i[:, None] % CHUNK
    t_col = i[None, :] % CHUNK
    masks = [same & (t_col < t_row), same & (t_col <= t_row), same & (t_col > t_row), same & (t_col >= t_row),
             np.eye(PAIR, dtype=bool)]
    return jnp.asarray(np.stack(masks).astype(np.float32))


def _block_diag(w):
    h, n, _ = w.shape
    eye = jnp.eye(h, dtype=w.dtype)
    return jnp.einsum("hij,hg->higj", w, eye).reshape(h * n, h * n)


def _pad_rank(w, d):
    r = w.shape[1]
    z = jnp.zeros_like(w[d])
    return jnp.concatenate([w[d], z] if d == 0 else [z, w[d]], axis=0)


def _state_to_pairs(s0):
    b = s0.shape[0]
    st = jnp.swapaxes(s0, -1, -2).reshape(b, 2, N_PAIRS, 2, WKV_N, WKV_N)
    eye = jnp.eye(2, dtype=s0.dtype)
    return jnp.einsum("bdpeji,ef->bdpejfi", st, eye).reshape(b, 2, N_PAIRS, PAIR, PAIR)


def _pairs_to_state(m):
    b = m.shape[0]
    m7 = m.reshape(b, 2, N_PAIRS, 2, WKV_N, 2, WKV_N)
    st = jnp.stack([m7[:, :, :, 0, :, 0, :], m7[:, :, :, 1, :, 1, :]], axis=3)
    return jnp.swapaxes(st.reshape(b, 2, WKV_H, WKV_N, WKV_N), -1, -2)


def _tile_table(groups):
    cv, first, last = [], [], []
    row = 0
    for gi, (n_seq, s_len) in enumerate(groups):
        nt = s_len // ROW_TILE
        for b in range(n_seq):
            for i in range(nt):
                cv.append(0 if gi == 0 else 1 + b)
                first.append(int(i == 0))
                last.append(int(i == nt - 1))
        row += n_seq * s_len
    return jnp.asarray(np.array([cv, first, last], np.int32))


def kernel(x_prompt, x_sample, state_lru, state_wkv, c, c_ctx, w_mod, b_mod, ln1_g, ln1_b, ln2_g, ln2_b,
           w_in_e, w_out_e, sconv_w, sconv_b, w_in_o, w_out_o, lru_conv_w, lru_conv_b, lru_wa, lru_ba,
           lru_wx, lru_bx, lru_lambda, wkv_mu, wkv_w0, wkv_w2, wkv_a0, wkv_a2, wkv_kk, wkv_ka, wkv_rk,
           wkv_g2, wkv_gn_g, wkv_gn_b, peer_wq, peer_keys, peer_u, peer_v):
    bp, sp, d = x_prompt.shape
    bs, ss, _ = x_sample.shape
    depth = w_mod.shape[0]
    assert sp % ROW_TILE == 0 and ss % PEER_TM == 0 and (bp * sp) % ss == 0
    assert bs + 1 <= 8
    groups = ((bp, sp), (bs, ss))
    tp = bp * sp
    tab = _tile_table(groups)

    cv8 = jnp.concatenate([c_ctx[None, :], c, jnp.zeros((8 - 1 - bs, d), F32)], axis=0)
    mods = _modulation(cv8, w_mod, b_mod).reshape(depth, 8, 6, d)

    x = _embed(x_prompt.reshape(tp, d), x_sample.reshape(bs * ss, d), _grid_pos_embed(ss).astype(x_sample.dtype))

    wc = _channel_dft_table()
    dft = {s: _seq_dft_tables(s) for s in sorted({sp, ss})}
    seg = _segment_ones()
    tri = _chunk_tri()
    masks = _scan_masks()
    row2 = lambda a: a.reshape(1, -1)

    lru_fin = []
    wkv_fin = []
    for l in range(depth):
        j = l // 2
        m_l = mods[l]
        wq = peer_wq[l].astype(BF16)
        keys = peer_keys[l].reshape(2 * PEER_HEADS, N_KEYS, PEER_HALF).astype(BF16)
        if l % 2 == 0:
            zr, pq = _inproj(x, tab, m_l, w_in_e[j].astype(BF16), wc)
            y = jnp.concatenate([_seqdft(pq, bp * sp, 0, bp, sp, *dft[sp]),
                                 _seqdft(pq, bs * ss, tp, bs, ss, *dft[ss])], axis=0)
            x1, h2t, th1, p1, p2, s2 = _post_even(x, tab, zr, y, m_l, sconv_w[j], row2(sconv_b[j]), w_out_e[j].astype(BF16),
                                     row2(ln1_g[l]), row2(ln1_b[l]), wq, keys)
        else:
            zl, zw = _inproj(x, tab, m_l, w_in_o[j].astype(BF16))
            lru_args = (lru_conv_w[j], lru_conv_b[j],
                        jnp.stack([_block_diag(lru_wa[j, dd]) for dd in range(2)]).astype(BF16), lru_ba[j],
                        jnp.stack([_block_diag(lru_wx[j, dd]) for dd in range(2)]).astype(BF16), lru_bx[j],
                        lru_lambda[j])
            hf_p, hb_p, hfin_p = _lru(zl, 0, bp, sp, jnp.zeros((bp, 2, LRU_W), F32), *lru_args)
            hf_s, hb_s, _ = _lru(zl, tp, bs, ss, state_lru[:, j], *lru_args)
            lru_fin.append(hfin_p)
            prep = _wkvprep(zw, tab, row2(wkv_mu[j]), wkv_w0[j],
                            jnp.stack([_pad_rank(wkv_w2[j], dd) for dd in range(2)]).astype(BF16), wkv_a0[j],
                            jnp.stack([_pad_rank(wkv_a2[j], dd) for dd in range(2)]).astype(BF16),
                            wkv_kk[j], wkv_ka[j], row2(wkv_rk[j]), wkv_g2[j].astype(BF16), seg, tri)
            v, bonus, g = prep[0], prep[1], prep[2]
            yf_p, yb_p, mfin_p = _wkvscan(prep[3:], v, 0, bp, sp,
                                          jnp.zeros((bp, 2, N_PAIRS, PAIR, PAIR), F32), masks)
            yf_s, yb_s, _ = _wkvscan(prep[3:], v, tp, bs, ss, _state_to_pairs(state_wkv[:, j]), masks)
            wkv_fin.append(_pairs_to_state(mfin_p))
            x1, h2t, th1, p1, p2, s2 = _post_odd(x, tab, (hf_p, hf_s), (hb_p, hb_s), zl, (yf_p, yf_s), (yb_p, yb_s),
                                    bonus, g, m_l, seg, row2(wkv_gn_g[j]), row2(wkv_gn_b[j]),
                                    w_out_o[j].astype(BF16), row2(ln1_g[l]), row2(ln1_b[l]), wq, keys)
        x = _peer(x1, tab, h2t, s2, p2, th1, p1, peer_u[l].astype(BF16), peer_v[l].T.astype(BF16), m_l,
                  row2(ln2_g[l]), row2(ln2_b[l]))

    y_prompt = x[:tp].reshape(bp, sp, d)
    y_sample = x[tp:].reshape(bs, ss, d)
    return (y_prompt, y_sample, jnp.stack(lru_fin, 1).astype(x_prompt.dtype),
            jnp.stack(wkv_fin, 1).astype(x_prompt.dtype))
```

```python
import functools
import math

import numpy as np
import jax
import jax.numpy as jnp
from jax import lax
from jax.experimental import pallas as pl
from jax.experimental.pallas import tpu as pltpu

F32 = jnp.float32
BF16 = jnp.bfloat16

D_MODEL = 1024
GRID_W = 64
FNET_W = 512
FNET_GROUPS = 4
FNET_GW = FNET_W // FNET_GROUPS
CONV_W = 512
LRU_W = 512
LRU_HEADS = 8
LRU_CONV_K = 4
LRU_C = 8.0
WKV_W = 512
WKV_N = 64
WKV_H = 8
WKV_IN = 1920
DECAY_SCALE = math.exp(-0.5)
WKV_GN_EPS = 64e-5
PEER_HEADS = 8
N_KEYS = 128
PEER_TOPK = 16
PEER_HALF = 128
DEPTH = 4
ALPHA = (2 * DEPTH) ** 0.25
LN_EPS = 1e-6

ROW_TILE = 256
LANES = 128
HALO = 8
CHUNK = 64
PAIR = 2 * WKV_N
N_PAIRS = WKV_H // 2
PEER_TM = 512
PEER_TE = 1024
ROW_GROUP = 2
KEY_CHUNK = 64
BLK_PER_ITER = 2
VMEM_LIMIT = 56 * 1024 * 1024


def _cparams(sem):
    return pltpu.CompilerParams(dimension_semantics=sem, vmem_limit_bytes=VMEM_LIMIT)


def _dot(a, b):
    return jnp.dot(a, b, preferred_element_type=F32)


def _dot_nt(a, b):
    return lax.dot_general(a, b, (((1,), (1,)), ((), ())), preferred_element_type=F32)


def _mm3(a, b):
    ah = a.astype(BF16)
    al = (a - ah.astype(F32)).astype(BF16)
    bh = b.astype(BF16)
    bl = (b - bh.astype(F32)).astype(BF16)
    return _dot(ah, bh) + (_dot(ah, bl) + _dot(al, bh))


def _mm1(a, b):
    return _dot(a.astype(BF16), b.astype(BF16))


def _mm01(a, b01):
    h = a.astype(BF16)
    r = a - h.astype(F32)
    m = r.astype(BF16)
    lo = (r - m.astype(F32)).astype(BF16)
    return _dot(h, b01) + (_dot(m, b01) + _dot(lo, b01))


def _ln(x):
    mu = jnp.mean(x, axis=-1, keepdims=True)
    xc = x - mu
    var = jnp.mean(xc * xc, axis=-1, keepdims=True)
    return xc * lax.rsqrt(var + LN_EPS)


def _gelu(x):
    z = x * (0.7978845608028654 + 0.035677408136300125 * (x * x))
    hx = 0.5 * x
    return hx + hx * jnp.tanh(z)


def _sigmoid(x):
    return 1.0 / (1.0 + jnp.exp(-x))


def _mod_kernel(c_ref, w_ref, b_ref, o_ref):
    c = c_ref[...]
    sc = c * _sigmoid(c)
    o_ref[0] = _dot(sc.astype(BF16), w_ref[0].astype(BF16)) + b_ref[0]


def _modulation(cv8, w_mod, b_mod):
    depth, d, n = w_mod.shape
    tn = 1536
    return pl.pallas_call(
        _mod_kernel,
        grid=(depth, n // tn),
        in_specs=[pl.BlockSpec((8, d), lambda l, j: (0, 0)),
                  pl.BlockSpec((1, d, tn), lambda l, j: (l, 0, j)),
                  pl.BlockSpec((1, 1, tn), lambda l, j: (l, 0, j))],
        out_specs=pl.BlockSpec((1, 8, tn), lambda l, j: (l, 0, j)),
        out_shape=jax.ShapeDtypeStruct((depth, 8, n), F32),
        compiler_params=_cparams(("arbitrary", "arbitrary")),
        name="modulation",
    )(cv8, w_mod, b_mod.reshape(depth, 1, n))


def _embed_kernel(xp_ref, xs_ref, p_ref, o_ref, *, n_prompt):
    i = pl.program_id(0)

    @pl.when(i < n_prompt)
    def _():
        o_ref[...] = xp_ref[...]

    @pl.when(i >= n_prompt)
    def _():
        o_ref[...] = xs_ref[...] + p_ref[...]


def _embed(xp, xs, pos):
    tp, d = xp.shape
    ts = xs.shape[0]
    s_len = pos.shape[0]
    tm = 512
    n_prompt = tp // tm
    per_seq = s_len // tm
    return pl.pallas_call(
        functools.partial(_embed_kernel, n_prompt=n_prompt),
        grid=((tp + ts) // tm,),
        in_specs=[pl.BlockSpec((tm, d), lambda i: (jnp.minimum(i, n_prompt - 1), 0)),
                  pl.BlockSpec((tm, d), lambda i: (jnp.maximum(i - n_prompt, 0), 0)),
                  pl.BlockSpec((tm, d), lambda i: (jnp.maximum(i - n_prompt, 0) % per_seq, 0))],
        out_specs=pl.BlockSpec((tm, d), lambda i: (i, 0)),
        out_shape=jax.ShapeDtypeStruct((tp + ts, d), xp.dtype),
        compiler_params=_cparams(("arbitrary",)),
        name="embed",
    )(xp, xs, pos)


def _inproj_even_kernel(tab_ref, x_ref, m_ref, w_ref, wc_ref, zr_ref, pq_ref):
    m = m_ref[0]
    h = _ln(x_ref[...]) * (1.0 + m[1:2]) + m[0:1]
    z = _dot(h.astype(BF16), w_ref[...])
    zr_ref[...] = z[:, FNET_W:]
    pq_ref[...] = _dot(z[:, :FNET_W].astype(BF16), wc_ref[...]).astype(BF16)


def _inproj_odd_kernel(tab_ref, x_ref, m_ref, w_ref, zl_ref, zw_ref):
    m = m_ref[0]
    h = _ln(x_ref[...]) * (1.0 + m[1:2]) + m[0:1]
    z = _dot(h.astype(BF16), w_ref[...])
    zl_ref[...] = z[:, :2 * LRU_W]
    zw_ref[...] = z[:, 2 * LRU_W:]


def _const_spec(shape):
    nd = len(shape)
    return pl.BlockSpec(shape, lambda i, t, _n=nd: (0,) * _n)


def _inproj(x, tab, mods, w_in, wc=None):
    t, d = x.shape
    n = w_in.shape[1]
    tm = ROW_TILE
    even = wc is not None
    in_specs = [pl.BlockSpec((tm, d), lambda i, t_: (i, 0)),
                pl.BlockSpec((1, 6, d), lambda i, t_: (t_[0, i], 0, 0)),
                _const_spec((d, n))]
    if even:
        in_specs.append(_const_spec(wc.shape))
        widths = (n - FNET_W, 2 * FNET_W)
        dtypes = (F32, BF16)
        kern = _inproj_even_kernel
        args = (tab, x, mods, w_in, wc)
    else:
        widths = (2 * LRU_W, n - 2 * LRU_W)
        dtypes = (F32, F32)
        kern = _inproj_odd_kernel
        args = (tab, x, mods, w_in)
    return pl.pallas_call(
        kern,
        grid_spec=pltpu.PrefetchScalarGridSpec(
            num_scalar_prefetch=1, grid=(t // tm,), in_specs=in_specs,
            out_specs=[pl.BlockSpec((tm, w), lambda i, t_: (i, 0)) for w in widths]),
        out_shape=[jax.ShapeDtypeStruct((t, w), dt) for w, dt in zip(widths, dtypes)],
        compiler_params=_cparams(("arbitrary",)),
        name="inproj_even" if even else "inproj_odd",
    )(*args)


def _seqdft_kernel(c_ref, s_ref, p_ref, q_ref, y_ref):
    y_ref[...] = (_dot(c_ref[...], p_ref[...]) + _dot(s_ref[...], q_ref[...])).astype(BF16)


def _seqdft(pq, y_rows, row0, n_seq, s_len, cmat, smat):
    tm = min(s_len, ROW_TILE)
    nt = s_len // tm
    blk0 = row0 // s_len
    return pl.pallas_call(
        _seqdft_kernel,
        grid=(n_seq, nt),
        in_specs=[pl.BlockSpec((tm, s_len), lambda b, i: (i, 0)),
                  pl.BlockSpec((tm, s_len), lambda b, i: (i, 0)),
                  pl.BlockSpec((s_len, FNET_W), lambda b, i: (blk0 + b, 0)),
                  pl.BlockSpec((s_len, FNET_W), lambda b, i: (blk0 + b, 1))],
        out_specs=pl.BlockSpec((tm, FNET_W), lambda b, i: (b * nt + i, 0)),
        out_shape=jax.ShapeDtypeStruct((y_rows, FNET_W), BF16),
        compiler_params=_cparams(("arbitrary", "arbitrary")),
        name="seqdft_%d" % s_len,
    )(cmat, smat, pq, pq)


def _post_and_query(x, mix, m, g1, b1, wq_ref, keys_ref, x1_ref, h2t_ref, topk_refs, st_ref, v1_ref, v2_ref):
    x1 = _ln(ALPHA * x + m[2:3] * mix) * g1 + b1
    x1_ref[...] = x1
    h2 = _ln(x1) * (1.0 + m[4:5]) + m[3:4]
    h2b = h2.astype(BF16)
    h2t_ref[...] = h2.T.astype(BF16)
    q = _dot(h2b, wq_ref[...])
    for hp in range(2 * PEER_HEADS):
        qb = q[:, hp * PEER_HALF:(hp + 1) * PEER_HALF].astype(BF16)
        st_ref[hp * N_KEYS:(hp + 1) * N_KEYS, :] = _dot_nt(keys_ref[hp], qb)
    _topk_heads(st_ref, *topk_refs, v1_ref, v2_ref)


def _post_even_kernel(tab_ref, x_ref, bg_ref, cg_ref, xi_ref, cgp_ref, xip_ref, cgn_ref, xin_ref, y_ref,
                      m_ref, cw_ref, cb_ref, wo_ref, g1_ref, b1_ref, wq_ref, keys_ref,
                      x1_ref, h2t_ref, th1_ref, p1_ref, p2_ref, s2_ref, ext_ref, st_ref, v1_ref, v2_ref):
    i = pl.program_id(0)
    tm = ROW_TILE
    first = tab_ref[1, i] == 1
    last = tab_ref[2, i] == 1
    u = cg_ref[...] * xi_ref[...]
    ext_ref[0:HALO] = jnp.where(first, 0.0, cgp_ref[...] * xip_ref[...])
    ext_ref[HALO:HALO + tm] = u
    ext_ref[HALO + tm:2 * HALO + tm] = jnp.where(last, 0.0, cgn_ref[...] * xin_ref[...])
    cw = cw_ref[...]
    conv = (ext_ref[HALO - 1:HALO - 1 + tm] * cw[0:1] + u * cw[1:2]
            + ext_ref[HALO + 1:HALO + 1 + tm] * cw[2:3] + cb_ref[...])
    ymix = (bg_ref[...] * conv).astype(BF16)
    mix = _dot(y_ref[...], wo_ref[0:FNET_W]) + _dot(ymix, wo_ref[FNET_W:])
    _post_and_query(x_ref[...], mix, m_ref[0], g1_ref[...], b1_ref[...], wq_ref, keys_ref,
                    x1_ref, h2t_ref, (th1_ref, p1_ref, p2_ref, s2_ref), st_ref, v1_ref, v2_ref)


def _post_out_specs(t, d, tm):
    rshape = jax.ShapeDtypeStruct((t // LANES, PEER_HEADS, N_KEYS, LANES), F32)
    rspec = pl.BlockSpec((tm // LANES, PEER_HEADS, N_KEYS, LANES), lambda i, t_: (i, 0, 0, 0))
    specs = [pl.BlockSpec((tm, d), lambda i, t_: (i, 0)),
             pl.BlockSpec((d, tm), lambda i, t_: (0, i))] + [rspec] * 4
    shapes = [jax.ShapeDtypeStruct((t, d), F32),
              jax.ShapeDtypeStruct((d, t), BF16)] + [rshape] * 4
    return specs, shapes


def _post_scratch(tm):
    return [pltpu.VMEM((2 * PEER_HEADS * N_KEYS, tm), F32), pltpu.VMEM((TOP_ROWS, tm), F32),
            pltpu.VMEM((TOP_ROWS, tm), F32)]


def _post_even(x, tab, zr, y, mods, cw, cb, wo, g1, b1, wq, keys):
    t, d = x.shape
    tm = ROW_TILE
    hb = tm // HALO
    nblk = t // HALO
    prev = lambda c: pl.BlockSpec((HALO, CONV_W), lambda i, t_, _c=c: (jnp.maximum(i * hb - 1, 0), _c))
    nxt = lambda c: pl.BlockSpec((HALO, CONV_W), lambda i, t_, _c=c: (jnp.minimum((i + 1) * hb, nblk - 1), _c))
    col = lambda c: pl.BlockSpec((tm, CONV_W), lambda i, t_, _c=c: (i, _c))
    in_specs = [pl.BlockSpec((tm, d), lambda i, t_: (i, 0)),
                col(0), col(1), col(2), prev(1), prev(2), nxt(1), nxt(2),
                pl.BlockSpec((tm, FNET_W), lambda i, t_: (i, 0)),
                pl.BlockSpec((1, 6, d), lambda i, t_: (t_[0, i], 0, 0)),
                _const_spec(cw.shape), _const_spec(cb.shape), _const_spec(wo.shape),
                _const_spec(g1.shape), _const_spec(b1.shape), _const_spec(wq.shape), _const_spec(keys.shape)]
    out_specs, out_shape = _post_out_specs(t, d, tm)
    return pl.pallas_call(
        _post_even_kernel,
        grid_spec=pltpu.PrefetchScalarGridSpec(
            num_scalar_prefetch=1, grid=(t // tm,), in_specs=in_specs, out_specs=out_specs,
            scratch_shapes=[pltpu.VMEM((tm + 2 * HALO, CONV_W), F32)] + _post_scratch(tm)),
        out_shape=out_shape,
        compiler_params=_cparams(("arbitrary",)),
        name="post_even",
    )(tab, x, zr, zr, zr, zr, zr, zr, zr, y, mods, cw, cb, wo, g1, b1, wq, keys)


def _lru_kernel(xf_ref, xfh_ref, xb_ref, xbh_ref, cw_ref, cb_ref, wa_ref, ba_ref, wx_ref, bx_ref, lam_ref,
                h0_ref, hf_ref, hb_ref, hfin_ref, ext_ref, carry_ref):
    i = pl.program_id(1)
    nt = pl.num_programs(1)
    tm = ROW_TILE

    @pl.when(i == 0)
    def _():
        carry_ref[...] = h0_ref[0]

    row = lax.broadcasted_iota(jnp.int32, (tm, 1), 0)
    for d in range(2):
        x = (xf_ref if d == 0 else xb_ref)[...]
        halo = jnp.where(i == 0, 0.0, (xfh_ref if d == 0 else xbh_ref)[...])
        ext_ref[HALO:HALO + tm] = x
        if d == 0:
            ext_ref[0:HALO] = halo
        else:
            ext_ref[HALO + tm:2 * HALO + tm] = halo
        cw = cw_ref[d]
        xc = cb_ref[d:d + 1] + x * cw[LRU_CONV_K - 1:LRU_CONV_K]
        for j in range(LRU_CONV_K - 1):
            k = LRU_CONV_K - 1 - j
            off = HALO - k if d == 0 else HALO + k
            xc = xc + ext_ref[off:off + tm] * cw[j:j + 1]
        xcb = xc.astype(BF16)
        gate_r = _sigmoid(_dot(xcb, wa_ref[d]) + ba_ref[d:d + 1])
        gate_i = _sigmoid(_dot(xcb, wx_ref[d]) + bx_ref[d:d + 1])
        nl = -lam_ref[d:d + 1]
        softplus = jnp.maximum(nl, 0.0) + jnp.log1p(jnp.exp(-jnp.abs(nl)))
        log_a = -LRU_C * gate_r * softplus
        a = jnp.exp(log_a)
        b = jnp.sqrt(-jnp.tanh(log_a) * (a * a + 1.0)) * (gate_i * xc)
        s = 1
        while s < tm:
            if d == 0:
                keep = row >= s
                sh = s
            else:
                keep = row < tm - s
                sh = tm - s
            a_sh = jnp.where(keep, pltpu.roll(a, sh, 0), 1.0)
            b_sh = jnp.where(keep, pltpu.roll(b, sh, 0), 0.0)
            b = a * b_sh + b
            a = a * a_sh
            s *= 2
        h = a * carry_ref[d:d + 1] + b
        if d == 0:
            hf_ref[...] = h
            carry_ref[0:1] = h[tm - 1:tm]
        else:
            hb_ref[...] = h
            carry_ref[1:2] = h[0:1]

    @pl.when(i == nt - 1)
    def _():
        hfin_ref[0] = carry_ref[...]


def _lru(zl, row0, n_seq, s_len, h0, cw, cb, wa, ba, wx, bx, lam):
    tm = ROW_TILE
    nt = s_len // tm
    t0 = row0 // tm
    hb = tm // HALO
    nblk = zl.shape[0] // HALO
    rows = n_seq * s_len
    fwd = lambda b, i: (t0 + b * nt + i, 0)
    bwd = lambda b, i: (t0 + b * nt + nt - 1 - i, 0)
    fwd_h = lambda b, i: (jnp.maximum((t0 + b * nt + i) * hb - 1, 0), 0)
    bwd_h = lambda b, i: (jnp.minimum((t0 + b * nt + nt - i) * hb, nblk - 1), 0)
    out_f = lambda b, i: (b * nt + i, 0)
    out_b = lambda b, i: (b * nt + nt - 1 - i, 0)
    cst = lambda a: pl.BlockSpec(a.shape, lambda b, i, _n=a.ndim: (0,) * _n)
    return pl.pallas_call(
        _lru_kernel,
        grid=(n_seq, nt),
        in_specs=[pl.BlockSpec((tm, LRU_W), fwd), pl.BlockSpec((HALO, LRU_W), fwd_h),
                  pl.BlockSpec((tm, LRU_W), bwd), pl.BlockSpec((HALO, LRU_W), bwd_h),
                  cst(cw), cst(cb), cst(wa), cst(ba), cst(wx), cst(bx), cst(lam),
                  pl.BlockSpec((1, 2, LRU_W), lambda b, i: (b, 0, 0))],
        out_specs=[pl.BlockSpec((tm, LRU_W), out_f), pl.BlockSpec((tm, LRU_W), out_b),
                   pl.BlockSpec((1, 2, LRU_W), lambda b, i: (b, 0, 0))],
        out_shape=[jax.ShapeDtypeStruct((rows, LRU_W), F32), jax.ShapeDtypeStruct((rows, LRU_W), F32),
                   jax.ShapeDtypeStruct((n_seq, 2, LRU_W), F32)],
        scratch_shapes=[pltpu.VMEM((tm + 2 * HALO, LRU_W), F32), pltpu.VMEM((2, LRU_W), F32)],
        compiler_params=_cparams(("arbitrary", "arbitrary")),
        name="lru_%d" % s_len,
    )(zl, zl, zl, zl, cw, cb, wa, ba, wx, bx, lam, h0)


def _wkvprep_kernel(tab_ref, z_ref, zp_ref, zn_ref, mu_ref, w0_ref, w2_ref, a0_ref, a2_ref, kk_ref, ka_ref,
                    rk_ref, g2_ref, seg_ref, tri_ref,
                    v_ref, bg_ref, g_ref, rf_ref, kf_ref, bf_ref, qf_ref, cf_ref,
                    rb_ref, kb_ref, bb_ref, qb_ref, cb_ref, ext_ref):
    i = pl.program_id(0)
    tm = ROW_TILE
    w = WKV_W
    first = tab_ref[1, i] == 1
    last = tab_ref[2, i] == 1
    z = z_ref[...]
    ext_ref[0:HALO] = jnp.where(first, 0.0, zp_ref[...])
    ext_ref[HALO:HALO + tm] = z
    ext_ref[HALO + tm:2 * HALO + tm] = jnp.where(last, 0.0, zn_ref[...])
    z = z + mu_ref[...] * (0.5 * (ext_ref[HALO - 1:HALO - 1 + tm] + ext_ref[HALO + 1:HALO + 1 + tm]) - z)
    r = z[:, 0:w]
    k = z[:, w:2 * w]
    v = z[:, 2 * w:3 * w]
    wd = jnp.tanh(z[:, 3 * w:3 * w + 128]).astype(BF16)
    ad = z[:, 3 * w + 128:3 * w + 256].astype(BF16)
    gd = _sigmoid(z[:, 3 * w + 256:3 * w + 384]).astype(BF16)
    v_ref[...] = v
    g_ref[...] = _dot(gd, g2_ref[...])
    seg = seg_ref[...]
    rk = rk_ref[...]
    bonus = jnp.zeros((tm, w), F32)
    outs = ((rf_ref, kf_ref, bf_ref, qf_ref, cf_ref), (rb_ref, kb_ref, bb_ref, qb_ref, cb_ref))
    nch = tm // CHUNK
    for d in range(2):
        r_ref, k_ref, b_ref, q_ref, c_ref = outs[d]
        wz = w0_ref[d:d + 1] + _dot(wd, w2_ref[d])
        logw = -DECAY_SCALE * _sigmoid(wz)
        iclr = _sigmoid(a0_ref[d:d + 1] + _dot(ad, a2_ref[d]))
        kk = k * kk_ref[d:d + 1]
        kk = kk * lax.rsqrt(jnp.maximum(_mm01(kk * kk, seg), 1e-24))
        km = k * (1.0 + (iclr - 1.0) * ka_ref[d:d + 1])
        bonus = bonus + _mm1(r * km * rk, seg) * v
        lw_h = logw.astype(BF16)
        lw_r = logw - lw_h.astype(F32)
        lw_m = lw_r.astype(BF16)
        lw_l = (lw_r - lw_m.astype(F32)).astype(BF16)
        tri = tri_ref[d]
        cl = _dot(tri, lw_h) + (_dot(tri, lw_m) + _dot(tri, lw_l))
        c = jnp.exp(cl)
        cinv = jnp.exp(-cl)
        r_ref[...] = r * c
        k_ref[...] = km * cinv
        b_ref[...] = kk * iclr * cinv
        q_ref[...] = kk * jnp.exp(cl - logw)
        for j in range(nch):
            edge = (j + 1) * CHUNK - 1 if d == 0 else j * CHUNK
            c_ref[j] = c[edge:edge + 1]
    bg_ref[...] = bonus


def _wkvprep(zw, tab, mu, w0, w2p, a0, a2p, kk, ka, rk, g2, seg, tri):
    t, n = zw.shape
    tm = ROW_TILE
    hb = tm // HALO
    nblk = t // HALO
    nch = tm // CHUNK
    w = WKV_W
    in_specs = [pl.BlockSpec((tm, n), lambda i, t_: (i, 0)),
                pl.BlockSpec((HALO, n), lambda i, t_: (jnp.maximum(i * hb - 1, 0), 0)),
                pl.BlockSpec((HALO, n), lambda i, t_: (jnp.minimum((i + 1) * hb, nblk - 1), 0))]
    in_specs += [_const_spec(a.shape) for a in (mu, w0, w2p, a0, a2p, kk, ka, rk, g2, seg, tri)]
    row = pl.BlockSpec((tm, w), lambda i, t_: (i, 0))
    cspec = pl.BlockSpec((nch, 1, w), lambda i, t_: (i, 0, 0))
    rshape = jax.ShapeDtypeStruct((t, w), F32)
    cshape = jax.ShapeDtypeStruct((t // CHUNK, 1, w), F32)
    out_specs = [row, row, row] + [row, row, row, row, cspec] * 2
    out_shape = [rshape, rshape, rshape] + [rshape, rshape, rshape, rshape, cshape] * 2
    return pl.pallas_call(
        _wkvprep_kernel,
        grid_spec=pltpu.PrefetchScalarGridSpec(
            num_scalar_prefetch=1, grid=(t // tm,), in_specs=in_specs, out_specs=out_specs,
            scratch_shapes=[pltpu.VMEM((tm + 2 * HALO, n), F32)]),
        out_shape=out_shape,
        compiler_params=_cparams(("arbitrary",)),
        name="wkv_prep",
    )(tab, zw, zw, zw, mu, w0, w2p, a0, a2p, kk, ka, rk, g2, seg, tri)


def _wkv_chunks(items, eye):
    lane = lax.broadcasted_iota(jnp.int32, (1, PAIR), 1)
    m0 = (lane < WKV_N).astype(F32)
    m1 = 1.0 - m0
    stack = lambda x: jnp.concatenate([x * m0, x * m1], axis=0)
    n = range(len(items))
    rh_s = [stack(it[0]) for it in items]
    kh_s = [stack(it[1]) for it in items]
    bh_s = [stack(it[2]) for it in items]
    kq_s = [stack(it[3]) for it in items]
    v_s = [stack(it[4]) for it in items]
    bh_t = [x.T for x in bh_s]
    kh_t = [x.T for x in kh_s]
    bk_t = [jnp.concatenate([bh_t[i], kh_t[i]], axis=1) for i in n]
    att_s = [_mm3(kq_s[i], bk_t[i]) for i in n]
    att_y = [_mm1(rh_s[i], bk_t[i]) for i in n]
    n1 = [att_s[i][:, 0:PAIR] * items[i][7] for i in n]
    ak = [att_s[i][:, PAIR:] * items[i][7] for i in n]
    gb = [att_y[i][:, 0:PAIR] * items[i][8] for i in n]
    gk = [att_y[i][:, PAIR:] * items[i][8] for i in n]
    t_inv = [eye - n1[i] for i in n]
    npow = n1
    for _ in range(5):
        npow = [_mm1(npow[i], npow[i]) for i in n]
        t_inv = [t_inv[i] + _mm1(t_inv[i], npow[i]) for i in n]
    kv = [_mm1(jnp.concatenate([kh_t[i], gk[i], ak[i]], axis=0), v_s[i]) for i in n]
    x = [_mm1(t_inv[i], jnp.concatenate([kq_s[i], kv[i][2 * PAIR:]], axis=1)) for i in n]
    bx = [_mm1(jnp.concatenate([bh_t[i], gb[i]], axis=0), x[i]) for i in n]
    p_mat = [eye - bx[i][0:PAIR, 0:PAIR] for i in n]
    q_mat = [kv[i][0:PAIR] - bx[i][0:PAIR, PAIR:] for i in n]
    r_til = [rh_s[i] - bx[i][PAIR:, 0:PAIR] for i in n]
    y0 = [kv[i][PAIR:2 * PAIR] - bx[i][PAIR:, PAIR:] for i in n]
    pm = [_mm3(p_mat[i], items[i][6]) for i in n]
    y_st = [_mm1(r_til[i], items[i][6]) + y0[i] for i in n]
    ys = [y_st[i][0:CHUNK] + y_st[i][CHUNK:] for i in n]
    c_col = [jnp.broadcast_to(items[i][5], (PAIR, PAIR)).T for i in n]
    m_new = [(pm[i] + q_mat[i]) * c_col[i] for i in n]
    return ys, m_new


def _wkvscan_kernel(rf_ref, kf_ref, bf_ref, qf_ref, vf_ref, cf_ref, rb_ref, kb_ref, bb_ref, qb_ref, vb_ref, cb_ref,
                    m0_ref, msk_ref, yf_ref, yb_ref, mfin_ref, m_ref):
    i = pl.program_id(1)
    nc = pl.num_programs(1)

    @pl.when(i == 0)
    def _():
        m_ref[...] = m0_ref[0]

    eye = msk_ref[4]
    ins = ((rf_ref, kf_ref, bf_ref, qf_ref, vf_ref, cf_ref, yf_ref),
           (rb_ref, kb_ref, bb_ref, qb_ref, vb_ref, cb_ref, yb_ref))
    items = []
    for d in range(2):
        r_ref, k_ref, b_ref, q_ref, v_ref, c_ref, _ = ins[d]
        for p in range(N_PAIRS):
            sl = slice(p * PAIR, (p + 1) * PAIR)
            items.append((r_ref[:, sl], k_ref[:, sl], b_ref[:, sl], q_ref[:, sl], v_ref[:, sl],
                          c_ref[0, :, sl], m_ref[d, p], msk_ref[2 * d], msk_ref[2 * d + 1]))
    ys, m_new = _wkv_chunks(items, eye)
    for d in range(2):
        for p in range(N_PAIRS):
            sl = slice(p * PAIR, (p + 1) * PAIR)
            ins[d][6][:, sl] = ys[d * N_PAIRS + p]
            m_ref[d, p] = m_new[d * N_PAIRS + p]

    @pl.when(i == nc - 1)
    def _():
        mfin_ref[0] = m_ref[...]


def _wkvscan(prep, v, row0, n_seq, s_len, m0, masks):
    rf, kf, bf, qf, cf, rb, kb, bb, qb, cb = prep
    nc = s_len // CHUNK
    c0 = row0 // CHUNK
    rows = n_seq * s_len
    w = WKV_W
    fwd = lambda b, i: (c0 + b * nc + i, 0)
    bwd = lambda b, i: (c0 + b * nc + nc - 1 - i, 0)
    fwd3 = lambda b, i: (c0 + b * nc + i, 0, 0)
    bwd3 = lambda b, i: (c0 + b * nc + nc - 1 - i, 0, 0)
    blk = lambda im: pl.BlockSpec((CHUNK, w), im)
    cblk = lambda im: pl.BlockSpec((1, 1, w), im)
    mspec = pl.BlockSpec((1, 2, N_PAIRS, PAIR, PAIR), lambda b, i: (b, 0, 0, 0, 0))
    return pl.pallas_call(
        _wkvscan_kernel,
        grid=(n_seq, nc),
        in_specs=[blk(fwd)] * 5 + [cblk(fwd3)] + [blk(bwd)] * 5 + [cblk(bwd3)]
        + [mspec, pl.BlockSpec(masks.shape, lambda b, i: (0, 0, 0))],
        out_specs=[pl.BlockSpec((CHUNK, w), lambda b, i: (b * nc + i, 0)),
                   pl.BlockSpec((CHUNK, w), lambda b, i: (b * nc + nc - 1 - i, 0)),
                   mspec],
        out_shape=[jax.ShapeDtypeStruct((rows, w), F32), jax.ShapeDtypeStruct((rows, w), F32),
                   jax.ShapeDtypeStruct(m0.shape, F32)],
        scratch_shapes=[pltpu.VMEM((2, N_PAIRS, PAIR, PAIR), F32)],
        compiler_params=_cparams(("arbitrary", "arbitrary")),
        name="wkv_scan_%d" % s_len,
    )(rf, kf, bf, qf, v, cf, rb, kb, bb, qb, v, cb, m0, masks)


def _post_odd_kernel(tab_ref, x_ref, hfp_ref, hfs_ref, hbp_ref, hbs_ref, gb_ref, yfp_ref, yfs_ref, ybp_ref, ybs_ref,
                     bon_ref, g_ref, m_ref, seg_ref, gng_ref, gnb_ref, wo_ref, g1_ref, b1_ref, wq_ref, keys_ref,
                     x1_ref, h2t_ref, th1_ref, p1_ref, p2_ref, s2_ref, st_ref, v1_ref, v2_ref, *, n_prompt_tiles):
    is_prompt = pl.program_id(0) < n_prompt_tiles
    pick = lambda p_ref, s_ref: jnp.where(is_prompt, p_ref[...], s_ref[...])
    y_lru = ((pick(hfp_ref, hfs_ref) + pick(hbp_ref, hbs_ref)) * _gelu(gb_ref[...])).astype(BF16)
    seg = seg_ref[...]
    ys = pick(yfp_ref, yfs_ref) + pick(ybp_ref, ybs_ref)
    mean = _mm1(ys, seg) * (1.0 / WKV_N)
    yc = ys - mean
    var = _mm1(yc * yc, seg) * (1.0 / WKV_N)
    yn = yc * lax.rsqrt(var + WKV_GN_EPS) * gng_ref[...] + gnb_ref[...]
    y_wkv = ((yn + bon_ref[...]) * g_ref[...]).astype(BF16)
    mix = _dot(y_lru, wo_ref[0:LRU_W]) + _dot(y_wkv, wo_ref[LRU_W:])
    _post_and_query(x_ref[...], mix, m_ref[0], g1_ref[...], b1_ref[...], wq_ref, keys_ref,
                    x1_ref, h2t_ref, (th1_ref, p1_ref, p2_ref, s2_ref), st_ref, v1_ref, v2_ref)


def _post_odd(x, tab, hf, hb, zl, yf, yb, bonus, g, mods, seg, gng, gnb, wo, g1, b1, wq, keys):
    t, d = x.shape
    tm = ROW_TILE
    w = WKV_W
    n_p = hf[0].shape[0] // tm
    n_s = hf[1].shape[0] // tm
    row = pl.BlockSpec((tm, w), lambda i, t_: (i, 0))
    row_p = pl.BlockSpec((tm, w), lambda i, t_: (jnp.minimum(i, n_p - 1), 0))
    row_s = pl.BlockSpec((tm, w), lambda i, t_: (jnp.clip(i - n_p, 0, n_s - 1), 0))
    in_specs = [pl.BlockSpec((tm, d), lambda i, t_: (i, 0)), row_p, row_s, row_p, row_s,
                pl.BlockSpec((tm, LRU_W), lambda i, t_: (i, 1)), row_p, row_s, row_p, row_s, row, row,
                pl.BlockSpec((1, 6, d), lambda i, t_: (t_[0, i], 0, 0))]
    in_specs += [_const_spec(a.shape) for a in (seg, gng, gnb, wo, g1, b1, wq, keys)]
    out_specs, out_shape = _post_out_specs(t, d, tm)
    return pl.pallas_call(
        functools.partial(_post_odd_kernel, n_prompt_tiles=n_p),
        grid_spec=pltpu.PrefetchScalarGridSpec(
            num_scalar_prefetch=1, grid=(t // tm,), in_specs=in_specs, out_specs=out_specs,
            scratch_shapes=_post_scratch(tm)),
        out_shape=out_shape,
        compiler_params=_cparams(("arbitrary",)),
        name="post_odd",
    )(tab, x, hf[0], hf[1], hb[0], hb[1], zl, yf[0], yf[1], yb[0], yb[1], bonus, g, mods, seg, gng, gnb, wo,
      g1, b1, wq, keys)


N_TOP = PEER_TOPK + 1
TOP_ROWS = 24
SUBLANES = 8


def _batcher_network(n):
    def merge(lo, hi, r):
        step = r * 2
        if step < hi - lo:
            yield from merge(lo, hi, step)
            yield from merge(lo + r, hi, step)
            yield from [(i, i + r) for i in range(lo + r, hi - r, step)]
        else:
            yield (lo, lo + r)

    def sort(lo, hi):
        if hi - lo >= 1:
            mid = lo + (hi - lo) // 2
            yield from sort(lo, mid)
            yield from sort(mid + 1, hi)
            yield from merge(lo, hi, 1)

    return tuple(sort(0, n - 1))


def _sort_levels(levels, net):
    lv = list(levels)
    for i, j in net:
        a, b = lv[i], lv[j]
        lv[i] = jnp.maximum(a, b)
        lv[j] = jnp.minimum(a, b)
    return lv


def _pop_sorted(levels, n_top, emit):
    lv = list(levels)
    for it in range(n_top):
        m = jnp.max(lv[0], axis=0, keepdims=True)
        emit(it, m)
        hit = lv[0] == m
        live = min(len(lv), n_top - it - 1)
        for k in range(live):
            below = lv[k + 1] if k + 1 < len(lv) else -jnp.inf
            lv[k] = jnp.where(hit, below, lv[k])


def _topk_heads(st_ref, th1_ref, p1_ref, p2_ref, s2_ref, v1_ref, v2_ref):
    tl = st_ref.shape[-1]
    neg = -jnp.inf
    net16 = _batcher_network(N_KEYS // SUBLANES)
    net8 = _batcher_network(SUBLANES)
    row = lax.broadcasted_iota(jnp.int32, (SUBLANES, tl), 0)
    roll = lambda x, sh: pltpu.roll(x, sh, 0)

    def top_values(x, store):
        store[...] = jnp.full(store.shape, neg, F32)
        lv = _sort_levels([x[SUBLANES * k:SUBLANES * (k + 1)] for k in range(N_KEYS // SUBLANES)], net16)

        def emit(it, m):
            store[it:it + 1, :] = m

        _pop_sorted(lv, N_TOP, emit)

    def head(h, carry):
        s1 = st_ref[pl.ds(pl.multiple_of(2 * h * N_KEYS, N_KEYS), N_KEYS), :]
        s2 = st_ref[pl.ds(pl.multiple_of((2 * h + 1) * N_KEYS, N_KEYS), N_KEYS), :]
        top_values(s1, v1_ref)
        top_values(s2, v2_ref)
        one = lambda ref, a: ref[a:a + 1, :]
        v2a = v2_ref[0:8, :]
        v1b2 = roll(v1_ref[8:16, :], 2)
        cands = [one(v1_ref, 0) + v2a, one(v1_ref, 0) + v2_ref[8:16, :], one(v1_ref, 0) + v2_ref[16:24, :],
                 one(v1_ref, 1) + v2a,
                 jnp.where(row < 5, one(v1_ref, 2), one(v1_ref, 4)) + jnp.where(row < 5, v2a, roll(v2a, 5)),
                 jnp.where(row < 4, one(v1_ref, 3), jnp.where(row < 6, one(v1_ref, 5), one(v1_ref, 6)))
                 + jnp.where(row < 4, v2a, jnp.where(row < 6, roll(v2a, 4), roll(v2a, 6))),
                 jnp.where(row < 2, one(v1_ref, 7), v1b2) + jnp.where(row < 2, v2a, one(v2_ref, 0)),
                 jnp.where(row < 2, v1b2, roll(v1_ref[16:24, :], 2)) + one(v2_ref, 0)]
        top = []
        _pop_sorted(_sort_levels(cands, net8), N_TOP, lambda it, m: top.append(m))
        tau = 0.5 * (top[PEER_TOPK - 1] + top[PEER_TOPK])
        mx1 = v1_ref[0:1, :]
        mx2 = v2_ref[0:1, :]
        zacc = jnp.zeros((SUBLANES, tl), F32)
        for cnd in cands:
            zacc = zacc + jnp.where(cnd >= tau, jnp.exp(cnd - (mx1 + mx2)), 0.0)
        zsum = jnp.sum(zacc, axis=0, keepdims=True)
        th1 = tau - s1
        p1 = jnp.exp(s1 - mx1) / zsum
        p2 = jnp.exp(s2 - mx2)
        for cb in range(tl // LANES):
            cs = slice(cb * LANES, (cb + 1) * LANES)
            th1_ref[cb, h] = th1[:, cs]
            p1_ref[cb, h] = p1[:, cs]
            p2_ref[cb, h] = p2[:, cs]
            s2_ref[cb, h] = s2[:, cs]
        return carry

    lax.fori_loop(0, PEER_HEADS, head, 0)


def _peer_kernel(tab_ref, h2t_ref, s2_ref, p2_ref, th1a_ref, p1a_ref, th1b_ref, p1b_ref, u_ref, vt_ref, x1_ref, m_ref,
                 g2_ref, b2_ref, o_ref, acc_ref, act_ref, ga_ref):
    s = pl.program_id(1)
    n_steps = pl.num_programs(1)
    n_blk = PEER_TM // LANES
    rows_per_tile = PEER_TE // N_KEYS
    n_iter = n_blk // BLK_PER_ITER
    d_rows = acc_ref.shape[0] // n_iter
    u_rows = PEER_TE // n_iter

    def project(half, q):
        ro = pl.multiple_of(q * u_rows, u_rows)
        return _dot(u_ref[pl.ds(half * PEER_TE + ro, u_rows), :], h2t_ref[...])

    def store_act(slot, q, a):
        ro = pl.multiple_of(q * u_rows, u_rows)
        for k in range(n_blk):
            act_ref[slot, k, pl.ds(ro, u_rows), :] = a[:, k * LANES:(k + 1) * LANES]

    def apply_v(slot, half, q):
        ro = pl.multiple_of(q * d_rows, d_rows)
        ga = jnp.concatenate([ga_ref[slot, k] for k in range(n_blk)], axis=1)
        return _dot(vt_ref[pl.ds(ro, d_rows), half * PEER_TE:(half + 1) * PEER_TE], ga)

    def stage(cur, th1_ref, p1_ref):
        nxt = 1 - cur

        def body(it, carry):
            a_next = project(nxt, it)
            v_prev = apply_v(nxt, nxt, it)
            for sub in range(BLK_PER_ITER):
                cb = it * BLK_PER_ITER + sub
                th = [th1_ref[cb, h] for h in range(PEER_HEADS)]
                p1 = [p1_ref[cb, h] for h in range(PEER_HEADS)]
                for rg in range(0, rows_per_tile, ROW_GROUP):
                    for part in range(N_KEYS // KEY_CHUNK):
                        ks = slice(part * KEY_CHUNK, (part + 1) * KEY_CHUNK)
                        gates = [None] * ROW_GROUP
                        for h in range(PEER_HEADS):
                            s2 = s2_ref[cb, h, ks, :]
                            p2 = p2_ref[cb, h, ks, :]
                            for g in range(ROW_GROUP):
                                r = rg + g
                                term = jnp.where(s2 >= th[h][r:r + 1], p2, 0.0) * p1[h][r:r + 1]
                                gates[g] = term if gates[g] is None else gates[g] + term
                        for g in range(ROW_GROUP):
                            lo = (rg + g) * N_KEYS + part * KEY_CHUNK
                            rs = slice(lo, lo + KEY_CHUNK)
                            ga_ref[cur, cb, rs, :] = (gates[g] * _gelu(act_ref[cur, cb, rs, :])).astype(BF16)
            store_act(nxt, it, a_next)
            ro = pl.multiple_of(it * d_rows, d_rows)
            acc_ref[pl.ds(ro, d_rows), :] += v_prev
            return carry

        lax.fori_loop(0, n_iter, body, 0)

    @pl.when(s == 0)
    def _():
        acc_ref[...] = jnp.zeros(acc_ref.shape, F32)
        ga_ref[...] = jnp.zeros(ga_ref.shape, BF16)
        for q in range(n_iter):
            store_act(0, q, project(0, q))

    @pl.when(s > 0)
    def _():
        stage(1, th1a_ref, p1a_ref)

    @pl.when(s < n_steps - 1)
    def _():
        stage(0, th1b_ref, p1b_ref)

    @pl.when(s == n_steps - 1)
    def _():
        m = m_ref[0]
        for q in range(n_iter):
            acc_ref[q * d_rows:(q + 1) * d_rows, :] += apply_v(1, 1, q)
        ffn = acc_ref[...].T
        o_ref[...] = _ln(ALPHA * x1_ref[...] + m[5:6] * ffn) * g2_ref[...] + b2_ref[...]


def _peer(x1, tab, h2t, s2, p2, th1, p1, u_bf, vt_bf, mods, g2, b2):
    t, d = x1.shape
    tm = PEER_TM
    te = PEER_TE
    n_tiles = u_bf.shape[0] // te
    per = tm // ROW_TILE
    n_blk = tm // LANES
    hk = pl.BlockSpec((n_blk, PEER_HEADS, N_KEYS, LANES), lambda i, j, t_: (i, 0, 0, 0))
    rows = te // N_KEYS
    rows_a = pl.BlockSpec((n_blk, PEER_HEADS, rows, LANES), lambda i, j, t_: (i, 0, jnp.maximum(2 * j - 1, 0), 0))
    rows_b = pl.BlockSpec((n_blk, PEER_HEADS, rows, LANES), lambda i, j, t_: (i, 0, jnp.minimum(2 * j, n_tiles - 1), 0))
    in_specs = [pl.BlockSpec((d, tm), lambda i, j, t_: (0, i)),
                hk, hk, rows_a, rows_a, rows_b, rows_b,
                pl.BlockSpec((2 * te, d), lambda i, j, t_: (jnp.minimum(j, n_tiles // 2 - 1), 0)),
                pl.BlockSpec((d, 2 * te), lambda i, j, t_: (0, jnp.maximum(j - 1, 0))),
                pl.BlockSpec((tm, d), lambda i, j, t_: (i, 0)),
                pl.BlockSpec((1, 6, d), lambda i, j, t_: (t_[0, i * per], 0, 0)),
                pl.BlockSpec(g2.shape, lambda i, j, t_: (0, 0)),
                pl.BlockSpec(b2.shape, lambda i, j, t_: (0, 0))]
    return pl.pallas_call(
        _peer_kernel,
        grid_spec=pltpu.PrefetchScalarGridSpec(
            num_scalar_prefetch=1, grid=(t // tm, n_tiles // 2 + 1), in_specs=in_specs,
            out_specs=pl.BlockSpec((tm, d), lambda i, j, t_: (i, 0)),
            scratch_shapes=[pltpu.VMEM((d, tm), F32), pltpu.VMEM((2, n_blk, te, LANES), F32),
                            pltpu.VMEM((2, n_blk, te, LANES), BF16)]),
        out_shape=jax.ShapeDtypeStruct((t, d), F32),
        compiler_params=_cparams(("arbitrary", "arbitrary")),
        name="peer_mix",
    )(tab, h2t, s2, p2, th1, p1, th1, p1, u_bf, vt_bf, x1, mods, g2, b2)


def _sincos(pos, dim):
    omega = 1.0 / (10000.0 ** (jnp.arange(dim // 2, dtype=F32) / (dim // 2)))
    ang = pos.astype(F32)[:, None] * omega[None, :]
    return jnp.concatenate([jnp.sin(ang), jnp.cos(ang)], -1)


def _grid_pos_embed(n_tok):
    rows = n_tok // GRID_W
    half = D_MODEL // 2
    er = _sincos(jnp.arange(rows), half)
    ec = _sincos(jnp.arange(GRID_W), half)
    emb = jnp.concatenate([jnp.broadcast_to(er[:, None, :], (rows, GRID_W, half)),
                           jnp.broadcast_to(ec[None, :, :], (rows, GRID_W, half))], -1)
    return emb.reshape(rows * GRID_W, D_MODEL)


def _seq_dft_tables(s_len):
    r = 1
    while r * r < s_len:
        r *= 2
    k = jnp.arange(s_len, dtype=jnp.int32)
    w = 2.0 * math.pi / s_len
    ang_a = ((jnp.arange(s_len // r, dtype=jnp.int32)[:, None] * r * k[None, :]) % s_len).astype(F32) * w
    ang_b = ((jnp.arange(r, dtype=jnp.int32)[:, None] * k[None, :]) % s_len).astype(F32) * w
    ca, sa = jnp.cos(ang_a)[:, None, :], jnp.sin(ang_a)[:, None, :]
    cb, sb = jnp.cos(ang_b)[None, :, :], jnp.sin(ang_b)[None, :, :]
    scale = 1.0 / math.sqrt(s_len * FNET_GW)
    cmat = ((ca * cb - sa * sb) * scale).reshape(s_len, s_len)
    smat = ((sa * cb + ca * sb) * (-scale)).reshape(s_len, s_len)
    return cmat.astype(BF16), smat.astype(BF16)


def _channel_dft_table():
    n = np.arange(FNET_GW)
    ang = ((n[:, None] * n[None, :]) % FNET_GW) * (2.0 * np.pi / FNET_GW)
    wc = np.zeros((FNET_W, 2 * FNET_W), np.float32)
    for g in range(FNET_GROUPS):
        sl = slice(g * FNET_GW, (g + 1) * FNET_GW)
        wc[sl, sl] = np.cos(ang)
        wc[sl, FNET_W + g * FNET_GW:FNET_W + (g + 1) * FNET_GW] = np.sin(ang)
    return jnp.asarray(wc, BF16)


def _segment_ones():
    idx = np.arange(WKV_W) // WKV_N
    return jnp.asarray((idx[:, None] == idx[None, :]).astype(np.float32), BF16)


def _chunk_tri():
    i = np.arange(ROW_TILE)
    same = (i[:, None] // CHUNK) == (i[None, :] // CHUNK)
    lower = same & (i[None, :] <= i[:, None])
    upper = same & (i[None, :] >= i[:, None])
    return jnp.asarray(np.stack([lower, upper]).astype(np.float32), BF16)


def _scan_masks():
    i = np.arange(PAIR)
    same = (i[:, None] // CHUNK) == (i[None, :] // CHUNK)
    t_row = i[:, None] % CHUNK
    t_col = i[None, :] % CHUNK
    masks = [same & (t_col < t_row), same & (t_col <= t_row), same & (t_col > t_row), same & (t_col >= t_row),
             np.eye(PAIR, dtype=bool)]
    return jnp.asarray(np.stack(masks).astype(np.float32))


def _block_diag(w):
    h, n, _ = w.shape
    eye = jnp.eye(h, dtype=w.dtype)
    return jnp.einsum("hij,hg->higj", w, eye).reshape(h * n, h * n)


def _pad_rank(w, d):
    r = w.shape[1]
    z = jnp.zeros_like(w[d])
    return jnp.concatenate([w[d], z] if d == 0 else [z, w[d]], axis=0)


def _state_to_pairs(s0):
    b = s0.shape[0]
    st = jnp.swapaxes(s0, -1, -2).reshape(b, 2, N_PAIRS, 2, WKV_N, WKV_N)
    eye = jnp.eye(2, dtype=s0.dtype)
    return jnp.einsum("bdpeji,ef->bdpejfi", st, eye).reshape(b, 2, N_PAIRS, PAIR, PAIR)


def _pairs_to_state(m):
    b = m.shape[0]
    m7 = m.reshape(b, 2, N_PAIRS, 2, WKV_N, 2, WKV_N)
    st = jnp.stack([m7[:, :, :, 0, :, 0, :], m7[:, :, :, 1, :, 1, :]], axis=3)
    return jnp.swapaxes(st.reshape(b, 2, WKV_H, WKV_N, WKV_N), -1, -2)


def _tile_table(groups):
    cv, first, last = [], [], []
    row = 0
    for gi, (n_seq, s_len) in enumerate(groups):
        nt = s_len // ROW_TILE
        for b in range(n_seq):
            for i in range(nt):
                cv.append(0 if gi == 0 else 1 + b)
                first.append(int(i == 0))
                last.append(int(i == nt - 1))
        row += n_seq * s_len
    return jnp.asarray(np.array([cv, first, last], np.int32))


def kernel(x_prompt, x_sample, state_lru, state_wkv, c, c_ctx, w_mod, b_mod, ln1_g, ln1_b, ln2_g, ln2_b,
           w_in_e, w_out_e, sconv_w, sconv_b, w_in_o, w_out_o, lru_conv_w, lru_conv_b, lru_wa, lru_ba,
           lru_wx, lru_bx, lru_lambda, wkv_mu, wkv_w0, wkv_w2, wkv_a0, wkv_a2, wkv_kk, wkv_ka, wkv_rk,
           wkv_g2, wkv_gn_g, wkv_gn_b, peer_wq, peer_keys, peer_u, peer_v):
    bp, sp, d = x_prompt.shape
    bs, ss, _ = x_sample.shape
    depth = w_mod.shape[0]
    assert sp % ROW_TILE == 0 and ss % PEER_TM == 0 and (bp * sp) % ss == 0
    assert bs + 1 <= 8
    groups = ((bp, sp), (bs, ss))
    tp = bp * sp
    tab = _tile_table(groups)

    cv8 = jnp.concatenate([c_ctx[None, :], c, jnp.zeros((8 - 1 - bs, d), F32)], axis=0)
    mods = _modulation(cv8, w_mod, b_mod).reshape(depth, 8, 6, d)

    x = _embed(x_prompt.reshape(tp, d), x_sample.reshape(bs * ss, d), _grid_pos_embed(ss).astype(x_sample.dtype))

    wc = _channel_dft_table()
    dft = {s: _seq_dft_tables(s) for s in sorted({sp, ss})}
    seg = _segment_ones()
    tri = _chunk_tri()
    masks = _scan_masks()
    row2 = lambda a: a.reshape(1, -1)

    lru_fin = []
    wkv_fin = []
    for l in range(depth):
        j = l // 2
        m_l = mods[l]
        wq = peer_wq[l].astype(BF16)
        keys = peer_keys[l].reshape(2 * PEER_HEADS, N_KEYS, PEER_HALF).astype(BF16)
        if l % 2 == 0:
            zr, pq = _inproj(x, tab, m_l, w_in_e[j].astype(BF16), wc)
            y = jnp.concatenate([_seqdft(pq, bp * sp, 0, bp, sp, *dft[sp]),
                                 _seqdft(pq, bs * ss, tp, bs, ss, *dft[ss])], axis=0)
            x1, h2t, th1, p1, p2, s2 = _post_even(x, tab, zr, y, m_l, sconv_w[j], row2(sconv_b[j]), w_out_e[j].astype(BF16),
                                     row2(ln1_g[l]), row2(ln1_b[l]), wq, keys)
        else:
            zl, zw = _inproj(x, tab, m_l, w_in_o[j].astype(BF16))
            lru_args = (lru_conv_w[j], lru_conv_b[j],
                        jnp.stack([_block_diag(lru_wa[j, dd]) for dd in range(2)]).astype(BF16), lru_ba[j],
                        jnp.stack([_block_diag(lru_wx[j, dd]) for dd in range(2)]).astype(BF16), lru_bx[j],
                        lru_lambda[j])
            hf_p, hb_p, hfin_p = _lru(zl, 0, bp, sp, jnp.zeros((bp, 2, LRU_W), F32), *lru_args)
            hf_s, hb_s, _ = _lru(zl, tp, bs, ss, state_lru[:, j], *lru_args)
            lru_fin.append(hfin_p)
            prep = _wkvprep(zw, tab, row2(wkv_mu[j]), wkv_w0[j],
                            jnp.stack([_pad_rank(wkv_w2[j], dd) for dd in range(2)]).astype(BF16), wkv_a0[j],
                            jnp.stack([_pad_rank(wkv_a2[j], dd) for dd in range(2)]).astype(BF16),
                            wkv_kk[j], wkv_ka[j], row2(wkv_rk[j]), wkv_g2[j].astype(BF16), seg, tri)
            v, bonus, g = prep[0], prep[1], prep[2]
            yf_p, yb_p, mfin_p = _wkvscan(prep[3:], v, 0, bp, sp,
                                          jnp.zeros((bp, 2, N_PAIRS, PAIR, PAIR), F32), masks)
            yf_s, yb_s, _ = _wkvscan(prep[3:], v, tp, bs, ss, _state_to_pairs(state_wkv[:, j]), masks)
            wkv_fin.append(_pairs_to_state(mfin_p))
            x1, h2t, th1, p1, p2, s2 = _post_odd(x, tab, (hf_p, hf_s), (hb_p, hb_s), zl, (yf_p, yf_s), (yb_p, yb_s),
                                    bonus, g, m_l, seg, row2(wkv_gn_g[j]), row2(wkv_gn_b[j]),
                                    w_out_o[j].astype(BF16), row2(ln1_g[l]), row2(ln1_b[l]), wq, keys)
        x = _peer(x1, tab, h2t, s2, p2, th1, p1, peer_u[l].astype(BF16), peer_v[l].T.astype(BF16), m_l,
                  row2(ln2_g[l]), row2(ln2_b[l]))

    y_prompt = x[:tp].reshape(bp, sp, d)
    y_sample = x[tp:].reshape(bs, ss, d)
    return (y_prompt, y_sample, jnp.stack(lru_fin, 1).astype(x_prompt.dtype),
            jnp.stack(wkv_fin, 1).astype(x_prompt.dtype))
```

```python
import functools
import math

import numpy as np
import jax
import jax.numpy as jnp
from jax import lax
from jax.experimental import pallas as pl
from jax.experimental.pallas import tpu as pltpu

F32 = jnp.float32
BF16 = jnp.bfloat16

D_MODEL = 1024
GRID_W = 64
FNET_W = 512
FNET_GROUPS = 4
FNET_GW = FNET_W // FNET_GROUPS
CONV_W = 512
LRU_W = 512
LRU_HEADS = 8
LRU_CONV_K = 4
LRU_C = 8.0
WKV_W = 512
WKV_N = 64
WKV_H = 8
WKV_IN = 1920
DECAY_SCALE = math.exp(-0.5)
WKV_GN_EPS = 64e-5
PEER_HEADS = 8
N_KEYS = 128
PEER_TOPK = 16
PEER_HALF = 128
DEPTH = 4
ALPHA = (2 * DEPTH) ** 0.25
LN_EPS = 1e-6

ROW_TILE = 256
LANES = 128
HALO = 8
CHUNK = 64
PAIR = 2 * WKV_N
N_PAIRS = WKV_H // 2
PEER_TM = 512
PEER_TE = 1024
ROW_GROUP = 4
KEY_CHUNK = 32
BLK_PER_ITER = 2
VMEM_LIMIT = 56 * 1024 * 1024


def _cparams(sem):
    return pltpu.CompilerParams(dimension_semantics=sem, vmem_limit_bytes=VMEM_LIMIT)


def _dot(a, b):
    return jnp.dot(a, b, preferred_element_type=F32)


def _dot_nt(a, b):
    return lax.dot_general(a, b, (((1,), (1,)), ((), ())), preferred_element_type=F32)


def _mm1(a, b):
    return _dot(a.astype(BF16), b.astype(BF16))


def _mm01(a, b01):
    h = a.astype(BF16)
    r = a - h.astype(F32)
    m = r.astype(BF16)
    lo = (r - m.astype(F32)).astype(BF16)
    return _dot(h, b01) + (_dot(m, b01) + _dot(lo, b01))


def _ln(x):
    mu = jnp.mean(x, axis=-1, keepdims=True)
    xc = x - mu
    var = jnp.mean(xc * xc, axis=-1, keepdims=True)
    return xc * lax.rsqrt(var + LN_EPS)


def _gelu(x):
    z = x * (0.7978845608028654 + 0.035677408136300125 * (x * x))
    hx = 0.5 * x
    return hx + hx * jnp.tanh(z)


def _sigmoid(x):
    return 1.0 / (1.0 + jnp.exp(-x))


def _mod_kernel(c_ref, w_ref, b_ref, o_ref):
    c = c_ref[...]
    sc = c * _sigmoid(c)
    o_ref[0] = _dot(sc.astype(BF16), w_ref[0].astype(BF16)) + b_ref[0]


def _modulation(cv8, w_mod, b_mod):
    depth, d, n = w_mod.shape
    tn = 1536
    return pl.pallas_call(
        _mod_kernel,
        grid=(depth, n // tn),
        in_specs=[pl.BlockSpec((8, d), lambda l, j: (0, 0)),
                  pl.BlockSpec((1, d, tn), lambda l, j: (l, 0, j)),
                  pl.BlockSpec((1, 1, tn), lambda l, j: (l, 0, j))],
        out_specs=pl.BlockSpec((1, 8, tn), lambda l, j: (l, 0, j)),
        out_shape=jax.ShapeDtypeStruct((depth, 8, n), F32),
        compiler_params=_cparams(("arbitrary", "arbitrary")),
        name="modulation",
    )(cv8, w_mod, b_mod.reshape(depth, 1, n))


def _embed_kernel(xp_ref, xs_ref, p_ref, o_ref, *, n_prompt):
    i = pl.program_id(0)

    @pl.when(i < n_prompt)
    def _():
        o_ref[...] = xp_ref[...]

    @pl.when(i >= n_prompt)
    def _():
        o_ref[...] = xs_ref[...] + p_ref[...]


def _embed(xp, xs, pos):
    tp, d = xp.shape
    ts = xs.shape[0]
    s_len = pos.shape[0]
    tm = 512
    n_prompt = tp // tm
    per_seq = s_len // tm
    return pl.pallas_call(
        functools.partial(_embed_kernel, n_prompt=n_prompt),
        grid=((tp + ts) // tm,),
        in_specs=[pl.BlockSpec((tm, d), lambda i: (jnp.minimum(i, n_prompt - 1), 0)),
                  pl.BlockSpec((tm, d), lambda i: (jnp.maximum(i - n_prompt, 0), 0)),
                  pl.BlockSpec((tm, d), lambda i: (jnp.maximum(i - n_prompt, 0) % per_seq, 0))],
        out_specs=pl.BlockSpec((tm, d), lambda i: (i, 0)),
        out_shape=jax.ShapeDtypeStruct((tp + ts, d), xp.dtype),
        compiler_params=_cparams(("arbitrary",)),
        name="embed",
    )(xp, xs, pos)


def _inproj_even_kernel(tab_ref, x_ref, m_ref, w_ref, wc_ref, zr_ref, pq_ref):
    m = m_ref[0]
    h = _ln(x_ref[...]) * (1.0 + m[1:2]) + m[0:1]
    z = _dot(h.astype(BF16), w_ref[...])
    zr_ref[...] = z[:, FNET_W:]
    pq_ref[...] = _dot(z[:, :FNET_W].astype(BF16), wc_ref[...]).astype(BF16)


def _inproj_odd_kernel(tab_ref, x_ref, m_ref, w_ref, zl_ref, zw_ref):
    m = m_ref[0]
    h = _ln(x_ref[...]) * (1.0 + m[1:2]) + m[0:1]
    z = _dot(h.astype(BF16), w_ref[...])
    zl_ref[...] = z[:, :2 * LRU_W]
    zw_ref[...] = z[:, 2 * LRU_W:]


def _const_spec(shape):
    nd = len(shape)
    return pl.BlockSpec(shape, lambda i, t, _n=nd: (0,) * _n)


def _inproj(x, tab, mods, w_in, wc=None):
    t, d = x.shape
    n = w_in.shape[1]
    tm = ROW_TILE
    even = wc is not None
    in_specs = [pl.BlockSpec((tm, d), lambda i, t_: (i, 0)),
                pl.BlockSpec((1, 6, d), lambda i, t_: (t_[0, i], 0, 0)),
                _const_spec((d, n))]
    if even:
        in_specs.append(_const_spec(wc.shape))
        widths = (n - FNET_W, 2 * FNET_W)
        dtypes = (F32, BF16)
        kern = _inproj_even_kernel
        args = (tab, x, mods, w_in, wc)
    else:
        widths = (2 * LRU_W, n - 2 * LRU_W)
        dtypes = (F32, F32)
        kern = _inproj_odd_kernel
        args = (tab, x, mods, w_in)
    return pl.pallas_call(
        kern,
        grid_spec=pltpu.PrefetchScalarGridSpec(
            num_scalar_prefetch=1, grid=(t // tm,), in_specs=in_specs,
            out_specs=[pl.BlockSpec((tm, w), lambda i, t_: (i, 0)) for w in widths]),
        out_shape=[jax.ShapeDtypeStruct((t, w), dt) for w, dt in zip(widths, dtypes)],
        compiler_params=_cparams(("arbitrary",)),
        name="inproj_even" if even else "inproj_odd",
    )(*args)


def _seqdft_kernel(c_ref, s_ref, p_ref, q_ref, y_ref):
    y_ref[...] = (_dot(c_ref[...], p_ref[...]) + _dot(s_ref[...], q_ref[...])).astype(BF16)


def _seqdft(pq, y_rows, row0, n_seq, s_len, cmat, smat):
    tm = min(s_len, ROW_TILE)
    nt = s_len // tm
    blk0 = row0 // s_len
    return pl.pallas_call(
        _seqdft_kernel,
        grid=(n_seq, nt),
        in_specs=[pl.BlockSpec((tm, s_len), lambda b, i: (i, 0)),
                  pl.BlockSpec((tm, s_len), lambda b, i: (i, 0)),
                  pl.BlockSpec((s_len, FNET_W), lambda b, i: (blk0 + b, 0)),
                  pl.BlockSpec((s_len, FNET_W), lambda b, i: (blk0 + b, 1))],
        out_specs=pl.BlockSpec((tm, FNET_W), lambda b, i: (b * nt + i, 0)),
        out_shape=jax.ShapeDtypeStruct((y_rows, FNET_W), BF16),
        compiler_params=_cparams(("arbitrary", "arbitrary")),
        name="seqdft_%d" % s_len,
    )(cmat, smat, pq, pq)


def _post_and_query(x, mix, m, g1, b1, wq_ref, keys_ref, x1_ref, h2t_ref, topk_refs, st_ref, v1_ref, v2_ref):
    x1 = _ln(ALPHA * x + m[2:3] * mix) * g1 + b1
    x1_ref[...] = x1
    h2 = _ln(x1) * (1.0 + m[4:5]) + m[3:4]
    h2b = h2.astype(BF16)
    h2t_ref[...] = h2.T.astype(BF16)
    q = _dot(h2b, wq_ref[...])
    for hp in range(2 * PEER_HEADS):
        qb = q[:, hp * PEER_HALF:(hp + 1) * PEER_HALF].astype(BF16)
        st_ref[hp * N_KEYS:(hp + 1) * N_KEYS, :] = _dot_nt(keys_ref[hp], qb)
    _topk_heads(st_ref, *topk_refs, v1_ref, v2_ref)


def _post_even_kernel(tab_ref, x_ref, bg_ref, cg_ref, xi_ref, cgp_ref, xip_ref, cgn_ref, xin_ref, y_ref,
                      m_ref, cw_ref, cb_ref, wo_ref, g1_ref, b1_ref, wq_ref, keys_ref,
                      x1_ref, h2t_ref, th1_ref, p1_ref, p2_ref, s2_ref, ext_ref, st_ref, v1_ref, v2_ref):
    i = pl.program_id(0)
    tm = ROW_TILE
    first = tab_ref[1, i] == 1
    last = tab_ref[2, i] == 1
    u = cg_ref[...] * xi_ref[...]
    ext_ref[0:HALO] = jnp.where(first, 0.0, cgp_ref[...] * xip_ref[...])
    ext_ref[HALO:HALO + tm] = u
    ext_ref[HALO + tm:2 * HALO + tm] = jnp.where(last, 0.0, cgn_ref[...] * xin_ref[...])
    cw = cw_ref[...]
    conv = (ext_ref[HALO - 1:HALO - 1 + tm] * cw[0:1] + u * cw[1:2]
            + ext_ref[HALO + 1:HALO + 1 + tm] * cw[2:3] + cb_ref[...])
    ymix = (bg_ref[...] * conv).astype(BF16)
    mix = _dot(y_ref[...], wo_ref[0:FNET_W]) + _dot(ymix, wo_ref[FNET_W:])
    _post_and_query(x_ref[...], mix, m_ref[0], g1_ref[...], b1_ref[...], wq_ref, keys_ref,
                    x1_ref, h2t_ref, (th1_ref, p1_ref, p2_ref, s2_ref), st_ref, v1_ref, v2_ref)


def _post_out_specs(t, d, tm):
    rshape = jax.ShapeDtypeStruct((t // LANES, PEER_HEADS, N_KEYS, LANES), F32)
    rspec = pl.BlockSpec((tm // LANES, PEER_HEADS, N_KEYS, LANES), lambda i, t_: (i, 0, 0, 0))
    specs = [pl.BlockSpec((tm, d), lambda i, t_: (i, 0)),
             pl.BlockSpec((d, tm), lambda i, t_: (0, i))] + [rspec] * 4
    shapes = [jax.ShapeDtypeStruct((t, d), F32),
              jax.ShapeDtypeStruct((d, t), BF16)] + [rshape] * 4
    return specs, shapes


def _post_scratch(tm):
    return [pltpu.VMEM((2 * PEER_HEADS * N_KEYS, tm), F32), pltpu.VMEM((TOP_ROWS, tm), F32),
            pltpu.VMEM((TOP_ROWS, tm), F32)]


def _post_even(x, tab, zr, y, mods, cw, cb, wo, g1, b1, wq, keys):
    t, d = x.shape
    tm = ROW_TILE
    hb = tm // HALO
    nblk = t // HALO
    prev = lambda c: pl.BlockSpec((HALO, CONV_W), lambda i, t_, _c=c: (jnp.maximum(i * hb - 1, 0), _c))
    nxt = lambda c: pl.BlockSpec((HALO, CONV_W), lambda i, t_, _c=c: (jnp.minimum((i + 1) * hb, nblk - 1), _c))
    col = lambda c: pl.BlockSpec((tm, CONV_W), lambda i, t_, _c=c: (i, _c))
    in_specs = [pl.BlockSpec((tm, d), lambda i, t_: (i, 0)),
                col(0), col(1), col(2), prev(1), prev(2), nxt(1), nxt(2),
                pl.BlockSpec((tm, FNET_W), lambda i, t_: (i, 0)),
                pl.BlockSpec((1, 6, d), lambda i, t_: (t_[0, i], 0, 0)),
                _const_spec(cw.shape), _const_spec(cb.shape), _const_spec(wo.shape),
                _const_spec(g1.shape), _const_spec(b1.shape), _const_spec(wq.shape), _const_spec(keys.shape)]
    out_specs, out_shape = _post_out_specs(t, d, tm)
    return pl.pallas_call(
        _post_even_kernel,
        grid_spec=pltpu.PrefetchScalarGridSpec(
            num_scalar_prefetch=1, grid=(t // tm,), in_specs=in_specs, out_specs=out_specs,
            scratch_shapes=[pltpu.VMEM((tm + 2 * HALO, CONV_W), F32)] + _post_scratch(tm)),
        out_shape=out_shape,
        compiler_params=_cparams(("arbitrary",)),
        name="post_even",
    )(tab, x, zr, zr, zr, zr, zr, zr, zr, y, mods, cw, cb, wo, g1, b1, wq, keys)


def _lru_kernel(xf_ref, xfh_ref, xb_ref, xbh_ref, cw_ref, cb_ref, wa_ref, ba_ref, wx_ref, bx_ref, lam_ref,
                h0_ref, hf_ref, hb_ref, hfin_ref, ext_ref, carry_ref):
    i = pl.program_id(1)
    nt = pl.num_programs(1)
    tm = ROW_TILE

    @pl.when(i == 0)
    def _():
        carry_ref[...] = h0_ref[0]

    row = lax.broadcasted_iota(jnp.int32, (tm, 1), 0)
    for d in range(2):
        x = (xf_ref if d == 0 else xb_ref)[...]
        halo = jnp.where(i == 0, 0.0, (xfh_ref if d == 0 else xbh_ref)[...])
        ext_ref[HALO:HALO + tm] = x
        if d == 0:
            ext_ref[0:HALO] = halo
        else:
            ext_ref[HALO + tm:2 * HALO + tm] = halo
        cw = cw_ref[d]
        xc = cb_ref[d:d + 1] + x * cw[LRU_CONV_K - 1:LRU_CONV_K]
        for j in range(LRU_CONV_K - 1):
            k = LRU_CONV_K - 1 - j
            off = HALO - k if d == 0 else HALO + k
            xc = xc + ext_ref[off:off + tm] * cw[j:j + 1]
        xcb = xc.astype(BF16)
        gate_r = _sigmoid(_dot(xcb, wa_ref[d]) + ba_ref[d:d + 1])
        gate_i = _sigmoid(_dot(xcb, wx_ref[d]) + bx_ref[d:d + 1])
        nl = -lam_ref[d:d + 1]
        softplus = jnp.maximum(nl, 0.0) + jnp.log1p(jnp.exp(-jnp.abs(nl)))
        log_a = -LRU_C * gate_r * softplus
        a = jnp.exp(log_a)
        b = jnp.sqrt(-jnp.tanh(log_a) * (a * a + 1.0)) * (gate_i * xc)
        s = 1
        while s < tm:
            if d == 0:
                keep = row >= s
                sh = s
            else:
                keep = row < tm - s
                sh = tm - s
            a_sh = jnp.where(keep, pltpu.roll(a, sh, 0), 1.0)
            b_sh = jnp.where(keep, pltpu.roll(b, sh, 0), 0.0)
            b = a * b_sh + b
            a = a * a_sh
            s *= 2
        h = a * carry_ref[d:d + 1] + b
        if d == 0:
            hf_ref[...] = h
            carry_ref[0:1] = h[tm - 1:tm]
        else:
            hb_ref[...] = h
            carry_ref[1:2] = h[0:1]

    @pl.when(i == nt - 1)
    def _():
        hfin_ref[0] = carry_ref[...]


def _lru(zl, row0, n_seq, s_len, h0, cw, cb, wa, ba, wx, bx, lam):
    tm = ROW_TILE
    nt = s_len // tm
    t0 = row0 // tm
    hb = tm // HALO
    nblk = zl.shape[0] // HALO
    rows = n_seq * s_len
    fwd = lambda b, i: (t0 + b * nt + i, 0)
    bwd = lambda b, i: (t0 + b * nt + nt - 1 - i, 0)
    fwd_h = lambda b, i: (jnp.maximum((t0 + b * nt + i) * hb - 1, 0), 0)
    bwd_h = lambda b, i: (jnp.minimum((t0 + b * nt + nt - i) * hb, nblk - 1), 0)
    out_f = lambda b, i: (b * nt + i, 0)
    out_b = lambda b, i: (b * nt + nt - 1 - i, 0)
    cst = lambda a: pl.BlockSpec(a.shape, lambda b, i, _n=a.ndim: (0,) * _n)
    return pl.pallas_call(
        _lru_kernel,
        grid=(n_seq, nt),
        in_specs=[pl.BlockSpec((tm, LRU_W), fwd), pl.BlockSpec((HALO, LRU_W), fwd_h),
                  pl.BlockSpec((tm, LRU_W), bwd), pl.BlockSpec((HALO, LRU_W), bwd_h),
                  cst(cw), cst(cb), cst(wa), cst(ba), cst(wx), cst(bx), cst(lam),
                  pl.BlockSpec((1, 2, LRU_W), lambda b, i: (b, 0, 0))],
        out_specs=[pl.BlockSpec((tm, LRU_W), out_f), pl.BlockSpec((tm, LRU_W), out_b),
                   pl.BlockSpec((1, 2, LRU_W), lambda b, i: (b, 0, 0))],
        out_shape=[jax.ShapeDtypeStruct((rows, LRU_W), F32), jax.ShapeDtypeStruct((rows, LRU_W), F32),
                   jax.ShapeDtypeStruct((n_seq, 2, LRU_W), F32)],
        scratch_shapes=[pltpu.VMEM((tm + 2 * HALO, LRU_W), F32), pltpu.VMEM((2, LRU_W), F32)],
        compiler_params=_cparams(("arbitrary", "arbitrary")),
        name="lru_%d" % s_len,
    )(zl, zl, zl, zl, cw, cb, wa, ba, wx, bx, lam, h0)


def _wkvprep_kernel(tab_ref, z_ref, zp_ref, zn_ref, mu_ref, w0_ref, w2_ref, a0_ref, a2_ref, kk_ref, ka_ref,
                    rk_ref, g2_ref, seg_ref, tri_ref,
                    v_ref, bg_ref, g_ref, rf_ref, kf_ref, bf_ref, qf_ref, cf_ref,
                    rb_ref, kb_ref, bb_ref, qb_ref, cb_ref, ext_ref):
    i = pl.program_id(0)
    tm = ROW_TILE
    w = WKV_W
    first = tab_ref[1, i] == 1
    last = tab_ref[2, i] == 1
    z = z_ref[...]
    ext_ref[0:HALO] = jnp.where(first, 0.0, zp_ref[...])
    ext_ref[HALO:HALO + tm] = z
    ext_ref[HALO + tm:2 * HALO + tm] = jnp.where(last, 0.0, zn_ref[...])
    z = z + mu_ref[...] * (0.5 * (ext_ref[HALO - 1:HALO - 1 + tm] + ext_ref[HALO + 1:HALO + 1 + tm]) - z)
    r = z[:, 0:w]
    k = z[:, w:2 * w]
    v = z[:, 2 * w:3 * w]
    wd = jnp.tanh(z[:, 3 * w:3 * w + 128]).astype(BF16)
    ad = z[:, 3 * w + 128:3 * w + 256].astype(BF16)
    gd = _sigmoid(z[:, 3 * w + 256:3 * w + 384]).astype(BF16)
    v_ref[...] = v
    g_ref[...] = _dot(gd, g2_ref[...])
    seg = seg_ref[...]
    rk = rk_ref[...]
    bonus = jnp.zeros((tm, w), F32)
    outs = ((rf_ref, kf_ref, bf_ref, qf_ref, cf_ref), (rb_ref, kb_ref, bb_ref, qb_ref, cb_ref))
    nch = tm // CHUNK
    for d in range(2):
        r_ref, k_ref, b_ref, q_ref, c_ref = outs[d]
        wz = w0_ref[d:d + 1] + _dot(wd, w2_ref[d])
        logw = -DECAY_SCALE * _sigmoid(wz)
        iclr = _sigmoid(a0_ref[d:d + 1] + _dot(ad, a2_ref[d]))
        kk = k * kk_ref[d:d + 1]
        kk = kk * lax.rsqrt(jnp.maximum(_mm01(kk * kk, seg), 1e-24))
        km = k * (1.0 + (iclr - 1.0) * ka_ref[d:d + 1])
        bonus = bonus + _mm1(r * km * rk, seg) * v
        lw_h = logw.astype(BF16)
        lw_r = logw - lw_h.astype(F32)
        lw_m = lw_r.astype(BF16)
        lw_l = (lw_r - lw_m.astype(F32)).astype(BF16)
        tri = tri_ref[d]
        cl = _dot(tri, lw_h) + (_dot(tri, lw_m) + _dot(tri, lw_l))
        c = jnp.exp(cl)
        cinv = jnp.exp(-cl)
        r_ref[...] = r * c
        k_ref[...] = km * cinv
        b_ref[...] = kk * iclr * cinv
        q_ref[...] = kk * jnp.exp(cl - logw)
        for j in range(nch):
            edge = (j + 1) * CHUNK - 1 if d == 0 else j * CHUNK
            c_ref[j] = c[edge:edge + 1]
    bg_ref[...] = bonus


def _wkvprep(zw, tab, mu, w0, w2p, a0, a2p, kk, ka, rk, g2, seg, tri):
    t, n = zw.shape
    tm = ROW_TILE
    hb = tm // HALO
    nblk = t // HALO
    nch = tm // CHUNK
    w = WKV_W
    in_specs = [pl.BlockSpec((tm, n), lambda i, t_: (i, 0)),
                pl.BlockSpec((HALO, n), lambda i, t_: (jnp.maximum(i * hb - 1, 0), 0)),
                pl.BlockSpec((HALO, n), lambda i, t_: (jnp.minimum((i + 1) * hb, nblk - 1), 0))]
    in_specs += [_const_spec(a.shape) for a in (mu, w0, w2p, a0, a2p, kk, ka, rk, g2, seg, tri)]
    row = pl.BlockSpec((tm, w), lambda i, t_: (i, 0))
    cspec = pl.BlockSpec((nch, 1, w), lambda i, t_: (i, 0, 0))
    rshape = jax.ShapeDtypeStruct((t, w), F32)
    cshape = jax.ShapeDtypeStruct((t // CHUNK, 1, w), F32)
    out_specs = [row, row, row] + [row, row, row, row, cspec] * 2
    out_shape = [rshape, rshape, rshape] + [rshape, rshape, rshape, rshape, cshape] * 2
    return pl.pallas_call(
        _wkvprep_kernel,
        grid_spec=pltpu.PrefetchScalarGridSpec(
            num_scalar_prefetch=1, grid=(t // tm,), in_specs=in_specs, out_specs=out_specs,
            scratch_shapes=[pltpu.VMEM((tm + 2 * HALO, n), F32)]),
        out_shape=out_shape,
        compiler_params=_cparams(("arbitrary",)),
        name="wkv_prep",
    )(tab, zw, zw, zw, mu, w0, w2p, a0, a2p, kk, ka, rk, g2, seg, tri)


def _wkv_chunks(items, eye):
    lane = lax.broadcasted_iota(jnp.int32, (1, PAIR), 1)
    m0 = (lane < WKV_N).astype(F32)
    m1 = 1.0 - m0
    stack = lambda x: jnp.concatenate([x * m0, x * m1], axis=0)
    n = range(len(items))
    rh_s = [stack(it[0]) for it in items]
    kh_s = [stack(it[1]) for it in items]
    bh_s = [stack(it[2]) for it in items]
    kq_s = [stack(it[3]) for it in items]
    v_s = [stack(it[4]) for it in items]
    bh_t = [x.T for x in bh_s]
    kh_t = [x.T for x in kh_s]
    bk_t = [jnp.concatenate([bh_t[i], kh_t[i]], axis=1) for i in n]
    att = [_mm1(jnp.concatenate([kq_s[i], rh_s[i]], axis=0), bk_t[i]) for i in n]
    n1 = [att[i][0:PAIR, 0:PAIR] * items[i][7] for i in n]
    ak = [att[i][0:PAIR, PAIR:] * items[i][7] for i in n]
    gb = [att[i][PAIR:, 0:PAIR] * items[i][8] for i in n]
    gk = [att[i][PAIR:, PAIR:] * items[i][8] for i in n]
    t_inv = [eye - n1[i] for i in n]
    npow = n1
    for _ in range(5):
        npow = [_mm1(npow[i], npow[i]) for i in n]
        t_inv = [t_inv[i] + _mm1(t_inv[i], npow[i]) for i in n]
    kv = [_mm1(jnp.concatenate([kh_t[i], gk[i], ak[i]], axis=0), v_s[i]) for i in n]
    x = [_mm1(t_inv[i], jnp.concatenate([kq_s[i], kv[i][2 * PAIR:]], axis=1)) for i in n]
    bx = [_mm1(jnp.concatenate([bh_t[i], gb[i]], axis=0), x[i]) for i in n]
    p_mat = [eye - bx[i][0:PAIR, 0:PAIR] for i in n]
    q_mat = [kv[i][0:PAIR] - bx[i][0:PAIR, PAIR:] for i in n]
    r_til = [rh_s[i] - bx[i][PAIR:, 0:PAIR] for i in n]
    y0 = [kv[i][PAIR:2 * PAIR] - bx[i][PAIR:, PAIR:] for i in n]
    sm = [_mm1(jnp.concatenate([r_til[i], p_mat[i]], axis=0), items[i][6]) for i in n]
    y_st = [sm[i][0:PAIR] + y0[i] for i in n]
    ys = [y_st[i][0:CHUNK] + y_st[i][CHUNK:] for i in n]
    c_col = [jnp.broadcast_to(items[i][5], (PAIR, PAIR)).T for i in n]
    m_new = [(sm[i][PAIR:] + q_mat[i]) * c_col[i] for i in n]
    return ys, m_new


def _wkvscan_kernel(rf_ref, kf_ref, bf_ref, qf_ref, vf_ref, cf_ref, rb_ref, kb_ref, bb_ref, qb_ref, vb_ref, cb_ref,
                    m0_ref, msk_ref, yf_ref, yb_ref, mfin_ref, m_ref):
    i = pl.program_id(1)
    nc = pl.num_programs(1)

    @pl.when(i == 0)
    def _():
        m_ref[...] = m0_ref[0]

    eye = msk_ref[4]
    ins = ((rf_ref, kf_ref, bf_ref, qf_ref, vf_ref, cf_ref, yf_ref),
           (rb_ref, kb_ref, bb_ref, qb_ref, vb_ref, cb_ref, yb_ref))
    items = []
    for d in range(2):
        r_ref, k_ref, b_ref, q_ref, v_ref, c_ref, _ = ins[d]
        for p in range(N_PAIRS):
            sl = slice(p * PAIR, (p + 1) * PAIR)
            items.append((r_ref[:, sl], k_ref[:, sl], b_ref[:, sl], q_ref[:, sl], v_ref[:, sl],
                          c_ref[0, :, sl], m_ref[d, p], msk_ref[2 * d], msk_ref[2 * d + 1]))
    ys, m_new = _wkv_chunks(items, eye)
    for d in range(2):
        for p in range(N_PAIRS):
            sl = slice(p * PAIR, (p + 1) * PAIR)
            ins[d][6][:, sl] = ys[d * N_PAIRS + p]
            m_ref[d, p] = m_new[d * N_PAIRS + p]

    @pl.when(i == nc - 1)
    def _():
        mfin_ref[0] = m_ref[...]


def _wkvscan(prep, v, row0, n_seq, s_len, m0, masks):
    rf, kf, bf, qf, cf, rb, kb, bb, qb, cb = prep
    nc = s_len // CHUNK
    c0 = row0 // CHUNK
    rows = n_seq * s_len
    w = WKV_W
    fwd = lambda b, i: (c0 + b * nc + i, 0)
    bwd = lambda b, i: (c0 + b * nc + nc - 1 - i, 0)
    fwd3 = lambda b, i: (c0 + b * nc + i, 0, 0)
    bwd3 = lambda b, i: (c0 + b * nc + nc - 1 - i, 0, 0)
    blk = lambda im: pl.BlockSpec((CHUNK, w), im)
    cblk = lambda im: pl.BlockSpec((1, 1, w), im)
    mspec = pl.BlockSpec((1, 2, N_PAIRS, PAIR, PAIR), lambda b, i: (b, 0, 0, 0, 0))
    return pl.pallas_call(
        _wkvscan_kernel,
        grid=(n_seq, nc),
        in_specs=[blk(fwd)] * 5 + [cblk(fwd3)] + [blk(bwd)] * 5 + [cblk(bwd3)]
        + [mspec, pl.BlockSpec(masks.shape, lambda b, i: (0, 0, 0))],
        out_specs=[pl.BlockSpec((CHUNK, w), lambda b, i: (b * nc + i, 0)),
                   pl.BlockSpec((CHUNK, w), lambda b, i: (b * nc + nc - 1 - i, 0)),
                   mspec],
        out_shape=[jax.ShapeDtypeStruct((rows, w), F32), jax.ShapeDtypeStruct((rows, w), F32),
                   jax.ShapeDtypeStruct(m0.shape, F32)],
        scratch_shapes=[pltpu.VMEM((2, N_PAIRS, PAIR, PAIR), F32)],
        compiler_params=_cparams(("arbitrary", "arbitrary")),
        name="wkv_scan_%d" % s_len,
    )(rf, kf, bf, qf, v, cf, rb, kb, bb, qb, v, cb, m0, masks)


def _post_odd_kernel(tab_ref, x_ref, hfp_ref, hfs_ref, hbp_ref, hbs_ref, gb_ref, yfp_ref, yfs_ref, ybp_ref, ybs_ref,
                     bon_ref, g_ref, m_ref, seg_ref, gng_ref, gnb_ref, wo_ref, g1_ref, b1_ref, wq_ref, keys_ref,
                     x1_ref, h2t_ref, th1_ref, p1_ref, p2_ref, s2_ref, st_ref, v1_ref, v2_ref, *, n_prompt_tiles):
    is_prompt = pl.program_id(0) < n_prompt_tiles
    pick = lambda p_ref, s_ref: jnp.where(is_prompt, p_ref[...], s_ref[...])
    y_lru = ((pick(hfp_ref, hfs_ref) + pick(hbp_ref, hbs_ref)) * _gelu(gb_ref[...])).astype(BF16)
    seg = seg_ref[...]
    ys = pick(yfp_ref, yfs_ref) + pick(ybp_ref, ybs_ref)
    mean = _mm1(ys, seg) * (1.0 / WKV_N)
    yc = ys - mean
    var = _mm1(yc * yc, seg) * (1.0 / WKV_N)
    yn = yc * lax.rsqrt(var + WKV_GN_EPS) * gng_ref[...] + gnb_ref[...]
    y_wkv = ((yn + bon_ref[...]) * g_ref[...]).astype(BF16)
    mix = _dot(y_lru, wo_ref[0:LRU_W]) + _dot(y_wkv, wo_ref[LRU_W:])
    _post_and_query(x_ref[...], mix, m_ref[0], g1_ref[...], b1_ref[...], wq_ref, keys_ref,
                    x1_ref, h2t_ref, (th1_ref, p1_ref, p2_ref, s2_ref), st_ref, v1_ref, v2_ref)


def _post_odd(x, tab, hf, hb, zl, yf, yb, bonus, g, mods, seg, gng, gnb, wo, g1, b1, wq, keys):
    t, d = x.shape
    tm = ROW_TILE
    w = WKV_W
    n_p = hf[0].shape[0] // tm
    n_s = hf[1].shape[0] // tm
    row = pl.BlockSpec((tm, w), lambda i, t_: (i, 0))
    row_p = pl.BlockSpec((tm, w), lambda i, t_: (jnp.minimum(i, n_p - 1), 0))
    row_s = pl.BlockSpec((tm, w), lambda i, t_: (jnp.clip(i - n_p, 0, n_s - 1), 0))
    in_specs = [pl.BlockSpec((tm, d), lambda i, t_: (i, 0)), row_p, row_s, row_p, row_s,
                pl.BlockSpec((tm, LRU_W), lambda i, t_: (i, 1)), row_p, row_s, row_p, row_s, row, row,
                pl.BlockSpec((1, 6, d), lambda i, t_: (t_[0, i], 0, 0))]
    in_specs += [_const_spec(a.shape) for a in (seg, gng, gnb, wo, g1, b1, wq, keys)]
    out_specs, out_shape = _post_out_specs(t, d, tm)
    return pl.pallas_call(
        functools.partial(_post_odd_kernel, n_prompt_tiles=n_p),
        grid_spec=pltpu.PrefetchScalarGridSpec(
            num_scalar_prefetch=1, grid=(t // tm,), in_specs=in_specs, out_specs=out_specs,
            scratch_shapes=_post_scratch(tm)),
        out_shape=out_shape,
        compiler_params=_cparams(("arbitrary",)),
        name="post_odd",
    )(tab, x, hf[0], hf[1], hb[0], hb[1], zl, yf[0], yf[1], yb[0], yb[1], bonus, g, mods, seg, gng, gnb, wo,
      g1, b1, wq, keys)


N_TOP = PEER_TOPK + 1
TOP_ROWS = 24
SUBLANES = 8


def _batcher_network(n):
    def merge(lo, hi, r):
        step = r * 2
        if step < hi - lo:
            yield from merge(lo, hi, step)
            yield from merge(lo + r, hi, step)
            yield from [(i, i + r) for i in range(lo + r, hi - r, step)]
        else:
            yield (lo, lo + r)

    def sort(lo, hi):
        if hi - lo >= 1:
            mid = lo + (hi - lo) // 2
            yield from sort(lo, mid)
            yield from sort(mid + 1, hi)
            yield from merge(lo, hi, 1)

    return tuple(sort(0, n - 1))


def _sort_levels(levels, net):
    lv = list(levels)
    for i, j in net:
        a, b = lv[i], lv[j]
        lv[i] = jnp.maximum(a, b)
        lv[j] = jnp.minimum(a, b)
    return lv


def _pop_sorted(levels, n_top, emit):
    lv = list(levels)
    for it in range(n_top):
        m = jnp.max(lv[0], axis=0, keepdims=True)
        emit(it, m)
        hit = lv[0] == m
        live = min(len(lv), n_top - it - 1)
        for k in range(live):
            below = lv[k + 1] if k + 1 < len(lv) else -jnp.inf
            lv[k] = jnp.where(hit, below, lv[k])


def _topk_heads(st_ref, th1_ref, p1_ref, p2_ref, s2_ref, v1_ref, v2_ref):
    tl = st_ref.shape[-1]
    neg = -jnp.inf
    net16 = _batcher_network(N_KEYS // SUBLANES)
    net8 = _batcher_network(SUBLANES)
    row = lax.broadcasted_iota(jnp.int32, (SUBLANES, tl), 0)
    roll = lambda x, sh: pltpu.roll(x, sh, 0)

    def top_values(x, store):
        store[...] = jnp.full(store.shape, neg, F32)
        lv = _sort_levels([x[SUBLANES * k:SUBLANES * (k + 1)] for k in range(N_KEYS // SUBLANES)], net16)

        def emit(it, m):
            store[it:it + 1, :] = m

        _pop_sorted(lv, N_TOP, emit)

    def head(h, carry):
        s1 = st_ref[pl.ds(pl.multiple_of(2 * h * N_KEYS, N_KEYS), N_KEYS), :]
        s2 = st_ref[pl.ds(pl.multiple_of((2 * h + 1) * N_KEYS, N_KEYS), N_KEYS), :]
        top_values(s1, v1_ref)
        top_values(s2, v2_ref)
        one = lambda ref, a: ref[a:a + 1, :]
        v2a = v2_ref[0:8, :]
        v1b2 = roll(v1_ref[8:16, :], 2)
        cands = [one(v1_ref, 0) + v2a, one(v1_ref, 0) + v2_ref[8:16, :], one(v1_ref, 0) + v2_ref[16:24, :],
                 one(v1_ref, 1) + v2a,
                 jnp.where(row < 5, one(v1_ref, 2), one(v1_ref, 4)) + jnp.where(row < 5, v2a, roll(v2a, 5)),
                 jnp.where(row < 4, one(v1_ref, 3), jnp.where(row < 6, one(v1_ref, 5), one(v1_ref, 6)))
                 + jnp.where(row < 4, v2a, jnp.where(row < 6, roll(v2a, 4), roll(v2a, 6))),
                 jnp.where(row < 2, one(v1_ref, 7), v1b2) + jnp.where(row < 2, v2a, one(v2_ref, 0)),
                 jnp.where(row < 2, v1b2, roll(v1_ref[16:24, :], 2)) + one(v2_ref, 0)]
        top = []
        _pop_sorted(_sort_levels(cands, net8), N_TOP, lambda it, m: top.append(m))
        tau = 0.5 * (top[PEER_TOPK - 1] + top[PEER_TOPK])
        mx1 = v1_ref[0:1, :]
        mx2 = v2_ref[0:1, :]
        zacc = jnp.zeros((SUBLANES, tl), F32)
        for cnd in cands:
            zacc = zacc + jnp.where(cnd >= tau, jnp.exp(cnd - (mx1 + mx2)), 0.0)
        zsum = jnp.sum(zacc, axis=0, keepdims=True)
        th1 = tau - s1
        p1 = jnp.exp(s1 - mx1) / zsum
        p2 = jnp.exp(s2 - mx2)
        for cb in range(tl // LANES):
            cs = slice(cb * LANES, (cb + 1) * LANES)
            th1_ref[cb, h] = th1[:, cs]
            p1_ref[cb, h] = p1[:, cs]
            p2_ref[cb, h] = p2[:, cs]
            s2_ref[cb, h] = s2[:, cs]
        return carry

    lax.fori_loop(0, PEER_HEADS, head, 0)


def _peer_kernel(tab_ref, h2t_ref, s2_ref, p2_ref, th1a_ref, p1a_ref, th1b_ref, p1b_ref, u_ref, vt_ref, x1_ref, m_ref,
                 g2_ref, b2_ref, o_ref, acc_ref, act_ref, ga_ref):
    s = pl.program_id(1)
    n_steps = pl.num_programs(1)
    n_blk = PEER_TM // LANES
    rows_per_tile = PEER_TE // N_KEYS
    n_iter = n_blk // BLK_PER_ITER
    d_rows = acc_ref.shape[0] // n_iter
    u_rows = PEER_TE // n_iter

    def project(half, q):
        ro = pl.multiple_of(q * u_rows, u_rows)
        return _dot(u_ref[pl.ds(half * PEER_TE + ro, u_rows), :], h2t_ref[...])

    def store_act(slot, q, a):
        ro = pl.multiple_of(q * u_rows, u_rows)
        for k in range(n_blk):
            act_ref[slot, k, pl.ds(ro, u_rows), :] = a[:, k * LANES:(k + 1) * LANES]

    def apply_v(slot, half, q):
        ro = pl.multiple_of(q * d_rows, d_rows)
        ga = jnp.concatenate([ga_ref[slot, k] for k in range(n_blk)], axis=1)
        return _dot(vt_ref[pl.ds(ro, d_rows), half * PEER_TE:(half + 1) * PEER_TE], ga)

    def stage(cur, th1_ref, p1_ref):
        nxt = 1 - cur

        def body(it, carry):
            a_next = project(nxt, it)
            v_prev = apply_v(nxt, nxt, it)
            for sub in range(BLK_PER_ITER):
                cb = it * BLK_PER_ITER + sub
                th = [th1_ref[cb, h] for h in range(PEER_HEADS)]
                p1 = [p1_ref[cb, h] for h in range(PEER_HEADS)]
                for rg in range(0, rows_per_tile, ROW_GROUP):
                    for part in range(N_KEYS // KEY_CHUNK):
                        ks = slice(part * KEY_CHUNK, (part + 1) * KEY_CHUNK)
                        gates = [None] * ROW_GROUP
                        for h in range(PEER_HEADS):
                            s2 = s2_ref[cb, h, ks, :]
                            p2 = p2_ref[cb, h, ks, :]
                            for g in range(ROW_GROUP):
                                r = rg + g
                                term = jnp.where(s2 >= th[h][r:r + 1], p2, 0.0) * p1[h][r:r + 1]
                                gates[g] = term if gates[g] is None else gates[g] + term
                        for g in range(ROW_GROUP):
                            lo = (rg + g) * N_KEYS + part * KEY_CHUNK
                            rs = slice(lo, lo + KEY_CHUNK)
                            ga_ref[cur, cb, rs, :] = (gates[g] * _gelu(act_ref[cur, cb, rs, :])).astype(BF16)
            store_act(nxt, it, a_next)
            ro = pl.multiple_of(it * d_rows, d_rows)
            acc_ref[pl.ds(ro, d_rows), :] += v_prev
            return carry

        lax.fori_loop(0, n_iter, body, 0)

    @pl.when(s == 0)
    def _():
        acc_ref[...] = jnp.zeros(acc_ref.shape, F32)
        ga_ref[...] = jnp.zeros(ga_ref.shape, BF16)
        for q in range(n_iter):
            store_act(0, q, project(0, q))

    @pl.when(s > 0)
    def _():
        stage(1, th1a_ref, p1a_ref)

    @pl.when(s < n_steps - 1)
    def _():
        stage(0, th1b_ref, p1b_ref)

    @pl.when(s == n_steps - 1)
    def _():
        m = m_ref[0]
        for q in range(n_iter):
            acc_ref[q * d_rows:(q + 1) * d_rows, :] += apply_v(1, 1, q)
        ffn = acc_ref[...].T
        o_ref[...] = _ln(ALPHA * x1_ref[...] + m[5:6] * ffn) * g2_ref[...] + b2_ref[...]


def _peer(x1, tab, h2t, s2, p2, th1, p1, u_bf, vt_bf, mods, g2, b2):
    t, d = x1.shape
    tm = PEER_TM
    te = PEER_TE
    n_tiles = u_bf.shape[0] // te
    per = tm // ROW_TILE
    n_blk = tm // LANES
    hk = pl.BlockSpec((n_blk, PEER_HEADS, N_KEYS, LANES), lambda i, j, t_: (i, 0, 0, 0))
    rows = te // N_KEYS
    rows_a = pl.BlockSpec((n_blk, PEER_HEADS, rows, LANES), lambda i, j, t_: (i, 0, jnp.maximum(2 * j - 1, 0), 0))
    rows_b = pl.BlockSpec((n_blk, PEER_HEADS, rows, LANES), lambda i, j, t_: (i, 0, jnp.minimum(2 * j, n_tiles - 1), 0))
    in_specs = [pl.BlockSpec((d, tm), lambda i, j, t_: (0, i)),
                hk, hk, rows_a, rows_a, rows_b, rows_b,
                pl.BlockSpec((2 * te, d), lambda i, j, t_: (jnp.minimum(j, n_tiles // 2 - 1), 0)),
                pl.BlockSpec((d, 2 * te), lambda i, j, t_: (0, jnp.maximum(j - 1, 0))),
                pl.BlockSpec((tm, d), lambda i, j, t_: (i, 0)),
                pl.BlockSpec((1, 6, d), lambda i, j, t_: (t_[0, i * per], 0, 0)),
                pl.BlockSpec(g2.shape, lambda i, j, t_: (0, 0)),
                pl.BlockSpec(b2.shape, lambda i, j, t_: (0, 0))]
    return pl.pallas_call(
        _peer_kernel,
        grid_spec=pltpu.PrefetchScalarGridSpec(
            num_scalar_prefetch=1, grid=(t // tm, n_tiles // 2 + 1), in_specs=in_specs,
            out_specs=pl.BlockSpec((tm, d), lambda i, j, t_: (i, 0)),
            scratch_shapes=[pltpu.VMEM((d, tm), F32), pltpu.VMEM((2, n_blk, te, LANES), F32),
                            pltpu.VMEM((2, n_blk, te, LANES), BF16)]),
        out_shape=jax.ShapeDtypeStruct((t, d), F32),
        compiler_params=_cparams(("arbitrary", "arbitrary")),
        name="peer_mix",
    )(tab, h2t, s2, p2, th1, p1, th1, p1, u_bf, vt_bf, x1, mods, g2, b2)


def _sincos(pos, dim):
    omega = 1.0 / (10000.0 ** (jnp.arange(dim // 2, dtype=F32) / (dim // 2)))
    ang = pos.astype(F32)[:, None] * omega[None, :]
    return jnp.concatenate([jnp.sin(ang), jnp.cos(ang)], -1)


def _grid_pos_embed(n_tok):
    rows = n_tok // GRID_W
    half = D_MODEL // 2
    er = _sincos(jnp.arange(rows), half)
    ec = _sincos(jnp.arange(GRID_W), half)
    emb = jnp.concatenate([jnp.broadcast_to(er[:, None, :], (rows, GRID_W, half)),
                           jnp.broadcast_to(ec[None, :, :], (rows, GRID_W, half))], -1)
    return emb.reshape(rows * GRID_W, D_MODEL)


def _seq_dft_tables(s_len):
    r = 1
    while r * r < s_len:
        r *= 2
    k = jnp.arange(s_len, dtype=jnp.int32)
    w = 2.0 * math.pi / s_len
    ang_a = ((jnp.arange(s_len // r, dtype=jnp.int32)[:, None] * r * k[None, :]) % s_len).astype(F32) * w
    ang_b = ((jnp.arange(r, dtype=jnp.int32)[:, None] * k[None, :]) % s_len).astype(F32) * w
    ca, sa = jnp.cos(ang_a)[:, None, :], jnp.sin(ang_a)[:, None, :]
    cb, sb = jnp.cos(ang_b)[None, :, :], jnp.sin(ang_b)[None, :, :]
    scale = 1.0 / math.sqrt(s_len * FNET_GW)
    cmat = ((ca * cb - sa * sb) * scale).reshape(s_len, s_len)
    smat = ((sa * cb + ca * sb) * (-scale)).reshape(s_len, s_len)
    return cmat.astype(BF16), smat.astype(BF16)


def _channel_dft_table():
    n = np.arange(FNET_GW)
    ang = ((n[:, None] * n[None, :]) % FNET_GW) * (2.0 * np.pi / FNET_GW)
    wc = np.zeros((FNET_W, 2 * FNET_W), np.float32)
    for g in range(FNET_GROUPS):
        sl = slice(g * FNET_GW, (g + 1) * FNET_GW)
        wc[sl, sl] = np.cos(ang)
        wc[sl, FNET_W + g * FNET_GW:FNET_W + (g + 1) * FNET_GW] = np.sin(ang)
    return jnp.asarray(wc, BF16)


def _segment_ones():
    idx = np.arange(WKV_W) // WKV_N
    return jnp.asarray((idx[:, None] == idx[None, :]).astype(np.float32), BF16)


def _chunk_tri():
    i = np.arange(ROW_TILE)
    same = (i[:, None] // CHUNK) == (i[None, :] // CHUNK)
    lower = same & (i[None, :] <= i[:, None])
    upper = same & (i[None, :] >= i[:, None])
    return jnp.asarray(np.stack([lower, upper]).astype(np.float32), BF16)


def _scan_masks():
    i = np.arange(PAIR)
    same = (i[:, None] // CHUNK) == (i[None, :] // CHUNK)
    t_row = i[:, None] % CHUNK
    t_col = i[None, :] % CHUNK
    masks = [same & (t_col < t_row), same & (t_col <= t_row), same & (t_col > t_row), same & (t_col >= t_row),
             np.eye(PAIR, dtype=bool)]
    return jnp.asarray(np.stack(masks).astype(np.float32))


def _block_diag(w):
    h, n, _ = w.shape
    eye = jnp.eye(h, dtype=w.dtype)
    return jnp.einsum("hij,hg->higj", w, eye).reshape(h * n, h * n)


def _pad_rank(w, d):
    r = w.shape[1]
    z = jnp.zeros_like(w[d])
    return jnp.concatenate([w[d], z] if d == 0 else [z, w[d]], axis=0)


def _state_to_pairs(s0):
    b = s0.shape[0]
    st = jnp.swapaxes(s0, -1, -2).reshape(b, 2, N_PAIRS, 2, WKV_N, WKV_N)
    eye = jnp.eye(2, dtype=s0.dtype)
    return jnp.einsum("bdpeji,ef->bdpejfi", st, eye).reshape(b, 2, N_PAIRS, PAIR, PAIR)


def _pairs_to_state(m):
    b = m.shape[0]
    m7 = m.reshape(b, 2, N_PAIRS, 2, WKV_N, 2, WKV_N)
    st = jnp.stack([m7[:, :, :, 0, :, 0, :], m7[:, :, :, 1, :, 1, :]], axis=3)
    return jnp.swapaxes(st.reshape(b, 2, WKV_H, WKV_N, WKV_N), -1, -2)


def _tile_table(groups):
    cv, first, last = [], [], []
    row = 0
    for gi, (n_seq, s_len) in enumerate(groups):
        nt = s_len // ROW_TILE
        for b in range(n_seq):
            for i in range(nt):
                cv.append(0 if gi == 0 else 1 + b)
                first.append(int(i == 0))
                last.append(int(i == nt - 1))
        row += n_seq * s_len
    return jnp.asarray(np.array([cv, first, last], np.int32))


def kernel(x_prompt, x_sample, state_lru, state_wkv, c, c_ctx, w_mod, b_mod, ln1_g, ln1_b, ln2_g, ln2_b,
           w_in_e, w_out_e, sconv_w, sconv_b, w_in_o, w_out_o, lru_conv_w, lru_conv_b, lru_wa, lru_ba,
           lru_wx, lru_bx, lru_lambda, wkv_mu, wkv_w0, wkv_w2, wkv_a0, wkv_a2, wkv_kk, wkv_ka, wkv_rk,
           wkv_g2, wkv_gn_g, wkv_gn_b, peer_wq, peer_keys, peer_u, peer_v):
    bp, sp, d = x_prompt.shape
    bs, ss, _ = x_sample.shape
    depth = w_mod.shape[0]
    assert sp % ROW_TILE == 0 and ss % PEER_TM == 0 and (bp * sp) % ss == 0
    assert bs + 1 <= 8
    groups = ((bp, sp), (bs, ss))
    tp = bp * sp
    tab = _tile_table(groups)

    cv8 = jnp.concatenate([c_ctx[None, :], c, jnp.zeros((8 - 1 - bs, d), F32)], axis=0)
    mods = _modulation(cv8, w_mod, b_mod).reshape(depth, 8, 6, d)

    x = _embed(x_prompt.reshape(tp, d), x_sample.reshape(bs * ss, d), _grid_pos_embed(ss).astype(x_sample.dtype))

    wc = _channel_dft_table()
    dft = {s: _seq_dft_tables(s) for s in sorted({sp, ss})}
    seg = _segment_ones()
    tri = _chunk_tri()
    masks = _scan_masks()
    row2 = lambda a: a.reshape(1, -1)

    lru_fin = []
    wkv_fin = []
    for l in range(depth):
        j = l // 2
        m_l = mods[l]
        wq = peer_wq[l].astype(BF16)
        keys = peer_keys[l].reshape(2 * PEER_HEADS, N_KEYS, PEER_HALF).astype(BF16)
        if l % 2 == 0:
            zr, pq = _inproj(x, tab, m_l, w_in_e[j].astype(BF16), wc)
            y = jnp.concatenate([_seqdft(pq, bp * sp, 0, bp, sp, *dft[sp]),
                                 _seqdft(pq, bs * ss, tp, bs, ss, *dft[ss])], axis=0)
            x1, h2t, th1, p1, p2, s2 = _post_even(x, tab, zr, y, m_l, sconv_w[j], row2(sconv_b[j]), w_out_e[j].astype(BF16),
                                     row2(ln1_g[l]), row2(ln1_b[l]), wq, keys)
        else:
            zl, zw = _inproj(x, tab, m_l, w_in_o[j].astype(BF16))
            lru_args = (lru_conv_w[j], lru_conv_b[j],
                        jnp.stack([_block_diag(lru_wa[j, dd]) for dd in range(2)]).astype(BF16), lru_ba[j],
                        jnp.stack([_block_diag(lru_wx[j, dd]) for dd in range(2)]).astype(BF16), lru_bx[j],
                        lru_lambda[j])
            hf_p, hb_p, hfin_p = _lru(zl, 0, bp, sp, jnp.zeros((bp, 2, LRU_W), F32), *lru_args)
            hf_s, hb_s, _ = _lru(zl, tp, bs, ss, state_lru[:, j], *lru_args)
            lru_fin.append(hfin_p)
            prep = _wkvprep(zw, tab, row2(wkv_mu[j]), wkv_w0[j],
                            jnp.stack([_pad_rank(wkv_w2[j], dd) for dd in range(2)]).astype(BF16), wkv_a0[j],
                            jnp.stack([_pad_rank(wkv_a2[j], dd) for dd in range(2)]).astype(BF16),
                            wkv_kk[j], wkv_ka[j], row2(wkv_rk[j]), wkv_g2[j].astype(BF16), seg, tri)
            v, bonus, g = prep[0], prep[1], prep[2]
            yf_p, yb_p, mfin_p = _wkvscan(prep[3:], v, 0, bp, sp,
                                          jnp.zeros((bp, 2, N_PAIRS, PAIR, PAIR), F32), masks)
            yf_s, yb_s, _ = _wkvscan(prep[3:], v, tp, bs, ss, _state_to_pairs(state_wkv[:, j]), masks)
            wkv_fin.append(_pairs_to_state(mfin_p))
            x1, h2t, th1, p1, p2, s2 = _post_odd(x, tab, (hf_p, hf_s), (hb_p, hb_s), zl, (yf_p, yf_s), (yb_p, yb_s),
                                    bonus, g, m_l, seg, row2(wkv_gn_g[j]), row2(wkv_gn_b[j]),
                                    w_out_o[j].astype(BF16), row2(ln1_g[l]), row2(ln1_b[l]), wq, keys)
        x = _peer(x1, tab, h2t, s2, p2, th1, p1, peer_u[l].astype(BF16), peer_v[l].T.astype(BF16), m_l,
                  row2(ln2_g[l]), row2(ln2_b[l]))

    y_prompt = x[:tp].reshape(bp, sp, d)
    y_sample = x[tp:].reshape(bs, ss, d)
    return (y_prompt, y_sample, jnp.stack(lru_fin, 1).astype(x_prompt.dtype),
            jnp.stack(wkv_fin, 1).astype(x_prompt.dtype))
```

```python
import functools
import math

import numpy as np
import jax
import jax.numpy as jnp
from jax import lax
from jax.experimental import pallas as pl
from jax.experimental.pallas import tpu as pltpu

F32 = jnp.float32
BF16 = jnp.bfloat16

D_MODEL = 1024
GRID_W = 64
FNET_W = 512
FNET_GROUPS = 4
FNET_GW = FNET_W // FNET_GROUPS
CONV_W = 512
LRU_W = 512
LRU_HEADS = 8
LRU_CONV_K = 4
LRU_C = 8.0
WKV_W = 512
WKV_N = 64
WKV_H = 8
WKV_IN = 1920
DECAY_SCALE = math.exp(-0.5)
WKV_GN_EPS = 64e-5
PEER_HEADS = 8
N_KEYS = 128
PEER_TOPK = 16
PEER_HALF = 128
DEPTH = 4
ALPHA = (2 * DEPTH) ** 0.25
LN_EPS = 1e-6

ROW_TILE = 256
LANES = 128
HALO = 8
CHUNK = 64
PAIR = 2 * WKV_N
N_PAIRS = WKV_H // 2
PEER_TM = 512
PEER_TE = 1024
ROW_GROUP = 4
KEY_CHUNK = 32
BLK_PER_ITER = 2
VMEM_LIMIT = 56 * 1024 * 1024


def _cparams(sem):
    return pltpu.CompilerParams(dimension_semantics=sem, vmem_limit_bytes=VMEM_LIMIT)


def _dot(a, b):
    return jnp.dot(a, b, preferred_element_type=F32)


def _dot_nt(a, b):
    return lax.dot_general(a, b, (((1,), (1,)), ((), ())), preferred_element_type=F32)


def _mm1(a, b):
    return _dot(a.astype(BF16), b.astype(BF16))


def _mm01(a, b01):
    h = a.astype(BF16)
    r = a - h.astype(F32)
    m = r.astype(BF16)
    lo = (r - m.astype(F32)).astype(BF16)
    return _dot(h, b01) + (_dot(m, b01) + _dot(lo, b01))


def _ln(x):
    mu = jnp.mean(x, axis=-1, keepdims=True)
    xc = x - mu
    var = jnp.mean(xc * xc, axis=-1, keepdims=True)
    return xc * lax.rsqrt(var + LN_EPS)


def _gelu(x):
    z = x * (0.7978845608028654 + 0.035677408136300125 * (x * x))
    hx = 0.5 * x
    return hx + hx * jnp.tanh(z)


def _sigmoid(x):
    return 1.0 / (1.0 + jnp.exp(-x))


def _mod_kernel(c_ref, w_ref, b_ref, o_ref):
    c = c_ref[...]
    sc = c * _sigmoid(c)
    o_ref[0] = _dot(sc.astype(BF16), w_ref[0].astype(BF16)) + b_ref[0]


def _modulation(cv8, w_mod, b_mod):
    depth, d, n = w_mod.shape
    tn = 1536
    return pl.pallas_call(
        _mod_kernel,
        grid=(depth, n // tn),
        in_specs=[pl.BlockSpec((8, d), lambda l, j: (0, 0)),
                  pl.BlockSpec((1, d, tn), lambda l, j: (l, 0, j)),
                  pl.BlockSpec((1, 1, tn), lambda l, j: (l, 0, j))],
        out_specs=pl.BlockSpec((1, 8, tn), lambda l, j: (l, 0, j)),
        out_shape=jax.ShapeDtypeStruct((depth, 8, n), F32),
        compiler_params=_cparams(("arbitrary", "arbitrary")),
        name="modulation",
    )(cv8, w_mod, b_mod.reshape(depth, 1, n))


def _embed_kernel(xp_ref, xs_ref, p_ref, o_ref, *, n_prompt):
    i = pl.program_id(0)

    @pl.when(i < n_prompt)
    def _():
        o_ref[...] = xp_ref[...]

    @pl.when(i >= n_prompt)
    def _():
        o_ref[...] = xs_ref[...] + p_ref[...]


def _embed(xp, xs, pos):
    tp, d = xp.shape
    ts = xs.shape[0]
    s_len = pos.shape[0]
    tm = 512
    n_prompt = tp // tm
    per_seq = s_len // tm
    return pl.pallas_call(
        functools.partial(_embed_kernel, n_prompt=n_prompt),
        grid=((tp + ts) // tm,),
        in_specs=[pl.BlockSpec((tm, d), lambda i: (jnp.minimum(i, n_prompt - 1), 0)),
                  pl.BlockSpec((tm, d), lambda i: (jnp.maximum(i - n_prompt, 0), 0)),
                  pl.BlockSpec((tm, d), lambda i: (jnp.maximum(i - n_prompt, 0) % per_seq, 0))],
        out_specs=pl.BlockSpec((tm, d), lambda i: (i, 0)),
        out_shape=jax.ShapeDtypeStruct((tp + ts, d), xp.dtype),
        compiler_params=_cparams(("arbitrary",)),
        name="embed",
    )(xp, xs, pos)


def _inproj_even_kernel(tab_ref, x_ref, m_ref, w_ref, wc_ref, zr_ref, pq_ref):
    m = m_ref[0]
    h = _ln(x_ref[...]) * (1.0 + m[1:2]) + m[0:1]
    z = _dot(h.astype(BF16), w_ref[...])
    zr_ref[...] = z[:, FNET_W:]
    pq_ref[...] = _dot(z[:, :FNET_W].astype(BF16), wc_ref[...]).astype(BF16)


def _inproj_odd_kernel(tab_ref, x_ref, m_ref, w_ref, zl_ref, zw_ref):
    m = m_ref[0]
    h = _ln(x_ref[...]) * (1.0 + m[1:2]) + m[0:1]
    z = _dot(h.astype(BF16), w_ref[...])
    zl_ref[...] = z[:, :2 * LRU_W]
    zw_ref[...] = z[:, 2 * LRU_W:]


def _const_spec(shape):
    nd = len(shape)
    return pl.BlockSpec(shape, lambda i, t, _n=nd: (0,) * _n)


def _inproj(x, tab, mods, w_in, wc=None):
    t, d = x.shape
    n = w_in.shape[1]
    tm = ROW_TILE
    even = wc is not None
    in_specs = [pl.BlockSpec((tm, d), lambda i, t_: (i, 0)),
                pl.BlockSpec((1, 6, d), lambda i, t_: (t_[0, i], 0, 0)),
                _const_spec((d, n))]
    if even:
        in_specs.append(_const_spec(wc.shape))
        widths = (n - FNET_W, 2 * FNET_W)
        dtypes = (F32, BF16)
        kern = _inproj_even_kernel
        args = (tab, x, mods, w_in, wc)
    else:
        widths = (2 * LRU_W, n - 2 * LRU_W)
        dtypes = (F32, F32)
        kern = _inproj_odd_kernel
        args = (tab, x, mods, w_in)
    return pl.pallas_call(
        kern,
        grid_spec=pltpu.PrefetchScalarGridSpec(
            num_scalar_prefetch=1, grid=(t // tm,), in_specs=in_specs,
            out_specs=[pl.BlockSpec((tm, w), lambda i, t_: (i, 0)) for w in widths]),
        out_shape=[jax.ShapeDtypeStruct((t, w), dt) for w, dt in zip(widths, dtypes)],
        compiler_params=_cparams(("arbitrary",)),
        name="inproj_even" if even else "inproj_odd",
    )(*args)


def _seqdft_kernel(c_ref, s_ref, p_ref, q_ref, y_ref):
    y_ref[...] = (_dot(c_ref[...], p_ref[...]) + _dot(s_ref[...], q_ref[...])).astype(BF16)


def _seqdft(pq, y_rows, row0, n_seq, s_len, cmat, smat):
    tm = min(s_len, ROW_TILE)
    nt = s_len // tm
    blk0 = row0 // s_len
    return pl.pallas_call(
        _seqdft_kernel,
        grid=(n_seq, nt),
        in_specs=[pl.BlockSpec((tm, s_len), lambda b, i: (i, 0)),
                  pl.BlockSpec((tm, s_len), lambda b, i: (i, 0)),
                  pl.BlockSpec((s_len, FNET_W), lambda b, i: (blk0 + b, 0)),
                  pl.BlockSpec((s_len, FNET_W), lambda b, i: (blk0 + b, 1))],
        out_specs=pl.BlockSpec((tm, FNET_W), lambda b, i: (b * nt + i, 0)),
        out_shape=jax.ShapeDtypeStruct((y_rows, FNET_W), BF16),
        compiler_params=_cparams(("arbitrary", "arbitrary")),
        name="seqdft_%d" % s_len,
    )(cmat, smat, pq, pq)


def _post_and_query(x, mix, m, g1, b1, wq_ref, keys_ref, x1_ref, h2t_ref, topk_refs, st_ref, v1_ref, v2_ref):
    x1 = _ln(ALPHA * x + m[2:3] * mix) * g1 + b1
    x1_ref[...] = x1
    h2 = _ln(x1) * (1.0 + m[4:5]) + m[3:4]
    h2b = h2.astype(BF16)
    h2t_ref[...] = h2.T.astype(BF16)
    q = _dot(h2b, wq_ref[...])
    for hp in range(2 * PEER_HEADS):
        qb = q[:, hp * PEER_HALF:(hp + 1) * PEER_HALF].astype(BF16)
        st_ref[hp * N_KEYS:(hp + 1) * N_KEYS, :] = _dot_nt(keys_ref[hp], qb)
    _topk_heads(st_ref, *topk_refs, v1_ref, v2_ref)


def _post_even_kernel(tab_ref, x_ref, bg_ref, cg_ref, xi_ref, cgp_ref, xip_ref, cgn_ref, xin_ref, y_ref,
                      m_ref, cw_ref, cb_ref, wo_ref, g1_ref, b1_ref, wq_ref, keys_ref,
                      x1_ref, h2t_ref, th1_ref, p1_ref, p2_ref, s2_ref, ext_ref, st_ref, v1_ref, v2_ref):
    i = pl.program_id(0)
    tm = ROW_TILE
    first = tab_ref[1, i] == 1
    last = tab_ref[2, i] == 1
    u = cg_ref[...] * xi_ref[...]
    ext_ref[0:HALO] = jnp.where(first, 0.0, cgp_ref[...] * xip_ref[...])
    ext_ref[HALO:HALO + tm] = u
    ext_ref[HALO + tm:2 * HALO + tm] = jnp.where(last, 0.0, cgn_ref[...] * xin_ref[...])
    cw = cw_ref[...]
    conv = (ext_ref[HALO - 1:HALO - 1 + tm] * cw[0:1] + u * cw[1:2]
            + ext_ref[HALO + 1:HALO + 1 + tm] * cw[2:3] + cb_ref[...])
    ymix = (bg_ref[...] * conv).astype(BF16)
    mix = _dot(y_ref[...], wo_ref[0:FNET_W]) + _dot(ymix, wo_ref[FNET_W:])
    _post_and_query(x_ref[...], mix, m_ref[0], g1_ref[...], b1_ref[...], wq_ref, keys_ref,
                    x1_ref, h2t_ref, (th1_ref, p1_ref, p2_ref, s2_ref), st_ref, v1_ref, v2_ref)


def _post_out_specs(t, d, tm):
    rshape = jax.ShapeDtypeStruct((t // LANES, PEER_HEADS, N_KEYS, LANES), F32)
    rspec = pl.BlockSpec((tm // LANES, PEER_HEADS, N_KEYS, LANES), lambda i, t_: (i, 0, 0, 0))
    specs = [pl.BlockSpec((tm, d), lambda i, t_: (i, 0)),
             pl.BlockSpec((d, tm), lambda i, t_: (0, i))] + [rspec] * 4
    shapes = [jax.ShapeDtypeStruct((t, d), F32),
              jax.ShapeDtypeStruct((d, t), BF16)] + [rshape] * 4
    return specs, shapes


def _post_scratch(tm):
    return [pltpu.VMEM((2 * PEER_HEADS * N_KEYS, tm), F32), pltpu.VMEM((TOP_ROWS, tm), F32),
            pltpu.VMEM((TOP_ROWS, tm), F32)]


def _post_even(x, tab, zr, y, mods, cw, cb, wo, g1, b1, wq, keys):
    t, d = x.shape
    tm = ROW_TILE
    hb = tm // HALO
    nblk = t // HALO
    prev = lambda c: pl.BlockSpec((HALO, CONV_W), lambda i, t_, _c=c: (jnp.maximum(i * hb - 1, 0), _c))
    nxt = lambda c: pl.BlockSpec((HALO, CONV_W), lambda i, t_, _c=c: (jnp.minimum((i + 1) * hb, nblk - 1), _c))
    col = lambda c: pl.BlockSpec((tm, CONV_W), lambda i, t_, _c=c: (i, _c))
    in_specs = [pl.BlockSpec((tm, d), lambda i, t_: (i, 0)),
                col(0), col(1), col(2), prev(1), prev(2), nxt(1), nxt(2),
                pl.BlockSpec((tm, FNET_W), lambda i, t_: (i, 0)),
                pl.BlockSpec((1, 6, d), lambda i, t_: (t_[0, i], 0, 0)),
                _const_spec(cw.shape), _const_spec(cb.shape), _const_spec(wo.shape),
                _const_spec(g1.shape), _const_spec(b1.shape), _const_spec(wq.shape), _const_spec(keys.shape)]
    out_specs, out_shape = _post_out_specs(t, d, tm)
    return pl.pallas_call(
        _post_even_kernel,
        grid_spec=pltpu.PrefetchScalarGridSpec(
            num_scalar_prefetch=1, grid=(t // tm,), in_specs=in_specs, out_specs=out_specs,
            scratch_shapes=[pltpu.VMEM((tm + 2 * HALO, CONV_W), F32)] + _post_scratch(tm)),
        out_shape=out_shape,
        compiler_params=_cparams(("arbitrary",)),
        name="post_even",
    )(tab, x, zr, zr, zr, zr, zr, zr, zr, y, mods, cw, cb, wo, g1, b1, wq, keys)


def _lru_kernel(xf_ref, xfh_ref, xb_ref, xbh_ref, cw_ref, cb_ref, wa_ref, ba_ref, wx_ref, bx_ref, lam_ref,
                h0_ref, hf_ref, hb_ref, hfin_ref, ext_ref, carry_ref):
    i = pl.program_id(1)
    nt = pl.num_programs(1)
    tm = ROW_TILE

    @pl.when(i == 0)
    def _():
        carry_ref[...] = h0_ref[0]

    row = lax.broadcasted_iota(jnp.int32, (tm, 1), 0)
    for d in range(2):
        x = (xf_ref if d == 0 else xb_ref)[...]
        halo = jnp.where(i == 0, 0.0, (xfh_ref if d == 0 else xbh_ref)[...])
        ext_ref[HALO:HALO + tm] = x
        if d == 0:
            ext_ref[0:HALO] = halo
        else:
            ext_ref[HALO + tm:2 * HALO + tm] = halo
        cw = cw_ref[d]
        xc = cb_ref[d:d + 1] + x * cw[LRU_CONV_K - 1:LRU_CONV_K]
        for j in range(LRU_CONV_K - 1):
            k = LRU_CONV_K - 1 - j
            off = HALO - k if d == 0 else HALO + k
            xc = xc + ext_ref[off:off + tm] * cw[j:j + 1]
        xcb = xc.astype(BF16)
        gate_r = _sigmoid(_dot(xcb, wa_ref[d]) + ba_ref[d:d + 1])
        gate_i = _sigmoid(_dot(xcb, wx_ref[d]) + bx_ref[d:d + 1])
        nl = -lam_ref[d:d + 1]
        softplus = jnp.maximum(nl, 0.0) + jnp.log1p(jnp.exp(-jnp.abs(nl)))
        log_a = -LRU_C * gate_r * softplus
        a = jnp.exp(log_a)
        b = jnp.sqrt(-jnp.tanh(log_a) * (a * a + 1.0)) * (gate_i * xc)
        s = 1
        while s < tm:
            if d == 0:
                keep = row >= s
                sh = s
            else:
                keep = row < tm - s
                sh = tm - s
            a_sh = jnp.where(keep, pltpu.roll(a, sh, 0), 1.0)
            b_sh = jnp.where(keep, pltpu.roll(b, sh, 0), 0.0)
            b = a * b_sh + b
            a = a * a_sh
            s *= 2
        h = a * carry_ref[d:d + 1] + b
        if d == 0:
            hf_ref[...] = h
            carry_ref[0:1] = h[tm - 1:tm]
        else:
            hb_ref[...] = h
            carry_ref[1:2] = h[0:1]

    @pl.when(i == nt - 1)
    def _():
        hfin_ref[0] = carry_ref[...]


def _lru(zl, row0, n_seq, s_len, h0, cw, cb, wa, ba, wx, bx, lam):
    tm = ROW_TILE
    nt = s_len // tm
    t0 = row0 // tm
    hb = tm // HALO
    nblk = zl.shape[0] // HALO
    rows = n_seq * s_len
    fwd = lambda b, i: (t0 + b * nt + i, 0)
    bwd = lambda b, i: (t0 + b * nt + nt - 1 - i, 0)
    fwd_h = lambda b, i: (jnp.maximum((t0 + b * nt + i) * hb - 1, 0), 0)
    bwd_h = lambda b, i: (jnp.minimum((t0 + b * nt + nt - i) * hb, nblk - 1), 0)
    out_f = lambda b, i: (b * nt + i, 0)
    out_b = lambda b, i: (b * nt + nt - 1 - i, 0)
    cst = lambda a: pl.BlockSpec(a.shape, lambda b, i, _n=a.ndim: (0,) * _n)
    return pl.pallas_call(
        _lru_kernel,
        grid=(n_seq, nt),
        in_specs=[pl.BlockSpec((tm, LRU_W), fwd), pl.BlockSpec((HALO, LRU_W), fwd_h),
                  pl.BlockSpec((tm, LRU_W), bwd), pl.BlockSpec((HALO, LRU_W), bwd_h),
                  cst(cw), cst(cb), cst(wa), cst(ba), cst(wx), cst(bx), cst(lam),
                  pl.BlockSpec((1, 2, LRU_W), lambda b, i: (b, 0, 0))],
        out_specs=[pl.BlockSpec((tm, LRU_W), out_f), pl.BlockSpec((tm, LRU_W), out_b),
                   pl.BlockSpec((1, 2, LRU_W), lambda b, i: (b, 0, 0))],
        out_shape=[jax.ShapeDtypeStruct((rows, LRU_W), F32), jax.ShapeDtypeStruct((rows, LRU_W), F32),
                   jax.ShapeDtypeStruct((n_seq, 2, LRU_W), F32)],
        scratch_shapes=[pltpu.VMEM((tm + 2 * HALO, LRU_W), F32), pltpu.VMEM((2, LRU_W), F32)],
        compiler_params=_cparams(("arbitrary", "arbitrary")),
        name="lru_%d" % s_len,
    )(zl, zl, zl, zl, cw, cb, wa, ba, wx, bx, lam, h0)


def _wkvprep_kernel(tab_ref, z_ref, zp_ref, zn_ref, mu_ref, w0_ref, w2_ref, a0_ref, a2_ref, kk_ref, ka_ref,
                    rk_ref, g2_ref, seg_ref, tri_ref,
                    v_ref, bg_ref, g_ref, rf_ref, kf_ref, bf_ref, qf_ref, cf_ref,
                    rb_ref, kb_ref, bb_ref, qb_ref, cb_ref, ext_ref):
    i = pl.program_id(0)
    tm = ROW_TILE
    w = WKV_W
    first = tab_ref[1, i] == 1
    last = tab_ref[2, i] == 1
    z = z_ref[...]
    ext_ref[0:HALO] = jnp.where(first, 0.0, zp_ref[...])
    ext_ref[HALO:HALO + tm] = z
    ext_ref[HALO + tm:2 * HALO + tm] = jnp.where(last, 0.0, zn_ref[...])
    z = z + mu_ref[...] * (0.5 * (ext_ref[HALO - 1:HALO - 1 + tm] + ext_ref[HALO + 1:HALO + 1 + tm]) - z)
    r = z[:, 0:w]
    k = z[:, w:2 * w]
    v = z[:, 2 * w:3 * w]
    wd = jnp.tanh(z[:, 3 * w:3 * w + 128]).astype(BF16)
    ad = z[:, 3 * w + 128:3 * w + 256].astype(BF16)
    gd = _sigmoid(z[:, 3 * w + 256:3 * w + 384]).astype(BF16)
    v_ref[...] = v
    g_ref[...] = _dot(gd, g2_ref[...])
    seg = seg_ref[...]
    rk = rk_ref[...]
    bonus = jnp.zeros((tm, w), F32)
    outs = ((rf_ref, kf_ref, bf_ref, qf_ref, cf_ref), (rb_ref, kb_ref, bb_ref, qb_ref, cb_ref))
    nch = tm // CHUNK
    for d in range(2):
        r_ref, k_ref, b_ref, q_ref, c_ref = outs[d]
        wz = w0_ref[d:d + 1] + _dot(wd, w2_ref[d])
        logw = -DECAY_SCALE * _sigmoid(wz)
        iclr = _sigmoid(a0_ref[d:d + 1] + _dot(ad, a2_ref[d]))
        kk = k * kk_ref[d:d + 1]
        kk = kk * lax.rsqrt(jnp.maximum(_mm01(kk * kk, seg), 1e-24))
        km = k * (1.0 + (iclr - 1.0) * ka_ref[d:d + 1])
        bonus = bonus + _mm1(r * km * rk, seg) * v
        lw_h = logw.astype(BF16)
        lw_r = logw - lw_h.astype(F32)
        lw_m = lw_r.astype(BF16)
        lw_l = (lw_r - lw_m.astype(F32)).astype(BF16)
        tri = tri_ref[d]
        cl = _dot(tri, lw_h) + (_dot(tri, lw_m) + _dot(tri, lw_l))
        c = jnp.exp(cl)
        cinv = jnp.exp(-cl)
        r_ref[...] = r * c
        k_ref[...] = km * cinv
        b_ref[...] = kk * iclr * cinv
        q_ref[...] = kk * jnp.exp(cl - logw)
        for j in range(nch):
            edge = (j + 1) * CHUNK - 1 if d == 0 else j * CHUNK
            c_ref[j] = c[edge:edge + 1]
    bg_ref[...] = bonus


def _wkvprep(zw, tab, mu, w0, w2p, a0, a2p, kk, ka, rk, g2, seg, tri):
    t, n = zw.shape
    tm = ROW_TILE
    hb = tm // HALO
    nblk = t // HALO
    nch = tm // CHUNK
    w = WKV_W
    in_specs = [pl.BlockSpec((tm, n), lambda i, t_: (i, 0)),
                pl.BlockSpec((HALO, n), lambda i, t_: (jnp.maximum(i * hb - 1, 0), 0)),
                pl.BlockSpec((HALO, n), lambda i, t_: (jnp.minimum((i + 1) * hb, nblk - 1), 0))]
    in_specs += [_const_spec(a.shape) for a in (mu, w0, w2p, a0, a2p, kk, ka, rk, g2, seg, tri)]
    row = pl.BlockSpec((tm, w), lambda i, t_: (i, 0))
    cspec = pl.BlockSpec((nch, 1, w), lambda i, t_: (i, 0, 0))
    rshape = jax.ShapeDtypeStruct((t, w), F32)
    cshape = jax.ShapeDtypeStruct((t // CHUNK, 1, w), F32)
    out_specs = [row, row, row] + [row, row, row, row, cspec] * 2
    out_shape = [rshape, rshape, rshape] + [rshape, rshape, rshape, rshape, cshape] * 2
    return pl.pallas_call(
        _wkvprep_kernel,
        grid_spec=pltpu.PrefetchScalarGridSpec(
            num_scalar_prefetch=1, grid=(t // tm,), in_specs=in_specs, out_specs=out_specs,
            scratch_shapes=[pltpu.VMEM((tm + 2 * HALO, n), F32)]),
        out_shape=out_shape,
        compiler_params=_cparams(("arbitrary",)),
        name="wkv_prep",
    )(tab, zw, zw, zw, mu, w0, w2p, a0, a2p, kk, ka, rk, g2, seg, tri)


def _wkv_chunks(items, eye):
    lane = lax.broadcasted_iota(jnp.int32, (1, PAIR), 1)
    m0 = (lane < WKV_N).astype(F32)
    m1 = 1.0 - m0
    stack = lambda x: jnp.concatenate([x * m0, x * m1], axis=0)
    n = range(len(items))
    rh_s = [stack(it[0]) for it in items]
    kh_s = [stack(it[1]) for it in items]
    bh_s = [stack(it[2]) for it in items]
    kq_s = [stack(it[3]) for it in items]
    v_s = [stack(it[4]) for it in items]
    bh_t = [x.T for x in bh_s]
    kh_t = [x.T for x in kh_s]
    bk_t = [jnp.concatenate([bh_t[i], kh_t[i]], axis=1) for i in n]
    att = [_mm1(jnp.concatenate([kq_s[i], rh_s[i]], axis=0), bk_t[i]) for i in n]
    n1 = [att[i][0:PAIR, 0:PAIR] * items[i][7] for i in n]
    ak = [att[i][0:PAIR, PAIR:] * items[i][7] for i in n]
    gb = [att[i][PAIR:, 0:PAIR] * items[i][8] for i in n]
    gk = [att[i][PAIR:, PAIR:] * items[i][8] for i in n]
    t_inv = [eye - n1[i] for i in n]
    npow = n1
    for _ in range(5):
        npow = [_mm1(npow[i], npow[i]) for i in n]
        t_inv = [t_inv[i] + _mm1(t_inv[i], npow[i]) for i in n]
    kv = [_mm1(jnp.concatenate([kh_t[i], gk[i], ak[i]], axis=0), v_s[i]) for i in n]
    x = [_mm1(t_inv[i], jnp.concatenate([kq_s[i], kv[i][2 * PAIR:]], axis=1)) for i in n]
    bx = [_mm1(jnp.concatenate([bh_t[i], gb[i]], axis=0), x[i]) for i in n]
    p_mat = [eye - bx[i][0:PAIR, 0:PAIR] for i in n]
    q_mat = [kv[i][0:PAIR] - bx[i][0:PAIR, PAIR:] for i in n]
    r_til = [rh_s[i] - bx[i][PAIR:, 0:PAIR] for i in n]
    y0 = [kv[i][PAIR:2 * PAIR] - bx[i][PAIR:, PAIR:] for i in n]
    sm = [_mm1(jnp.concatenate([r_til[i], p_mat[i]], axis=0), items[i][6]) for i in n]
    y_st = [sm[i][0:PAIR] + y0[i] for i in n]
    ys = [y_st[i][0:CHUNK] + y_st[i][CHUNK:] for i in n]
    c_col = [jnp.broadcast_to(items[i][5], (PAIR, PAIR)).T for i in n]
    m_new = [(sm[i][PAIR:] + q_mat[i]) * c_col[i] for i in n]
    return ys, m_new


def _wkvscan_kernel(rf_ref, kf_ref, bf_ref, qf_ref, vf_ref, cf_ref, rb_ref, kb_ref, bb_ref, qb_ref, vb_ref, cb_ref,
                    m0_ref, msk_ref, yf_ref, yb_ref, mfin_ref, m_ref):
    i = pl.program_id(1)
    nc = pl.num_programs(1)

    @pl.when(i == 0)
    def _():
        m_ref[...] = m0_ref[0]

    eye = msk_ref[4]
    ins = ((rf_ref, kf_ref, bf_ref, qf_ref, vf_ref, cf_ref, yf_ref),
           (rb_ref, kb_ref, bb_ref, qb_ref, vb_ref, cb_ref, yb_ref))
    items = []
    for d in range(2):
        r_ref, k_ref, b_ref, q_ref, v_ref, c_ref, _ = ins[d]
        for p in range(N_PAIRS):
            sl = slice(p * PAIR, (p + 1) * PAIR)
            items.append((r_ref[:, sl], k_ref[:, sl], b_ref[:, sl], q_ref[:, sl], v_ref[:, sl],
                          c_ref[0, :, sl], m_ref[d, p], msk_ref[2 * d], msk_ref[2 * d + 1]))
    ys, m_new = _wkv_chunks(items, eye)
    for d in range(2):
        for p in range(N_PAIRS):
            sl = slice(p * PAIR, (p + 1) * PAIR)
            ins[d][6][:, sl] = ys[d * N_PAIRS + p]
            m_ref[d, p] = m_new[d * N_PAIRS + p]

    @pl.when(i == nc - 1)
    def _():
        mfin_ref[0] = m_ref[...]


def _wkvscan(prep, v, row0, n_seq, s_len, m0, masks):
    rf, kf, bf, qf, cf, rb, kb, bb, qb, cb = prep
    nc = s_len // CHUNK
    c0 = row0 // CHUNK
    rows = n_seq * s_len
    w = WKV_W
    fwd = lambda b, i: (c0 + b * nc + i, 0)
    bwd = lambda b, i: (c0 + b * nc + nc - 1 - i, 0)
    fwd3 = lambda b, i: (c0 + b * nc + i, 0, 0)
    bwd3 = lambda b, i: (c0 + b * nc + nc - 1 - i, 0, 0)
    blk = lambda im: pl.BlockSpec((CHUNK, w), im)
    cblk = lambda im: pl.BlockSpec((1, 1, w), im)
    mspec = pl.BlockSpec((1, 2, N_PAIRS, PAIR, PAIR), lambda b, i: (b, 0, 0, 0, 0))
    return pl.pallas_call(
        _wkvscan_kernel,
        grid=(n_seq, nc),
        in_specs=[blk(fwd)] * 5 + [cblk(fwd3)] + [blk(bwd)] * 5 + [cblk(bwd3)]
        + [mspec, pl.BlockSpec(masks.shape, lambda b, i: (0, 0, 0))],
        out_specs=[pl.BlockSpec((CHUNK, w), lambda b, i: (b * nc + i, 0)),
                   pl.BlockSpec((CHUNK, w), lambda b, i: (b * nc + nc - 1 - i, 0)),
                   mspec],
        out_shape=[jax.ShapeDtypeStruct((rows, w), F32), jax.ShapeDtypeStruct((rows, w), F32),
                   jax.ShapeDtypeStruct(m0.shape, F32)],
        scratch_shapes=[pltpu.VMEM((2, N_PAIRS, PAIR, PAIR), F32)],
        compiler_params=_cparams(("arbitrary", "arbitrary")),
        name="wkv_scan_%d" % s_len,
    )(rf, kf, bf, qf, v, cf, rb, kb, bb, qb, v, cb, m0, masks)


def _post_odd_kernel(tab_ref, x_ref, hfp_ref, hfs_ref, hbp_ref, hbs_ref, gb_ref, yfp_ref, yfs_ref, ybp_ref, ybs_ref,
                     bon_ref, g_ref, m_ref, seg_ref, gng_ref, gnb_ref, wo_ref, g1_ref, b1_ref, wq_ref, keys_ref,
                     x1_ref, h2t_ref, th1_ref, p1_ref, p2_ref, s2_ref, st_ref, v1_ref, v2_ref, *, n_prompt_tiles):
    is_prompt = pl.program_id(0) < n_prompt_tiles
    pick = lambda p_ref, s_ref: jnp.where(is_prompt, p_ref[...], s_ref[...])
    y_lru = ((pick(hfp_ref, hfs_ref) + pick(hbp_ref, hbs_ref)) * _gelu(gb_ref[...])).astype(BF16)
    seg = seg_ref[...]
    ys = pick(yfp_ref, yfs_ref) + pick(ybp_ref, ybs_ref)
    mean = _mm1(ys, seg) * (1.0 / WKV_N)
    yc = ys - mean
    var = _mm1(yc * yc, seg) * (1.0 / WKV_N)
    yn = yc * lax.rsqrt(var + WKV_GN_EPS) * gng_ref[...] + gnb_ref[...]
    y_wkv = ((yn + bon_ref[...]) * g_ref[...]).astype(BF16)
    mix = _dot(y_lru, wo_ref[0:LRU_W]) + _dot(y_wkv, wo_ref[LRU_W:])
    _post_and_query(x_ref[...], mix, m_ref[0], g1_ref[...], b1_ref[...], wq_ref, keys_ref,
                    x1_ref, h2t_ref, (th1_ref, p1_ref, p2_ref, s2_ref), st_ref, v1_ref, v2_ref)


def _post_odd(x, tab, hf, hb, zl, yf, yb, bonus, g, mods, seg, gng, gnb, wo, g1, b1, wq, keys):
    t, d = x.shape
    tm = ROW_TILE
    w = WKV_W
    n_p = hf[0].shape[0] // tm
    n_s = hf[1].shape[0] // tm
    row = pl.BlockSpec((tm, w), lambda i, t_: (i, 0))
    row_p = pl.BlockSpec((tm, w), lambda i, t_: (jnp.minimum(i, n_p - 1), 0))
    row_s = pl.BlockSpec((tm, w), lambda i, t_: (jnp.clip(i - n_p, 0, n_s - 1), 0))
    in_specs = [pl.BlockSpec((tm, d), lambda i, t_: (i, 0)), row_p, row_s, row_p, row_s,
                pl.BlockSpec((tm, LRU_W), lambda i, t_: (i, 1)), row_p, row_s, row_p, row_s, row, row,
                pl.BlockSpec((1, 6, d), lambda i, t_: (t_[0, i], 0, 0))]
    in_specs += [_const_spec(a.shape) for a in (seg, gng, gnb, wo, g1, b1, wq, keys)]
    out_specs, out_shape = _post_out_specs(t, d, tm)
    return pl.pallas_call(
        functools.partial(_post_odd_kernel, n_prompt_tiles=n_p),
        grid_spec=pltpu.PrefetchScalarGridSpec(
            num_scalar_prefetch=1, grid=(t // tm,), in_specs=in_specs, out_specs=out_specs,
            scratch_shapes=_post_scratch(tm)),
        out_shape=out_shape,
        compiler_params=_cparams(("arbitrary",)),
        name="post_odd",
    )(tab, x, hf[0], hf[1], hb[0], hb[1], zl, yf[0], yf[1], yb[0], yb[1], bonus, g, mods, seg, gng, gnb, wo,
      g1, b1, wq, keys)


N_TOP = PEER_TOPK + 1
TOP_ROWS = 24
SUBLANES = 8


def _batcher_network(n):
    def merge(lo, hi, r):
        step = r * 2
        if step < hi - lo:
            yield from merge(lo, hi, step)
            yield from merge(lo + r, hi, step)
            yield from [(i, i + r) for i in range(lo + r, hi - r, step)]
        else:
            yield (lo, lo + r)

    def sort(lo, hi):
        if hi - lo >= 1:
            mid = lo + (hi - lo) // 2
            yield from sort(lo, mid)
            yield from sort(mid + 1, hi)
            yield from merge(lo, hi, 1)

    return tuple(sort(0, n - 1))


def _sort_levels(levels, net):
    lv = list(levels)
    for i, j in net:
        a, b = lv[i], lv[j]
        lv[i] = jnp.maximum(a, b)
        lv[j] = jnp.minimum(a, b)
    return lv


def _pop_sorted(levels, n_top, emit):
    lv = list(levels)
    for it in range(n_top):
        m = jnp.max(lv[0], axis=0, keepdims=True)
        emit(it, m)
        hit = lv[0] == m
        live = min(len(lv), n_top - it - 1)
        for k in range(live):
            below = lv[k + 1] if k + 1 < len(lv) else -jnp.inf
            lv[k] = jnp.where(hit, below, lv[k])


def _topk_heads(st_ref, th1_ref, p1_ref, p2_ref, s2_ref, v1_ref, v2_ref):
    tl = st_ref.shape[-1]
    neg = -jnp.inf
    net16 = _batcher_network(N_KEYS // SUBLANES)
    net8 = _batcher_network(SUBLANES)
    row = lax.broadcasted_iota(jnp.int32, (SUBLANES, tl), 0)
    roll = lambda x, sh: pltpu.roll(x, sh, 0)

    def top_values(x, store):
        store[...] = jnp.full(store.shape, neg, F32)
        lv = _sort_levels([x[SUBLANES * k:SUBLANES * (k + 1)] for k in range(N_KEYS // SUBLANES)], net16)

        def emit(it, m):
            store[it:it + 1, :] = m

        _pop_sorted(lv, N_TOP, emit)

    def head(h, carry):
        s1 = st_ref[pl.ds(pl.multiple_of(2 * h * N_KEYS, N_KEYS), N_KEYS), :]
        s2 = st_ref[pl.ds(pl.multiple_of((2 * h + 1) * N_KEYS, N_KEYS), N_KEYS), :]
        top_values(s1, v1_ref)
        top_values(s2, v2_ref)
        one = lambda ref, a: ref[a:a + 1, :]
        v2a = v2_ref[0:8, :]
        v1b2 = roll(v1_ref[8:16, :], 2)
        cands = [one(v1_ref, 0) + v2a, one(v1_ref, 0) + v2_ref[8:16, :], one(v1_ref, 0) + v2_ref[16:24, :],
                 one(v1_ref, 1) + v2a,
                 jnp.where(row < 5, one(v1_ref, 2), one(v1_ref, 4)) + jnp.where(row < 5, v2a, roll(v2a, 5)),
                 jnp.where(row < 4, one(v1_ref, 3), jnp.where(row < 6, one(v1_ref, 5), one(v1_ref, 6)))
                 + jnp.where(row < 4, v2a, jnp.where(row < 6, roll(v2a, 4), roll(v2a, 6))),
                 jnp.where(row < 2, one(v1_ref, 7), v1b2) + jnp.where(row < 2, v2a, one(v2_ref, 0)),
                 jnp.where(row < 2, v1b2, roll(v1_ref[16:24, :], 2)) + one(v2_ref, 0)]
        top = []
        _pop_sorted(_sort_levels(cands, net8), N_TOP, lambda it, m: top.append(m))
        tau = 0.5 * (top[PEER_TOPK - 1] + top[PEER_TOPK])
        mx1 = v1_ref[0:1, :]
        mx2 = v2_ref[0:1, :]
        zacc = jnp.zeros((SUBLANES, tl), F32)
        for cnd in cands:
            zacc = zacc + jnp.where(cnd >= tau, jnp.exp(cnd - (mx1 + mx2)), 0.0)
        zsum = jnp.sum(zacc, axis=0, keepdims=True)
        th1 = tau - s1
        p1 = jnp.exp(s1 - mx1) / zsum
        p2 = jnp.exp(s2 - mx2)
        for cb in range(tl // LANES):
            cs = slice(cb * LANES, (cb + 1) * LANES)
            th1_ref[cb, h] = th1[:, cs]
            p1_ref[cb, h] = p1[:, cs]
            p2_ref[cb, h] = p2[:, cs]
            s2_ref[cb, h] = s2[:, cs]
        return carry

    lax.fori_loop(0, PEER_HEADS, head, 0)


def _peer_kernel(tab_ref, h2t_hbm, s2_hbm, p2_hbm, th1a_ref, p1a_ref, th1b_ref, p1b_ref, u_ref, vt_ref, x1_ref, m_ref,
                 g2_ref, b2_ref, o_ref, acc_ref, act_ref, ga_ref, h2t_buf, s2_buf, p2_buf, h2t_stage, s2_stage,
                 p2_stage, sem):
    i = pl.program_id(0)
    n_tok = pl.num_programs(0)
    s = pl.program_id(1)

    def tile_copies(tile):
        c0 = pl.multiple_of(tile * PEER_TM, PEER_TM)
        b0 = tile * (PEER_TM // LANES)
        return (pltpu.make_async_copy(h2t_hbm.at[:, pl.ds(c0, PEER_TM)], h2t_stage, sem.at[0]),
                pltpu.make_async_copy(s2_hbm.at[pl.ds(b0, PEER_TM // LANES)], s2_stage, sem.at[1]),
                pltpu.make_async_copy(p2_hbm.at[pl.ds(b0, PEER_TM // LANES)], p2_stage, sem.at[2]))

    @pl.when(s == 0)
    def _():
        @pl.when(i == 0)
        def _():
            for c in tile_copies(0):
                c.start()

        for c in tile_copies(i):
            c.wait()
        h2t_buf[...] = h2t_stage[...]
        s2_buf[...] = s2_stage[...]
        p2_buf[...] = p2_stage[...]

        @pl.when(i + 1 < n_tok)
        def _():
            for c in tile_copies(i + 1):
                c.start()

    n_steps = pl.num_programs(1)
    n_blk = PEER_TM // LANES
    rows_per_tile = PEER_TE // N_KEYS
    n_iter = n_blk // BLK_PER_ITER
    d_rows = acc_ref.shape[0] // n_iter
    u_rows = PEER_TE // n_iter

    def project(half, q):
        ro = pl.multiple_of(q * u_rows, u_rows)
        return _dot(u_ref[pl.ds(half * PEER_TE + ro, u_rows), :], h2t_buf[...])

    def store_act(slot, q, a):
        ro = pl.multiple_of(q * u_rows, u_rows)
        for k in range(n_blk):
            act_ref[slot, k, pl.ds(ro, u_rows), :] = a[:, k * LANES:(k + 1) * LANES]

    def apply_v(slot, half, q):
        ro = pl.multiple_of(q * d_rows, d_rows)
        ga = jnp.concatenate([ga_ref[slot, k] for k in range(n_blk)], axis=1)
        return _dot(vt_ref[pl.ds(ro, d_rows), half * PEER_TE:(half + 1) * PEER_TE], ga)

    def stage(cur, th1_ref, p1_ref):
        nxt = 1 - cur

        def body(it, carry):
            a_next = project(nxt, it)
            v_prev = apply_v(nxt, nxt, it)
            for sub in range(BLK_PER_ITER):
                cb = it * BLK_PER_ITER + sub
                th = [th1_ref[cb, h] for h in range(PEER_HEADS)]
                p1 = [p1_ref[cb, h] for h in range(PEER_HEADS)]
                for rg in range(0, rows_per_tile, ROW_GROUP):
                    for part in range(N_KEYS // KEY_CHUNK):
                        ks = slice(part * KEY_CHUNK, (part + 1) * KEY_CHUNK)
                        gates = [None] * ROW_GROUP
                        for h in range(PEER_HEADS):
                            s2 = s2_buf[cb, h, ks, :]
                            p2 = p2_buf[cb, h, ks, :]
                            for g in range(ROW_GROUP):
                                r = rg + g
                                term = jnp.where(s2 >= th[h][r:r + 1], p2, 0.0) * p1[h][r:r + 1]
                                gates[g] = term if gates[g] is None else gates[g] + term
                        for g in range(ROW_GROUP):
                            lo = (rg + g) * N_KEYS + part * KEY_CHUNK
                            rs = slice(lo, lo + KEY_CHUNK)
                            ga_ref[cur, cb, rs, :] = (gates[g] * _gelu(act_ref[cur, cb, rs, :])).astype(BF16)
            store_act(nxt, it, a_next)
            ro = pl.multiple_of(it * d_rows, d_rows)
            acc_ref[pl.ds(ro, d_rows), :] += v_prev
            return carry

        lax.fori_loop(0, n_iter, body, 0)

    @pl.when(s == 0)
    def _():
        acc_ref[...] = jnp.zeros(acc_ref.shape, F32)
        ga_ref[...] = jnp.zeros(ga_ref.shape, BF16)
        for q in range(n_iter):
            store_act(0, q, project(0, q))

    @pl.when(s > 0)
    def _():
        stage(1, th1a_ref, p1a_ref)

    @pl.when(s < n_steps - 1)
    def _():
        stage(0, th1b_ref, p1b_ref)

    @pl.when(s == n_steps - 1)
    def _():
        m = m_ref[0]
        for q in range(n_iter):
            acc_ref[q * d_rows:(q + 1) * d_rows, :] += apply_v(1, 1, q)
        ffn = acc_ref[...].T
        o_ref[...] = _ln(ALPHA * x1_ref[...] + m[5:6] * ffn) * g2_ref[...] + b2_ref[...]


def _peer(x1, tab, h2t, s2, p2, th1, p1, u_bf, vt_bf, mods, g2, b2):
    t, d = x1.shape
    tm = PEER_TM
    te = PEER_TE
    n_tiles = u_bf.shape[0] // te
    per = tm // ROW_TILE
    n_blk = tm // LANES
    rows = te // N_KEYS
    rows_a = pl.BlockSpec((n_blk, PEER_HEADS, rows, LANES), lambda i, j, t_: (i, 0, jnp.maximum(2 * j - 1, 0), 0))
    rows_b = pl.BlockSpec((n_blk, PEER_HEADS, rows, LANES), lambda i, j, t_: (i, 0, jnp.minimum(2 * j, n_tiles - 1), 0))
    hbm = pl.BlockSpec(memory_space=pl.ANY)
    in_specs = [hbm, hbm, hbm, rows_a, rows_a, rows_b, rows_b,
                pl.BlockSpec((2 * te, d), lambda i, j, t_: (jnp.minimum(j, n_tiles // 2 - 1), 0)),
                pl.BlockSpec((d, 2 * te), lambda i, j, t_: (0, jnp.maximum(j - 1, 0))),
                pl.BlockSpec((tm, d), lambda i, j, t_: (i, 0)),
                pl.BlockSpec((1, 6, d), lambda i, j, t_: (t_[0, i * per], 0, 0)),
                pl.BlockSpec(g2.shape, lambda i, j, t_: (0, 0)),
                pl.BlockSpec(b2.shape, lambda i, j, t_: (0, 0))]
    return pl.pallas_call(
        _peer_kernel,
        grid_spec=pltpu.PrefetchScalarGridSpec(
            num_scalar_prefetch=1, grid=(t // tm, n_tiles // 2 + 1), in_specs=in_specs,
            out_specs=pl.BlockSpec((tm, d), lambda i, j, t_: (i, 0)),
            scratch_shapes=[pltpu.VMEM((d, tm), F32), pltpu.VMEM((2, n_blk, te, LANES), F32),
                            pltpu.VMEM((2, n_blk, te, LANES), BF16),
                            pltpu.VMEM((d, tm), BF16),
                            pltpu.VMEM((n_blk, PEER_HEADS, N_KEYS, LANES), F32),
                            pltpu.VMEM((n_blk, PEER_HEADS, N_KEYS, LANES), F32),
                            pltpu.VMEM((d, tm), BF16),
                            pltpu.VMEM((n_blk, PEER_HEADS, N_KEYS, LANES), F32),
                            pltpu.VMEM((n_blk, PEER_HEADS, N_KEYS, LANES), F32),
                            pltpu.SemaphoreType.DMA((3,))]),
        out_shape=jax.ShapeDtypeStruct((t, d), F32),
        compiler_params=_cparams(("arbitrary", "arbitrary")),
        name="peer_mix",
    )(tab, h2t, s2, p2, th1, p1, th1, p1, u_bf, vt_bf, x1, mods, g2, b2)


def _sincos(pos, dim):
    omega = 1.0 / (10000.0 ** (jnp.arange(dim // 2, dtype=F32) / (dim // 2)))
    ang = pos.astype(F32)[:, None] * omega[None, :]
    return jnp.concatenate([jnp.sin(ang), jnp.cos(ang)], -1)


def _grid_pos_embed(n_tok):
    rows = n_tok // GRID_W
    half = D_MODEL // 2
    er = _sincos(jnp.arange(rows), half)
    ec = _sincos(jnp.arange(GRID_W), half)
    emb = jnp.concatenate([jnp.broadcast_to(er[:, None, :], (rows, GRID_W, half)),
                           jnp.broadcast_to(ec[None, :, :], (rows, GRID_W, half))], -1)
    return emb.reshape(rows * GRID_W, D_MODEL)


def _seq_dft_tables(s_len):
    r = 1
    while r * r < s_len:
        r *= 2
    k = jnp.arange(s_len, dtype=jnp.int32)
    w = 2.0 * math.pi / s_len
    ang_a = ((jnp.arange(s_len // r, dtype=jnp.int32)[:, None] * r * k[None, :]) % s_len).astype(F32) * w
    ang_b = ((jnp.arange(r, dtype=jnp.int32)[:, None] * k[None, :]) % s_len).astype(F32) * w
    ca, sa = jnp.cos(ang_a)[:, None, :], jnp.sin(ang_a)[:, None, :]
    cb, sb = jnp.cos(ang_b)[None, :, :], jnp.sin(ang_b)[None, :, :]
    scale = 1.0 / math.sqrt(s_len * FNET_GW)
    cmat = ((ca * cb - sa * sb) * scale).reshape(s_len, s_len)
    smat = ((sa * cb + ca * sb) * (-scale)).reshape(s_len, s_len)
    return cmat.astype(BF16), smat.astype(BF16)


def _channel_dft_table():
    n = np.arange(FNET_GW)
    ang = ((n[:, None] * n[None, :]) % FNET_GW) * (2.0 * np.pi / FNET_GW)
    wc = np.zeros((FNET_W, 2 * FNET_W), np.float32)
    for g in range(FNET_GROUPS):
        sl = slice(g * FNET_GW, (g + 1) * FNET_GW)
        wc[sl, sl] = np.cos(ang)
        wc[sl, FNET_W + g * FNET_GW:FNET_W + (g + 1) * FNET_GW] = np.sin(ang)
    return jnp.asarray(wc, BF16)


def _segment_ones():
    idx = np.arange(WKV_W) // WKV_N
    return jnp.asarray((idx[:, None] == idx[None, :]).astype(np.float32), BF16)


def _chunk_tri():
    i = np.arange(ROW_TILE)
    same = (i[:, None] // CHUNK) == (i[None, :] // CHUNK)
    lower = same & (i[None, :] <= i[:, None])
    upper = same & (i[None, :] >= i[:, None])
    return jnp.asarray(np.stack([lower, upper]).astype(np.float32), BF16)


def _scan_masks():
    i = np.arange(PAIR)
    same = (i[:, None] // CHUNK) == (i[None, :] // CHUNK)
    t_row = i[:, None] % CHUNK
    t_col = i[None, :] % CHUNK
    masks = [same & (t_col < t_row), same & (t_col <= t_row), same & (t_col > t_row), same & (t_col >= t_row),
             np.eye(PAIR, dtype=bool)]
    return jnp.asarray(np.stack(masks).astype(np.float32))


def _block_diag(w):
    h, n, _ = w.shape
    eye = jnp.eye(h, dtype=w.dtype)
    return jnp.einsum("hij,hg->higj", w, eye).reshape(h * n, h * n)


def _pad_rank(w, d):
    r = w.shape[1]
    z = jnp.zeros_like(w[d])
    return jnp.concatenate([w[d], z] if d == 0 else [z, w[d]], axis=0)


def _state_to_pairs(s0):
    b = s0.shape[0]
    st = jnp.swapaxes(s0, -1, -2).reshape(b, 2, N_PAIRS, 2, WKV_N, WKV_N)
    eye = jnp.eye(2, dtype=s0.dtype)
    return jnp.einsum("bdpeji,ef->bdpejfi", st, eye).reshape(b, 2, N_PAIRS, PAIR, PAIR)


def _pairs_to_state(m):
    b = m.shape[0]
    m7 = m.reshape(b, 2, N_PAIRS, 2, WKV_N, 2, WKV_N)
    st = jnp.stack([m7[:, :, :, 0, :, 0, :], m7[:, :, :, 1, :, 1, :]], axis=3)
    return jnp.swapaxes(st.reshape(b, 2, WKV_H, WKV_N, WKV_N), -1, -2)


def _tile_table(groups):
    cv, first, last = [], [], []
    row = 0
    for gi, (n_seq, s_len) in enumerate(groups):
        nt = s_len // ROW_TILE
        for b in range(n_seq):
            for i in range(nt):
                cv.append(0 if gi == 0 else 1 + b)
                first.append(int(i == 0))
                last.append(int(i == nt - 1))
        row += n_seq * s_len
    return jnp.asarray(np.array([cv, first, last], np.int32))


def kernel(x_prompt, x_sample, state_lru, state_wkv, c, c_ctx, w_mod, b_mod, ln1_g, ln1_b, ln2_g, ln2_b,
           w_in_e, w_out_e, sconv_w, sconv_b, w_in_o, w_out_o, lru_conv_w, lru_conv_b, lru_wa, lru_ba,
           lru_wx, lru_bx, lru_lambda, wkv_mu, wkv_w0, wkv_w2, wkv_a0, wkv_a2, wkv_kk, wkv_ka, wkv_rk,
           wkv_g2, wkv_gn_g, wkv_gn_b, peer_wq, peer_keys, peer_u, peer_v):
    bp, sp, d = x_prompt.shape
    bs, ss, _ = x_sample.shape
    depth = w_mod.shape[0]
    assert sp % ROW_TILE == 0 and ss % PEER_TM == 0 and (bp * sp) % ss == 0
    assert bs + 1 <= 8
    groups = ((bp, sp), (bs, ss))
    tp = bp * sp
    tab = _tile_table(groups)

    cv8 = jnp.concatenate([c_ctx[None, :], c, jnp.zeros((8 - 1 - bs, d), F32)], axis=0)
    mods = _modulation(cv8, w_mod, b_mod).reshape(depth, 8, 6, d)

    x = _embed(x_prompt.reshape(tp, d), x_sample.reshape(bs * ss, d), _grid_pos_embed(ss).astype(x_sample.dtype))

    wc = _channel_dft_table()
    dft = {s: _seq_dft_tables(s) for s in sorted({sp, ss})}
    seg = _segment_ones()
    tri = _chunk_tri()
    masks = _scan_masks()
    row2 = lambda a: a.reshape(1, -1)

    lru_fin = []
    wkv_fin = []
    for l in range(depth):
        j = l // 2
        m_l = mods[l]
        wq = peer_wq[l].astype(BF16)
        keys = peer_keys[l].reshape(2 * PEER_HEADS, N_KEYS, PEER_HALF).astype(BF16)
        if l % 2 == 0:
            zr, pq = _inproj(x, tab, m_l, w_in_e[j].astype(BF16), wc)
            y = jnp.concatenate([_seqdft(pq, bp * sp, 0, bp, sp, *dft[sp]),
                                 _seqdft(pq, bs * ss, tp, bs, ss, *dft[ss])], axis=0)
            x1, h2t, th1, p1, p2, s2 = _post_even(x, tab, zr, y, m_l, sconv_w[j], row2(sconv_b[j]), w_out_e[j].astype(BF16),
                                     row2(ln1_g[l]), row2(ln1_b[l]), wq, keys)
        else:
            zl, zw = _inproj(x, tab, m_l, w_in_o[j].astype(BF16))
            lru_args = (lru_conv_w[j], lru_conv_b[j],
                        jnp.stack([_block_diag(lru_wa[j, dd]) for dd in range(2)]).astype(BF16), lru_ba[j],
                        jnp.stack([_block_diag(lru_wx[j, dd]) for dd in range(2)]).astype(BF16), lru_bx[j],
                        lru_lambda[j])
            hf_p, hb_p, hfin_p = _lru(zl, 0, bp, sp, jnp.zeros((bp, 2, LRU_W), F32), *lru_args)
            hf_s, hb_s, _ = _lru(zl, tp, bs, ss, state_lru[:, j], *lru_args)
            lru_fin.append(hfin_p)
            prep = _wkvprep(zw, tab, row2(wkv_mu[j]), wkv_w0[j],
                            jnp.stack([_pad_rank(wkv_w2[j], dd) for dd in range(2)]).astype(BF16), wkv_a0[j],
                            jnp.stack([_pad_rank(wkv_a2[j], dd) for dd in range(2)]).astype(BF16),
                            wkv_kk[j], wkv_ka[j], row2(wkv_rk[j]), wkv_g2[j].astype(BF16), seg, tri)
            v, bonus, g = prep[0], prep[1], prep[2]
            yf_p, yb_p, mfin_p = _wkvscan(prep[3:], v, 0, bp, sp,
                                          jnp.zeros((bp, 2, N_PAIRS, PAIR, PAIR), F32), masks)
            yf_s, yb_s, _ = _wkvscan(prep[3:], v, tp, bs, ss, _state_to_pairs(state_wkv[:, j]), masks)
            wkv_fin.append(_pairs_to_state(mfin_p))
            x1, h2t, th1, p1, p2, s2 = _post_odd(x, tab, (hf_p, hf_s), (hb_p, hb_s), zl, (yf_p, yf_s), (yb_p, yb_s),
                                    bonus, g, m_l, seg, row2(wkv_gn_g[j]), row2(wkv_gn_b[j]),
                                    w_out_o[j].astype(BF16), row2(ln1_g[l]), row2(ln1_b[l]), wq, keys)
        x = _peer(x1, tab, h2t, s2, p2, th1, p1, peer_u[l].astype(BF16), peer_v[l].T.astype(BF16), m_l,
                  row2(ln2_g[l]), row2(ln2_b[l]))

    y_prompt = x[:tp].reshape(bp, sp, d)
    y_sample = x[tp:].reshape(bs, ss, d)
    return (y_prompt, y_sample, jnp.stack(lru_fin, 1).astype(x_prompt.dtype),
            jnp.stack(wkv_fin, 1).astype(x_prompt.dtype))
```
